```python
import jax, jax.numpy as jnp
from jax import lax
import numpy as np

D_MODEL = 1024
BATCH = 16
SEQ = 256
DEPTH = 2
DEC_BATCH = 2
DEC_SEQ = 2048
PAST_LEN = 512

GRID_W = 64
NA_HEADS = 8
NA_HEAD_DIM = 64
NA_WIDTH = NA_HEADS * NA_HEAD_DIM
WIN_R = 8
WIN_C = 16
NA_QB = WIN_C
NA_KBW = 2 * WIN_C
CTX_QBLOCK = 128
NEG_INF = -1e30
RET_HEADS = 4
RET_DK = 64
RET_DV = 128
RET_QK_WIDTH = RET_HEADS * RET_DK
RET_V_WIDTH = RET_HEADS * RET_DV
RET_CHUNK = 64
RET_GN_EPS = 1e-5
ROPE_BASE = 10000.0
RWKV_HEADS = 8
RWKV_N = 64
RWKV_WIDTH = RWKV_HEADS * RWKV_N
RWKV_LORA_W = 64
RWKV_LORA_A = 64
RWKV_LORA_G = 128
RWKV_DECAY_SCALE = 0.606531
RWKV_GN_EPS = 64e-5
RWKV_SIZES = (RWKV_WIDTH, RWKV_WIDTH, RWKV_WIDTH, RWKV_LORA_W, RWKV_LORA_W, RWKV_LORA_A, RWKV_LORA_A, RWKV_LORA_G)
RWKV_SHIFT_WIDTH = 3 * RWKV_WIDTH + 2 * RWKV_LORA_W + 2 * RWKV_LORA_A + RWKV_LORA_G
N_BRANCH = 3
BRANCH_WIDTH = 512
IN_SIZES = (NA_WIDTH, NA_WIDTH, NA_WIDTH, RET_QK_WIDTH, RET_QK_WIDTH, RET_V_WIDTH, RET_V_WIDTH,
            RWKV_SHIFT_WIDTH, D_MODEL, D_MODEL, D_MODEL)
P_IN = 3 * NA_WIDTH + 2 * RET_QK_WIDTH + 2 * RET_V_WIDTH + RWKV_SHIFT_WIDTH + N_BRANCH * D_MODEL
PEER_HEADS = 8
PEER_NKEYS = 128
PEER_N_EXPERTS = PEER_NKEYS * PEER_NKEYS
PEER_DK = 128
PEER_TOPK = 16
PEER_TOK_BLOCK = 128
LN_EPS = 1e-5
DEEPNORM_ALPHA = (2 * DEPTH) ** 0.25
DEEPNORM_BETA = (8 * DEPTH) ** -0.25

kernel_name = 'hybrid_natten_retnet_rwkv7_peer_diffusion_step'


def _split_points(sizes):
    return [int(s) for s in np.cumsum(sizes)[:-1]]


def flip(t):
    return t[:, ::-1]


def layer_norm(x, g=None, b=None):
    xf = x.astype(jnp.float32)
    mu = jnp.mean(xf, -1, keepdims=True)
    var = jnp.mean(jnp.square(xf - mu), -1, keepdims=True)
    y = (xf - mu) * lax.rsqrt(var + LN_EPS)
    if g is not None:
        y = y * g.astype(jnp.float32) + b.astype(jnp.float32)
    return y.astype(x.dtype)


def head_norm(x, g, b, eps):
    bsz, t_len, nh, n = x.shape
    xf = x.astype(jnp.float32)
    mu = jnp.mean(xf, -1, keepdims=True)
    var = jnp.mean(jnp.square(xf - mu), -1, keepdims=True)
    y = ((xf - mu) * lax.rsqrt(var + eps)).reshape(bsz, t_len, nh * n)
    return y * g.astype(jnp.float32) + b.astype(jnp.float32)


def centered_shift(z, mu):
    zp = jnp.pad(z, ((0, 0), (1, 1), (0, 0)))
    return z + mu * (0.5 * (zp[:, :-2] + zp[:, 2:]) - z)


def axial_rope(x):
    t_len, d = x.shape[1], x.shape[-1]
    half, quarter = d // 2, d // 4
    pos = jnp.arange(t_len)
    freqs = 1.0 / (ROPE_BASE ** (jnp.arange(quarter, dtype=jnp.float32) / quarter))
    xf = x.astype(jnp.float32)

    def rot(part, p_axis):
        ang = p_axis.astype(jnp.float32)[:, None] * freqs[None, :]
        cos, sin = jnp.cos(ang)[None, :, None, :], jnp.sin(ang)[None, :, None, :]
        p1, p2 = part[..., :quarter], part[..., quarter:]
        return jnp.concatenate([p1 * cos - p2 * sin, p1 * sin + p2 * cos], axis=-1)

    out = jnp.concatenate([rot(xf[..., :half], pos // GRID_W), rot(xf[..., half:], pos % GRID_W)], axis=-1)
    return out.astype(x.dtype)


def ctx_attention(q, k, v):
    bsz, l_len, nh, dh = q.shape
    nb = l_len // CTX_QBLOCK
    kt, vt = k.transpose(0, 2, 1, 3), v.transpose(0, 2, 1, 3)
    qblocks = q.reshape(bsz, nb, CTX_QBLOCK, nh, dh).transpose(1, 0, 3, 2, 4)

    def blk(qb):
        s = jnp.einsum('bhqd,bhkd->bhqk', qb, kt).astype(jnp.float32) * dh ** -0.5
        p = jax.nn.softmax(s, axis=-1).astype(v.dtype)
        return jnp.einsum('bhqk,bhkd->bhqd', p, vt)

    o = lax.map(blk, qblocks)
    return o.transpose(1, 0, 3, 2, 4).reshape(bsz, l_len, nh * dh)


def na_latent(q, k, v, ctx_k, ctx_v, rpb):
    bsz, t_len, nh, dh = q.shape
    rows = t_len // GRID_W
    kr = min(WIN_R, rows)
    ncb = GRID_W // NA_QB
    scale = dh ** -0.5

    def grid(z):
        return z.reshape(bsz, rows, GRID_W, nh, dh).transpose(0, 3, 1, 2, 4)

    qg, kg, vg = grid(q), grid(k), grid(v)
    r_ar = jnp.arange(rows)
    row_idx = jnp.clip(r_ar - kr // 2, 0, rows - kr)[:, None] + jnp.arange(kr)[None, :]
    cb_start = jnp.clip(jnp.arange(ncb) * NA_QB - WIN_C // 2, 0, GRID_W - NA_KBW)
    col_idx = cb_start[:, None] + jnp.arange(NA_KBW)[None, :]
    nw = kr * NA_KBW

    def gather(z):
        zb = z[:, :, row_idx[:, :, None, None], col_idx[None, None, :, :]]
        return zb.transpose(0, 1, 2, 4, 3, 5, 6).reshape(bsz, nh, rows, ncb, nw, dh)

    kb, vb = gather(kg), gather(vg)
    qb = qg.reshape(bsz, nh, rows, ncb, NA_QB, dh)
    q_col = jnp.arange(GRID_W).reshape(ncb, NA_QB)
    q_start = jnp.clip(q_col - WIN_C // 2, 0, GRID_W - WIN_C)
    rel = col_idx[:, None, :] - q_start[:, :, None]
    valid = (rel >= 0) & (rel < WIN_C)
    valid = jnp.broadcast_to(valid[:, :, None, :], (ncb, NA_QB, kr, NA_KBW)).reshape(ncb, NA_QB, nw)
    dr_i = row_idx - r_ar[:, None] + WIN_R - 1
    dc_i = jnp.clip(col_idx[:, None, :] - q_col[:, :, None], -(WIN_C - 1), WIN_C - 1) + WIN_C - 1
    bias = rpb.astype(jnp.float32)[:, dr_i[:, None, None, :, None], dc_i[None, :, :, None, :]]
    bias = bias.reshape(nh, rows, ncb, NA_QB, nw)
    s_win = jnp.einsum('bhrcqd,bhrckd->bhrcqk', qb, kb).astype(jnp.float32) * scale + bias[None]
    s_win = jnp.where(valid[None, None, None], s_win, NEG_INF)
    s_ctx = jnp.einsum('bhrcqd,bhld->bhrcql', qb, ctx_k).astype(jnp.float32) * scale
    p = jax.nn.softmax(jnp.concatenate([s_win, s_ctx], axis=-1), axis=-1).astype(v.dtype)
    o = (jnp.einsum('bhrcqk,bhrckd->bhrcqd', p[..., :nw], vb)
         + jnp.einsum('bhrcql,bhld->bhrcqd', p[..., nw:], ctx_v))
    return o.reshape(bsz, nh, rows, GRID_W, dh).transpose(0, 2, 3, 1, 4).reshape(bsz, t_len, nh * dh)


def retention_scan(q, k, v, gamma, s0):
    bsz, t_len, nh, dk = q.shape
    dv = v.shape[-1]
    c = RET_CHUNK
    n = t_len // c
    lg = jnp.log(gamma)
    i = jnp.arange(c, dtype=jnp.float32)
    diff = i[:, None] - i[None, :]
    mask = jnp.where(diff >= 0, jnp.exp(jnp.maximum(diff, 0.0)[None] * lg[:, None, None]), 0.0)
    q_dec = jnp.exp((i + 1.0)[:, None] * lg[None, :])[None, :, :, None]
    k_dec = jnp.exp((c - 1.0 - i)[:, None] * lg[None, :])[None, :, :, None]
    c_dec = jnp.exp(c * lg)[None, :, None, None]

    def to_chunks(z):
        return z.astype(jnp.float32).reshape(bsz, n, c, nh, z.shape[-1]).transpose(1, 0, 2, 3, 4)

    def step(s, inp):
        qc, kc, vc = inp
        att = jnp.einsum('bihd,bjhd->bhij', qc, kc) * mask
        o = jnp.einsum('bhij,bjhv->bihv', att, vc) + jnp.einsum('bihd,bhdv->bihv', qc * q_dec, s)
        s = s * c_dec + jnp.einsum('bjhd,bjhv->bhdv', kc * k_dec, vc)
        return s, o

    s_fin, o = lax.scan(step, s0.astype(jnp.float32), (to_chunks(q), to_chunks(k), to_chunks(v)))
    return o.transpose(1, 0, 2, 3, 4).reshape(bsz, t_len, nh, dv), s_fin


def rwkv7_scan(r, w, k, v, kk, a, s0):
    def tm(z):
        return jnp.moveaxis(z.astype(jnp.float32), 1, 0)

    def step(s, inp):
        r_t, w_t, k_t, v_t, kk_t, a_t = inp
        sk = jnp.einsum('bhvk,bhk->bhv', s, kk_t)
        s = s * w_t[:, :, None, :] - sk[..., None] * (kk_t * a_t)[:, :, None, :] + v_t[..., None] * k_t[:, :, None, :]
        return s, jnp.einsum('bhvk,bhk->bhv', s, r_t)

    s_fin, o = lax.scan(step, s0.astype(jnp.float32), (tm(r), tm(w), tm(k), tm(v), tm(kk), tm(a)))
    return jnp.moveaxis(o, 0, 1), s_fin


def token_mix(h, p, ctx):
    bsz, t_len, _ = h.shape
    f32 = jnp.float32
    lat = ctx is not None

    def heads(t, nh):
        return t.reshape(bsz, t_len, nh, -1)

    z = h @ p['w_in']
    (q_na, k_na, v_na, q_rt, k_rt, v_rt, g_rt, z_rw, g_a, g_b, g_c) = jnp.split(z, _split_points(IN_SIZES), axis=-1)

    q_na, k_na, v_na = heads(q_na, NA_HEADS), heads(k_na, NA_HEADS), heads(v_na, NA_HEADS)
    if lat:
        o_na = na_latent(q_na, k_na, v_na, ctx[0], ctx[1], p['na_rpb'])
    else:
        o_na = ctx_attention(q_na, k_na, v_na)

    q_rt, k_rt, v_rt = heads(q_rt, RET_HEADS), heads(k_rt, RET_HEADS), heads(v_rt, RET_HEADS)
    if lat:
        q_rt, k_rt = axial_rope(q_rt), axial_rope(k_rt)
        s_ret0 = ctx[2]
    else:
        s_ret0 = jnp.zeros((bsz, 2, RET_HEADS, RET_DK, RET_DV), f32)
    k_rt = k_rt * RET_DK ** -0.5
    gam = jax.nn.sigmoid(p['ret_decay_logit'].astype(f32))
    o_f, s_f = retention_scan(q_rt, k_rt, v_rt, gam[0], s_ret0[:, 0])
    o_b, s_b = retention_scan(flip(q_rt), flip(k_rt), flip(v_rt), gam[1], s_ret0[:, 1])
    o_rt = (head_norm(o_f + flip(o_b), p['ret_gn_w'], p['ret_gn_b'], RET_GN_EPS)
            * jax.nn.silu(g_rt.astype(f32))).astype(h.dtype)

    z_rw = centered_shift(z_rw, p['rwkv_mu'])
    r_c, k_c, v_c, wl_f, wl_b, al_f, al_b, gl = jnp.split(z_rw, _split_points(RWKV_SIZES), axis=-1)
    g_rw = jax.nn.sigmoid(gl) @ p['rwkv_g_up']
    kk = heads((k_c * p['rwkv_k_k']).astype(f32), RWKV_HEADS)
    kk = kk * lax.rsqrt(jnp.maximum(jnp.sum(kk * kk, -1, keepdims=True), 1e-24))
    r_h, v_h, k_h = heads(r_c, RWKV_HEADS), heads(v_c, RWKV_HEADS), heads(k_c, RWKV_HEADS)
    s_rw0 = ctx[3] if lat else jnp.zeros((bsz, 2, RWKV_HEADS, RWKV_N, RWKV_N), f32)
    outs, states = [], []
    for d, (wl, al) in enumerate(((wl_f, al_f), (wl_b, al_b))):
        w = jnp.exp(-RWKV_DECAY_SCALE * jax.nn.sigmoid((p['rwkv_w0'][d] + jnp.tanh(wl) @ p['rwkv_w_up'][d]).astype(f32)))
        a = jax.nn.sigmoid((p['rwkv_a0'][d] + al @ p['rwkv_a_up'][d]).astype(f32))
        k_d = k_c.astype(f32) * (1.0 + (a - 1.0) * p['rwkv_k_a'].astype(f32))
        seq = (r_h, heads(w, RWKV_HEADS), heads(k_d, RWKV_HEADS), v_h, kk, heads(a, RWKV_HEADS))
        if d == 1:
            seq = tuple(flip(t) for t in seq)
        o, s = rwkv7_scan(*seq, s_rw0[:, d])
        outs.append(flip(o) if d == 1 else o)
        states.append(s)
    bonus = jnp.sum(r_h.astype(f32) * k_h.astype(f32) * p['rwkv_r_k'].astype(f32), -1, keepdims=True) * v_h.astype(f32)
    o_rw = ((head_norm(outs[0] + outs[1], p['rwkv_gn_w'], p['rwkv_gn_b'], RWKV_GN_EPS)
             + bonus.reshape(bsz, t_len, RWKV_WIDTH)) * g_rw.astype(f32)).astype(h.dtype)

    w_br = p['w_br']
    merged = (jax.nn.sigmoid(g_a) * (o_na @ w_br[0])
              + jax.nn.sigmoid(g_b) * (o_rt @ w_br[1])
              + jax.nn.sigmoid(g_c) * (o_rw @ w_br[2]))
    out = merged @ p['w_out']
    if lat:
        return out, None
    return out, (k_na.transpose(0, 2, 1, 3), v_na.transpose(0, 2, 1, 3),
                 jnp.stack([s_f, s_b], axis=1), jnp.stack(states, axis=1))


def peer(h, wq, keys, u, v):
    bsz, t_len, d = h.shape
    xs = h.reshape(-1, PEER_TOK_BLOCK, d)

    def blk(xb):
        n = xb.shape[0]
        q = (xb @ wq).reshape(n, PEER_HEADS, 2, PEER_DK // 2)
        s = jnp.einsum('nhpd,hpkd->nhpk', q, keys).astype(jnp.float32)
        ts, ti = lax.top_k(s, PEER_TOPK)
        cand = (ts[:, :, 0, :, None] + ts[:, :, 1, None, :]).reshape(n, PEER_HEADS, PEER_TOPK * PEER_TOPK)
        cidx = (ti[:, :, 0, :, None] * PEER_NKEYS + ti[:, :, 1, None, :]).reshape(n, PEER_HEADS, PEER_TOPK * PEER_TOPK)
        fs, fpos = lax.top_k(cand, PEER_TOPK)
        eidx = jnp.take_along_axis(cidx, fpos, axis=-1).reshape(n, PEER_HEADS * PEER_TOPK)
        g = jax.nn.softmax(fs, axis=-1).reshape(n, PEER_HEADS * PEER_TOPK)
        act = jax.nn.gelu(jnp.einsum('nkd,nd->nk', u[eidx], xb).astype(jnp.float32), approximate=False) * g
        return jnp.einsum('nk,nkd->nd', act.astype(xb.dtype), v[eidx])

    return lax.map(blk, xs).reshape(bsz, t_len, d)


def trunk_layer(x, mod, p, ctx):
    sh_a, sc_a, gt_a, sh_f, sc_f, gt_f = jnp.split(mod, 6, axis=-1)
    h = layer_norm(x) * (1.0 + sc_a) + sh_a
    t, new_ctx = token_mix(h, p, ctx)
    x = layer_norm(DEEPNORM_ALPHA * x + gt_a * t, p['ln_a_g'], p['ln_a_b'])
    h = layer_norm(x) * (1.0 + sc_f) + sh_f
    f = peer(h, p['peer_wq'], p['peer_keys'], p['peer_u'], p['peer_v'])
    x = layer_norm(DEEPNORM_ALPHA * x + gt_f * f, p['ln_f_g'], p['ln_f_b'])
    return x, new_ctx


def setup_inputs(seed: int = 0) -> dict:
    key = jax.random.key(seed)
    ks = iter(jax.random.split(key, 48))
    f32 = jnp.float32
    D = D_MODEL
    L = DEPTH

    def nrm(shape, scale):
        return jax.random.normal(next(ks), shape, f32) * scale

    ret_base = 1.0 - 2.0 ** (-5.0 - jnp.arange(RET_HEADS, dtype=f32))
    ret_logit = jnp.log(ret_base / (1.0 - ret_base))
    return {
        'x_prompt': nrm((BATCH, SEQ, D), 1.0),
        'x_sample': nrm((DEC_BATCH, DEC_SEQ, D), 1.0),
        'cache_na_k': nrm((DEC_BATCH, L, NA_HEADS, PAST_LEN, NA_HEAD_DIM), 1.0),
        'cache_na_v': nrm((DEC_BATCH, L, NA_HEADS, PAST_LEN, NA_HEAD_DIM), 1.0),
        'state_ret': nrm((DEC_BATCH, L, 2, RET_HEADS, RET_DK, RET_DV), 0.5),
        'state_rwkv': nrm((DEC_BATCH, L, 2, RWKV_HEADS, RWKV_N, RWKV_N), 0.5),
        'c': nrm((DEC_BATCH, D), 1.0),
        'c_ctx': nrm((D,), 1.0),
        'w_mod': nrm((L, D, 6 * D), D ** -0.5),
        'b_mod': nrm((L, 6 * D), 0.02),
        'w_in': nrm((L, D, P_IN), D ** -0.5),
        'na_rpb': nrm((L, NA_HEADS, 2 * WIN_R - 1, 2 * WIN_C - 1), 0.02),
        'ret_decay_logit': ret_logit[None, None, :] + nrm((L, 2, RET_HEADS), 0.1),
        'ret_gn_w': 1.0 + nrm((L, RET_V_WIDTH), 0.02),
        'ret_gn_b': nrm((L, RET_V_WIDTH), 0.02),
        'rwkv_mu': jax.random.uniform(next(ks), (L, RWKV_SHIFT_WIDTH), f32),
        'rwkv_w0': nrm((L, 2, RWKV_WIDTH), 0.5),
        'rwkv_w_up': nrm((L, 2, RWKV_LORA_W, RWKV_WIDTH), RWKV_LORA_W ** -0.5),
        'rwkv_a0': nrm((L, 2, RWKV_WIDTH), 0.5),
        'rwkv_a_up': nrm((L, 2, RWKV_LORA_A, RWKV_WIDTH), RWKV_LORA_A ** -0.5),
        'rwkv_g_up': nrm((L, RWKV_LORA_G, RWKV_WIDTH), RWKV_LORA_G ** -0.5),
        'rwkv_k_k': 0.85 + nrm((L, RWKV_WIDTH), 0.02),
        'rwkv_k_a': 1.0 + nrm((L, RWKV_WIDTH), 0.02),
        'rwkv_r_k': nrm((L, RWKV_HEADS, RWKV_N), 0.1),
        'rwkv_gn_w': 1.0 + nrm((L, RWKV_WIDTH), 0.02),
        'rwkv_gn_b': nrm((L, RWKV_WIDTH), 0.02),
        'w_br': nrm((L, N_BRANCH, BRANCH_WIDTH, D), BRANCH_WIDTH ** -0.5),
        'w_out': nrm((L, D, D), D ** -0.5 * DEEPNORM_BETA),
        'ln_a_g': 1.0 + nrm((L, D), 0.02),
        'ln_a_b': nrm((L, D), 0.02),
        'ln_f_g': 1.0 + nrm((L, D), 0.02),
        'ln_f_b': nrm((L, D), 0.02),
        'peer_wq': nrm((L, D, PEER_HEADS * PEER_DK), D ** -0.5),
        'peer_keys': nrm((L, PEER_HEADS, 2, PEER_NKEYS, PEER_DK // 2), (PEER_DK // 2) ** -0.5),
        'peer_u': nrm((L, PEER_N_EXPERTS, D), D ** -0.5),
        'peer_v': nrm((L, PEER_N_EXPERTS, D), (PEER_HEADS * PEER_TOPK) ** -0.5 * DEEPNORM_BETA),
    }


def reference(x_prompt, x_sample, cache_na_k, cache_na_v, state_ret, state_rwkv, c, c_ctx,
              w_mod, b_mod, w_in, na_rpb, ret_decay_logit, ret_gn_w, ret_gn_b,
              rwkv_mu, rwkv_w0, rwkv_w_up, rwkv_a0, rwkv_a_up, rwkv_g_up, rwkv_k_k, rwkv_k_a,
              rwkv_r_k, rwkv_gn_w, rwkv_gn_b, w_br, w_out, ln_a_g, ln_a_b, ln_f_g, ln_f_b,
              peer_wq, peer_keys, peer_u, peer_v):
    weights = {
        'w_in': w_in, 'na_rpb': na_rpb, 'ret_decay_logit': ret_decay_logit,
        'ret_gn_w': ret_gn_w, 'ret_gn_b': ret_gn_b, 'rwkv_mu': rwkv_mu, 'rwkv_w0': rwkv_w0,
        'rwkv_w_up': rwkv_w_up, 'rwkv_a0': rwkv_a0, 'rwkv_a_up': rwkv_a_up, 'rwkv_g_up': rwkv_g_up,
        'rwkv_k_k': rwkv_k_k, 'rwkv_k_a': rwkv_k_a, 'rwkv_r_k': rwkv_r_k, 'rwkv_gn_w': rwkv_gn_w,
        'rwkv_gn_b': rwkv_gn_b, 'w_br': w_br, 'w_out': w_out, 'ln_a_g': ln_a_g, 'ln_a_b': ln_a_b,
        'ln_f_g': ln_f_g, 'ln_f_b': ln_f_b, 'peer_wq': peer_wq, 'peer_keys': peer_keys,
        'peer_u': peer_u, 'peer_v': peer_v,
    }
    y_prompt, y_sample = x_prompt, x_sample
    nk_list, nv_list, sr_list, sw_list = [], [], [], []
    for l in range(DEPTH):
        p = {name: arr[l] for name, arr in weights.items()}
        mod_ctx = (jax.nn.silu(c_ctx) @ w_mod[l] + b_mod[l])[None, None, :]
        y_prompt, (nk, nv, sr, sw) = trunk_layer(y_prompt, mod_ctx, p, None)
        nk_list.append(nk)
        nv_list.append(nv)
        sr_list.append(sr)
        sw_list.append(sw)
        mod_lat = (jax.nn.silu(c) @ w_mod[l] + b_mod[l])[:, None, :]
        y_sample, _ = trunk_layer(y_sample, mod_lat, p,
                                  (cache_na_k[:, l], cache_na_v[:, l], state_ret[:, l], state_rwkv[:, l]))
    dt = x_prompt.dtype
    new_na_k = jnp.stack(nk_list, axis=1).astype(dt)
    new_na_v = jnp.stack(nv_list, axis=1).astype(dt)
    new_state_ret = jnp.stack(sr_list, axis=1).astype(dt)
    new_state_rwkv = jnp.stack(sw_list, axis=1).astype(dt)
    return (y_prompt, y_sample, new_na_k, new_na_v, new_state_ret, new_state_rwkv)
```

```python
import functools
import math

import jax
import jax.numpy as jnp
import numpy as np
from jax import lax
from jax.experimental import pallas as pl
from jax.experimental.pallas import tpu as pltpu

F32 = jnp.float32
BF16 = jnp.bfloat16

D_MODEL = 1024
DEPTH = 2
GRID_W = 64
NA_HEADS = 8
NA_HEAD_DIM = 64
WIN_R = 8
WIN_C = 16
NEG_INF = -1e30
RET_HEADS = 4
RET_DK = 64
RET_DV = 128
RET_CHUNK = 64
RET_GN_EPS = 1e-5
ROPE_BASE = 10000.0
RWKV_HEADS = 8
RWKV_N = 64
RWKV_WIDTH = 512
RWKV_DECAY_SCALE = 0.606531
RWKV_GN_EPS = 64e-5
RWKV_CHUNK = 64
PEER_HEADS = 8
PEER_NKEYS = 128
PEER_TOPK = 16
LN_EPS = 1e-5
DEEPNORM_ALPHA = (2 * DEPTH) ** 0.25

OFF_GATE = 0
OFF_NA = 3072
OFF_RET_QK = 4608
OFF_RET_V = 5120
OFF_RW_MAIN = 6144
OFF_RW_LORA = 7680
P_IN = 8064

VMEM_LIMIT_BYTES = 56 * 1024 * 1024


def _cparams(*sem):
    return pltpu.CompilerParams(dimension_semantics=sem, vmem_limit_bytes=VMEM_LIMIT_BYTES)


def _ln(x, eps=LN_EPS):
    mu = jnp.mean(x, axis=-1, keepdims=True)
    xc = x - mu
    var = jnp.mean(xc * xc, axis=-1, keepdims=True)
    return xc * lax.rsqrt(var + eps)


def _sigmoid(x):
    return 1.0 / (1.0 + jnp.exp(-x))


def _dot(a, b):
    return jnp.dot(a.astype(BF16), b.astype(BF16), preferred_element_type=F32)


def _dot_nt(a, b):
    return lax.dot_general(a.astype(BF16), b.astype(BF16), (((1,), (1,)), ((), ())),
                           preferred_element_type=F32)


def _dot_tn(a, b):
    return lax.dot_general(a.astype(BF16), b.astype(BF16), (((0,), (0,)), ((), ())),
                           preferred_element_type=F32)


def _split3(x):
    hi = x.astype(BF16)
    r1 = x - hi.astype(F32)
    mid = r1.astype(BF16)
    lo = (r1 - mid.astype(F32)).astype(BF16)
    return hi, mid, lo


def _dot_exact_lhs(sel, x):
    hi, mid, lo = _split3(x)
    s = sel.astype(BF16)
    out = jnp.dot(s, lo, preferred_element_type=F32)
    out = out + jnp.dot(s, mid, preferred_element_type=F32)
    return out + jnp.dot(s, hi, preferred_element_type=F32)


def _dot_exact_rhs(x, sel):
    hi, mid, lo = _split3(x)
    s = sel.astype(BF16)
    out = jnp.dot(lo, s, preferred_element_type=F32)
    out = out + jnp.dot(mid, s, preferred_element_type=F32)
    return out + jnp.dot(hi, s, preferred_element_type=F32)


def _mod_kernel(c_ref, w_ref, b_ref, o_ref):
    c = c_ref[...]
    s = c * _sigmoid(c)
    o_ref[0] = _dot(s, w_ref[0]) + b_ref[0]


def _modulation(c8, w_mod, b_mod):
    tn = 1536
    n = w_mod.shape[-1]
    return pl.pallas_call(
        _mod_kernel,
        grid=(DEPTH, n // tn),
        in_specs=[pl.BlockSpec((8, D_MODEL), lambda l, j: (0, 0)),
                  pl.BlockSpec((1, D_MODEL, tn), lambda l, j: (l, 0, j)),
                  pl.BlockSpec((1, 1, tn), lambda l, j: (l, 0, j))],
        out_specs=pl.BlockSpec((1, 8, tn), lambda l, j: (l, 0, j)),
        out_shape=jax.ShapeDtypeStruct((DEPTH, 8, n), F32),
        compiler_params=_cparams("parallel", "parallel"),
        name="modulation",
    )(c8, w_mod, b_mod.reshape(DEPTH, 1, n))


def _lnmod_matmul_kernel(x_ref, sh_ref, sc_ref, w_ref, o_ref, h_ref):
    @pl.when(pl.program_id(1) == 0)
    def _():
        h = _ln(x_ref[...]) * (1.0 + sc_ref[0]) + sh_ref[0]
        h_ref[...] = h.astype(BF16)

    o_ref[...] = jnp.dot(h_ref[...], w_ref[...], preferred_element_type=F32)


def _lnmod_matmul(x, mod3, w, row_of_tile, tm, tn, sh_blk, sc_blk):
    m = x.shape[0]
    n = w.shape[1]
    return pl.pallas_call(
        _lnmod_matmul_kernel,
        grid=(m // tm, n // tn),
        in_specs=[pl.BlockSpec((tm, D_MODEL), lambda i, j: (i, 0)),
                  pl.BlockSpec((1, 1, D_MODEL), lambda i, j: (row_of_tile(i, tm), 0, sh_blk)),
                  pl.BlockSpec((1, 1, D_MODEL), lambda i, j: (row_of_tile(i, tm), 0, sc_blk)),
                  pl.BlockSpec((D_MODEL, tn), lambda i, j: (0, j))],
        out_specs=pl.BlockSpec((tm, tn), lambda i, j: (i, j)),
        out_shape=jax.ShapeDtypeStruct((m, n), F32),
        scratch_shapes=[pltpu.VMEM((tm, D_MODEL), BF16)],
        compiler_params=_cparams("parallel", "arbitrary"),
        name="adaln_in_proj",
    )(x, mod3, mod3, w)


def _ctx_attn_kernel(q_ref, k_ref, v_ref, o_ref):
    scale = NA_HEAD_DIM ** -0.5
    for hh in range(2):
        sl = slice(hh * NA_HEAD_DIM, (hh + 1) * NA_HEAD_DIM)
        s = _dot_nt(q_ref[:, sl], k_ref[:, sl]) * scale
        m = jnp.max(s, axis=-1, keepdims=True)
        p = jnp.exp(s - m)
        l = jnp.sum(p, axis=-1, keepdims=True)
        o_ref[:, sl] = _dot(p, v_ref[:, sl]) / l


def _ctx_attention(z, n_seq, seq_len):
    cb = OFF_NA // 128
    return pl.pallas_call(
        _ctx_attn_kernel,
        grid=(n_seq, NA_HEADS // 2),
        in_specs=[pl.BlockSpec((seq_len, 128), lambda b, h: (b, cb + h)),
                  pl.BlockSpec((seq_len, 128), lambda b, h: (b, cb + 4 + h)),
                  pl.BlockSpec((seq_len, 128), lambda b, h: (b, cb + 8 + h))],
        out_specs=pl.BlockSpec((seq_len, 128), lambda b, h: (b, h)),
        out_shape=jax.ShapeDtypeStruct((n_seq * seq_len, NA_HEADS * NA_HEAD_DIM), F32),
        compiler_params=_cparams("parallel", "parallel"),
        name="ctx_attention",
    )(z, z, z)


def _na_bias_table(rpb, rows):
    kr = min(WIN_R, rows)
    rep = np.array([0, 1, 2, 3, 4, rows - 3, rows - 2, rows - 1])
    start = np.clip(rep - kr // 2, 0, rows - kr)
    dr = start[:, None] - rep[:, None] + WIN_R - 1 + np.arange(kr)[None, :]
    qcol = np.arange(GRID_W)
    kcol = np.arange(GRID_W)
    q_start = np.clip(qcol - WIN_C // 2, 0, GRID_W - WIN_C)
    rel = kcol[None, :] - q_start[:, None]
    valid = (rel >= 0) & (rel < WIN_C)
    dc = np.clip(kcol[None, :] - qcol[:, None], -(WIN_C - 1), WIN_C - 1) + WIN_C - 1
    b = rpb.astype(F32)[:, dr[:, :, None, None], dc[None, None, :, :]]
    b = jnp.where(valid[None, None, None], b, NEG_INF)
    return b.transpose(0, 1, 3, 2, 4).reshape(rpb.shape[0], 8, GRID_W, kr * GRID_W)


def _na_latent_kernel(q_ref, k_ref, v_ref, ck_ref, cv_ref, b_ref, o_ref, *, rows):
    scale = NA_HEAD_DIM ** -0.5
    kr = min(WIN_R, rows)

    def body(r, carry):
        start = jnp.clip(r - kr // 2, 0, rows - kr)
        typ = jnp.where(r < 4, r, jnp.where(r > rows - 4, r - (rows - 8), 4))
        q0 = pl.multiple_of(r * GRID_W, GRID_W)
        k0 = pl.multiple_of(start * GRID_W, GRID_W)
        for hh in range(2):
            sl = slice(hh * NA_HEAD_DIM, (hh + 1) * NA_HEAD_DIM)
            q = q_ref[pl.ds(q0, GRID_W), sl]
            kw = k_ref[pl.ds(k0, kr * GRID_W), sl]
            vw = v_ref[pl.ds(k0, kr * GRID_W), sl]
            s_w = _dot_nt(q, kw) * scale + b_ref[hh, typ]
            s_c = _dot_nt(q, ck_ref[0, hh]) * scale
            m = jnp.maximum(jnp.max(s_w, axis=-1, keepdims=True), jnp.max(s_c, axis=-1, keepdims=True))
            p_w = jnp.exp(s_w - m)
            p_c = jnp.exp(s_c - m)
            l = jnp.sum(p_w, axis=-1, keepdims=True) + jnp.sum(p_c, axis=-1, keepdims=True)
            o_ref[pl.ds(q0, GRID_W), sl] = (_dot(p_w, vw) + _dot(p_c, cv_ref[0, hh])) / l
        return carry

    lax.fori_loop(0, rows, body, 0)


def _na_latent(z, ctx_k, ctx_v, bias, n_seq, seq_len, row_blk0):
    cb = OFF_NA // 128
    rows = seq_len // GRID_W
    past = ctx_k.shape[2]
    return pl.pallas_call(
        functools.partial(_na_latent_kernel, rows=rows),
        grid=(n_seq, NA_HEADS // 2),
        in_specs=[pl.BlockSpec((seq_len, 128), lambda b, h: (row_blk0 + b, cb + h)),
                  pl.BlockSpec((seq_len, 128), lambda b, h: (row_blk0 + b, cb + 4 + h)),
                  pl.BlockSpec((seq_len, 128), lambda b, h: (row_blk0 + b, cb + 8 + h)),
                  pl.BlockSpec((1, 2, past, NA_HEAD_DIM), lambda b, h: (b, h, 0, 0)),
                  pl.BlockSpec((1, 2, past, NA_HEAD_DIM), lambda b, h: (b, h, 0, 0)),
                  pl.BlockSpec((2, 8, GRID_W, bias.shape[-1]), lambda b, h: (h, 0, 0, 0))],
        out_specs=pl.BlockSpec((seq_len, 128), lambda b, h: (b, h)),
        out_shape=jax.ShapeDtypeStruct((n_seq * seq_len, NA_HEADS * NA_HEAD_DIM), F32),
        compiler_params=_cparams("parallel", "parallel"),
        name="na_latent",
    )(z, z, z, ctx_k, ctx_v, bias)


def _ret_decay_tables(decay_logit):
    c = RET_CHUNK
    lg = jnp.log(jax.nn.sigmoid(decay_logit.astype(F32)))[:, :, None, None]
    i = jnp.arange(c, dtype=F32)
    diff = i[:, None] - i[None, :]
    m_f = jnp.where(diff >= 0, jnp.exp(jnp.maximum(diff, 0.0) * lg[0]), 0.0)
    m_b = jnp.where(diff <= 0, jnp.exp(jnp.maximum(-diff, 0.0) * lg[1]), 0.0)
    mask = jnp.stack([m_f, m_b])
    ones = jnp.ones((1, 1, 1, RET_DK), F32)
    q_f = jnp.exp((i + 1.0)[None, :, None] * lg[0]) * ones[0]
    q_b = jnp.exp((c - i)[None, :, None] * lg[1]) * ones[0]
    k_f = jnp.exp((c - 1.0 - i)[None, :, None] * lg[0]) * ones[0]
    k_b = jnp.exp(i[None, :, None] * lg[1]) * ones[0]
    c_dec = jnp.exp(c * lg) * jnp.ones((1, 1, RET_DK, RET_DV), F32)
    return mask, jnp.stack([q_f, q_b]), jnp.stack([k_f, k_b]), c_dec


def _rope_tables(seq_len):
    quarter = RET_DK // 4
    pos = np.arange(seq_len)
    freqs = 1.0 / (ROPE_BASE ** (jnp.arange(quarter, dtype=F32) / quarter))
    a_row = jnp.asarray(pos // GRID_W, F32)[:, None] * freqs[None, :]
    a_col = jnp.asarray(pos % GRID_W, F32)[:, None] * freqs[None, :]
    cos = jnp.concatenate([jnp.cos(a_row)] * 2 + [jnp.cos(a_col)] * 2, axis=-1)
    sin = jnp.concatenate([-jnp.sin(a_row), jnp.sin(a_row), -jnp.sin(a_col), jnp.sin(a_col)], axis=-1)
    return jnp.tile(cos, (1, RET_HEADS)), jnp.tile(sin, (1, RET_HEADS))


def _retention_kernel(*refs, seq_len, rope):
    if rope:
        (q_ref, k_ref, v_ref, g_ref, s0_ref, mask_ref, qd_ref, kd_ref, cd_ref, gnw_ref, gnb_ref,
         cos_ref, sin_ref, o_ref, sfin_ref, qs_ref, ks_ref, acc_ref) = refs
    else:
        (q_ref, k_ref, v_ref, g_ref, s0_ref, mask_ref, qd_ref, kd_ref, cd_ref, gnw_ref, gnb_ref,
         o_ref, sfin_ref, qs_ref, ks_ref, acc_ref) = refs
    c = RET_CHUNK
    n = seq_len // c
    quarter = RET_DK // 4
    q = q_ref[...]
    k = k_ref[...] * (RET_DK ** -0.5)
    if rope:
        lane = lax.broadcasted_iota(jnp.int32, q.shape, 1)
        first = (lane % (2 * quarter)) < quarter
        width = q.shape[1]

        def rot(x):
            swapped = jnp.where(first, pltpu.roll(x, width - quarter, 1), pltpu.roll(x, quarter, 1))
            return x * cos_ref[...] + swapped * sin_ref[...]

        q = rot(q)
        k = rot(k)
    qs_ref[...] = q
    ks_ref[...] = k

    for h in range(RET_HEADS):
        ksl = slice(h * RET_DK, (h + 1) * RET_DK)
        vsl = slice(h * RET_DV, (h + 1) * RET_DV)
        for d in range(2):
            mask = mask_ref[d, h]
            qd = qd_ref[d, h]
            kd = kd_ref[d, h]
            cd = cd_ref[d, h]

            def step(ci, s, ksl=ksl, vsl=vsl, d=d, mask=mask, qd=qd, kd=kd, cd=cd):
                cc = ci if d == 0 else n - 1 - ci
                r0 = pl.multiple_of(cc * c, c)
                qc = qs_ref[pl.ds(r0, c), ksl]
                kc = ks_ref[pl.ds(r0, c), ksl]
                vc = v_ref[pl.ds(r0, c), vsl]
                att = _dot_nt(qc, kc) * mask
                o = _dot(att, vc) + _dot(qc * qd, s)
                if d == 0:
                    acc_ref[pl.ds(r0, c), vsl] = o
                else:
                    acc_ref[pl.ds(r0, c), vsl] += o
                return s * cd + _dot_tn(kc * kd, vc)

            s_fin = lax.fori_loop(0, n, step, s0_ref[0, d, h])
            sfin_ref[0, d, h] = s_fin

    for h in range(RET_HEADS):
        vsl = slice(h * RET_DV, (h + 1) * RET_DV)
        y = _ln(acc_ref[:, vsl], RET_GN_EPS) * gnw_ref[:, vsl] + gnb_ref[:, vsl]
        g = g_ref[:, vsl]
        o_ref[:, vsl] = y * (g * _sigmoid(g))


def _retention(z, s0, tables, gn_w, gn_b, n_seq, seq_len, row_blk0, rope_tabs):
    mask, qd, kd, cd = tables
    rope = rope_tabs is not None
    qk_w = RET_HEADS * RET_DK
    v_w = RET_HEADS * RET_DV
    full = lambda a: pl.BlockSpec(a.shape, lambda b: (0,) * a.ndim)
    in_specs = [pl.BlockSpec((seq_len, qk_w), lambda b: (row_blk0 + b, OFF_RET_QK // qk_w)),
                pl.BlockSpec((seq_len, qk_w), lambda b: (row_blk0 + b, OFF_RET_QK // qk_w + 1)),
                pl.BlockSpec((seq_len, v_w), lambda b: (row_blk0 + b, OFF_RET_V // v_w)),
                pl.BlockSpec((seq_len, v_w), lambda b: (row_blk0 + b, OFF_RET_V // v_w + 1)),
                pl.BlockSpec((1, 2, RET_HEADS, RET_DK, RET_DV), lambda b: (b, 0, 0, 0, 0)),
                full(mask), full(qd), full(kd), full(cd), full(gn_w), full(gn_b)]
    args = [z, z, z, z, s0, mask, qd, kd, cd, gn_w, gn_b]
    if rope:
        in_specs += [full(rope_tabs[0]), full(rope_tabs[1])]
        args += list(rope_tabs)
    return pl.pallas_call(
        functools.partial(_retention_kernel, seq_len=seq_len, rope=rope),
        grid=(n_seq,),
        in_specs=in_specs,
        out_specs=[pl.BlockSpec((seq_len, v_w), lambda b: (b, 0)),
                   pl.BlockSpec((1, 2, RET_HEADS, RET_DK, RET_DV), lambda b: (b, 0, 0, 0, 0))],
        out_shape=[jax.ShapeDtypeStruct((n_seq * seq_len, v_w), F32),
                   jax.ShapeDtypeStruct((n_seq, 2, RET_HEADS, RET_DK, RET_DV), F32)],
        scratch_shapes=[pltpu.VMEM((seq_len, qk_w), F32), pltpu.VMEM((seq_len, qk_w), F32),
                        pltpu.VMEM((seq_len, v_w), F32)],
        compiler_params=_cparams("parallel"),
        name="retention_rope" if rope else "retention",
    )(*args)


def _rwkv_prep_kernel(zm_ref, zl_ref, pm_ref, nm_ref, pl_ref, nl_ref, mum_ref, mul_ref,
                      w0_ref, wup_ref, a0_ref, aup_ref, gup_ref, kk_ref, ka_ref, rk_ref, ones_ref,
                      r_out, kap_out, v_out, lw_out, kd_out, beta_out, g_out, bonus_out,
                      *, tiles_per_seq_of_tile):
    i = pl.program_id(0)
    pos, per = tiles_per_seq_of_tile(i)
    has_prev = (pos != 0).astype(F32)
    has_next = (pos != per - 1).astype(F32)
    tt = zm_ref.shape[0]

    def shift(z_ref, p_ref, n_ref, mu_ref):
        z = z_ref[...]
        row = lax.broadcasted_iota(jnp.int32, z.shape, 0)
        prev = jnp.where(row == 0, p_ref[7:8, :] * has_prev, pltpu.roll(z, 1, 0))
        nxt = jnp.where(row == tt - 1, n_ref[0:1, :] * has_next, pltpu.roll(z, tt - 1, 0))
        return z + mu_ref[...] * (0.5 * (prev + nxt) - z)

    zm = shift(zm_ref, pm_ref, nm_ref, mum_ref)
    zl = shift(zl_ref, pl_ref, nl_ref, mul_ref)
    w = RWKV_WIDTH
    r_c, k_c, v_c = zm[:, 0:w], zm[:, w:2 * w], zm[:, 2 * w:3 * w]
    ones = ones_ref[...]
    g_out[...] = _dot(_sigmoid(zl[:, 256:384]), gup_ref[...])
    kk = k_c * kk_ref[...]
    ss = _dot_exact_rhs(kk * kk, ones)
    kap = kk * lax.rsqrt(jnp.maximum(ss, 1e-24))
    bonus_out[...] = _dot_exact_rhs(r_c * k_c * rk_ref[...], ones) * v_c
    for h in range(RWKV_HEADS):
        sl = slice(h * RWKV_N, (h + 1) * RWKV_N)
        r_out[h] = r_c[:, sl]
        kap_out[h] = kap[:, sl]
        v_out[h] = v_c[:, sl]
    for d in range(2):
        wl = zl[:, d * 64:(d + 1) * 64]
        al = zl[:, 128 + d * 64:128 + (d + 1) * 64]
        lw = -RWKV_DECAY_SCALE * _sigmoid(w0_ref[d] + _dot(jnp.tanh(wl), wup_ref[d]))
        a = _sigmoid(a0_ref[d] + _dot(al, aup_ref[d]))
        k_d = k_c * (1.0 + (a - 1.0) * ka_ref[...])
        beta = kap * a
        for h in range(RWKV_HEADS):
            sl = slice(h * RWKV_N, (h + 1) * RWKV_N)
            lw_out[d, h] = lw[:, sl]
            kd_out[d, h] = k_d[:, sl]
            beta_out[d, h] = beta[:, sl]


def _rwkv_prep(z, p, tiles_per_seq_of_tile, tt):
    m = z.shape[0]
    nt = m // tt
    w = RWKV_WIDTH
    mb, lb = OFF_RW_MAIN // (3 * w), OFF_RW_LORA // 384
    hm = tt // 8
    last8 = m // 8 - 1
    prev_idx = lambda i: jnp.maximum(i * hm - 1, 0)
    next_idx = lambda i: jnp.minimum((i + 1) * hm, last8)
    full = lambda a: pl.BlockSpec(a.shape, lambda i: (0,) * a.ndim)
    heads = jnp.arange(w) // RWKV_N
    ones = (heads[:, None] == heads[None, :]).astype(BF16)
    params = [p['mu_main'], p['mu_lora'], p['rwkv_w0'], p['rwkv_w_up'], p['rwkv_a0'], p['rwkv_a_up'],
              p['rwkv_g_up'], p['rwkv_k_k'], p['rwkv_k_a'], p['rwkv_r_k'], ones]
    hd = lambda: jax.ShapeDtypeStruct((RWKV_HEADS, m, RWKV_N), F32)
    dhd = lambda: jax.ShapeDtypeStruct((2, RWKV_HEADS, m, RWKV_N), F32)
    tok = lambda: jax.ShapeDtypeStruct((m, w), F32)
    hd_spec = pl.BlockSpec((RWKV_HEADS, tt, RWKV_N), lambda i: (0, i, 0))
    dhd_spec = pl.BlockSpec((2, RWKV_HEADS, tt, RWKV_N), lambda i: (0, 0, i, 0))
    tok_spec = pl.BlockSpec((tt, w), lambda i: (i, 0))
    return pl.pallas_call(
        functools.partial(_rwkv_prep_kernel, tiles_per_seq_of_tile=tiles_per_seq_of_tile),
        grid=(nt,),
        in_specs=[pl.BlockSpec((tt, 3 * w), lambda i: (i, mb)),
                  pl.BlockSpec((tt, 384), lambda i: (i, lb)),
                  pl.BlockSpec((8, 3 * w), lambda i: (prev_idx(i), mb)),
                  pl.BlockSpec((8, 3 * w), lambda i: (next_idx(i), mb)),
                  pl.BlockSpec((8, 384), lambda i: (prev_idx(i), lb)),
                  pl.BlockSpec((8, 384), lambda i: (next_idx(i), lb))] + [full(a) for a in params],
        out_specs=[hd_spec, hd_spec, hd_spec, dhd_spec, dhd_spec, dhd_spec, tok_spec, tok_spec],
        out_shape=[hd(), hd(), hd(), dhd(), dhd(), dhd(), tok(), tok()],
        compiler_params=_cparams("parallel"),
        name="rwkv_prep",
    )(z, z, z, z, z, z, *params)


def _dot3(a, b, dims):
    a_hi = a.astype(BF16)
    b_hi = b.astype(BF16)
    a_lo = (a - a_hi.astype(F32)).astype(BF16)
    b_lo = (b - b_hi.astype(F32)).astype(BF16)
    dn = (dims, ((), ()))
    out = lax.dot_general(a_lo, b_hi, dn, preferred_element_type=F32)
    out = out + lax.dot_general(a_hi, b_lo, dn, preferred_element_type=F32)
    return out + lax.dot_general(a_hi, b_hi, dn, preferred_element_type=F32)


_NN = ((1,), (0,))
_NT = ((1,), (1,))
_TN = ((0,), (0,))


def _rwkv_scan_kernel(r_ref, kap_ref, v_ref, lw_ref, kd_ref, beta_ref, h0_ref, o_ref, hfin_ref, *, seq_len):
    c = RWKV_CHUNK
    n = seq_len // c
    d = pl.program_id(0)
    sgn = 1 - 2 * d
    ti = lax.broadcasted_iota(jnp.int32, (c, c), 0) * sgn
    tj = lax.broadcasted_iota(jnp.int32, (c, c), 1) * sgn
    strict = tj < ti
    incl = tj <= ti
    eye = ti == tj
    incl_f = incl.astype(F32)
    eye_f = eye.astype(F32)

    def step(ci, hstate):
        cc = ci + d * (n - 1 - 2 * ci)
        r0 = pl.multiple_of(cc * c, c)
        rows = pl.ds(r0, c)
        r = r_ref[0, rows, :]
        kap = kap_ref[0, rows, :]
        v = v_ref[0, rows, :]
        lw = lw_ref[0, 0, rows, :]
        kd = kd_ref[0, 0, rows, :]
        beta = beta_ref[0, 0, rows, :]
        cs = _dot_exact_lhs(incl_f, lw)
        tot = jnp.sum(lw, axis=0, keepdims=True)
        e_tot = jnp.exp(tot)
        kap_t = kap * jnp.exp(cs - lw)
        r_t = r * jnp.exp(cs)
        e_ncs = jnp.exp(-cs)
        k_t = kd * e_ncs
        b_t = beta * e_ncs
        e_end = jnp.exp(tot - cs)
        k_end = kd * e_end
        b_end = beta * e_end
        l_b = jnp.where(strict, _dot3(kap_t, b_t, _NT), 0.0)
        l_k = jnp.where(strict, _dot3(kap_t, k_t, _NT), 0.0)
        m_b = jnp.where(incl, _dot3(r_t, b_t, _NT), 0.0)
        m_k = jnp.where(incl, _dot3(r_t, k_t, _NT), 0.0)
        x = -l_b
        t_inv = eye_f + x
        pw = x
        for _ in range(5):
            pw = _dot3(pw, pw, _NN)
            t_inv = t_inv + _dot3(t_inv, pw, _NN)
        w1 = _dot3(t_inv, kap_t, _NN)
        w2 = _dot3(t_inv, _dot3(l_k, v, _NN), _NN)
        rr = r_t - _dot3(m_b, w1, _NN)
        o0 = _dot3(m_k, v, _NN) - _dot3(m_b, w2, _NN)
        pc = eye_f * e_tot - _dot3(b_end, w1, _TN)
        qc = _dot3(k_end, v, _TN) - _dot3(b_end, w2, _TN)
        o_ref[0, 0, rows, :] = _dot3(rr, hstate, _NN) + o0
        return _dot3(pc, hstate, _NN) + qc

    hfin_ref[0, 0, 0] = lax.fori_loop(0, n, step, h0_ref[0, 0, 0])


def _rwkv_scan(r, kap, v, lw, kd, beta, h0, n_seq, seq_len, row_blk0):
    hd_spec = pl.BlockSpec((1, seq_len, RWKV_N), lambda d, h, b: (h, row_blk0 + b, 0))
    dhd_spec = pl.BlockSpec((1, 1, seq_len, RWKV_N), lambda d, h, b: (d, h, row_blk0 + b, 0))
    st_spec = pl.BlockSpec((1, 1, 1, RWKV_N, RWKV_N), lambda d, h, b: (b, d, h, 0, 0))
    return pl.pallas_call(
        functools.partial(_rwkv_scan_kernel, seq_len=seq_len),
        grid=(2, RWKV_HEADS, n_seq),
        in_specs=[hd_spec, hd_spec, hd_spec, dhd_spec, dhd_spec, dhd_spec, st_spec],
        out_specs=[pl.BlockSpec((1, 1, seq_len, RWKV_N), lambda d, h, b: (d, h, b, 0)), st_spec],
        out_shape=[jax.ShapeDtypeStruct((2, RWKV_HEADS, n_seq * seq_len, RWKV_N), F32),
                   jax.ShapeDtypeStruct((n_seq, 2, RWKV_HEADS, RWKV_N, RWKV_N), F32)],
        compiler_params=_cparams("parallel", "parallel", "parallel"),
        name="rwkv_scan",
    )(r, kap, v, lw, kd, beta, h0)


def _merge_kernel(oscan_ref, bonus_ref, grw_ref, ona_ref, ort_ref, ga_ref, gb_ref, gc_ref, x_ref, gt_ref,
                  gnw_ref, gnb_ref, wbr_ref, wout_ref, lng_ref, lnb_ref, o_ref, orw_ref):
    for h in range(RWKV_HEADS):
        sl = slice(h * RWKV_N, (h + 1) * RWKV_N)
        y = _ln(oscan_ref[0, h] + oscan_ref[1, h], RWKV_GN_EPS) * gnw_ref[h] + gnb_ref[h]
        orw_ref[:, sl] = (y + bonus_ref[:, sl]) * grw_ref[:, sl]
    merged = (_sigmoid(ga_ref[...]) * _dot(ona_ref[...], wbr_ref[0])
              + _sigmoid(gb_ref[...]) * _dot(ort_ref[...], wbr_ref[1])
              + _sigmoid(gc_ref[...]) * _dot(orw_ref[...], wbr_ref[2]))
    t = _dot(merged, wout_ref[...])
    o_ref[...] = _ln(DEEPNORM_ALPHA * x_ref[...] + gt_ref[0] * t) * lng_ref[...] + lnb_ref[...]


def _merge(oscan, bonus, g_rw, o_na, o_rt, z, x, mod3, p, row_of_tile, tm):
    m = x.shape[0]
    bw = RWKV_WIDTH
    full = lambda a: pl.BlockSpec(a.shape, lambda i: (0,) * a.ndim)
    tok = pl.BlockSpec((tm, bw), lambda i: (i, 0))
    params = [p['rwkv_gn_w'], p['rwkv_gn_b'], p['w_br'], p['w_out'], p['ln_a_g'], p['ln_a_b']]
    return pl.pallas_call(
        _merge_kernel,
        grid=(m // tm,),
        in_specs=[pl.BlockSpec((2, RWKV_HEADS, tm, RWKV_N), lambda i: (0, 0, i, 0)),
                  tok, tok, tok, tok,
                  pl.BlockSpec((tm, D_MODEL), lambda i: (i, 0)),
                  pl.BlockSpec((tm, D_MODEL), lambda i: (i, 1)),
                  pl.BlockSpec((tm, D_MODEL), lambda i: (i, 2)),
                  pl.BlockSpec((tm, D_MODEL), lambda i: (i, 0)),
                  pl.BlockSpec((1, 1, D_MODEL), lambda i: (row_of_tile(i, tm), 0, 2))]
                 + [full(a) for a in params],
        out_specs=pl.BlockSpec((tm, D_MODEL), lambda i: (i, 0)),
        out_shape=jax.ShapeDtypeStruct((m, D_MODEL), F32),
        scratch_shapes=[pltpu.VMEM((tm, bw), F32)],
        compiler_params=_cparams("parallel"),
        name="merge_out_proj",
    )(oscan, bonus, g_rw, o_na, o_rt, z, z, z, x, mod3, *params)


def _top_values(s, k):
    out = []
    for _ in range(k):
        m = jnp.max(s, axis=0, keepdims=True)
        out.append(m)
        s = jnp.where(s == m, -jnp.inf, s)
    return out


def _peer_gate_kernel(x_ref, sh_ref, sc_ref, wq_ref, keys_ref, ht_ref, s0_ref, s1_ref, e0_ref, e1_ref, tau_ref):
    h = _ln(x_ref[...]) * (1.0 + sc_ref[0]) + sh_ref[0]
    ht = h.T.astype(BF16)
    ht_ref[...] = ht
    qt = jnp.dot(wq_ref[...], ht, preferred_element_type=F32)
    half = keys_ref.shape[-1]
    for hd in range(PEER_HEADS):
        scores, tops = [], []
        for part in range(2):
            row0 = (hd * 2 + part) * half
            s = _dot(keys_ref[hd, part], qt[row0:row0 + half, :])
            scores.append(s)
            tops.append(_top_values(s, PEER_TOPK))
        cand = jnp.concatenate([tops[0][a] + jnp.concatenate(tops[1], axis=0) for a in range(PEER_TOPK)],
                               axis=0)
        best = _top_values(cand, PEER_TOPK)
        zsum = jnp.exp(best[0] - best[0])
        for t in range(1, PEER_TOPK):
            zsum = zsum + jnp.exp(best[t] - best[0])
        s0_ref[hd] = scores[0]
        s1_ref[hd] = scores[1]
        e0_ref[hd] = jnp.exp(scores[0] - tops[0][0])
        e1_ref[hd] = jnp.exp(scores[1] - tops[1][0]) / zsum
        tau_ref[hd] = best[PEER_TOPK - 1]


def _peer_gate(x, mod3, wq_t, keys, row_of_tile, tn):
    m = x.shape[0]
    full = lambda a: pl.BlockSpec(a.shape, lambda i: (0,) * a.ndim)
    sk = pl.BlockSpec((PEER_HEADS, PEER_NKEYS, tn), lambda i: (0, 0, i))
    sk_shape = jax.ShapeDtypeStruct((PEER_HEADS, PEER_NKEYS, m), F32)
    return pl.pallas_call(
        _peer_gate_kernel,
        grid=(m // tn,),
        in_specs=[pl.BlockSpec((tn, D_MODEL), lambda i: (i, 0)),
                  pl.BlockSpec((1, 1, D_MODEL), lambda i: (row_of_tile(i, tn), 0, 3)),
                  pl.BlockSpec((1, 1, D_MODEL), lambda i: (row_of_tile(i, tn), 0, 4)),
                  full(wq_t), full(keys)],
        out_specs=[pl.BlockSpec((D_MODEL, tn), lambda i: (0, i)), sk, sk, sk, sk,
                   pl.BlockSpec((PEER_HEADS, 1, tn), lambda i: (0, 0, i))],
        out_shape=[jax.ShapeDtypeStruct((D_MODEL, m), BF16), sk_shape, sk_shape, sk_shape, sk_shape,
                   jax.ShapeDtypeStruct((PEER_HEADS, 1, m), F32)],
        compiler_params=_cparams("parallel"),
        name="peer_gate",
    )(x, mod3, mod3, wq_t, keys)


def _gelu_exact(x):
    return 0.5 * x * (1.0 + lax.erf(x * (2.0 ** -0.5)))


def _peer_expert_kernel(ht_ref, s0_ref, s1_ref, e0_ref, e1_ref, tau_ref, u_ref, vt_ref, x_ref, gt_ref,
                        lng_ref, lnb_ref, o_ref, acc_ref, act_ref, *, rows_per_tile):
    e = pl.program_id(1)

    @pl.when(e == 0)
    def _():
        acc_ref[...] = jnp.zeros_like(acc_ref)

    for ii in range(rows_per_tile):
        i = e * rows_per_tile + ii
        rs = slice(ii * PEER_NKEYS, (ii + 1) * PEER_NKEYS)
        st = jnp.dot(u_ref[rs, :], ht_ref[...], preferred_element_type=F32)
        gate = None
        for hd in range(PEER_HEADS):
            s0 = s0_ref[hd, pl.ds(i, 1), :]
            e0 = e0_ref[hd, pl.ds(i, 1), :]
            term = jnp.where(s0 + s1_ref[hd] >= tau_ref[hd], e1_ref[hd] * e0, 0.0)
            gate = term if gate is None else gate + term
        act_ref[rs, :] = (_gelu_exact(st) * gate).astype(BF16)
    acc_ref[...] += jnp.dot(vt_ref[...], act_ref[...], preferred_element_type=F32)

    @pl.when(e == pl.num_programs(1) - 1)
    def _():
        f = acc_ref[...].T
        o_ref[...] = _ln(DEEPNORM_ALPHA * x_ref[...] + gt_ref[0] * f) * lng_ref[...] + lnb_ref[...]


def _peer_expert(ht, s0, s1, e0, e1, tau, u, v_t, x, mod3, ln_g, ln_b, row_of_tile, tn, eb):
    m = x.shape[0]
    n_exp = u.shape[0]
    sk = pl.BlockSpec((PEER_HEADS, PEER_NKEYS, tn), lambda i, e: (0, 0, i))
    full = lambda a: pl.BlockSpec(a.shape, lambda i, e: (0,) * a.ndim)
    return pl.pallas_call(
        functools.partial(_peer_expert_kernel, rows_per_tile=eb // PEER_NKEYS),
        grid=(m // tn, n_exp // eb),
        in_specs=[pl.BlockSpec((D_MODEL, tn), lambda i, e: (0, i)), sk, sk, sk, sk,
                  pl.BlockSpec((PEER_HEADS, 1, tn), lambda i, e: (0, 0, i)),
                  pl.BlockSpec((eb, D_MODEL), lambda i, e: (e, 0)),
                  pl.BlockSpec((D_MODEL, eb), lambda i, e: (0, e)),
                  pl.BlockSpec((tn, D_MODEL), lambda i, e: (i, 0)),
                  pl.BlockSpec((1, 1, D_MODEL), lambda i, e: (row_of_tile(i, tn), 0, 5)),
                  full(ln_g), full(ln_b)],
        out_specs=pl.BlockSpec((tn, D_MODEL), lambda i, e: (i, 0)),
        out_shape=jax.ShapeDtypeStruct((m, D_MODEL), F32),
        scratch_shapes=[pltpu.VMEM((D_MODEL, tn), F32), pltpu.VMEM((eb, tn), BF16)],
        compiler_params=_cparams("parallel", "arbitrary"),
        name="peer_experts",
    )(ht, s0, s1, e0, e1, tau, u, v_t, x, mod3, ln_g, ln_b)


def _permute_in_proj(w_in):
    na, qk, rv = 3 * 512, 2 * 256, 2 * 512
    o_na, o_qk, o_rv, o_rw, o_g = 0, na, na + qk, na + qk + rv, na + qk + rv + 1920
    return jnp.concatenate([w_in[:, o_g:], w_in[:, o_na:o_qk], w_in[:, o_qk:o_rv], w_in[:, o_rv:o_rw],
                            w_in[:, o_rw:o_g]], axis=1)


def kernel(x_prompt, x_sample, cache_na_k, cache_na_v, state_ret, state_rwkv, c, c_ctx, w_mod, b_mod, w_in, na_rpb, ret_decay_logit, ret_gn_w, ret_gn_b, rwkv_mu, rwkv_w0, rwkv_w_up, rwkv_a0, rwkv_a_up, rwkv_g_up, rwkv_k_k, rwkv_k_a, rwkv_r_k, rwkv_gn_w, rwkv_gn_b, w_br, w_out, ln_a_g, ln_a_b, ln_f_g, ln_f_b, peer_wq, peer_keys, peer_u, peer_v):
    n_ctx, ctx_len, d = x_prompt.shape
    n_lat, lat_len, _ = x_sample.shape
    m_ctx = n_ctx * ctx_len
    assert m_ctx % lat_len == 0 and n_lat == 2
    lat_blk0 = m_ctx // lat_len

    def row_of_tile(i, tm):
        return jnp.where(i < m_ctx // tm, 0, 1 + (i - m_ctx // tm) // (lat_len // tm))

    prep_tt = 256
    assert ctx_len == prep_tt

    def tiles_per_seq_of_tile(i):
        lat = i >= m_ctx // prep_tt
        per = jnp.where(lat, lat_len // prep_tt, 1)
        pos = jnp.where(lat, (i - m_ctx // prep_tt) % (lat_len // prep_tt), 0)
        return pos, per

    x = jnp.concatenate([x_prompt.reshape(m_ctx, d), x_sample.reshape(n_lat * lat_len, d)], axis=0)
    c8 = jnp.concatenate([c_ctx[None], c, jnp.zeros((8 - 1 - n_lat, d), F32)], axis=0)
    mod = _modulation(c8, w_mod, b_mod)
    rope_tabs = _rope_tables(lat_len)
    zeros_ret = jnp.zeros((n_ctx, 2, RET_HEADS, RET_DK, RET_DV), F32)
    zeros_rw = jnp.zeros((n_ctx, 2, RWKV_HEADS, RWKV_N, RWKV_N), F32)

    nk, nv, sr, sw = [], [], [], []
    for l in range(DEPTH):
        mod3 = mod[l].reshape(8, 1, 6 * d)
        w_in_p = _permute_in_proj(w_in[l]).astype(BF16)
        z = _lnmod_matmul(x, mod3, w_in_p, row_of_tile, 1024, 1152, 0, 1)

        o_na = jnp.concatenate([
            _ctx_attention(z, n_ctx, ctx_len),
            _na_latent(z, cache_na_k[:, l], cache_na_v[:, l], _na_bias_table(na_rpb[l], lat_len // GRID_W),
                       n_lat, lat_len, lat_blk0)], axis=0)
        k_na = z[:m_ctx, OFF_NA + 512:OFF_NA + 1024].reshape(n_ctx, ctx_len, NA_HEADS, NA_HEAD_DIM)
        v_na = z[:m_ctx, OFF_NA + 1024:OFF_NA + 1536].reshape(n_ctx, ctx_len, NA_HEADS, NA_HEAD_DIM)
        nk.append(k_na.transpose(0, 2, 1, 3))
        nv.append(v_na.transpose(0, 2, 1, 3))

        tables = _ret_decay_tables(ret_decay_logit[l])
        gn_w, gn_b = ret_gn_w[l][None], ret_gn_b[l][None]
        o_rt_c, s_ret = _retention(z, zeros_ret, tables, gn_w, gn_b, n_ctx, ctx_len, 0, None)
        o_rt_l, _ = _retention(z, state_ret[:, l], tables, gn_w, gn_b, n_lat, lat_len, lat_blk0, rope_tabs)
        o_rt = jnp.concatenate([o_rt_c, o_rt_l], axis=0)
        sr.append(s_ret)

        mu = rwkv_mu[l]
        prm = {
            'mu_main': mu[None, :1536], 'mu_lora': mu[None, 1536:],
            'rwkv_w0': rwkv_w0[l][:, None], 'rwkv_w_up': rwkv_w_up[l].astype(BF16),
            'rwkv_a0': rwkv_a0[l][:, None], 'rwkv_a_up': rwkv_a_up[l].astype(BF16),
            'rwkv_g_up': rwkv_g_up[l].astype(BF16), 'rwkv_k_k': rwkv_k_k[l][None],
            'rwkv_k_a': rwkv_k_a[l][None], 'rwkv_r_k': rwkv_r_k[l].reshape(1, RWKV_WIDTH),
        }
        r, kap, v, lw, kd, beta, g_rw, bonus = _rwkv_prep(z, prm, tiles_per_seq_of_tile, prep_tt)
        o_c, h_fin = _rwkv_scan(r, kap, v, lw, kd, beta, zeros_rw, n_ctx, ctx_len, 0)
        h0_lat = jnp.swapaxes(state_rwkv[:, l], -1, -2)
        o_l, _ = _rwkv_scan(r, kap, v, lw, kd, beta, h0_lat, n_lat, lat_len, lat_blk0)
        oscan = jnp.concatenate([o_c, o_l], axis=2)
        sw.append(jnp.swapaxes(h_fin, -1, -2))

        mp = {
            'rwkv_gn_w': rwkv_gn_w[l].reshape(RWKV_HEADS, 1, RWKV_N),
            'rwkv_gn_b': rwkv_gn_b[l].reshape(RWKV_HEADS, 1, RWKV_N),
            'w_br': w_br[l].astype(BF16), 'w_out': w_out[l].astype(BF16),
            'ln_a_g': ln_a_g[l][None], 'ln_a_b': ln_a_b[l][None],
        }
        x = _merge(oscan, bonus, g_rw, o_na, o_rt, z, x, mod3, mp, row_of_tile, 256)

        wq_t = peer_wq[l].T.astype(BF16)
        ht, s0, s1, e0, e1, tau = _peer_gate(x, mod3, wq_t, peer_keys[l].astype(BF16), row_of_tile, 256)
        x = _peer_expert(ht, s0, s1, e0, e1, tau, peer_u[l].astype(BF16), peer_v[l].T.astype(BF16), x, mod3,
                         ln_f_g[l][None], ln_f_b[l][None], row_of_tile, 256, 512)

    dt = x_prompt.dtype
    y_prompt = x[:m_ctx].reshape(n_ctx, ctx_len, d)
    y_sample = x[m_ctx:].reshape(n_lat, lat_len, d)
    return (y_prompt, y_sample, jnp.stack(nk, axis=1).astype(dt), jnp.stack(nv, axis=1).astype(dt),
            jnp.stack(sr, axis=1).astype(dt), jnp.stack(sw, axis=1).astype(dt))
```

```python
import functools
import math

import jax
import jax.numpy as jnp
import numpy as np
from jax import lax
from jax.experimental import pallas as pl
from jax.experimental.pallas import tpu as pltpu

F32 = jnp.float32
BF16 = jnp.bfloat16

D_MODEL = 1024
DEPTH = 2
GRID_W = 64
NA_HEADS = 8
NA_HEAD_DIM = 64
WIN_R = 8
WIN_C = 16
NEG_INF = -1e30
RET_HEADS = 4
RET_DK = 64
RET_DV = 128
RET_CHUNK = 64
RET_GN_EPS = 1e-5
ROPE_BASE = 10000.0
RWKV_HEADS = 8
RWKV_N = 64
RWKV_WIDTH = 512
RWKV_DECAY_SCALE = 0.606531
RWKV_GN_EPS = 64e-5
RWKV_CHUNK = 64
PEER_HEADS = 8
PEER_NKEYS = 128
PEER_TOPK = 16
LN_EPS = 1e-5
DEEPNORM_ALPHA = (2 * DEPTH) ** 0.25

OFF_GATE = 0
OFF_NA = 3072
OFF_RET_QK = 4608
OFF_RET_V = 5120
OFF_RW_MAIN = 6144
OFF_RW_LORA = 7680
P_IN = 8064

VMEM_LIMIT_BYTES = 56 * 1024 * 1024


def _cparams(*sem):
    return pltpu.CompilerParams(dimension_semantics=sem, vmem_limit_bytes=VMEM_LIMIT_BYTES)


def _ln(x, eps=LN_EPS):
    mu = jnp.mean(x, axis=-1, keepdims=True)
    xc = x - mu
    var = jnp.mean(xc * xc, axis=-1, keepdims=True)
    return xc * lax.rsqrt(var + eps)


def _sigmoid(x):
    return 1.0 / (1.0 + jnp.exp(-x))


def _dot(a, b):
    return jnp.dot(a.astype(BF16), b.astype(BF16), preferred_element_type=F32)


def _dot_nt(a, b):
    return lax.dot_general(a.astype(BF16), b.astype(BF16), (((1,), (1,)), ((), ())),
                           preferred_element_type=F32)


def _dot_tn(a, b):
    return lax.dot_general(a.astype(BF16), b.astype(BF16), (((0,), (0,)), ((), ())),
                           preferred_element_type=F32)


def _split3(x):
    hi = x.astype(BF16)
    r1 = x - hi.astype(F32)
    mid = r1.astype(BF16)
    lo = (r1 - mid.astype(F32)).astype(BF16)
    return hi, mid, lo


def _dot_exact_lhs(sel, x):
    hi, mid, lo = _split3(x)
    s = sel.astype(BF16)
    out = jnp.dot(s, lo, preferred_element_type=F32)
    out = out + jnp.dot(s, mid, preferred_element_type=F32)
    return out + jnp.dot(s, hi, preferred_element_type=F32)


def _dot_exact_rhs(x, sel):
    hi, mid, lo = _split3(x)
    s = sel.astype(BF16)
    out = jnp.dot(lo, s, preferred_element_type=F32)
    out = out + jnp.dot(mid, s, preferred_element_type=F32)
    return out + jnp.dot(hi, s, preferred_element_type=F32)


def _mod_kernel(c_ref, w_ref, b_ref, o_ref):
    c = c_ref[...]
    s = c * _sigmoid(c)
    o_ref[0] = _dot(s, w_ref[0]) + b_ref[0]


def _modulation(c8, w_mod, b_mod):
    tn = 1536
    n = w_mod.shape[-1]
    return pl.pallas_call(
        _mod_kernel,
        grid=(DEPTH, n // tn),
        in_specs=[pl.BlockSpec((8, D_MODEL), lambda l, j: (0, 0)),
                  pl.BlockSpec((1, D_MODEL, tn), lambda l, j: (l, 0, j)),
                  pl.BlockSpec((1, 1, tn), lambda l, j: (l, 0, j))],
        out_specs=pl.BlockSpec((1, 8, tn), lambda l, j: (l, 0, j)),
        out_shape=jax.ShapeDtypeStruct((DEPTH, 8, n), F32),
        compiler_params=_cparams("parallel", "parallel"),
        name="modulation",
    )(c8, w_mod, b_mod.reshape(DEPTH, 1, n))


def _lnmod_matmul_kernel(x_ref, sh_ref, sc_ref, w_ref, o_ref, h_ref):
    @pl.when(pl.program_id(1) == 0)
    def _():
        h = _ln(x_ref[...]) * (1.0 + sc_ref[0]) + sh_ref[0]
        h_ref[...] = h.astype(BF16)

    o_ref[...] = jnp.dot(h_ref[...], w_ref[...], preferred_element_type=F32)


def _lnmod_matmul(x, mod3, w, row_of_tile, tm, tn, sh_blk, sc_blk):
    m = x.shape[0]
    n = w.shape[1]
    return pl.pallas_call(
        _lnmod_matmul_kernel,
        grid=(m // tm, n // tn),
        in_specs=[pl.BlockSpec((tm, D_MODEL), lambda i, j: (i, 0)),
                  pl.BlockSpec((1, 1, D_MODEL), lambda i, j: (row_of_tile(i, tm), 0, sh_blk)),
                  pl.BlockSpec((1, 1, D_MODEL), lambda i, j: (row_of_tile(i, tm), 0, sc_blk)),
                  pl.BlockSpec((D_MODEL, tn), lambda i, j: (0, j))],
        out_specs=pl.BlockSpec((tm, tn), lambda i, j: (i, j)),
        out_shape=jax.ShapeDtypeStruct((m, n), F32),
        scratch_shapes=[pltpu.VMEM((tm, D_MODEL), BF16)],
        compiler_params=_cparams("parallel", "arbitrary"),
        name="adaln_in_proj",
    )(x, mod3, mod3, w)


def _ctx_attn_kernel(q_ref, k_ref, v_ref, o_ref):
    scale = NA_HEAD_DIM ** -0.5
    for hh in range(2):
        sl = slice(hh * NA_HEAD_DIM, (hh + 1) * NA_HEAD_DIM)
        s = _dot_nt(q_ref[:, sl], k_ref[:, sl]) * scale
        m = jnp.max(s, axis=-1, keepdims=True)
        p = jnp.exp(s - m)
        l = jnp.sum(p, axis=-1, keepdims=True)
        o_ref[:, sl] = _dot(p, v_ref[:, sl]) / l


def _ctx_attention(z, n_seq, seq_len):
    cb = OFF_NA // 128
    return pl.pallas_call(
        _ctx_attn_kernel,
        grid=(n_seq, NA_HEADS // 2),
        in_specs=[pl.BlockSpec((seq_len, 128), lambda b, h: (b, cb + h)),
                  pl.BlockSpec((seq_len, 128), lambda b, h: (b, cb + 4 + h)),
                  pl.BlockSpec((seq_len, 128), lambda b, h: (b, cb + 8 + h))],
        out_specs=pl.BlockSpec((seq_len, 128), lambda b, h: (b, h)),
        out_shape=jax.ShapeDtypeStruct((n_seq * seq_len, NA_HEADS * NA_HEAD_DIM), F32),
        compiler_params=_cparams("parallel", "parallel"),
        name="ctx_attention",
    )(z, z, z)


def _na_bias_kernel(rpb_ref, rowsel_ref, colsel_ref, neg_ref, o_ref):
    picked = _dot_exact_lhs(rowsel_ref[...], rpb_ref[0])
    o_ref[0] = _dot_exact_rhs(picked, colsel_ref[...]) + neg_ref[...]


def _na_bias_table(rpb, rows):
    kr = min(WIN_R, rows)
    nh, n_dr, n_dc = rpb.shape
    rep = np.array([0, 1, 2, 3, 4, rows - 3, rows - 2, rows - 1])
    start = np.clip(rep - kr // 2, 0, rows - kr)
    dr = start[:, None] - rep[:, None] + WIN_R - 1 + np.arange(kr)[None, :]
    qcol = np.arange(GRID_W)
    kcol = np.arange(GRID_W)
    q_start = np.clip(qcol - WIN_C // 2, 0, GRID_W - WIN_C)
    rel = kcol[None, :] - q_start[:, None]
    valid = (rel >= 0) & (rel < WIN_C)
    dc = np.clip(kcol[None, :] - qcol[:, None], -(WIN_C - 1), WIN_C - 1) + WIN_C - 1
    row_sel = np.zeros((8 * kr, 16), np.float32)
    row_sel[np.arange(8 * kr), dr.reshape(-1)] = 1.0
    col_sel = np.zeros((32, GRID_W * GRID_W), np.float32)
    qq, kk = np.nonzero(valid)
    col_sel[dc[qq, kk], qq * GRID_W + kk] = 1.0
    neg = np.where(valid, 0.0, NEG_INF).astype(np.float32).reshape(1, -1)
    rpb_p = jnp.pad(rpb.astype(F32), ((0, 0), (0, 16 - n_dr), (0, 32 - n_dc)))
    full = lambda a: pl.BlockSpec(a.shape, lambda h: (0,) * a.ndim)
    consts = [jnp.asarray(row_sel, BF16), jnp.asarray(col_sel, BF16), jnp.asarray(neg)]
    b = pl.pallas_call(
        _na_bias_kernel,
        grid=(nh,),
        in_specs=[pl.BlockSpec((1, 16, 32), lambda h: (h, 0, 0))] + [full(a) for a in consts],
        out_specs=pl.BlockSpec((1, 8 * kr, GRID_W * GRID_W), lambda h: (h, 0, 0)),
        out_shape=jax.ShapeDtypeStruct((nh, 8 * kr, GRID_W * GRID_W), F32),
        compiler_params=_cparams("parallel"),
        name="na_bias",
    )(rpb_p, *consts)
    b = b.reshape(nh, 8, kr, GRID_W, GRID_W)
    return b.transpose(0, 1, 3, 2, 4).reshape(nh, 8, GRID_W, kr * GRID_W)


def _na_latent_kernel(q_ref, k_ref, v_ref, ck_ref, cv_ref, b_ref, o_ref, *, rows):
    scale = NA_HEAD_DIM ** -0.5
    kr = min(WIN_R, rows)

    def body(r, carry):
        start = jnp.clip(r - kr // 2, 0, rows - kr)
        typ = jnp.where(r < 4, r, jnp.where(r > rows - 4, r - (rows - 8), 4))
        q0 = pl.multiple_of(r * GRID_W, GRID_W)
        k0 = pl.multiple_of(start * GRID_W, GRID_W)
        for hh in range(2):
            sl = slice(hh * NA_HEAD_DIM, (hh + 1) * NA_HEAD_DIM)
            q = q_ref[pl.ds(q0, GRID_W), sl]
            kw = k_ref[pl.ds(k0, kr * GRID_W), sl]
            vw = v_ref[pl.ds(k0, kr * GRID_W), sl]
            s_w = _dot_nt(q, kw) * scale + b_ref[hh, typ]
            s_c = _dot_nt(q, ck_ref[0, hh]) * scale
            m = jnp.maximum(jnp.max(s_w, axis=-1, keepdims=True), jnp.max(s_c, axis=-1, keepdims=True))
            p_w = jnp.exp(s_w - m)
            p_c = jnp.exp(s_c - m)
            l = jnp.sum(p_w, axis=-1, keepdims=True) + jnp.sum(p_c, axis=-1, keepdims=True)
            o_ref[pl.ds(q0, GRID_W), sl] = (_dot(p_w, vw) + _dot(p_c, cv_ref[0, hh])) / l
        return carry

    lax.fori_loop(0, rows, body, 0)


def _na_latent(z, ctx_k, ctx_v, bias, n_seq, seq_len, row_blk0):
    cb = OFF_NA // 128
    rows = seq_len // GRID_W
    past = ctx_k.shape[2]
    return pl.pallas_call(
        functools.partial(_na_latent_kernel, rows=rows),
        grid=(n_seq, NA_HEADS // 2),
        in_specs=[pl.BlockSpec((seq_len, 128), lambda b, h: (row_blk0 + b, cb + h)),
                  pl.BlockSpec((seq_len, 128), lambda b, h: (row_blk0 + b, cb + 4 + h)),
                  pl.BlockSpec((seq_len, 128), lambda b, h: (row_blk0 + b, cb + 8 + h)),
                  pl.BlockSpec((1, 2, past, NA_HEAD_DIM), lambda b, h: (b, h, 0, 0)),
                  pl.BlockSpec((1, 2, past, NA_HEAD_DIM), lambda b, h: (b, h, 0, 0)),
                  pl.BlockSpec((2, 8, GRID_W, bias.shape[-1]), lambda b, h: (h, 0, 0, 0))],
        out_specs=pl.BlockSpec((seq_len, 128), lambda b, h: (b, h)),
        out_shape=jax.ShapeDtypeStruct((n_seq * seq_len, NA_HEADS * NA_HEAD_DIM), F32),
        compiler_params=_cparams("parallel", "parallel"),
        name="na_latent",
    )(z, z, z, ctx_k, ctx_v, bias)


def _ret_decay_tables(decay_logit):
    c = RET_CHUNK
    lg = jnp.log(jax.nn.sigmoid(decay_logit.astype(F32)))[:, :, None, None]
    i = jnp.arange(c, dtype=F32)
    diff = i[:, None] - i[None, :]
    m_f = jnp.where(diff >= 0, jnp.exp(jnp.maximum(diff, 0.0) * lg[0]), 0.0)
    m_b = jnp.where(diff <= 0, jnp.exp(jnp.maximum(-diff, 0.0) * lg[1]), 0.0)
    mask = jnp.stack([m_f, m_b])
    ones = jnp.ones((1, 1, 1, RET_DK), F32)
    q_f = jnp.exp((i + 1.0)[None, :, None] * lg[0]) * ones[0]
    q_b = jnp.exp((c - i)[None, :, None] * lg[1]) * ones[0]
    k_f = jnp.exp((c - 1.0 - i)[None, :, None] * lg[0]) * ones[0]
    k_b = jnp.exp(i[None, :, None] * lg[1]) * ones[0]
    c_dec = jnp.exp(c * lg) * jnp.ones((1, 1, RET_DK, RET_DV), F32)
    return mask, jnp.stack([q_f, q_b]), jnp.stack([k_f, k_b]), c_dec


def _rope_tables(seq_len):
    quarter = RET_DK // 4
    pos = np.arange(seq_len)
    freqs = 1.0 / (ROPE_BASE ** (jnp.arange(quarter, dtype=F32) / quarter))
    a_row = jnp.asarray(pos // GRID_W, F32)[:, None] * freqs[None, :]
    a_col = jnp.asarray(pos % GRID_W, F32)[:, None] * freqs[None, :]
    cos = jnp.concatenate([jnp.cos(a_row)] * 2 + [jnp.cos(a_col)] * 2, axis=-1)
    sin = jnp.concatenate([-jnp.sin(a_row), jnp.sin(a_row), -jnp.sin(a_col), jnp.sin(a_col)], axis=-1)
    return jnp.tile(cos, (1, RET_HEADS)), jnp.tile(sin, (1, RET_HEADS))


def _retention_kernel(*refs, seq_len, rope):
    if rope:
        (q_ref, k_ref, v_ref, g_ref, s0_ref, mask_ref, qd_ref, kd_ref, cd_ref, gnw_ref, gnb_ref,
         cos_ref, sin_ref, o_ref, sfin_ref, qs_ref, ks_ref, acc_ref) = refs
    else:
        (q_ref, k_ref, v_ref, g_ref, s0_ref, mask_ref, qd_ref, kd_ref, cd_ref, gnw_ref, gnb_ref,
         o_ref, sfin_ref, qs_ref, ks_ref, acc_ref) = refs
    c = RET_CHUNK
    n = seq_len // c
    quarter = RET_DK // 4
    q = q_ref[...]
    k = k_ref[...] * (RET_DK ** -0.5)
    if rope:
        lane = lax.broadcasted_iota(jnp.int32, q.shape, 1)
        first = (lane % (2 * quarter)) < quarter
        width = q.shape[1]

        def rot(x):
            swapped = jnp.where(first, pltpu.roll(x, width - quarter, 1), pltpu.roll(x, quarter, 1))
            return x * cos_ref[...] + swapped * sin_ref[...]

        q = rot(q)
        k = rot(k)
    qs_ref[...] = q
    ks_ref[...] = k

    for h in range(RET_HEADS):
        ksl = slice(h * RET_DK, (h + 1) * RET_DK)
        vsl = slice(h * RET_DV, (h + 1) * RET_DV)
        for d in range(2):
            mask = mask_ref[d, h]
            qd = qd_ref[d, h]
            kd = kd_ref[d, h]
            cd = cd_ref[d, h]

            def step(ci, s, ksl=ksl, vsl=vsl, d=d, mask=mask, qd=qd, kd=kd, cd=cd):
                cc = ci if d == 0 else n - 1 - ci
                r0 = pl.multiple_of(cc * c, c)
                qc = qs_ref[pl.ds(r0, c), ksl]
                kc = ks_ref[pl.ds(r0, c), ksl]
                vc = v_ref[pl.ds(r0, c), vsl]
                att = _dot_nt(qc, kc) * mask
                o = _dot(att, vc) + _dot(qc * qd, s)
                if d == 0:
                    acc_ref[pl.ds(r0, c), vsl] = o
                else:
                    acc_ref[pl.ds(r0, c), vsl] += o
                return s * cd + _dot_tn(kc * kd, vc)

            s_fin = lax.fori_loop(0, n, step, s0_ref[0, d, h])
            sfin_ref[0, d, h] = s_fin

    for h in range(RET_HEADS):
        vsl = slice(h * RET_DV, (h + 1) * RET_DV)
        y = _ln(acc_ref[:, vsl], RET_GN_EPS) * gnw_ref[:, vsl] + gnb_ref[:, vsl]
        g = g_ref[:, vsl]
        o_ref[:, vsl] = y * (g * _sigmoid(g))


def _retention(z, s0, tables, gn_w, gn_b, n_seq, seq_len, row_blk0, rope_tabs):
    mask, qd, kd, cd = tables
    rope = rope_tabs is not None
    qk_w = RET_HEADS * RET_DK
    v_w = RET_HEADS * RET_DV
    full = lambda a: pl.BlockSpec(a.shape, lambda b: (0,) * a.ndim)
    in_specs = [pl.BlockSpec((seq_len, qk_w), lambda b: (row_blk0 + b, OFF_RET_QK // qk_w)),
                pl.BlockSpec((seq_len, qk_w), lambda b: (row_blk0 + b, OFF_RET_QK // qk_w + 1)),
                pl.BlockSpec((seq_len, v_w), lambda b: (row_blk0 + b, OFF_RET_V // v_w)),
                pl.BlockSpec((seq_len, v_w), lambda b: (row_blk0 + b, OFF_RET_V // v_w + 1)),
                pl.BlockSpec((1, 2, RET_HEADS, RET_DK, RET_DV), lambda b: (b, 0, 0, 0, 0)),
                full(mask), full(qd), full(kd), full(cd), full(gn_w), full(gn_b)]
    args = [z, z, z, z, s0, mask, qd, kd, cd, gn_w, gn_b]
    if rope:
        in_specs += [full(rope_tabs[0]), full(rope_tabs[1])]
        args += list(rope_tabs)
    return pl.pallas_call(
        functools.partial(_retention_kernel, seq_len=seq_len, rope=rope),
        grid=(n_seq,),
        in_specs=in_specs,
        out_specs=[pl.BlockSpec((seq_len, v_w), lambda b: (b, 0)),
                   pl.BlockSpec((1, 2, RET_HEADS, RET_DK, RET_DV), lambda b: (b, 0, 0, 0, 0))],
        out_shape=[jax.ShapeDtypeStruct((n_seq * seq_len, v_w), F32),
                   jax.ShapeDtypeStruct((n_seq, 2, RET_HEADS, RET_DK, RET_DV), F32)],
        scratch_shapes=[pltpu.VMEM((seq_len, qk_w), F32), pltpu.VMEM((seq_len, qk_w), F32),
                        pltpu.VMEM((seq_len, v_w), F32)],
        compiler_params=_cparams("parallel"),
        name="retention_rope" if rope else "retention",
    )(*args)


def _rwkv_prep_kernel(zm_ref, zl_ref, pm_ref, nm_ref, pl_ref, nl_ref, mum_ref, mul_ref,
                      w0_ref, wup_ref, a0_ref, aup_ref, gup_ref, kk_ref, ka_ref, rk_ref, ones_ref, tri_ref,
                      r_out, kap_out, v_out, lw_out, cs_out, kd_out, beta_out, g_out, bonus_out,
                      *, tiles_per_seq_of_tile):
    i = pl.program_id(0)
    pos, per = tiles_per_seq_of_tile(i)
    has_prev = (pos != 0).astype(F32)
    has_next = (pos != per - 1).astype(F32)
    tt = zm_ref.shape[0]

    def shift(z_ref, p_ref, n_ref, mu_ref):
        z = z_ref[...]
        row = lax.broadcasted_iota(jnp.int32, z.shape, 0)
        prev = jnp.where(row == 0, p_ref[7:8, :] * has_prev, pltpu.roll(z, 1, 0))
        nxt = jnp.where(row == tt - 1, n_ref[0:1, :] * has_next, pltpu.roll(z, tt - 1, 0))
        return z + mu_ref[...] * (0.5 * (prev + nxt) - z)

    zm = shift(zm_ref, pm_ref, nm_ref, mum_ref)
    zl = shift(zl_ref, pl_ref, nl_ref, mul_ref)
    w = RWKV_WIDTH
    r_c, k_c, v_c = zm[:, 0:w], zm[:, w:2 * w], zm[:, 2 * w:3 * w]
    ones = ones_ref[...]
    g_out[...] = _dot(_sigmoid(zl[:, 256:384]), gup_ref[...])
    kk = k_c * kk_ref[...]
    ss = _dot_exact_rhs(kk * kk, ones)
    kap = kk * lax.rsqrt(jnp.maximum(ss, 1e-24))
    bonus_out[...] = _dot_exact_rhs(r_c * k_c * rk_ref[...], ones) * v_c
    for h in range(RWKV_HEADS):
        sl = slice(h * RWKV_N, (h + 1) * RWKV_N)
        r_out[h] = r_c[:, sl]
        kap_out[h] = kap[:, sl]
        v_out[h] = v_c[:, sl]
    for d in range(2):
        wl = zl[:, d * 64:(d + 1) * 64]
        al = zl[:, 128 + d * 64:128 + (d + 1) * 64]
        lw = -RWKV_DECAY_SCALE * _sigmoid(w0_ref[d] + _dot(jnp.tanh(wl), wup_ref[d]))
        a = _sigmoid(a0_ref[d] + _dot(al, aup_ref[d]))
        k_d = k_c * (1.0 + (a - 1.0) * ka_ref[...])
        beta = kap * a
        cs = _dot_exact_lhs(tri_ref[d], lw)
        for h in range(RWKV_HEADS):
            sl = slice(h * RWKV_N, (h + 1) * RWKV_N)
            lw_out[d, h] = lw[:, sl]
            cs_out[d, h] = cs[:, sl]
            kd_out[d, h] = k_d[:, sl]
            beta_out[d, h] = beta[:, sl]


def _rwkv_prep(z, p, tiles_per_seq_of_tile, tt):
    m = z.shape[0]
    nt = m // tt
    w = RWKV_WIDTH
    mb, lb = OFF_RW_MAIN // (3 * w), OFF_RW_LORA // 384
    hm = tt // 8
    last8 = m // 8 - 1
    prev_idx = lambda i: jnp.maximum(i * hm - 1, 0)
    next_idx = lambda i: jnp.minimum((i + 1) * hm, last8)
    full = lambda a: pl.BlockSpec(a.shape, lambda i: (0,) * a.ndim)
    heads = jnp.arange(w) // RWKV_N
    ones = (heads[:, None] == heads[None, :]).astype(BF16)
    t = np.arange(tt)
    same = (t[:, None] // RWKV_CHUNK) == (t[None, :] // RWKV_CHUNK)
    tri = jnp.asarray(np.stack([same & (t[None, :] <= t[:, None]), same & (t[None, :] >= t[:, None])]), BF16)
    params = [p['mu_main'], p['mu_lora'], p['rwkv_w0'], p['rwkv_w_up'], p['rwkv_a0'], p['rwkv_a_up'],
              p['rwkv_g_up'], p['rwkv_k_k'], p['rwkv_k_a'], p['rwkv_r_k'], ones, tri]
    hd = lambda: jax.ShapeDtypeStruct((RWKV_HEADS, m, RWKV_N), F32)
    dhd = lambda: jax.ShapeDtypeStruct((2, RWKV_HEADS, m, RWKV_N), F32)
    tok = lambda: jax.ShapeDtypeStruct((m, w), F32)
    hd_spec = pl.BlockSpec((RWKV_HEADS, tt, RWKV_N), lambda i: (0, i, 0))
    dhd_spec = pl.BlockSpec((2, RWKV_HEADS, tt, RWKV_N), lambda i: (0, 0, i, 0))
    tok_spec = pl.BlockSpec((tt, w), lambda i: (i, 0))
    return pl.pallas_call(
        functools.partial(_rwkv_prep_kernel, tiles_per_seq_of_tile=tiles_per_seq_of_tile),
        grid=(nt,),
        in_specs=[pl.BlockSpec((tt, 3 * w), lambda i: (i, mb)),
                  pl.BlockSpec((tt, 384), lambda i: (i, lb)),
                  pl.BlockSpec((8, 3 * w), lambda i: (prev_idx(i), mb)),
                  pl.BlockSpec((8, 3 * w), lambda i: (next_idx(i), mb)),
                  pl.BlockSpec((8, 384), lambda i: (prev_idx(i), lb)),
                  pl.BlockSpec((8, 384), lambda i: (next_idx(i), lb))] + [full(a) for a in params],
        out_specs=[hd_spec, hd_spec, hd_spec, dhd_spec, dhd_spec, dhd_spec, dhd_spec, tok_spec, tok_spec],
        out_shape=[hd(), hd(), hd(), dhd(), dhd(), dhd(), dhd(), tok(), tok()],
        compiler_params=_cparams("parallel"),
        name="rwkv_prep",
    )(z, z, z, z, z, z, *params)


def _dot3(a, b, dims):
    a_hi = a.astype(BF16)
    b_hi = b.astype(BF16)
    a_lo = (a - a_hi.astype(F32)).astype(BF16)
    b_lo = (b - b_hi.astype(F32)).astype(BF16)
    dn = (dims, ((), ()))
    out = lax.dot_general(a_lo, b_hi, dn, preferred_element_type=F32)
    out = out + lax.dot_general(a_hi, b_lo, dn, preferred_element_type=F32)
    return out + lax.dot_general(a_hi, b_hi, dn, preferred_element_type=F32)


_NN = ((1,), (0,))
_NT = ((1,), (1,))
_TN = ((0,), (0,))


def _dot1(a, b, dims):
    return lax.dot_general(a.astype(BF16), b.astype(BF16), (dims, ((), ())), preferred_element_type=F32)


_RWKV_MM = _dot1
RWKV_CHAINS = 16
RWKV_MAX_GROUP = 8


def _rwkv_scan_kernel(r_ref, kap_ref, v_ref, lw_ref, cs_ref, kd_ref, beta_ref, h0_ref, o_ref, hfin_ref, *,
                      seq_len, group, heads):
    c = RWKV_CHUNK
    n = seq_len // c
    d = pl.program_id(0)
    sgn = 1 - 2 * d
    ti = lax.broadcasted_iota(jnp.int32, (c, c), 0) * sgn
    tj = lax.broadcasted_iota(jnp.int32, (c, c), 1) * sgn
    strict = tj < ti
    incl = tj <= ti
    eye_f = (ti == tj).astype(F32)

    def chunk_terms(hh, cc):
        rows = pl.ds(pl.multiple_of(cc * c, c), c)
        r = r_ref[hh, rows, :]
        kap = kap_ref[hh, rows, :]
        v = v_ref[hh, rows, :]
        lw = lw_ref[0, hh, rows, :]
        cs = cs_ref[0, hh, rows, :]
        kd = kd_ref[0, hh, rows, :]
        beta = beta_ref[0, hh, rows, :]
        tot = jnp.sum(lw, axis=0, keepdims=True)
        kap_t = kap * jnp.exp(cs - lw)
        r_t = r * jnp.exp(cs)
        e_ncs = jnp.exp(-cs)
        k_t = kd * e_ncs
        b_t = beta * e_ncs
        e_end = jnp.exp(tot - cs)
        k_end = kd * e_end
        b_end = beta * e_end
        yield
        l_b = jnp.where(strict, _RWKV_MM(kap_t, b_t, _NT), 0.0)
        l_k = jnp.where(strict, _RWKV_MM(kap_t, k_t, _NT), 0.0)
        m_b = jnp.where(incl, _RWKV_MM(r_t, b_t, _NT), 0.0)
        m_k = jnp.where(incl, _RWKV_MM(r_t, k_t, _NT), 0.0)
        yield
        x = -l_b
        y = jnp.concatenate([kap_t, _RWKV_MM(l_k, v, _NN)], axis=1)
        yield
        y = y + _RWKV_MM(x, y, _NN)
        pw = x
        for _ in range(5):
            pw = _RWKV_MM(pw, pw, _NN)
            yield
            y = y + _RWKV_MM(pw, y, _NN)
        yield
        mby = _RWKV_MM(m_b, y, _NN)
        bty = _RWKV_MM(b_end, y, _TN)
        rr = r_t - mby[:, :RWKV_N]
        o0 = _RWKV_MM(m_k, v, _NN) - mby[:, RWKV_N:]
        pc = eye_f * jnp.exp(tot) - bty[:, :RWKV_N]
        qc = _RWKV_MM(k_end, v, _TN) - bty[:, RWKV_N:]
        return rows, rr, o0, pc, qc

    def step(gi, hstates):
        keys = [(hh, u) for u in range(group) for hh in range(heads)]
        gens = {(hh, u): chunk_terms(hh, (gi * group + u) + d * (n - 1 - 2 * (gi * group + u))) for hh, u in keys}
        terms = {}
        live = list(keys)
        while live:
            for key in list(live):
                try:
                    next(gens[key])
                except StopIteration as done:
                    terms[key] = done.value
                    live.remove(key)
        hstates = list(hstates)
        for u in range(group):
            for hh in range(heads):
                rows, rr, o0, pc, qc = terms[(hh, u)]
                o_ref[0, hh, rows, :] = _dot3(rr, hstates[hh], _NN) + o0
                hstates[hh] = _dot3(pc, hstates[hh], _NN) + qc
        return tuple(hstates)

    fin = lax.fori_loop(0, n // group, step, tuple(h0_ref[0, 0, hh] for hh in range(heads)))
    for hh in range(heads):
        hfin_ref[0, 0, hh] = fin[hh]


def _rwkv_scan(r, kap, v, lw, cs, kd, beta, h0, n_seq, seq_len, row_blk0):
    group = min(seq_len // RWKV_CHUNK, RWKV_MAX_GROUP)
    hp = RWKV_CHAINS // group
    hd_spec = pl.BlockSpec((hp, seq_len, RWKV_N), lambda d, h, b: (h, row_blk0 + b, 0))
    dhd_spec = pl.BlockSpec((1, hp, seq_len, RWKV_N), lambda d, h, b: (d, h, row_blk0 + b, 0))
    st_spec = pl.BlockSpec((1, 1, hp, RWKV_N, RWKV_N), lambda d, h, b: (b, d, h, 0, 0))
    return pl.pallas_call(
        functools.partial(_rwkv_scan_kernel, seq_len=seq_len, group=group, heads=hp),
        grid=(2, RWKV_HEADS // hp, n_seq),
        in_specs=[hd_spec, hd_spec, hd_spec, dhd_spec, dhd_spec, dhd_spec, dhd_spec, st_spec],
        out_specs=[pl.BlockSpec((1, hp, seq_len, RWKV_N), lambda d, h, b: (d, h, b, 0)), st_spec],
        out_shape=[jax.ShapeDtypeStruct((2, RWKV_HEADS, n_seq * seq_len, RWKV_N), F32),
                   jax.ShapeDtypeStruct((n_seq, 2, RWKV_HEADS, RWKV_N, RWKV_N), F32)],
        compiler_params=_cparams("parallel", "parallel", "parallel"),
        name="rwkv_scan",
    )(r, kap, v, lw, cs, kd, beta, h0)


def _merge_kernel(oscan_ref, bonus_ref, grw_ref, ona_ref, ort_ref, ga_ref, gb_ref, gc_ref, x_ref, gt_ref,
                  gnw_ref, gnb_ref, wbr_ref, wout_ref, lng_ref, lnb_ref, o_ref, orw_ref):
    for h in range(RWKV_HEADS):
        sl = slice(h * RWKV_N, (h + 1) * RWKV_N)
        y = _ln(oscan_ref[0, h] + oscan_ref[1, h], RWKV_GN_EPS) * gnw_ref[h] + gnb_ref[h]
        orw_ref[:, sl] = (y + bonus_ref[:, sl]) * grw_ref[:, sl]
    merged = (_sigmoid(ga_ref[...]) * _dot(ona_ref[...], wbr_ref[0])
              + _sigmoid(gb_ref[...]) * _dot(ort_ref[...], wbr_ref[1])
              + _sigmoid(gc_ref[...]) * _dot(orw_ref[...], wbr_ref[2]))
    t = _dot(merged, wout_ref[...])
    o_ref[...] = _ln(DEEPNORM_ALPHA * x_ref[...] + gt_ref[0] * t) * lng_ref[...] + lnb_ref[...]


def _merge(oscan, bonus, g_rw, o_na, o_rt, z, x, mod3, p, row_of_tile, tm):
    m = x.shape[0]
    bw = RWKV_WIDTH
    full = lambda a: pl.BlockSpec(a.shape, lambda i: (0,) * a.ndim)
    tok = pl.BlockSpec((tm, bw), lambda i: (i, 0))
    params = [p['rwkv_gn_w'], p['rwkv_gn_b'], p['w_br'], p['w_out'], p['ln_a_g'], p['ln_a_b']]
    return pl.pallas_call(
        _merge_kernel,
        grid=(m // tm,),
        in_specs=[pl.BlockSpec((2, RWKV_HEADS, tm, RWKV_N), lambda i: (0, 0, i, 0)),
                  tok, tok, tok, tok,
                  pl.BlockSpec((tm, D_MODEL), lambda i: (i, 0)),
                  pl.BlockSpec((tm, D_MODEL), lambda i: (i, 1)),
                  pl.BlockSpec((tm, D_MODEL), lambda i: (i, 2)),
                  pl.BlockSpec((tm, D_MODEL), lambda i: (i, 0)),
                  pl.BlockSpec((1, 1, D_MODEL), lambda i: (row_of_tile(i, tm), 0, 2))]
                 + [full(a) for a in params],
        out_specs=pl.BlockSpec((tm, D_MODEL), lambda i: (i, 0)),
        out_shape=jax.ShapeDtypeStruct((m, D_MODEL), F32),
        scratch_shapes=[pltpu.VMEM((tm, bw), F32)],
        compiler_params=_cparams("parallel"),
        name="merge_out_proj",
    )(oscan, bonus, g_rw, o_na, o_rt, z, z, z, x, mod3, *params)


def _top_values(s, k):
    out = []
    for _ in range(k):
        m = jnp.max(s, axis=0, keepdims=True)
        out.append(m)
        s = jnp.where(s == m, -jnp.inf, s)
    return out


def _peer_gate_kernel(x_ref, sh_ref, sc_ref, wq_ref, keys_ref, ht_ref, s0_ref, s1_ref, e0_ref, e1_ref, tau_ref):
    h = _ln(x_ref[...]) * (1.0 + sc_ref[0]) + sh_ref[0]
    ht = h.T.astype(BF16)
    ht_ref[...] = ht
    qt = jnp.dot(wq_ref[...], ht, preferred_element_type=F32)
    half = keys_ref.shape[-1]
    for hd in range(PEER_HEADS):
        scores, tops = [], []
        for part in range(2):
            row0 = (hd * 2 + part) * half
            s = _dot(keys_ref[hd, part], qt[row0:row0 + half, :])
            scores.append(s)
            tops.append(_top_values(s, PEER_TOPK))
        cand = jnp.concatenate([tops[0][a] + jnp.concatenate(tops[1], axis=0) for a in range(PEER_TOPK)],
                               axis=0)
        best = _top_values(cand, PEER_TOPK)
        zsum = jnp.exp(best[0] - best[0])
        for t in range(1, PEER_TOPK):
            zsum = zsum + jnp.exp(best[t] - best[0])
        s0_ref[hd] = scores[0]
        s1_ref[hd] = scores[1]
        e0_ref[hd] = jnp.exp(scores[0] - tops[0][0])
        e1_ref[hd] = jnp.exp(scores[1] - tops[1][0]) / zsum
        tau_ref[hd] = best[PEER_TOPK - 1]


def _peer_gate(x, mod3, wq_t, keys, row_of_tile, tn):
    m = x.shape[0]
    full = lambda a: pl.BlockSpec(a.shape, lambda i: (0,) * a.ndim)
    sk = pl.BlockSpec((PEER_HEADS, PEER_NKEYS, tn), lambda i: (0, 0, i))
    sk_shape = jax.ShapeDtypeStruct((PEER_HEADS, PEER_NKEYS, m), F32)
    return pl.pallas_call(
        _peer_gate_kernel,
        grid=(m // tn,),
        in_specs=[pl.BlockSpec((tn, D_MODEL), lambda i: (i, 0)),
                  pl.BlockSpec((1, 1, D_MODEL), lambda i: (row_of_tile(i, tn), 0, 3)),
                  pl.BlockSpec((1, 1, D_MODEL), lambda i: (row_of_tile(i, tn), 0, 4)),
                  full(wq_t), full(keys)],
        out_specs=[pl.BlockSpec((D_MODEL, tn), lambda i: (0, i)), sk, sk, sk, sk,
                   pl.BlockSpec((PEER_HEADS, 1, tn), lambda i: (0, 0, i))],
        out_shape=[jax.ShapeDtypeStruct((D_MODEL, m), BF16), sk_shape, sk_shape, sk_shape, sk_shape,
                   jax.ShapeDtypeStruct((PEER_HEADS, 1, m), F32)],
        compiler_params=_cparams("parallel"),
        name="peer_gate",
    )(x, mod3, mod3, wq_t, keys)


def _gelu_exact(x):
    return 0.5 * x * (1.0 + lax.erf(x * (2.0 ** -0.5)))


def _peer_expert_kernel(ht_ref, s0_ref, s1_ref, e0_ref, e1_ref, tau_ref, u_ref, vt_ref, x_ref, gt_ref,
                        lng_ref, lnb_ref, o_ref, acc_ref, act_ref, *, rows_per_tile):
    e = pl.program_id(1)

    @pl.when(e == 0)
    def _():
        acc_ref[...] = jnp.zeros_like(acc_ref)

    for ii in range(rows_per_tile):
        i = e * rows_per_tile + ii
        rs = slice(ii * PEER_NKEYS, (ii + 1) * PEER_NKEYS)
        st = jnp.dot(u_ref[rs, :], ht_ref[...], preferred_element_type=F32)
        gate = None
        for hd in range(PEER_HEADS):
            s0 = s0_ref[hd, pl.ds(i, 1), :]
            e0 = e0_ref[hd, pl.ds(i, 1), :]
            term = jnp.where(s0 + s1_ref[hd] >= tau_ref[hd], e1_ref[hd] * e0, 0.0)
            gate = term if gate is None else gate + term
        act_ref[rs, :] = (_gelu_exact(st) * gate).astype(BF16)
    acc_ref[...] += jnp.dot(vt_ref[...], act_ref[...], preferred_element_type=F32)

    @pl.when(e == pl.num_programs(1) - 1)
    def _():
        f = acc_ref[...].T
        o_ref[...] = _ln(DEEPNORM_ALPHA * x_ref[...] + gt_ref[0] * f) * lng_ref[...] + lnb_ref[...]


def _peer_expert(ht, s0, s1, e0, e1, tau, u, v_t, x, mod3, ln_g, ln_b, row_of_tile, tn, eb):
    m = x.shape[0]
    n_exp = u.shape[0]
    sk = pl.BlockSpec((PEER_HEADS, PEER_NKEYS, tn), lambda i, e: (0, 0, i))
    full = lambda a: pl.BlockSpec(a.shape, lambda i, e: (0,) * a.ndim)
    return pl.pallas_call(
        functools.partial(_peer_expert_kernel, rows_per_tile=eb // PEER_NKEYS),
        grid=(m // tn, n_exp // eb),
        in_specs=[pl.BlockSpec((D_MODEL, tn), lambda i, e: (0, i)), sk, sk, sk, sk,
                  pl.BlockSpec((PEER_HEADS, 1, tn), lambda i, e: (0, 0, i)),
                  pl.BlockSpec((eb, D_MODEL), lambda i, e: (e, 0)),
                  pl.BlockSpec((D_MODEL, eb), lambda i, e: (0, e)),
                  pl.BlockSpec((tn, D_MODEL), lambda i, e: (i, 0)),
                  pl.BlockSpec((1, 1, D_MODEL), lambda i, e: (row_of_tile(i, tn), 0, 5)),
                  full(ln_g), full(ln_b)],
        out_specs=pl.BlockSpec((tn, D_MODEL), lambda i, e: (i, 0)),
        out_shape=jax.ShapeDtypeStruct((m, D_MODEL), F32),
        scratch_shapes=[pltpu.VMEM((D_MODEL, tn), F32), pltpu.VMEM((eb, tn), BF16)],
        compiler_params=_cparams("parallel", "arbitrary"),
        name="peer_experts",
    )(ht, s0, s1, e0, e1, tau, u, v_t, x, mod3, ln_g, ln_b)


def _permute_in_proj(w_in):
    na, qk, rv = 3 * 512, 2 * 256, 2 * 512
    o_na, o_qk, o_rv, o_rw, o_g = 0, na, na + qk, na + qk + rv, na + qk + rv + 1920
    return jnp.concatenate([w_in[:, o_g:], w_in[:, o_na:o_qk], w_in[:, o_qk:o_rv], w_in[:, o_rv:o_rw],
                            w_in[:, o_rw:o_g]], axis=1)


def kernel(x_prompt, x_sample, cache_na_k, cache_na_v, state_ret, state_rwkv, c, c_ctx, w_mod, b_mod, w_in, na_rpb, ret_decay_logit, ret_gn_w, ret_gn_b, rwkv_mu, rwkv_w0, rwkv_w_up, rwkv_a0, rwkv_a_up, rwkv_g_up, rwkv_k_k, rwkv_k_a, rwkv_r_k, rwkv_gn_w, rwkv_gn_b, w_br, w_out, ln_a_g, ln_a_b, ln_f_g, ln_f_b, peer_wq, peer_keys, peer_u, peer_v):
    n_ctx, ctx_len, d = x_prompt.shape
    n_lat, lat_len, _ = x_sample.shape
    m_ctx = n_ctx * ctx_len
    assert m_ctx % lat_len == 0 and n_lat == 2
    lat_blk0 = m_ctx // lat_len

    def row_of_tile(i, tm):
        return jnp.where(i < m_ctx // tm, 0, 1 + (i - m_ctx // tm) // (lat_len // tm))

    prep_tt = 256
    assert ctx_len == prep_tt

    def tiles_per_seq_of_tile(i):
        lat = i >= m_ctx // prep_tt
        per = jnp.where(lat, lat_len // prep_tt, 1)
        pos = jnp.where(lat, (i - m_ctx // prep_tt) % (lat_len // prep_tt), 0)
        return pos, per

    x = jnp.concatenate([x_prompt.reshape(m_ctx, d), x_sample.reshape(n_lat * lat_len, d)], axis=0)
    c8 = jnp.concatenate([c_ctx[None], c, jnp.zeros((8 - 1 - n_lat, d), F32)], axis=0)
    mod = _modulation(c8, w_mod, b_mod)
    rope_tabs = _rope_tables(lat_len)
    zeros_ret = jnp.zeros((n_ctx, 2, RET_HEADS, RET_DK, RET_DV), F32)
    zeros_rw = jnp.zeros((n_ctx, 2, RWKV_HEADS, RWKV_N, RWKV_N), F32)

    nk, nv, sr, sw = [], [], [], []
    for l in range(DEPTH):
        mod3 = mod[l].reshape(8, 1, 6 * d)
        w_in_p = _permute_in_proj(w_in[l]).astype(BF16)
        z = _lnmod_matmul(x, mod3, w_in_p, row_of_tile, 1024, 1152, 0, 1)

        o_na = jnp.concatenate([
            _ctx_attention(z, n_ctx, ctx_len),
            _na_latent(z, cache_na_k[:, l], cache_na_v[:, l], _na_bias_table(na_rpb[l], lat_len // GRID_W),
                       n_lat, lat_len, lat_blk0)], axis=0)
        k_na = z[:m_ctx, OFF_NA + 512:OFF_NA + 1024].reshape(n_ctx, ctx_len, NA_HEADS, NA_HEAD_DIM)
        v_na = z[:m_ctx, OFF_NA + 1024:OFF_NA + 1536].reshape(n_ctx, ctx_len, NA_HEADS, NA_HEAD_DIM)
        nk.append(k_na.transpose(0, 2, 1, 3))
        nv.append(v_na.transpose(0, 2, 1, 3))

        tables = _ret_decay_tables(ret_decay_logit[l])
        gn_w, gn_b = ret_gn_w[l][None], ret_gn_b[l][None]
        o_rt_c, s_ret = _retention(z, zeros_ret, tables, gn_w, gn_b, n_ctx, ctx_len, 0, None)
        o_rt_l, _ = _retention(z, state_ret[:, l], tables, gn_w, gn_b, n_lat, lat_len, lat_blk0, rope_tabs)
        o_rt = jnp.concatenate([o_rt_c, o_rt_l], axis=0)
        sr.append(s_ret)

        mu = rwkv_mu[l]
        prm = {
            'mu_main': mu[None, :1536], 'mu_lora': mu[None, 1536:],
            'rwkv_w0': rwkv_w0[l][:, None], 'rwkv_w_up': rwkv_w_up[l].astype(BF16),
            'rwkv_a0': rwkv_a0[l][:, None], 'rwkv_a_up': rwkv_a_up[l].astype(BF16),
            'rwkv_g_up': rwkv_g_up[l].astype(BF16), 'rwkv_k_k': rwkv_k_k[l][None],
            'rwkv_k_a': rwkv_k_a[l][None], 'rwkv_r_k': rwkv_r_k[l].reshape(1, RWKV_WIDTH),
        }
        r, kap, v, lw, cs, kd, beta, g_rw, bonus = _rwkv_prep(z, prm, tiles_per_seq_of_tile, prep_tt)
        o_c, h_fin = _rwkv_scan(r, kap, v, lw, cs, kd, beta, zeros_rw, n_ctx, ctx_len, 0)
        h0_lat = jnp.swapaxes(state_rwkv[:, l], -1, -2)
        o_l, _ = _rwkv_scan(r, kap, v, lw, cs, kd, beta, h0_lat, n_lat, lat_len, lat_blk0)
        oscan = jnp.concatenate([o_c, o_l], axis=2)
        sw.append(jnp.swapaxes(h_fin, -1, -2))

        mp = {
            'rwkv_gn_w': rwkv_gn_w[l].reshape(RWKV_HEADS, 1, RWKV_N),
            'rwkv_gn_b': rwkv_gn_b[l].reshape(RWKV_HEADS, 1, RWKV_N),
            'w_br': w_br[l].astype(BF16), 'w_out': w_out[l].astype(BF16),
            'ln_a_g': ln_a_g[l][None], 'ln_a_b': ln_a_b[l][None],
        }
        x = _merge(oscan, bonus, g_rw, o_na, o_rt, z, x, mod3, mp, row_of_tile, 256)

        wq_t = peer_wq[l].T.astype(BF16)
        ht, s0, s1, e0, e1, tau = _peer_gate(x, mod3, wq_t, peer_keys[l].astype(BF16), row_of_tile, 256)
        x = _peer_expert(ht, s0, s1, e0, e1, tau, peer_u[l].astype(BF16), peer_v[l].T.astype(BF16), x, mod3,
                         ln_f_g[l][None], ln_f_b[l][None], row_of_tile, 256, 512)

    dt = x_prompt.dtype
    y_prompt = x[:m_ctx].reshape(n_ctx, ctx_len, d)
    y_sample = x[m_ctx:].reshape(n_lat, lat_len, d)
    return (y_prompt, y_sample, jnp.stack(nk, axis=1).astype(dt), jnp.stack(nv, axis=1).astype(dt),
            jnp.stack(sr, axis=1).astype(dt), jnp.stack(sw, axis=1).astype(dt))
```

```python
import functools
import math

import jax
import jax.numpy as jnp
import numpy as np
from jax import lax
from jax.experimental import pallas as pl
from jax.experimental.pallas import tpu as pltpu

F32 = jnp.float32
BF16 = jnp.bfloat16

D_MODEL = 1024
DEPTH = 2
GRID_W = 64
NA_HEADS = 8
NA_HEAD_DIM = 64
WIN_R = 8
WIN_C = 16
NEG_INF = -1e30
RET_HEADS = 4
RET_DK = 64
RET_DV = 128
RET_CHUNK = 64
RET_GN_EPS = 1e-5
ROPE_BASE = 10000.0
RWKV_HEADS = 8
RWKV_N = 64
RWKV_WIDTH = 512
RWKV_DECAY_SCALE = 0.606531
RWKV_GN_EPS = 64e-5
RWKV_CHUNK = 64
PEER_HEADS = 8
PEER_NKEYS = 128
PEER_TOPK = 16
LN_EPS = 1e-5
DEEPNORM_ALPHA = (2 * DEPTH) ** 0.25

OFF_GATE = 0
OFF_NA = 3072
OFF_RET_QK = 4608
OFF_RET_V = 5120
OFF_RW_MAIN = 6144
OFF_RW_LORA = 7680
P_IN = 8064

VMEM_LIMIT_BYTES = 56 * 1024 * 1024


def _cparams(*sem):
    return pltpu.CompilerParams(dimension_semantics=sem, vmem_limit_bytes=VMEM_LIMIT_BYTES)


def _ln(x, eps=LN_EPS):
    mu = jnp.mean(x, axis=-1, keepdims=True)
    xc = x - mu
    var = jnp.mean(xc * xc, axis=-1, keepdims=True)
    return xc * lax.rsqrt(var + eps)


def _sigmoid(x):
    return 1.0 / (1.0 + jnp.exp(-x))


def _dot(a, b):
    return jnp.dot(a.astype(BF16), b.astype(BF16), preferred_element_type=F32)


def _dot_nt(a, b):
    return lax.dot_general(a.astype(BF16), b.astype(BF16), (((1,), (1,)), ((), ())),
                           preferred_element_type=F32)


def _dot_tn(a, b):
    return lax.dot_general(a.astype(BF16), b.astype(BF16), (((0,), (0,)), ((), ())),
                           preferred_element_type=F32)


def _split3(x):
    hi = x.astype(BF16)
    r1 = x - hi.astype(F32)
    mid = r1.astype(BF16)
    lo = (r1 - mid.astype(F32)).astype(BF16)
    return hi, mid, lo


def _dot_exact_lhs(sel, x):
    hi, mid, lo = _split3(x)
    s = sel.astype(BF16)
    out = jnp.dot(s, lo, preferred_element_type=F32)
    out = out + jnp.dot(s, mid, preferred_element_type=F32)
    return out + jnp.dot(s, hi, preferred_element_type=F32)


def _dot_exact_rhs(x, sel):
    hi, mid, lo = _split3(x)
    s = sel.astype(BF16)
    out = jnp.dot(lo, s, preferred_element_type=F32)
    out = out + jnp.dot(mid, s, preferred_element_type=F32)
    return out + jnp.dot(hi, s, preferred_element_type=F32)


def _mod_kernel(c_ref, w_ref, b_ref, o_ref):
    c = c_ref[...]
    s = c * _sigmoid(c)
    o_ref[0] = _dot(s, w_ref[0]) + b_ref[0]


def _modulation(c8, w_mod, b_mod):
    tn = 1536
    n = w_mod.shape[-1]
    return pl.pallas_call(
        _mod_kernel,
        grid=(DEPTH, n // tn),
        in_specs=[pl.BlockSpec((8, D_MODEL), lambda l, j: (0, 0)),
                  pl.BlockSpec((1, D_MODEL, tn), lambda l, j: (l, 0, j)),
                  pl.BlockSpec((1, 1, tn), lambda l, j: (l, 0, j))],
        out_specs=pl.BlockSpec((1, 8, tn), lambda l, j: (l, 0, j)),
        out_shape=jax.ShapeDtypeStruct((DEPTH, 8, n), F32),
        compiler_params=_cparams("parallel", "parallel"),
        name="modulation",
    )(c8, w_mod, b_mod.reshape(DEPTH, 1, n))


def _lnmod_matmul_kernel(x_ref, sh_ref, sc_ref, w_ref, o_ref, h_ref):
    @pl.when(pl.program_id(1) == 0)
    def _():
        h = _ln(x_ref[...]) * (1.0 + sc_ref[0]) + sh_ref[0]
        h_ref[...] = h.astype(BF16)

    o_ref[...] = jnp.dot(h_ref[...], w_ref[...], preferred_element_type=F32)


def _lnmod_matmul(x, mod3, w, row_of_tile, tm, tn, sh_blk, sc_blk):
    m = x.shape[0]
    n = w.shape[1]
    return pl.pallas_call(
        _lnmod_matmul_kernel,
        grid=(m // tm, n // tn),
        in_specs=[pl.BlockSpec((tm, D_MODEL), lambda i, j: (i, 0)),
                  pl.BlockSpec((1, 1, D_MODEL), lambda i, j: (row_of_tile(i, tm), 0, sh_blk)),
                  pl.BlockSpec((1, 1, D_MODEL), lambda i, j: (row_of_tile(i, tm), 0, sc_blk)),
                  pl.BlockSpec((D_MODEL, tn), lambda i, j: (0, j))],
        out_specs=pl.BlockSpec((tm, tn), lambda i, j: (i, j)),
        out_shape=jax.ShapeDtypeStruct((m, n), F32),
        scratch_shapes=[pltpu.VMEM((tm, D_MODEL), BF16)],
        compiler_params=_cparams("parallel", "arbitrary"),
        name="adaln_in_proj",
    )(x, mod3, mod3, w)


def _ctx_attn_kernel(q_ref, k_ref, v_ref, o_ref):
    scale = NA_HEAD_DIM ** -0.5
    for hh in range(2):
        sl = slice(hh * NA_HEAD_DIM, (hh + 1) * NA_HEAD_DIM)
        s = _dot_nt(q_ref[:, sl], k_ref[:, sl]) * scale
        m = jnp.max(s, axis=-1, keepdims=True)
        p = jnp.exp(s - m)
        l = jnp.sum(p, axis=-1, keepdims=True)
        o_ref[:, sl] = _dot(p, v_ref[:, sl]) / l


def _ctx_attention(z, n_seq, seq_len):
    cb = OFF_NA // 128
    return pl.pallas_call(
        _ctx_attn_kernel,
        grid=(n_seq, NA_HEADS // 2),
        in_specs=[pl.BlockSpec((seq_len, 128), lambda b, h: (b, cb + h)),
                  pl.BlockSpec((seq_len, 128), lambda b, h: (b, cb + 4 + h)),
                  pl.BlockSpec((seq_len, 128), lambda b, h: (b, cb + 8 + h))],
        out_specs=pl.BlockSpec((seq_len, 128), lambda b, h: (b, h)),
        out_shape=jax.ShapeDtypeStruct((n_seq * seq_len, NA_HEADS * NA_HEAD_DIM), F32),
        compiler_params=_cparams("parallel", "parallel"),
        name="ctx_attention",
    )(z, z, z)


def _na_bias_kernel(rpb_ref, rowsel_ref, colsel_ref, neg_ref, o_ref):
    picked = _dot_exact_lhs(rowsel_ref[...], rpb_ref[0])
    o_ref[0] = _dot_exact_rhs(picked, colsel_ref[...]) + neg_ref[...]


def _na_bias_table(rpb, rows):
    kr = min(WIN_R, rows)
    nh, n_dr, n_dc = rpb.shape
    rep = np.array([0, 1, 2, 3, 4, rows - 3, rows - 2, rows - 1])
    start = np.clip(rep - kr // 2, 0, rows - kr)
    dr = start[:, None] - rep[:, None] + WIN_R - 1 + np.arange(kr)[None, :]
    qcol = np.arange(GRID_W)
    kcol = np.arange(GRID_W)
    q_start = np.clip(qcol - WIN_C // 2, 0, GRID_W - WIN_C)
    rel = kcol[None, :] - q_start[:, None]
    valid = (rel >= 0) & (rel < WIN_C)
    dc = np.clip(kcol[None, :] - qcol[:, None], -(WIN_C - 1), WIN_C - 1) + WIN_C - 1
    row_sel = np.zeros((8 * kr, 16), np.float32)
    row_sel[np.arange(8 * kr), dr.reshape(-1)] = 1.0
    col_sel = np.zeros((32, GRID_W * GRID_W), np.float32)
    qq, kk = np.nonzero(valid)
    col_sel[dc[qq, kk], qq * GRID_W + kk] = 1.0
    neg = np.where(valid, 0.0, NEG_INF).astype(np.float32).reshape(1, -1)
    rpb_p = jnp.pad(rpb.astype(F32), ((0, 0), (0, 16 - n_dr), (0, 32 - n_dc)))
    full = lambda a: pl.BlockSpec(a.shape, lambda h: (0,) * a.ndim)
    consts = [jnp.asarray(row_sel, BF16), jnp.asarray(col_sel, BF16), jnp.asarray(neg)]
    b = pl.pallas_call(
        _na_bias_kernel,
        grid=(nh,),
        in_specs=[pl.BlockSpec((1, 16, 32), lambda h: (h, 0, 0))] + [full(a) for a in consts],
        out_specs=pl.BlockSpec((1, 8 * kr, GRID_W * GRID_W), lambda h: (h, 0, 0)),
        out_shape=jax.ShapeDtypeStruct((nh, 8 * kr, GRID_W * GRID_W), F32),
        compiler_params=_cparams("parallel"),
        name="na_bias",
    )(rpb_p, *consts)
    b = b.reshape(nh, 8, kr, GRID_W, GRID_W)
    return b.transpose(0, 1, 3, 2, 4).reshape(nh, 8, GRID_W, kr * GRID_W)


def _na_latent_kernel(q_ref, k_ref, v_ref, ck_ref, cv_ref, b_ref, o_ref, *, rows):
    scale = NA_HEAD_DIM ** -0.5
    kr = min(WIN_R, rows)

    def body(r, carry):
        start = jnp.clip(r - kr // 2, 0, rows - kr)
        typ = jnp.where(r < 4, r, jnp.where(r > rows - 4, r - (rows - 8), 4))
        q0 = pl.multiple_of(r * GRID_W, GRID_W)
        k0 = pl.multiple_of(start * GRID_W, GRID_W)
        for hh in range(2):
            sl = slice(hh * NA_HEAD_DIM, (hh + 1) * NA_HEAD_DIM)
            q = q_ref[pl.ds(q0, GRID_W), sl]
            kw = k_ref[pl.ds(k0, kr * GRID_W), sl]
            vw = v_ref[pl.ds(k0, kr * GRID_W), sl]
            s_w = _dot_nt(q, kw) * scale + b_ref[hh, typ]
            s_c = _dot_nt(q, ck_ref[0, hh]) * scale
            m = jnp.maximum(jnp.max(s_w, axis=-1, keepdims=True), jnp.max(s_c, axis=-1, keepdims=True))
            p_w = jnp.exp(s_w - m)
            p_c = jnp.exp(s_c - m)
            l = jnp.sum(p_w, axis=-1, keepdims=True) + jnp.sum(p_c, axis=-1, keepdims=True)
            o_ref[pl.ds(q0, GRID_W), sl] = (_dot(p_w, vw) + _dot(p_c, cv_ref[0, hh])) / l
        return carry

    lax.fori_loop(0, rows, body, 0)


def _na_latent(z, ctx_k, ctx_v, bias, n_seq, seq_len, row_blk0):
    cb = OFF_NA // 128
    rows = seq_len // GRID_W
    past = ctx_k.shape[2]
    return pl.pallas_call(
        functools.partial(_na_latent_kernel, rows=rows),
        grid=(n_seq, NA_HEADS // 2),
        in_specs=[pl.BlockSpec((seq_len, 128), lambda b, h: (row_blk0 + b, cb + h)),
                  pl.BlockSpec((seq_len, 128), lambda b, h: (row_blk0 + b, cb + 4 + h)),
                  pl.BlockSpec((seq_len, 128), lambda b, h: (row_blk0 + b, cb + 8 + h)),
                  pl.BlockSpec((1, 2, past, NA_HEAD_DIM), lambda b, h: (b, h, 0, 0)),
                  pl.BlockSpec((1, 2, past, NA_HEAD_DIM), lambda b, h: (b, h, 0, 0)),
                  pl.BlockSpec((2, 8, GRID_W, bias.shape[-1]), lambda b, h: (h, 0, 0, 0))],
        out_specs=pl.BlockSpec((seq_len, 128), lambda b, h: (b, h)),
        out_shape=jax.ShapeDtypeStruct((n_seq * seq_len, NA_HEADS * NA_HEAD_DIM), F32),
        compiler_params=_cparams("parallel", "parallel"),
        name="na_latent",
    )(z, z, z, ctx_k, ctx_v, bias)


def _ret_decay_tables(decay_logit):
    c = RET_CHUNK
    lg = jnp.log(jax.nn.sigmoid(decay_logit.astype(F32)))[:, :, None, None]
    i = jnp.arange(c, dtype=F32)
    diff = i[:, None] - i[None, :]
    m_f = jnp.where(diff >= 0, jnp.exp(jnp.maximum(diff, 0.0) * lg[0]), 0.0)
    m_b = jnp.where(diff <= 0, jnp.exp(jnp.maximum(-diff, 0.0) * lg[1]), 0.0)
    mask = jnp.stack([m_f, m_b])
    ones = jnp.ones((1, 1, 1, RET_DK), F32)
    q_f = jnp.exp((i + 1.0)[None, :, None] * lg[0]) * ones[0]
    q_b = jnp.exp((c - i)[None, :, None] * lg[1]) * ones[0]
    k_f = jnp.exp((c - 1.0 - i)[None, :, None] * lg[0]) * ones[0]
    k_b = jnp.exp(i[None, :, None] * lg[1]) * ones[0]
    c_dec = jnp.exp(c * lg) * jnp.ones((1, 1, RET_DK, RET_DV), F32)
    return mask, jnp.stack([q_f, q_b]), jnp.stack([k_f, k_b]), c_dec


def _rope_tables(seq_len):
    quarter = RET_DK // 4
    pos = np.arange(seq_len)
    freqs = 1.0 / (ROPE_BASE ** (jnp.arange(quarter, dtype=F32) / quarter))
    a_row = jnp.asarray(pos // GRID_W, F32)[:, None] * freqs[None, :]
    a_col = jnp.asarray(pos % GRID_W, F32)[:, None] * freqs[None, :]
    cos = jnp.concatenate([jnp.cos(a_row)] * 2 + [jnp.cos(a_col)] * 2, axis=-1)
    sin = jnp.concatenate([-jnp.sin(a_row), jnp.sin(a_row), -jnp.sin(a_col), jnp.sin(a_col)], axis=-1)
    return jnp.tile(cos, (1, RET_HEADS)), jnp.tile(sin, (1, RET_HEADS))


def _retention_kernel(*refs, seq_len, rope):
    if rope:
        (q_ref, k_ref, v_ref, g_ref, s0_ref, mask_ref, qd_ref, kd_ref, cd_ref, gnw_ref, gnb_ref,
         cos_ref, sin_ref, o_ref, sfin_ref, qs_ref, ks_ref, acc_ref) = refs
    else:
        (q_ref, k_ref, v_ref, g_ref, s0_ref, mask_ref, qd_ref, kd_ref, cd_ref, gnw_ref, gnb_ref,
         o_ref, sfin_ref, qs_ref, ks_ref, acc_ref) = refs
    c = RET_CHUNK
    n = seq_len // c
    quarter = RET_DK // 4
    q = q_ref[...]
    k = k_ref[...] * (RET_DK ** -0.5)
    if rope:
        lane = lax.broadcasted_iota(jnp.int32, q.shape, 1)
        first = (lane % (2 * quarter)) < quarter
        width = q.shape[1]

        def rot(x):
            swapped = jnp.where(first, pltpu.roll(x, width - quarter, 1), pltpu.roll(x, quarter, 1))
            return x * cos_ref[...] + swapped * sin_ref[...]

        q = rot(q)
        k = rot(k)
    qs_ref[...] = q
    ks_ref[...] = k

    for h in range(RET_HEADS):
        ksl = slice(h * RET_DK, (h + 1) * RET_DK)
        vsl = slice(h * RET_DV, (h + 1) * RET_DV)
        for d in range(2):
            mask = mask_ref[d, h]
            qd = qd_ref[d, h]
            kd = kd_ref[d, h]
            cd = cd_ref[d, h]

            def step(ci, s, ksl=ksl, vsl=vsl, d=d, mask=mask, qd=qd, kd=kd, cd=cd):
                cc = ci if d == 0 else n - 1 - ci
                r0 = pl.multiple_of(cc * c, c)
                qc = qs_ref[pl.ds(r0, c), ksl]
                kc = ks_ref[pl.ds(r0, c), ksl]
                vc = v_ref[pl.ds(r0, c), vsl]
                att = _dot_nt(qc, kc) * mask
                o = _dot(att, vc) + _dot(qc * qd, s)
                if d == 0:
                    acc_ref[pl.ds(r0, c), vsl] = o
                else:
                    acc_ref[pl.ds(r0, c), vsl] += o
                return s * cd + _dot_tn(kc * kd, vc)

            s_fin = lax.fori_loop(0, n, step, s0_ref[0, d, h])
            sfin_ref[0, d, h] = s_fin

    for h in range(RET_HEADS):
        vsl = slice(h * RET_DV, (h + 1) * RET_DV)
        y = _ln(acc_ref[:, vsl], RET_GN_EPS) * gnw_ref[:, vsl] + gnb_ref[:, vsl]
        g = g_ref[:, vsl]
        o_ref[:, vsl] = y * (g * _sigmoid(g))


def _retention(z, s0, tables, gn_w, gn_b, n_seq, seq_len, row_blk0, rope_tabs):
    mask, qd, kd, cd = tables
    rope = rope_tabs is not None
    qk_w = RET_HEADS * RET_DK
    v_w = RET_HEADS * RET_DV
    full = lambda a: pl.BlockSpec(a.shape, lambda b: (0,) * a.ndim)
    in_specs = [pl.BlockSpec((seq_len, qk_w), lambda b: (row_blk0 + b, OFF_RET_QK // qk_w)),
                pl.BlockSpec((seq_len, qk_w), lambda b: (row_blk0 + b, OFF_RET_QK // qk_w + 1)),
                pl.BlockSpec((seq_len, v_w), lambda b: (row_blk0 + b, OFF_RET_V // v_w)),
                pl.BlockSpec((seq_len, v_w), lambda b: (row_blk0 + b, OFF_RET_V // v_w + 1)),
                pl.BlockSpec((1, 2, RET_HEADS, RET_DK, RET_DV), lambda b: (b, 0, 0, 0, 0)),
                full(mask), full(qd), full(kd), full(cd), full(gn_w), full(gn_b)]
    args = [z, z, z, z, s0, mask, qd, kd, cd, gn_w, gn_b]
    if rope:
        in_specs += [full(rope_tabs[0]), full(rope_tabs[1])]
        args += list(rope_tabs)
    return pl.pallas_call(
        functools.partial(_retention_kernel, seq_len=seq_len, rope=rope),
        grid=(n_seq,),
        in_specs=in_specs,
        out_specs=[pl.BlockSpec((seq_len, v_w), lambda b: (b, 0)),
                   pl.BlockSpec((1, 2, RET_HEADS, RET_DK, RET_DV), lambda b: (b, 0, 0, 0, 0))],
        out_shape=[jax.ShapeDtypeStruct((n_seq * seq_len, v_w), F32),
                   jax.ShapeDtypeStruct((n_seq, 2, RET_HEADS, RET_DK, RET_DV), F32)],
        scratch_shapes=[pltpu.VMEM((seq_len, qk_w), F32), pltpu.VMEM((seq_len, qk_w), F32),
                        pltpu.VMEM((seq_len, v_w), F32)],
        compiler_params=_cparams("parallel"),
        name="retention_rope" if rope else "retention",
    )(*args)


def _rwkv_prep_kernel(zm_ref, zl_ref, pm_ref, nm_ref, pl_ref, nl_ref, mum_ref, mul_ref,
                      w0_ref, wup_ref, a0_ref, aup_ref, gup_ref, kk_ref, ka_ref, rk_ref, ones_ref, tri_ref,
                      r_out, kap_out, v_out, lw_out, cs_out, kd_out, beta_out, g_out, bonus_out,
                      *, tiles_per_seq_of_tile):
    i = pl.program_id(0)
    pos, per = tiles_per_seq_of_tile(i)
    has_prev = (pos != 0).astype(F32)
    has_next = (pos != per - 1).astype(F32)
    tt = zm_ref.shape[0]

    def shift(z_ref, p_ref, n_ref, mu_ref):
        z = z_ref[...]
        row = lax.broadcasted_iota(jnp.int32, z.shape, 0)
        prev = jnp.where(row == 0, p_ref[7:8, :] * has_prev, pltpu.roll(z, 1, 0))
        nxt = jnp.where(row == tt - 1, n_ref[0:1, :] * has_next, pltpu.roll(z, tt - 1, 0))
        return z + mu_ref[...] * (0.5 * (prev + nxt) - z)

    zm = shift(zm_ref, pm_ref, nm_ref, mum_ref)
    zl = shift(zl_ref, pl_ref, nl_ref, mul_ref)
    w = RWKV_WIDTH
    r_c, k_c, v_c = zm[:, 0:w], zm[:, w:2 * w], zm[:, 2 * w:3 * w]
    ones = ones_ref[...]
    g_out[...] = _dot(_sigmoid(zl[:, 256:384]), gup_ref[...])
    kk = k_c * kk_ref[...]
    ss = _dot_exact_rhs(kk * kk, ones)
    kap = kk * lax.rsqrt(jnp.maximum(ss, 1e-24))
    bonus_out[...] = _dot_exact_rhs(r_c * k_c * rk_ref[...], ones) * v_c
    for h in range(RWKV_HEADS):
        sl = slice(h * RWKV_N, (h + 1) * RWKV_N)
        r_out[h] = r_c[:, sl]
        kap_out[h] = kap[:, sl]
        v_out[h] = v_c[:, sl]
    for d in range(2):
        wl = zl[:, d * 64:(d + 1) * 64]
        al = zl[:, 128 + d * 64:128 + (d + 1) * 64]
        lw = -RWKV_DECAY_SCALE * _sigmoid(w0_ref[d] + _dot(jnp.tanh(wl), wup_ref[d]))
        a = _sigmoid(a0_ref[d] + _dot(al, aup_ref[d]))
        k_d = k_c * (1.0 + (a - 1.0) * ka_ref[...])
        beta = kap * a
        cs = _dot_exact_lhs(tri_ref[d], lw)
        for h in range(RWKV_HEADS):
            sl = slice(h * RWKV_N, (h + 1) * RWKV_N)
            lw_out[d, h] = lw[:, sl]
            cs_out[d, h] = cs[:, sl]
            kd_out[d, h] = k_d[:, sl]
            beta_out[d, h] = beta[:, sl]


def _rwkv_prep(z, p, tiles_per_seq_of_tile, tt):
    m = z.shape[0]
    nt = m // tt
    w = RWKV_WIDTH
    mb, lb = OFF_RW_MAIN // (3 * w), OFF_RW_LORA // 384
    hm = tt // 8
    last8 = m // 8 - 1
    prev_idx = lambda i: jnp.maximum(i * hm - 1, 0)
    next_idx = lambda i: jnp.minimum((i + 1) * hm, last8)
    full = lambda a: pl.BlockSpec(a.shape, lambda i: (0,) * a.ndim)
    heads = jnp.arange(w) // RWKV_N
    ones = (heads[:, None] == heads[None, :]).astype(BF16)
    t = np.arange(tt)
    same = (t[:, None] // RWKV_CHUNK) == (t[None, :] // RWKV_CHUNK)
    tri = jnp.asarray(np.stack([same & (t[None, :] <= t[:, None]), same & (t[None, :] >= t[:, None])]), BF16)
    params = [p['mu_main'], p['mu_lora'], p['rwkv_w0'], p['rwkv_w_up'], p['rwkv_a0'], p['rwkv_a_up'],
              p['rwkv_g_up'], p['rwkv_k_k'], p['rwkv_k_a'], p['rwkv_r_k'], ones, tri]
    hd = lambda: jax.ShapeDtypeStruct((RWKV_HEADS, m, RWKV_N), F32)
    dhd = lambda: jax.ShapeDtypeStruct((2, RWKV_HEADS, m, RWKV_N), F32)
    tok = lambda: jax.ShapeDtypeStruct((m, w), F32)
    hd_spec = pl.BlockSpec((RWKV_HEADS, tt, RWKV_N), lambda i: (0, i, 0))
    dhd_spec = pl.BlockSpec((2, RWKV_HEADS, tt, RWKV_N), lambda i: (0, 0, i, 0))
    tok_spec = pl.BlockSpec((tt, w), lambda i: (i, 0))
    return pl.pallas_call(
        functools.partial(_rwkv_prep_kernel, tiles_per_seq_of_tile=tiles_per_seq_of_tile),
        grid=(nt,),
        in_specs=[pl.BlockSpec((tt, 3 * w), lambda i: (i, mb)),
                  pl.BlockSpec((tt, 384), lambda i: (i, lb)),
                  pl.BlockSpec((8, 3 * w), lambda i: (prev_idx(i), mb)),
                  pl.BlockSpec((8, 3 * w), lambda i: (next_idx(i), mb)),
                  pl.BlockSpec((8, 384), lambda i: (prev_idx(i), lb)),
                  pl.BlockSpec((8, 384), lambda i: (next_idx(i), lb))] + [full(a) for a in params],
        out_specs=[hd_spec, hd_spec, hd_spec, dhd_spec, dhd_spec, dhd_spec, dhd_spec, tok_spec, tok_spec],
        out_shape=[hd(), hd(), hd(), dhd(), dhd(), dhd(), dhd(), tok(), tok()],
        compiler_params=_cparams("parallel"),
        name="rwkv_prep",
    )(z, z, z, z, z, z, *params)


def _dot3(a, b, dims):
    a_hi = a.astype(BF16)
    b_hi = b.astype(BF16)
    a_lo = (a - a_hi.astype(F32)).astype(BF16)
    b_lo = (b - b_hi.astype(F32)).astype(BF16)
    dn = (dims, ((), ()))
    out = lax.dot_general(a_lo, b_hi, dn, preferred_element_type=F32)
    out = out + lax.dot_general(a_hi, b_lo, dn, preferred_element_type=F32)
    return out + lax.dot_general(a_hi, b_hi, dn, preferred_element_type=F32)


_NN = ((1,), (0,))
_NT = ((1,), (1,))
_TN = ((0,), (0,))


def _dot1(a, b, dims):
    return lax.dot_general(a.astype(BF16), b.astype(BF16), (dims, ((), ())), preferred_element_type=F32)


_RWKV_MM = _dot1
RWKV_CHAINS = 16
RWKV_MAX_GROUP = 8


def _rwkv_scan_kernel(r_ref, kap_ref, v_ref, lw_ref, cs_ref, kd_ref, beta_ref, h0_ref, o_ref, hfin_ref, *,
                      seq_len, group, heads):
    c = RWKV_CHUNK
    n = seq_len // c
    d = pl.program_id(0)
    sgn = 1 - 2 * d
    ti = lax.broadcasted_iota(jnp.int32, (c, c), 0) * sgn
    tj = lax.broadcasted_iota(jnp.int32, (c, c), 1) * sgn
    strict = tj < ti
    incl = tj <= ti
    eye_f = (ti == tj).astype(F32)

    def chunk_terms(hh, cc):
        rows = pl.ds(pl.multiple_of(cc * c, c), c)
        r = r_ref[hh, rows, :]
        kap = kap_ref[hh, rows, :]
        v = v_ref[hh, rows, :]
        lw = lw_ref[0, hh, rows, :]
        cs = cs_ref[0, hh, rows, :]
        kd = kd_ref[0, hh, rows, :]
        beta = beta_ref[0, hh, rows, :]
        tot = jnp.sum(lw, axis=0, keepdims=True)
        kap_t = kap * jnp.exp(cs - lw)
        r_t = r * jnp.exp(cs)
        e_ncs = jnp.exp(-cs)
        k_t = kd * e_ncs
        b_t = beta * e_ncs
        e_end = jnp.exp(tot - cs)
        k_end = kd * e_end
        b_end = beta * e_end
        yield
        l_b = jnp.where(strict, _RWKV_MM(kap_t, b_t, _NT), 0.0)
        l_k = jnp.where(strict, _RWKV_MM(kap_t, k_t, _NT), 0.0)
        m_b = jnp.where(incl, _RWKV_MM(r_t, b_t, _NT), 0.0)
        m_k = jnp.where(incl, _RWKV_MM(r_t, k_t, _NT), 0.0)
        yield
        x = -l_b
        y = jnp.concatenate([kap_t, _RWKV_MM(l_k, v, _NN)], axis=1)
        yield
        y = y + _RWKV_MM(x, y, _NN)
        pw = x
        for _ in range(5):
            pw = _RWKV_MM(pw, pw, _NN)
            yield
            y = y + _RWKV_MM(pw, y, _NN)
        yield
        mby = _RWKV_MM(m_b, y, _NN)
        bty = _RWKV_MM(b_end, y, _TN)
        rr = r_t - mby[:, :RWKV_N]
        o0 = _RWKV_MM(m_k, v, _NN) - mby[:, RWKV_N:]
        pc = eye_f * jnp.exp(tot) - bty[:, :RWKV_N]
        qc = _RWKV_MM(k_end, v, _TN) - bty[:, RWKV_N:]
        return rows, rr, o0, pc, qc

    def step(gi, hstates):
        keys = [(hh, u) for u in range(group) for hh in range(heads)]
        gens = {(hh, u): chunk_terms(hh, (gi * group + u) + d * (n - 1 - 2 * (gi * group + u))) for hh, u in keys}
        terms = {}
        live = list(keys)
        while live:
            for key in list(live):
                try:
                    next(gens[key])
                except StopIteration as done:
                    terms[key] = done.value
                    live.remove(key)
        hstates = list(hstates)
        for u in range(group):
            for hh in range(heads):
                rows, rr, o0, pc, qc = terms[(hh, u)]
                o_ref[0, hh, rows, :] = _dot3(rr, hstates[hh], _NN) + o0
                hstates[hh] = _dot3(pc, hstates[hh], _NN) + qc
        return tuple(hstates)

    fin = lax.fori_loop(0, n // group, step, tuple(h0_ref[0, 0, hh] for hh in range(heads)))
    for hh in range(heads):
        hfin_ref[0, 0, hh] = fin[hh]


def _rwkv_scan(r, kap, v, lw, cs, kd, beta, h0, n_seq, seq_len, row_blk0):
    group = min(seq_len // RWKV_CHUNK, RWKV_MAX_GROUP)
    hp = RWKV_CHAINS // group
    hd_spec = pl.BlockSpec((hp, seq_len, RWKV_N), lambda d, h, b: (h, row_blk0 + b, 0))
    dhd_spec = pl.BlockSpec((1, hp, seq_len, RWKV_N), lambda d, h, b: (d, h, row_blk0 + b, 0))
    st_spec = pl.BlockSpec((1, 1, hp, RWKV_N, RWKV_N), lambda d, h, b: (b, d, h, 0, 0))
    return pl.pallas_call(
        functools.partial(_rwkv_scan_kernel, seq_len=seq_len, group=group, heads=hp),
        grid=(2, RWKV_HEADS // hp, n_seq),
        in_specs=[hd_spec, hd_spec, hd_spec, dhd_spec, dhd_spec, dhd_spec, dhd_spec, st_spec],
        out_specs=[pl.BlockSpec((1, hp, seq_len, RWKV_N), lambda d, h, b: (d, h, b, 0)), st_spec],
        out_shape=[jax.ShapeDtypeStruct((2, RWKV_HEADS, n_seq * seq_len, RWKV_N), F32),
                   jax.ShapeDtypeStruct((n_seq, 2, RWKV_HEADS, RWKV_N, RWKV_N), F32)],
        compiler_params=_cparams("parallel", "parallel", "parallel"),
        name="rwkv_scan",
    )(r, kap, v, lw, cs, kd, beta, h0)


def _merge_kernel(oscan_ref, bonus_ref, grw_ref, ona_ref, ort_ref, ga_ref, gb_ref, gc_ref, x_ref, gt_ref,
                  gnw_ref, gnb_ref, wbr_ref, wout_ref, lng_ref, lnb_ref, o_ref, orw_ref):
    for h in range(RWKV_HEADS):
        sl = slice(h * RWKV_N, (h + 1) * RWKV_N)
        y = _ln(oscan_ref[0, h] + oscan_ref[1, h], RWKV_GN_EPS) * gnw_ref[h] + gnb_ref[h]
        orw_ref[:, sl] = (y + bonus_ref[:, sl]) * grw_ref[:, sl]
    merged = (_sigmoid(ga_ref[...]) * _dot(ona_ref[...], wbr_ref[0])
              + _sigmoid(gb_ref[...]) * _dot(ort_ref[...], wbr_ref[1])
              + _sigmoid(gc_ref[...]) * _dot(orw_ref[...], wbr_ref[2]))
    t = _dot(merged, wout_ref[...])
    o_ref[...] = _ln(DEEPNORM_ALPHA * x_ref[...] + gt_ref[0] * t) * lng_ref[...] + lnb_ref[...]


def _merge(oscan, bonus, g_rw, o_na, o_rt, z, x, mod3, p, row_of_tile, tm):
    m = x.shape[0]
    bw = RWKV_WIDTH
    full = lambda a: pl.BlockSpec(a.shape, lambda i: (0,) * a.ndim)
    tok = pl.BlockSpec((tm, bw), lambda i: (i, 0))
    params = [p['rwkv_gn_w'], p['rwkv_gn_b'], p['w_br'], p['w_out'], p['ln_a_g'], p['ln_a_b']]
    return pl.pallas_call(
        _merge_kernel,
        grid=(m // tm,),
        in_specs=[pl.BlockSpec((2, RWKV_HEADS, tm, RWKV_N), lambda i: (0, 0, i, 0)),
                  tok, tok, tok, tok,
                  pl.BlockSpec((tm, D_MODEL), lambda i: (i, 0)),
                  pl.BlockSpec((tm, D_MODEL), lambda i: (i, 1)),
                  pl.BlockSpec((tm, D_MODEL), lambda i: (i, 2)),
                  pl.BlockSpec((tm, D_MODEL), lambda i: (i, 0)),
                  pl.BlockSpec((1, 1, D_MODEL), lambda i: (row_of_tile(i, tm), 0, 2))]
                 + [full(a) for a in params],
        out_specs=pl.BlockSpec((tm, D_MODEL), lambda i: (i, 0)),
        out_shape=jax.ShapeDtypeStruct((m, D_MODEL), F32),
        scratch_shapes=[pltpu.VMEM((tm, bw), F32)],
        compiler_params=_cparams("parallel"),
        name="merge_out_proj",
    )(oscan, bonus, g_rw, o_na, o_rt, z, z, z, x, mod3, *params)


def _top_values(s, k):
    out = []
    for _ in range(k):
        m = jnp.max(s, axis=0, keepdims=True)
        out.append(m)
        s = jnp.where(s == m, -jnp.inf, s)
    return out


def _peer_gate_kernel(x_ref, sh_ref, sc_ref, wq_ref, keys_ref, ht_ref, cut_ref, s1_ref, e0_ref, e1_ref):
    h = _ln(x_ref[...]) * (1.0 + sc_ref[0]) + sh_ref[0]
    ht = h.T.astype(BF16)
    ht_ref[...] = ht
    qt = jnp.dot(wq_ref[...], ht, preferred_element_type=F32)
    half = keys_ref.shape[-1]
    for hd in range(PEER_HEADS):
        scores, tops = [], []
        for part in range(2):
            row0 = (hd * 2 + part) * half
            s = _dot(keys_ref[hd, part], qt[row0:row0 + half, :])
            scores.append(s)
            tops.append(_top_values(s, PEER_TOPK))
        cand = jnp.concatenate([tops[0][a] + jnp.concatenate(tops[1], axis=0) for a in range(PEER_TOPK)],
                               axis=0)
        best = _top_values(cand, PEER_TOPK)
        zsum = jnp.exp(best[0] - best[0])
        for t in range(1, PEER_TOPK):
            zsum = zsum + jnp.exp(best[t] - best[0])
        tau = best[PEER_TOPK - 1]
        cut = jnp.full(scores[0].shape, jnp.inf, F32)
        for t1 in tops[1]:
            cut = jnp.where(scores[0] + t1 >= tau, t1, cut)
        cut_ref[hd] = cut
        s1_ref[hd] = scores[1]
        e0_ref[hd] = jnp.exp(scores[0] - tops[0][0])
        e1_ref[hd] = jnp.exp(scores[1] - tops[1][0]) / zsum


def _peer_gate(x, mod3, wq_t, keys, row_of_tile, tn):
    m = x.shape[0]
    full = lambda a: pl.BlockSpec(a.shape, lambda i: (0,) * a.ndim)
    sk = pl.BlockSpec((PEER_HEADS, PEER_NKEYS, tn), lambda i: (0, 0, i))
    sk_shape = jax.ShapeDtypeStruct((PEER_HEADS, PEER_NKEYS, m), F32)
    return pl.pallas_call(
        _peer_gate_kernel,
        grid=(m // tn,),
        in_specs=[pl.BlockSpec((tn, D_MODEL), lambda i: (i, 0)),
                  pl.BlockSpec((1, 1, D_MODEL), lambda i: (row_of_tile(i, tn), 0, 3)),
                  pl.BlockSpec((1, 1, D_MODEL), lambda i: (row_of_tile(i, tn), 0, 4)),
                  full(wq_t), full(keys)],
        out_specs=[pl.BlockSpec((D_MODEL, tn), lambda i: (0, i)), sk, sk, sk, sk],
        out_shape=[jax.ShapeDtypeStruct((D_MODEL, m), BF16), sk_shape, sk_shape, sk_shape, sk_shape],
        compiler_params=_cparams("parallel"),
        name="peer_gate",
    )(x, mod3, mod3, wq_t, keys)


def _gelu_exact(x):
    return 0.5 * x * (1.0 + lax.erf(x * (2.0 ** -0.5)))


PEER_SUB_ROWS = 32
PEER_SUB_LANES = 256


def _peer_expert_kernel(ht_ref, cut_ref, s1_ref, e0_ref, e1_ref, u_ref, vt_ref, x_ref, gt_ref,
                        lng_ref, lnb_ref, o_ref, acc_ref, st_ref, st2_ref, act_ref, *, groups):
    e = pl.program_id(1)
    tn = ht_ref.shape[1]

    @pl.when(e == 0)
    def _():
        acc_ref[...] = jnp.zeros_like(acc_ref)

    def pre_activations(g, dst_ref):
        rows = pl.ds(pl.multiple_of(g * PEER_NKEYS, PEER_NKEYS), PEER_NKEYS)
        dst_ref[...] = jnp.dot(u_ref[rows, :], ht_ref[...], preferred_element_type=F32)

    def gated_activations(g, src_ref):
        i = e * groups + g
        row0 = pl.multiple_of(g * PEER_NKEYS, PEER_NKEYS)
        cut_row = [cut_ref[hd, pl.ds(i, 1), :] for hd in range(PEER_HEADS)]
        e0_row = [e0_ref[hd, pl.ds(i, 1), :] for hd in range(PEER_HEADS)]
        for lb in range(tn // PEER_SUB_LANES):
            ls = slice(lb * PEER_SUB_LANES, (lb + 1) * PEER_SUB_LANES)
            cut = [c[:, ls] for c in cut_row]
            e0 = [c[:, ls] for c in e0_row]
            for rb in range(PEER_NKEYS // PEER_SUB_ROWS):
                js = slice(rb * PEER_SUB_ROWS, (rb + 1) * PEER_SUB_ROWS)
                gate = None
                for hd in range(PEER_HEADS):
                    term = jnp.where(s1_ref[hd, js, ls] >= cut[hd], e1_ref[hd, js, ls], 0.0) * e0[hd]
                    gate = term if gate is None else gate + term
                act = _gelu_exact(src_ref[js, ls]) * gate
                act_ref[pl.ds(row0 + rb * PEER_SUB_ROWS, PEER_SUB_ROWS), ls] = act.astype(BF16)

    pre_activations(0, st_ref)

    def pair(p, carry):
        gated_activations(2 * p, st_ref)
        pre_activations(2 * p + 1, st2_ref)
        gated_activations(2 * p + 1, st2_ref)
        pre_activations(jnp.minimum(2 * p + 2, groups - 1), st_ref)
        return carry

    lax.fori_loop(0, groups // 2, pair, 0)
    acc_ref[...] += jnp.dot(vt_ref[...], act_ref[...], preferred_element_type=F32)

    @pl.when(e == pl.num_programs(1) - 1)
    def _():
        f = acc_ref[...].T
        o_ref[...] = _ln(DEEPNORM_ALPHA * x_ref[...] + gt_ref[0] * f) * lng_ref[...] + lnb_ref[...]


def _peer_expert(ht, cut, s1, e0, e1, u, v_t, x, mod3, ln_g, ln_b, row_of_tile, tn, eb):
    m = x.shape[0]
    n_exp = u.shape[0]
    groups = eb // PEER_NKEYS
    sk = pl.BlockSpec((PEER_HEADS, PEER_NKEYS, tn), lambda i, e: (0, 0, i))
    full = lambda a: pl.BlockSpec(a.shape, lambda i, e: (0,) * a.ndim)
    return pl.pallas_call(
        functools.partial(_peer_expert_kernel, groups=groups),
        grid=(m // tn, n_exp // eb),
        in_specs=[pl.BlockSpec((D_MODEL, tn), lambda i, e: (0, i)), sk, sk, sk, sk,
                  pl.BlockSpec((eb, D_MODEL), lambda i, e: (e, 0)),
                  pl.BlockSpec((D_MODEL, eb), lambda i, e: (0, e)),
                  pl.BlockSpec((tn, D_MODEL), lambda i, e: (i, 0)),
                  pl.BlockSpec((1, 1, D_MODEL), lambda i, e: (row_of_tile(i, tn), 0, 5)),
                  full(ln_g), full(ln_b)],
        out_specs=pl.BlockSpec((tn, D_MODEL), lambda i, e: (i, 0)),
        out_shape=jax.ShapeDtypeStruct((m, D_MODEL), F32),
        scratch_shapes=[pltpu.VMEM((D_MODEL, tn), F32), pltpu.VMEM((PEER_NKEYS, tn), F32),
                        pltpu.VMEM((PEER_NKEYS, tn), F32), pltpu.VMEM((eb, tn), BF16)],
        compiler_params=_cparams("parallel", "arbitrary"),
        name="peer_experts",
    )(ht, cut, s1, e0, e1, u, v_t, x, mod3, ln_g, ln_b)


def _permute_in_proj(w_in):
    na, qk, rv = 3 * 512, 2 * 256, 2 * 512
    o_na, o_qk, o_rv, o_rw, o_g = 0, na, na + qk, na + qk + rv, na + qk + rv + 1920
    return jnp.concatenate([w_in[:, o_g:], w_in[:, o_na:o_qk], w_in[:, o_qk:o_rv], w_in[:, o_rv:o_rw],
                            w_in[:, o_rw:o_g]], axis=1)


def kernel(x_prompt, x_sample, cache_na_k, cache_na_v, state_ret, state_rwkv, c, c_ctx, w_mod, b_mod, w_in, na_rpb, ret_decay_logit, ret_gn_w, ret_gn_b, rwkv_mu, rwkv_w0, rwkv_w_up, rwkv_a0, rwkv_a_up, rwkv_g_up, rwkv_k_k, rwkv_k_a, rwkv_r_k, rwkv_gn_w, rwkv_gn_b, w_br, w_out, ln_a_g, ln_a_b, ln_f_g, ln_f_b, peer_wq, peer_keys, peer_u, peer_v):
    n_ctx, ctx_len, d = x_prompt.shape
    n_lat, lat_len, _ = x_sample.shape
    m_ctx = n_ctx * ctx_len
    assert m_ctx % lat_len == 0 and n_lat == 2
    lat_blk0 = m_ctx // lat_len

    def row_of_tile(i, tm):
        return jnp.where(i < m_ctx // tm, 0, 1 + (i - m_ctx // tm) // (lat_len // tm))

    prep_tt = 256
    assert ctx_len == prep_tt

    def tiles_per_seq_of_tile(i):
        lat = i >= m_ctx // prep_tt
        per = jnp.where(lat, lat_len // prep_tt, 1)
        pos = jnp.where(lat, (i - m_ctx // prep_tt) % (lat_len // prep_tt), 0)
        return pos, per

    x = jnp.concatenate([x_prompt.reshape(m_ctx, d), x_sample.reshape(n_lat * lat_len, d)], axis=0)
    c8 = jnp.concatenate([c_ctx[None], c, jnp.zeros((8 - 1 - n_lat, d), F32)], axis=0)
    mod = _modulation(c8, w_mod, b_mod)
    rope_tabs = _rope_tables(lat_len)
    zeros_ret = jnp.zeros((n_ctx, 2, RET_HEADS, RET_DK, RET_DV), F32)
    zeros_rw = jnp.zeros((n_ctx, 2, RWKV_HEADS, RWKV_N, RWKV_N), F32)

    nk, nv, sr, sw = [], [], [], []
    for l in range(DEPTH):
        mod3 = mod[l].reshape(8, 1, 6 * d)
        w_in_p = _permute_in_proj(w_in[l]).astype(BF16)
        z = _lnmod_matmul(x, mod3, w_in_p, row_of_tile, 1024, 1152, 0, 1)

        o_na = jnp.concatenate([
            _ctx_attention(z, n_ctx, ctx_len),
            _na_latent(z, cache_na_k[:, l], cache_na_v[:, l], _na_bias_table(na_rpb[l], lat_len // GRID_W),
                       n_lat, lat_len, lat_blk0)], axis=0)
        k_na = z[:m_ctx, OFF_NA + 512:OFF_NA + 1024].reshape(n_ctx, ctx_len, NA_HEADS, NA_HEAD_DIM)
        v_na = z[:m_ctx, OFF_NA + 1024:OFF_NA + 1536].reshape(n_ctx, ctx_len, NA_HEADS, NA_HEAD_DIM)
        nk.append(k_na.transpose(0, 2, 1, 3))
        nv.append(v_na.transpose(0, 2, 1, 3))

        tables = _ret_decay_tables(ret_decay_logit[l])
        gn_w, gn_b = ret_gn_w[l][None], ret_gn_b[l][None]
        o_rt_c, s_ret = _retention(z, zeros_ret, tables, gn_w, gn_b, n_ctx, ctx_len, 0, None)
        o_rt_l, _ = _retention(z, state_ret[:, l], tables, gn_w, gn_b, n_lat, lat_len, lat_blk0, rope_tabs)
        o_rt = jnp.concatenate([o_rt_c, o_rt_l], axis=0)
        sr.append(s_ret)

        mu = rwkv_mu[l]
        prm = {
            'mu_main': mu[None, :1536], 'mu_lora': mu[None, 1536:],
            'rwkv_w0': rwkv_w0[l][:, None], 'rwkv_w_up': rwkv_w_up[l].astype(BF16),
            'rwkv_a0': rwkv_a0[l][:, None], 'rwkv_a_up': rwkv_a_up[l].astype(BF16),
            'rwkv_g_up': rwkv_g_up[l].astype(BF16), 'rwkv_k_k': rwkv_k_k[l][None],
            'rwkv_k_a': rwkv_k_a[l][None], 'rwkv_r_k': rwkv_r_k[l].reshape(1, RWKV_WIDTH),
        }
        r, kap, v, lw, cs, kd, beta, g_rw, bonus = _rwkv_prep(z, prm, tiles_per_seq_of_tile, prep_tt)
        o_c, h_fin = _rwkv_scan(r, kap, v, lw, cs, kd, beta, zeros_rw, n_ctx, ctx_len, 0)
        h0_lat = jnp.swapaxes(state_rwkv[:, l], -1, -2)
        o_l, _ = _rwkv_scan(r, kap, v, lw, cs, kd, beta, h0_lat, n_lat, lat_len, lat_blk0)
        oscan = jnp.concatenate([o_c, o_l], axis=2)
        sw.append(jnp.swapaxes(h_fin, -1, -2))

        mp = {
            'rwkv_gn_w': rwkv_gn_w[l].reshape(RWKV_HEADS, 1, RWKV_N),
            'rwkv_gn_b': rwkv_gn_b[l].reshape(RWKV_HEADS, 1, RWKV_N),
            'w_br': w_br[l].astype(BF16), 'w_out': w_out[l].astype(BF16),
            'ln_a_g': ln_a_g[l][None], 'ln_a_b': ln_a_b[l][None],
        }
        x = _merge(oscan, bonus, g_rw, o_na, o_rt, z, x, mod3, mp, row_of_tile, 256)

        wq_t = peer_wq[l].T.astype(BF16)
        ht, cut, s1, e0, e1 = _peer_gate(x, mod3, wq_t, peer_keys[l].astype(BF16), row_of_tile, 256)
        x = _peer_expert(ht, cut, s1, e0, e1, peer_u[l].astype(BF16), peer_v[l].T.astype(BF16), x, mod3,
                         ln_f_g[l][None], ln_f_b[l][None], row_of_tile, 512, 2048)

    dt = x_prompt.dtype
    y_prompt = x[:m_ctx].reshape(n_ctx, ctx_len, d)
    y_sample = x[m_ctx:].reshape(n_lat, lat_len, d)
    return (y_prompt, y_sample, jnp.stack(nk, axis=1).astype(dt), jnp.stack(nv, axis=1).astype(dt),
            jnp.stack(sr, axis=1).astype(dt), jnp.stack(sw, axis=1).astype(dt))
```

```python
import functools
import math

import jax
import jax.numpy as jnp
import numpy as np
from jax import lax
from jax.experimental import pallas as pl
from jax.experimental.pallas import tpu as pltpu

F32 = jnp.float32
BF16 = jnp.bfloat16

D_MODEL = 1024
DEPTH = 2
GRID_W = 64
NA_HEADS = 8
NA_HEAD_DIM = 64
WIN_R = 8
WIN_C = 16
NEG_INF = -1e30
RET_HEADS = 4
RET_DK = 64
RET_DV = 128
RET_CHUNK = 64
RET_GN_EPS = 1e-5
ROPE_BASE = 10000.0
RWKV_HEADS = 8
RWKV_N = 64
RWKV_WIDTH = 512
RWKV_DECAY_SCALE = 0.606531
RWKV_GN_EPS = 64e-5
RWKV_CHUNK = 64
PEER_HEADS = 8
PEER_NKEYS = 128
PEER_TOPK = 16
LN_EPS = 1e-5
DEEPNORM_ALPHA = (2 * DEPTH) ** 0.25

OFF_GATE = 0
OFF_NA = 3072
OFF_RET_QK = 4608
OFF_RET_V = 5120
OFF_RW_MAIN = 6144
OFF_RW_LORA = 7680
P_IN = 8064

VMEM_LIMIT_BYTES = 56 * 1024 * 1024


def _cparams(*sem):
    return pltpu.CompilerParams(dimension_semantics=sem, vmem_limit_bytes=VMEM_LIMIT_BYTES)


def _ln(x, eps=LN_EPS):
    mu = jnp.mean(x, axis=-1, keepdims=True)
    xc = x - mu
    var = jnp.mean(xc * xc, axis=-1, keepdims=True)
    return xc * lax.rsqrt(var + eps)


def _sigmoid(x):
    return 1.0 / (1.0 + jnp.exp(-x))


def _dot(a, b):
    return jnp.dot(a.astype(BF16), b.astype(BF16), preferred_element_type=F32)


def _dot_nt(a, b):
    return lax.dot_general(a.astype(BF16), b.astype(BF16), (((1,), (1,)), ((), ())),
                           preferred_element_type=F32)


def _dot_tn(a, b):
    return lax.dot_general(a.astype(BF16), b.astype(BF16), (((0,), (0,)), ((), ())),
                           preferred_element_type=F32)


def _lockstep(gens):
    out = [None] * len(gens)
    live = list(range(len(gens)))
    while live:
        for i in list(live):
            try:
                next(gens[i])
            except StopIteration as done:
                out[i] = done.value
                live.remove(i)
    return out


def _split3(x):
    hi = x.astype(BF16)
    r1 = x - hi.astype(F32)
    mid = r1.astype(BF16)
    lo = (r1 - mid.astype(F32)).astype(BF16)
    return hi, mid, lo


def _dot_exact_lhs(sel, x):
    hi, mid, lo = _split3(x)
    s = sel.astype(BF16)
    out = jnp.dot(s, lo, preferred_element_type=F32)
    out = out + jnp.dot(s, mid, preferred_element_type=F32)
    return out + jnp.dot(s, hi, preferred_element_type=F32)


def _dot_exact_rhs(x, sel):
    hi, mid, lo = _split3(x)
    s = sel.astype(BF16)
    out = jnp.dot(lo, s, preferred_element_type=F32)
    out = out + jnp.dot(mid, s, preferred_element_type=F32)
    return out + jnp.dot(hi, s, preferred_element_type=F32)


def _mod_kernel(c_ref, w_ref, b_ref, o_ref):
    c = c_ref[...]
    s = c * _sigmoid(c)
    o_ref[0] = _dot(s, w_ref[0]) + b_ref[0]


def _modulation(c8, w_mod, b_mod):
    tn = 1536
    n = w_mod.shape[-1]
    return pl.pallas_call(
        _mod_kernel,
        grid=(DEPTH, n // tn),
        in_specs=[pl.BlockSpec((8, D_MODEL), lambda l, j: (0, 0)),
                  pl.BlockSpec((1, D_MODEL, tn), lambda l, j: (l, 0, j)),
                  pl.BlockSpec((1, 1, tn), lambda l, j: (l, 0, j))],
        out_specs=pl.BlockSpec((1, 8, tn), lambda l, j: (l, 0, j)),
        out_shape=jax.ShapeDtypeStruct((DEPTH, 8, n), F32),
        compiler_params=_cparams("parallel", "parallel"),
        name="modulation",
    )(c8, w_mod, b_mod.reshape(DEPTH, 1, n))


def _lnmod_matmul_kernel(x_ref, sh_ref, sc_ref, w_ref, o_ref, h_ref):
    @pl.when(pl.program_id(1) == 0)
    def _():
        h = _ln(x_ref[...]) * (1.0 + sc_ref[0]) + sh_ref[0]
        h_ref[...] = h.astype(BF16)

    o_ref[...] = jnp.dot(h_ref[...], w_ref[...], preferred_element_type=F32)


def _lnmod_matmul(x, mod3, w, row_of_tile, tm, tn, sh_blk, sc_blk):
    m = x.shape[0]
    n = w.shape[1]
    return pl.pallas_call(
        _lnmod_matmul_kernel,
        grid=(m // tm, n // tn),
        in_specs=[pl.BlockSpec((tm, D_MODEL), lambda i, j: (i, 0)),
                  pl.BlockSpec((1, 1, D_MODEL), lambda i, j: (row_of_tile(i, tm), 0, sh_blk)),
                  pl.BlockSpec((1, 1, D_MODEL), lambda i, j: (row_of_tile(i, tm), 0, sc_blk)),
                  pl.BlockSpec((D_MODEL, tn), lambda i, j: (0, j))],
        out_specs=pl.BlockSpec((tm, tn), lambda i, j: (i, j)),
        out_shape=jax.ShapeDtypeStruct((m, n), F32),
        scratch_shapes=[pltpu.VMEM((tm, D_MODEL), BF16)],
        compiler_params=_cparams("parallel", "arbitrary"),
        name="adaln_in_proj",
    )(x, mod3, mod3, w)


def _ctx_attn_kernel(q_ref, k_ref, v_ref, o_ref, kc_ref, vc_ref):
    scale = NA_HEAD_DIM ** -0.5
    for hh in range(2):
        sl = slice(hh * NA_HEAD_DIM, (hh + 1) * NA_HEAD_DIM)
        k = k_ref[:, sl]
        v = v_ref[:, sl]
        kc_ref[0, hh] = k
        vc_ref[0, hh] = v
        s = _dot_nt(q_ref[:, sl], k) * scale
        m = jnp.max(s, axis=-1, keepdims=True)
        p = jnp.exp(s - m)
        l = jnp.sum(p, axis=-1, keepdims=True)
        o_ref[:, sl] = _dot(p, v) / l


def _ctx_attention(z, n_seq, seq_len):
    cb = OFF_NA // 128
    cache_spec = pl.BlockSpec((1, 2, seq_len, NA_HEAD_DIM), lambda b, h: (b, h, 0, 0))
    cache_shape = jax.ShapeDtypeStruct((n_seq, NA_HEADS, seq_len, NA_HEAD_DIM), F32)
    return pl.pallas_call(
        _ctx_attn_kernel,
        grid=(n_seq, NA_HEADS // 2),
        in_specs=[pl.BlockSpec((seq_len, 128), lambda b, h: (b, cb + h)),
                  pl.BlockSpec((seq_len, 128), lambda b, h: (b, cb + 4 + h)),
                  pl.BlockSpec((seq_len, 128), lambda b, h: (b, cb + 8 + h))],
        out_specs=[pl.BlockSpec((seq_len, 128), lambda b, h: (b, h)), cache_spec, cache_spec],
        out_shape=[jax.ShapeDtypeStruct((n_seq * seq_len, NA_HEADS * NA_HEAD_DIM), F32), cache_shape, cache_shape],
        compiler_params=_cparams("parallel", "parallel"),
        name="ctx_attention",
    )(z, z, z)


def _na_bias_kernel(rpb_ref, rowsel_ref, colsel_ref, neg_ref, o_ref):
    picked = _dot_exact_lhs(rowsel_ref[...], rpb_ref[0])
    o_ref[0] = _dot_exact_rhs(picked, colsel_ref[...]) + neg_ref[...]


def _na_bias_table(rpb, rows):
    kr = min(WIN_R, rows)
    nh, n_dr, n_dc = rpb.shape
    rep = np.array([0, 1, 2, 3, 4, rows - 3, rows - 2, rows - 1])
    start = np.clip(rep - kr // 2, 0, rows - kr)
    dr = start[:, None] - rep[:, None] + WIN_R - 1 + np.arange(kr)[None, :]
    qcol = np.arange(GRID_W)
    kcol = np.arange(GRID_W)
    q_start = np.clip(qcol - WIN_C // 2, 0, GRID_W - WIN_C)
    rel = kcol[None, :] - q_start[:, None]
    valid = (rel >= 0) & (rel < WIN_C)
    dc = np.clip(kcol[None, :] - qcol[:, None], -(WIN_C - 1), WIN_C - 1) + WIN_C - 1
    row_sel = np.zeros((8 * kr, 16), np.float32)
    row_sel[np.arange(8 * kr), dr.reshape(-1)] = 1.0
    col_sel = np.zeros((32, GRID_W * GRID_W), np.float32)
    qq, kk = np.nonzero(valid)
    col_sel[dc[qq, kk], qq * GRID_W + kk] = 1.0
    neg = np.where(valid, 0.0, NEG_INF).astype(np.float32).reshape(1, -1)
    rpb_p = jnp.pad(rpb.astype(F32), ((0, 0), (0, 16 - n_dr), (0, 32 - n_dc)))
    full = lambda a: pl.BlockSpec(a.shape, lambda h: (0,) * a.ndim)
    consts = [jnp.asarray(row_sel, BF16), jnp.asarray(col_sel, BF16), jnp.asarray(neg)]
    b = pl.pallas_call(
        _na_bias_kernel,
        grid=(nh,),
        in_specs=[pl.BlockSpec((1, 16, 32), lambda h: (h, 0, 0))] + [full(a) for a in consts],
        out_specs=pl.BlockSpec((1, 8 * kr, GRID_W * GRID_W), lambda h: (h, 0, 0)),
        out_shape=jax.ShapeDtypeStruct((nh, 8 * kr, GRID_W * GRID_W), F32),
        compiler_params=_cparams("parallel"),
        name="na_bias",
    )(rpb_p, *consts)
    b = b.reshape(nh, 8, kr, GRID_W, GRID_W)
    return b.transpose(0, 1, 3, 2, 4).reshape(nh, 8, GRID_W, kr * GRID_W)


NA_ROWS_PER_STEP = 2


def _na_latent_kernel(q_ref, k_ref, v_ref, ck_ref, cv_ref, b_ref, o_ref, *, rows):
    scale = NA_HEAD_DIM ** -0.5
    kr = min(WIN_R, rows)

    def query_row(r, hh):
        start = jnp.clip(r - kr // 2, 0, rows - kr)
        typ = jnp.where(r < 4, r, jnp.where(r > rows - 4, r - (rows - 8), 4))
        q0 = pl.multiple_of(r * GRID_W, GRID_W)
        k0 = pl.multiple_of(start * GRID_W, GRID_W)
        sl = slice(hh * NA_HEAD_DIM, (hh + 1) * NA_HEAD_DIM)
        q = q_ref[pl.ds(q0, GRID_W), sl]
        kw = k_ref[pl.ds(k0, kr * GRID_W), sl]
        vw = v_ref[pl.ds(k0, kr * GRID_W), sl]
        s_w = _dot_nt(q, kw) * scale + b_ref[hh, typ]
        s_c = _dot_nt(q, ck_ref[0, hh]) * scale
        yield
        m = jnp.maximum(jnp.max(s_w, axis=-1, keepdims=True), jnp.max(s_c, axis=-1, keepdims=True))
        p_w = jnp.exp(s_w - m)
        p_c = jnp.exp(s_c - m)
        l = jnp.sum(p_w, axis=-1, keepdims=True) + jnp.sum(p_c, axis=-1, keepdims=True)
        yield
        o_ref[pl.ds(q0, GRID_W), sl] = (_dot(p_w, vw) + _dot(p_c, cv_ref[0, hh])) / l

    def body(rp, carry):
        _lockstep([query_row(rp * NA_ROWS_PER_STEP + u, hh) for u in range(NA_ROWS_PER_STEP) for hh in range(2)])
        return carry

    lax.fori_loop(0, rows // NA_ROWS_PER_STEP, body, 0)


def _na_latent(z, ctx_k, ctx_v, bias, n_seq, seq_len, row_blk0):
    cb = OFF_NA // 128
    rows = seq_len // GRID_W
    past = ctx_k.shape[2]
    return pl.pallas_call(
        functools.partial(_na_latent_kernel, rows=rows),
        grid=(n_seq, NA_HEADS // 2),
        in_specs=[pl.BlockSpec((seq_len, 128), lambda b, h: (row_blk0 + b, cb + h)),
                  pl.BlockSpec((seq_len, 128), lambda b, h: (row_blk0 + b, cb + 4 + h)),
                  pl.BlockSpec((seq_len, 128), lambda b, h: (row_blk0 + b, cb + 8 + h)),
                  pl.BlockSpec((1, 2, past, NA_HEAD_DIM), lambda b, h: (b, h, 0, 0)),
                  pl.BlockSpec((1, 2, past, NA_HEAD_DIM), lambda b, h: (b, h, 0, 0)),
                  pl.BlockSpec((2, 8, GRID_W, bias.shape[-1]), lambda b, h: (h, 0, 0, 0))],
        out_specs=pl.BlockSpec((seq_len, 128), lambda b, h: (b, h)),
        out_shape=jax.ShapeDtypeStruct((n_seq * seq_len, NA_HEADS * NA_HEAD_DIM), F32),
        compiler_params=_cparams("parallel", "parallel"),
        name="na_latent",
    )(z, z, z, ctx_k, ctx_v, bias)


def _ret_decay_tables(decay_logit):
    c = RET_CHUNK
    lg = jnp.log(jax.nn.sigmoid(decay_logit.astype(F32)))[:, :, None, None]
    i = jnp.arange(c, dtype=F32)
    diff = i[:, None] - i[None, :]
    m_f = jnp.where(diff >= 0, jnp.exp(jnp.maximum(diff, 0.0) * lg[0]), 0.0)
    m_b = jnp.where(diff <= 0, jnp.exp(jnp.maximum(-diff, 0.0) * lg[1]), 0.0)
    mask = jnp.stack([m_f, m_b])
    ones = jnp.ones((1, 1, 1, RET_DK), F32)
    q_f = jnp.exp((i + 1.0)[None, :, None] * lg[0]) * ones[0]
    q_b = jnp.exp((c - i)[None, :, None] * lg[1]) * ones[0]
    k_f = jnp.exp((c - 1.0 - i)[None, :, None] * lg[0]) * ones[0]
    k_b = jnp.exp(i[None, :, None] * lg[1]) * ones[0]
    c_dec = jnp.exp(c * lg) * jnp.ones((1, 1, RET_DK, RET_DV), F32)
    return mask, jnp.stack([q_f, q_b]), jnp.stack([k_f, k_b]), c_dec


def _rope_tables(seq_len):
    quarter = RET_DK // 4
    pos = np.arange(seq_len)
    freqs = 1.0 / (ROPE_BASE ** (jnp.arange(quarter, dtype=F32) / quarter))
    a_row = jnp.asarray(pos // GRID_W, F32)[:, None] * freqs[None, :]
    a_col = jnp.asarray(pos % GRID_W, F32)[:, None] * freqs[None, :]
    cos = jnp.concatenate([jnp.cos(a_row)] * 2 + [jnp.cos(a_col)] * 2, axis=-1)
    sin = jnp.concatenate([-jnp.sin(a_row), jnp.sin(a_row), -jnp.sin(a_col), jnp.sin(a_col)], axis=-1)
    return jnp.tile(cos, (1, RET_HEADS)), jnp.tile(sin, (1, RET_HEADS))


def _retention_kernel(*refs, seq_len, rope):
    if rope:
        (q_ref, k_ref, v_ref, g_ref, s0_ref, mask_ref, qd_ref, kd_ref, cd_ref, gnw_ref, gnb_ref,
         cos_ref, sin_ref, o_ref, sfin_ref, qs_ref, ks_ref, acc_ref) = refs
    else:
        (q_ref, k_ref, v_ref, g_ref, s0_ref, mask_ref, qd_ref, kd_ref, cd_ref, gnw_ref, gnb_ref,
         o_ref, sfin_ref, qs_ref, ks_ref, acc_ref) = refs
    c = RET_CHUNK
    n = seq_len // c
    quarter = RET_DK // 4
    q = q_ref[...]
    k = k_ref[...] * (RET_DK ** -0.5)
    if rope:
        lane = lax.broadcasted_iota(jnp.int32, q.shape, 1)
        first = (lane % (2 * quarter)) < quarter
        width = q.shape[1]

        def rot(x):
            swapped = jnp.where(first, pltpu.roll(x, width - quarter, 1), pltpu.roll(x, quarter, 1))
            return x * cos_ref[...] + swapped * sin_ref[...]

        q = rot(q)
        k = rot(k)
    qs_ref[...] = q
    ks_ref[...] = k

    sfin_ref[...] = s0_ref[...]

    def chain(h, d, ci):
        ksl = slice(h * RET_DK, (h + 1) * RET_DK)
        vsl = slice(h * RET_DV, (h + 1) * RET_DV)
        cc = ci if d == 0 else n - 1 - ci
        rows = pl.ds(pl.multiple_of(cc * c, c), c)
        qc = qs_ref[rows, ksl]
        kc = ks_ref[rows, ksl]
        vc = v_ref[rows, vsl]
        att = _dot_nt(qc, kc) * mask_ref[d, h]
        s = sfin_ref[0, d, h]
        yield
        acc_ref[d, rows, vsl] = _dot(att, vc) + _dot(qc * qd_ref[d, h], s)
        yield
        sfin_ref[0, d, h] = s * cd_ref[d, h] + _dot_tn(kc * kd_ref[d, h], vc)

    def step(ci, carry):
        _lockstep([chain(h, d, ci) for h in range(RET_HEADS) for d in range(2)])
        return carry

    lax.fori_loop(0, n, step, 0)

    for h in range(RET_HEADS):
        vsl = slice(h * RET_DV, (h + 1) * RET_DV)
        y = _ln(acc_ref[0, :, vsl] + acc_ref[1, :, vsl], RET_GN_EPS) * gnw_ref[:, vsl] + gnb_ref[:, vsl]
        g = g_ref[:, vsl]
        o_ref[:, vsl] = y * (g * _sigmoid(g))


def _retention(z, s0, tables, gn_w, gn_b, n_seq, seq_len, row_blk0, rope_tabs):
    mask, qd, kd, cd = tables
    rope = rope_tabs is not None
    qk_w = RET_HEADS * RET_DK
    v_w = RET_HEADS * RET_DV
    full = lambda a: pl.BlockSpec(a.shape, lambda b: (0,) * a.ndim)
    in_specs = [pl.BlockSpec((seq_len, qk_w), lambda b: (row_blk0 + b, OFF_RET_QK // qk_w)),
                pl.BlockSpec((seq_len, qk_w), lambda b: (row_blk0 + b, OFF_RET_QK // qk_w + 1)),
                pl.BlockSpec((seq_len, v_w), lambda b: (row_blk0 + b, OFF_RET_V // v_w)),
                pl.BlockSpec((seq_len, v_w), lambda b: (row_blk0 + b, OFF_RET_V // v_w + 1)),
                pl.BlockSpec((1, 2, RET_HEADS, RET_DK, RET_DV), lambda b: (b, 0, 0, 0, 0)),
                full(mask), full(qd), full(kd), full(cd), full(gn_w), full(gn_b)]
    args = [z, z, z, z, s0, mask, qd, kd, cd, gn_w, gn_b]
    if rope:
        in_specs += [full(rope_tabs[0]), full(rope_tabs[1])]
        args += list(rope_tabs)
    return pl.pallas_call(
        functools.partial(_retention_kernel, seq_len=seq_len, rope=rope),
        grid=(n_seq,),
        in_specs=in_specs,
        out_specs=[pl.BlockSpec((seq_len, v_w), lambda b: (b, 0)),
                   pl.BlockSpec((1, 2, RET_HEADS, RET_DK, RET_DV), lambda b: (b, 0, 0, 0, 0))],
        out_shape=[jax.ShapeDtypeStruct((n_seq * seq_len, v_w), F32),
                   jax.ShapeDtypeStruct((n_seq, 2, RET_HEADS, RET_DK, RET_DV), F32)],
        scratch_shapes=[pltpu.VMEM((seq_len, qk_w), F32), pltpu.VMEM((seq_len, qk_w), F32),
                        pltpu.VMEM((2, seq_len, v_w), F32)],
        compiler_params=_cparams("parallel"),
        name="retention_rope" if rope else "retention",
    )(*args)


def _rwkv_prep_kernel(zm_ref, zl_ref, pm_ref, nm_ref, pl_ref, nl_ref, mum_ref, mul_ref,
                      w0_ref, wup_ref, a0_ref, aup_ref, gup_ref, kk_ref, ka_ref, rk_ref, ones_ref, tri_ref,
                      r_out, kap_out, v_out, lw_out, cs_out, kd_out, beta_out, g_out, bonus_out,
                      *, tiles_per_seq_of_tile):
    i = pl.program_id(0)
    pos, per = tiles_per_seq_of_tile(i)
    has_prev = (pos != 0).astype(F32)
    has_next = (pos != per - 1).astype(F32)
    tt = zm_ref.shape[0]

    def shift(z_ref, p_ref, n_ref, mu_ref):
        z = z_ref[...]
        row = lax.broadcasted_iota(jnp.int32, z.shape, 0)
        prev = jnp.where(row == 0, p_ref[7:8, :] * has_prev, pltpu.roll(z, 1, 0))
        nxt = jnp.where(row == tt - 1, n_ref[0:1, :] * has_next, pltpu.roll(z, tt - 1, 0))
        return z + mu_ref[...] * (0.5 * (prev + nxt) - z)

    zm = shift(zm_ref, pm_ref, nm_ref, mum_ref)
    zl = shift(zl_ref, pl_ref, nl_ref, mul_ref)
    w = RWKV_WIDTH
    r_c, k_c, v_c = zm[:, 0:w], zm[:, w:2 * w], zm[:, 2 * w:3 * w]
    ones = ones_ref[...]
    g_out[...] = _dot(_sigmoid(zl[:, 256:384]), gup_ref[...])
    kk = k_c * kk_ref[...]
    ss = _dot_exact_rhs(kk * kk, ones)
    kap = kk * lax.rsqrt(jnp.maximum(ss, 1e-24))
    bonus_out[...] = _dot_exact_rhs(r_c * k_c * rk_ref[...], ones) * v_c
    for h in range(RWKV_HEADS):
        sl = slice(h * RWKV_N, (h + 1) * RWKV_N)
        r_out[h] = r_c[:, sl]
        kap_out[h] = kap[:, sl]
        v_out[h] = v_c[:, sl]
    for d in range(2):
        wl = zl[:, d * 64:(d + 1) * 64]
        al = zl[:, 128 + d * 64:128 + (d + 1) * 64]
        lw = -RWKV_DECAY_SCALE * _sigmoid(w0_ref[d] + _dot(jnp.tanh(wl), wup_ref[d]))
        a = _sigmoid(a0_ref[d] + _dot(al, aup_ref[d]))
        k_d = k_c * (1.0 + (a - 1.0) * ka_ref[...])
        beta = kap * a
        cs = _dot_exact_lhs(tri_ref[d], lw)
        for h in range(RWKV_HEADS):
            sl = slice(h * RWKV_N, (h + 1) * RWKV_N)
            lw_out[d, h] = lw[:, sl]
            cs_out[d, h] = cs[:, sl]
            kd_out[d, h] = k_d[:, sl]
            beta_out[d, h] = beta[:, sl]


def _rwkv_prep(z, p, tiles_per_seq_of_tile, tt):
    m = z.shape[0]
    nt = m // tt
    w = RWKV_WIDTH
    mb, lb = OFF_RW_MAIN // (3 * w), OFF_RW_LORA // 384
    hm = tt // 8
    last8 = m // 8 - 1
    prev_idx = lambda i: jnp.maximum(i * hm - 1, 0)
    next_idx = lambda i: jnp.minimum((i + 1) * hm, last8)
    full = lambda a: pl.BlockSpec(a.shape, lambda i: (0,) * a.ndim)
    heads = jnp.arange(w) // RWKV_N
    ones = (heads[:, None] == heads[None, :]).astype(BF16)
    t = np.arange(tt)
    same = (t[:, None] // RWKV_CHUNK) == (t[None, :] // RWKV_CHUNK)
    tri = jnp.asarray(np.stack([same & (t[None, :] <= t[:, None]), same & (t[None, :] >= t[:, None])]), BF16)
    params = [p['mu_main'], p['mu_lora'], p['rwkv_w0'], p['rwkv_w_up'], p['rwkv_a0'], p['rwkv_a_up'],
              p['rwkv_g_up'], p['rwkv_k_k'], p['rwkv_k_a'], p['rwkv_r_k'], ones, tri]
    hd = lambda: jax.ShapeDtypeStruct((RWKV_HEADS, m, RWKV_N), F32)
    dhd = lambda: jax.ShapeDtypeStruct((2, RWKV_HEADS, m, RWKV_N), F32)
    tok = lambda: jax.ShapeDtypeStruct((m, w), F32)
    hd_spec = pl.BlockSpec((RWKV_HEADS, tt, RWKV_N), lambda i: (0, i, 0))
    dhd_spec = pl.BlockSpec((2, RWKV_HEADS, tt, RWKV_N), lambda i: (0, 0, i, 0))
    tok_spec = pl.BlockSpec((tt, w), lambda i: (i, 0))
    return pl.pallas_call(
        functools.partial(_rwkv_prep_kernel, tiles_per_seq_of_tile=tiles_per_seq_of_tile),
        grid=(nt,),
        in_specs=[pl.BlockSpec((tt, 3 * w), lambda i: (i, mb)),
                  pl.BlockSpec((tt, 384), lambda i: (i, lb)),
                  pl.BlockSpec((8, 3 * w), lambda i: (prev_idx(i), mb)),
                  pl.BlockSpec((8, 3 * w), lambda i: (next_idx(i), mb)),
                  pl.BlockSpec((8, 384), lambda i: (prev_idx(i), lb)),
                  pl.BlockSpec((8, 384), lambda i: (next_idx(i), lb))] + [full(a) for a in params],
        out_specs=[hd_spec, hd_spec, hd_spec, dhd_spec, dhd_spec, dhd_spec, dhd_spec, tok_spec, tok_spec],
        out_shape=[hd(), hd(), hd(), dhd(), dhd(), dhd(), dhd(), tok(), tok()],
        compiler_params=_cparams("parallel"),
        name="rwkv_prep",
    )(z, z, z, z, z, z, *params)


def _dot3(a, b, dims):
    a_hi = a.astype(BF16)
    b_hi = b.astype(BF16)
    a_lo = (a - a_hi.astype(F32)).astype(BF16)
    b_lo = (b - b_hi.astype(F32)).astype(BF16)
    dn = (dims, ((), ()))
    out = lax.dot_general(a_lo, b_hi, dn, preferred_element_type=F32)
    out = out + lax.dot_general(a_hi, b_lo, dn, preferred_element_type=F32)
    return out + lax.dot_general(a_hi, b_hi, dn, preferred_element_type=F32)


_NN = ((1,), (0,))
_NT = ((1,), (1,))
_TN = ((0,), (0,))


def _dot1(a, b, dims):
    return lax.dot_general(a.astype(BF16), b.astype(BF16), (dims, ((), ())), preferred_element_type=F32)


_RWKV_MM = _dot1
RWKV_CHAINS = 16
RWKV_MAX_GROUP = 8


def _rwkv_scan_kernel(r_ref, kap_ref, v_ref, lw_ref, cs_ref, kd_ref, beta_ref, h0_ref, o_ref, hfin_ref, *,
                      seq_len, group, heads):
    c = RWKV_CHUNK
    n = seq_len // c
    d = pl.program_id(0)
    sgn = 1 - 2 * d
    ti = lax.broadcasted_iota(jnp.int32, (c, c), 0) * sgn
    tj = lax.broadcasted_iota(jnp.int32, (c, c), 1) * sgn
    strict = tj < ti
    incl = tj <= ti
    eye_f = (ti == tj).astype(F32)

    def chunk_terms(hh, cc):
        rows = pl.ds(pl.multiple_of(cc * c, c), c)
        r = r_ref[hh, rows, :]
        kap = kap_ref[hh, rows, :]
        v = v_ref[hh, rows, :]
        lw = lw_ref[0, hh, rows, :]
        cs = cs_ref[0, hh, rows, :]
        kd = kd_ref[0, hh, rows, :]
        beta = beta_ref[0, hh, rows, :]
        tot = jnp.sum(lw, axis=0, keepdims=True)
        kap_t = kap * jnp.exp(cs - lw)
        r_t = r * jnp.exp(cs)
        e_ncs = jnp.exp(-cs)
        k_t = kd * e_ncs
        b_t = beta * e_ncs
        e_end = jnp.exp(tot - cs)
        k_end = kd * e_end
        b_end = beta * e_end
        yield
        l_b = jnp.where(strict, _RWKV_MM(kap_t, b_t, _NT), 0.0)
        l_k = jnp.where(strict, _RWKV_MM(kap_t, k_t, _NT), 0.0)
        m_b = jnp.where(incl, _RWKV_MM(r_t, b_t, _NT), 0.0)
        m_k = jnp.where(incl, _RWKV_MM(r_t, k_t, _NT), 0.0)
        yield
        x = -l_b
        y = jnp.concatenate([kap_t, _RWKV_MM(l_k, v, _NN)], axis=1)
        yield
        y = y + _RWKV_MM(x, y, _NN)
        pw = x
        for _ in range(5):
            pw = _RWKV_MM(pw, pw, _NN)
            yield
            y = y + _RWKV_MM(pw, y, _NN)
        yield
        mby = _RWKV_MM(m_b, y, _NN)
        bty = _RWKV_MM(b_end, y, _TN)
        rr = r_t - mby[:, :RWKV_N]
        o0 = _RWKV_MM(m_k, v, _NN) - mby[:, RWKV_N:]
        pc = eye_f * jnp.exp(tot) - bty[:, :RWKV_N]
        qc = _RWKV_MM(k_end, v, _TN) - bty[:, RWKV_N:]
        return rows, rr, o0, pc, qc

    def step(gi, hstates):
        keys = [(hh, u) for u in range(group) for hh in range(heads)]
        terms = dict(zip(keys, _lockstep(
            [chunk_terms(hh, (gi * group + u) + d * (n - 1 - 2 * (gi * group + u))) for hh, u in keys])))
        hstates = list(hstates)
        for u in range(group):
            for hh in range(heads):
                rows, rr, o0, pc, qc = terms[(hh, u)]
                o_ref[0, hh, rows, :] = _dot3(rr, hstates[hh], _NN) + o0
                hstates[hh] = _dot3(pc, hstates[hh], _NN) + qc
        return tuple(hstates)

    fin = lax.fori_loop(0, n // group, step, tuple(h0_ref[0, 0, hh] for hh in range(heads)))
    for hh in range(heads):
        hfin_ref[0, 0, hh] = fin[hh]


def _rwkv_scan(r, kap, v, lw, cs, kd, beta, h0, n_seq, seq_len, row_blk0):
    group = min(seq_len // RWKV_CHUNK, RWKV_MAX_GROUP)
    hp = RWKV_CHAINS // group
    hd_spec = pl.BlockSpec((hp, seq_len, RWKV_N), lambda d, h, b: (h, row_blk0 + b, 0))
    dhd_spec = pl.BlockSpec((1, hp, seq_len, RWKV_N), lambda d, h, b: (d, h, row_blk0 + b, 0))
    st_spec = pl.BlockSpec((1, 1, hp, RWKV_N, RWKV_N), lambda d, h, b: (b, d, h, 0, 0))
    return pl.pallas_call(
        functools.partial(_rwkv_scan_kernel, seq_len=seq_len, group=group, heads=hp),
        grid=(2, RWKV_HEADS // hp, n_seq),
        in_specs=[hd_spec, hd_spec, hd_spec, dhd_spec, dhd_spec, dhd_spec, dhd_spec, st_spec],
        out_specs=[pl.BlockSpec((1, hp, seq_len, RWKV_N), lambda d, h, b: (d, h, b, 0)), st_spec],
        out_shape=[jax.ShapeDtypeStruct((2, RWKV_HEADS, n_seq * seq_len, RWKV_N), F32),
                   jax.ShapeDtypeStruct((n_seq, 2, RWKV_HEADS, RWKV_N, RWKV_N), F32)],
        compiler_params=_cparams("parallel", "parallel", "parallel"),
        name="rwkv_scan",
    )(r, kap, v, lw, cs, kd, beta, h0)


def _merge_kernel(oscan_c_ref, oscan_l_ref, ona_c_ref, ona_l_ref, ort_c_ref, ort_l_ref, bonus_ref, grw_ref,
                  ga_ref, gb_ref, gc_ref, x_ref, gt_ref, gnw_ref, gnb_ref, wbr_ref, wout_ref, lng_ref, lnb_ref,
                  o_ref, orw_ref, *, ctx_tiles):
    is_ctx = pl.program_id(0) < ctx_tiles
    pick = lambda c_val, l_val: jnp.where(is_ctx, c_val, l_val)
    for h in range(RWKV_HEADS):
        sl = slice(h * RWKV_N, (h + 1) * RWKV_N)
        o = pick(oscan_c_ref[0, h] + oscan_c_ref[1, h], oscan_l_ref[0, h] + oscan_l_ref[1, h])
        y = _ln(o, RWKV_GN_EPS) * gnw_ref[h] + gnb_ref[h]
        orw_ref[:, sl] = (y + bonus_ref[:, sl]) * grw_ref[:, sl]
    merged = (_sigmoid(ga_ref[...]) * _dot(pick(ona_c_ref[...], ona_l_ref[...]), wbr_ref[0])
              + _sigmoid(gb_ref[...]) * _dot(pick(ort_c_ref[...], ort_l_ref[...]), wbr_ref[1])
              + _sigmoid(gc_ref[...]) * _dot(orw_ref[...], wbr_ref[2]))
    t = _dot(merged, wout_ref[...])
    o_ref[...] = _ln(DEEPNORM_ALPHA * x_ref[...] + gt_ref[0] * t) * lng_ref[...] + lnb_ref[...]


def _merge(oscan, o_na, o_rt, bonus, g_rw, z, x, mod3, p, row_of_tile, tm):
    m = x.shape[0]
    bw = RWKV_WIDTH
    ctx_tiles = o_na[0].shape[0] // tm
    full = lambda a: pl.BlockSpec(a.shape, lambda i: (0,) * a.ndim)
    tok = pl.BlockSpec((tm, bw), lambda i: (i, 0))
    c_row = lambda i: jnp.minimum(i, ctx_tiles - 1)
    l_row = lambda i: jnp.maximum(i - ctx_tiles, 0)
    tok_c = pl.BlockSpec((tm, bw), lambda i: (c_row(i), 0))
    tok_l = pl.BlockSpec((tm, bw), lambda i: (l_row(i), 0))
    params = [p['rwkv_gn_w'], p['rwkv_gn_b'], p['w_br'], p['w_out'], p['ln_a_g'], p['ln_a_b']]
    return pl.pallas_call(
        functools.partial(_merge_kernel, ctx_tiles=ctx_tiles),
        grid=(m // tm,),
        in_specs=[pl.BlockSpec((2, RWKV_HEADS, tm, RWKV_N), lambda i: (0, 0, c_row(i), 0)),
                  pl.BlockSpec((2, RWKV_HEADS, tm, RWKV_N), lambda i: (0, 0, l_row(i), 0)),
                  tok_c, tok_l, tok_c, tok_l, tok, tok,
                  pl.BlockSpec((tm, D_MODEL), lambda i: (i, 0)),
                  pl.BlockSpec((tm, D_MODEL), lambda i: (i, 1)),
                  pl.BlockSpec((tm, D_MODEL), lambda i: (i, 2)),
                  pl.BlockSpec((tm, D_MODEL), lambda i: (i, 0)),
                  pl.BlockSpec((1, 1, D_MODEL), lambda i: (row_of_tile(i, tm), 0, 2))]
                 + [full(a) for a in params],
        out_specs=pl.BlockSpec((tm, D_MODEL), lambda i: (i, 0)),
        out_shape=jax.ShapeDtypeStruct((m, D_MODEL), F32),
        scratch_shapes=[pltpu.VMEM((tm, bw), F32)],
        compiler_params=_cparams("parallel"),
        name="merge_out_proj",
    )(*oscan, *o_na, *o_rt, bonus, g_rw, z, z, z, x, mod3, *params)


def _top_values(s, k):
    out = []
    for _ in range(k):
        m = jnp.max(s, axis=0, keepdims=True)
        out.append(m)
        s = jnp.where(s == m, -jnp.inf, s)
    return out


def _peer_gate_kernel(x_ref, sh_ref, sc_ref, wq_ref, keys_ref, ht_ref, cut_ref, s1_ref, e0_ref, e1_ref):
    h = _ln(x_ref[...]) * (1.0 + sc_ref[0]) + sh_ref[0]
    ht = h.T.astype(BF16)
    ht_ref[...] = ht
    qt = jnp.dot(wq_ref[...], ht, preferred_element_type=F32)
    half = keys_ref.shape[-1]
    for hd in range(PEER_HEADS):
        scores, tops = [], []
        for part in range(2):
            row0 = (hd * 2 + part) * half
            s = _dot(keys_ref[hd, part], qt[row0:row0 + half, :])
            scores.append(s)
            tops.append(_top_values(s, PEER_TOPK))
        pairs = [tops[0][a] + tops[1][b] for a in range(PEER_TOPK) for b in range(PEER_TOPK // (a + 1))]
        pairs += [jnp.full_like(pairs[0], -jnp.inf)] * (-len(pairs) % 8)
        best = _top_values(jnp.concatenate(pairs, axis=0), PEER_TOPK)
        zsum = jnp.exp(best[0] - best[0])
        for t in range(1, PEER_TOPK):
            zsum = zsum + jnp.exp(best[t] - best[0])
        tau = best[PEER_TOPK - 1]
        cut = jnp.full(scores[0].shape, jnp.inf, F32)
        for t1 in tops[1]:
            cut = jnp.where(scores[0] + t1 >= tau, t1, cut)
        cut_ref[hd] = cut
        s1_ref[hd] = scores[1]
        e0_ref[hd] = jnp.exp(scores[0] - tops[0][0])
        e1_ref[hd] = jnp.exp(scores[1] - tops[1][0]) / zsum


def _peer_gate(x, mod3, wq_t, keys, row_of_tile, tn):
    m = x.shape[0]
    full = lambda a: pl.BlockSpec(a.shape, lambda i: (0,) * a.ndim)
    sk = pl.BlockSpec((PEER_HEADS, PEER_NKEYS, tn), lambda i: (0, 0, i))
    sk_shape = jax.ShapeDtypeStruct((PEER_HEADS, PEER_NKEYS, m), F32)
    return pl.pallas_call(
        _peer_gate_kernel,
        grid=(m // tn,),
        in_specs=[pl.BlockSpec((tn, D_MODEL), lambda i: (i, 0)),
                  pl.BlockSpec((1, 1, D_MODEL), lambda i: (row_of_tile(i, tn), 0, 3)),
                  pl.BlockSpec((1, 1, D_MODEL), lambda i: (row_of_tile(i, tn), 0, 4)),
                  full(wq_t), full(keys)],
        out_specs=[pl.BlockSpec((D_MODEL, tn), lambda i: (0, i)), sk, sk, sk, sk],
        out_shape=[jax.ShapeDtypeStruct((D_MODEL, m), BF16), sk_shape, sk_shape, sk_shape, sk_shape],
        compiler_params=_cparams("parallel"),
        name="peer_gate",
    )(x, mod3, mod3, wq_t, keys)


def _gelu_exact(x):
    return 0.5 * x * (1.0 + lax.erf(x * (2.0 ** -0.5)))


PEER_SUB_ROWS = 32
PEER_SUB_LANES = 256


def _peer_expert_kernel(ht_ref, cut_ref, s1_ref, e0_ref, e1_ref, u_ref, vt_ref, x_ref, gt_ref,
                        lng_ref, lnb_ref, o_ref, acc_ref, st_ref, st2_ref, act_ref, *, groups):
    e = pl.program_id(1)
    tn = ht_ref.shape[1]

    @pl.when(e == 0)
    def _():
        acc_ref[...] = jnp.zeros_like(acc_ref)

    def pre_activations(g, dst_ref):
        rows = pl.ds(pl.multiple_of(g * PEER_NKEYS, PEER_NKEYS), PEER_NKEYS)
        dst_ref[...] = jnp.dot(u_ref[rows, :], ht_ref[...], preferred_element_type=F32)

    def gated_activations(g, src_ref):
        i = e * groups + g
        row0 = pl.multiple_of(g * PEER_NKEYS, PEER_NKEYS)
        cut_row = [cut_ref[hd, pl.ds(i, 1), :] for hd in range(PEER_HEADS)]
        e0_row = [e0_ref[hd, pl.ds(i, 1), :] for hd in range(PEER_HEADS)]
        for lb in range(tn // PEER_SUB_LANES):
            ls = slice(lb * PEER_SUB_LANES, (lb + 1) * PEER_SUB_LANES)
            cut = [c[:, ls] for c in cut_row]
            e0 = [c[:, ls] for c in e0_row]
            for rb in range(PEER_NKEYS // PEER_SUB_ROWS):
                js = slice(rb * PEER_SUB_ROWS, (rb + 1) * PEER_SUB_ROWS)
                gate = None
                for hd in range(PEER_HEADS):
                    term = jnp.where(s1_ref[hd, js, ls] >= cut[hd], e1_ref[hd, js, ls], 0.0) * e0[hd]
                    gate = term if gate is None else gate + term
                act = _gelu_exact(src_ref[js, ls]) * gate
                act_ref[pl.ds(row0 + rb * PEER_SUB_ROWS, PEER_SUB_ROWS), ls] = act.astype(BF16)

    pre_activations(0, st_ref)

    def pair(p, carry):
        gated_activations(2 * p, st_ref)
        pre_activations(2 * p + 1, st2_ref)
        gated_activations(2 * p + 1, st2_ref)
        pre_activations(jnp.minimum(2 * p + 2, groups - 1), st_ref)
        return carry

    lax.fori_loop(0, groups // 2, pair, 0)
    acc_ref[...] += jnp.dot(vt_ref[...], act_ref[...], preferred_element_type=F32)

    @pl.when(e == pl.num_programs(1) - 1)
    def _():
        f = acc_ref[...].T
        o_ref[...] = _ln(DEEPNORM_ALPHA * x_ref[...] + gt_ref[0] * f) * lng_ref[...] + lnb_ref[...]


def _peer_expert(ht, cut, s1, e0, e1, u, v_t, x, mod3, ln_g, ln_b, row_of_tile, tn, eb):
    m = x.shape[0]
    n_exp = u.shape[0]
    groups = eb // PEER_NKEYS
    sk = pl.BlockSpec((PEER_HEADS, PEER_NKEYS, tn), lambda i, e: (0, 0, i))
    full = lambda a: pl.BlockSpec(a.shape, lambda i, e: (0,) * a.ndim)
    return pl.pallas_call(
        functools.partial(_peer_expert_kernel, groups=groups),
        grid=(m // tn, n_exp // eb),
        in_specs=[pl.BlockSpec((D_MODEL, tn), lambda i, e: (0, i)), sk, sk, sk, sk,
                  pl.BlockSpec((eb, D_MODEL), lambda i, e: (e, 0)),
                  pl.BlockSpec((D_MODEL, eb), lambda i, e: (0, e)),
                  pl.BlockSpec((tn, D_MODEL), lambda i, e: (i, 0)),
                  pl.BlockSpec((1, 1, D_MODEL), lambda i, e: (row_of_tile(i, tn), 0, 5)),
                  full(ln_g), full(ln_b)],
        out_specs=pl.BlockSpec((tn, D_MODEL), lambda i, e: (i, 0)),
        out_shape=jax.ShapeDtypeStruct((m, D_MODEL), F32),
        scratch_shapes=[pltpu.VMEM((D_MODEL, tn), F32), pltpu.VMEM((PEER_NKEYS, tn), F32),
                        pltpu.VMEM((PEER_NKEYS, tn), F32), pltpu.VMEM((eb, tn), BF16)],
        compiler_params=_cparams("parallel", "arbitrary"),
        name="peer_experts",
    )(ht, cut, s1, e0, e1, u, v_t, x, mod3, ln_g, ln_b)


def _permute_in_proj(w_in):
    na, qk, rv = 3 * 512, 2 * 256, 2 * 512
    o_na, o_qk, o_rv, o_rw, o_g = 0, na, na + qk, na + qk + rv, na + qk + rv + 1920
    return jnp.concatenate([w_in[:, o_g:], w_in[:, o_na:o_qk], w_in[:, o_qk:o_rv], w_in[:, o_rv:o_rw],
                            w_in[:, o_rw:o_g]], axis=1)


def kernel(x_prompt, x_sample, cache_na_k, cache_na_v, state_ret, state_rwkv, c, c_ctx, w_mod, b_mod, w_in, na_rpb, ret_decay_logit, ret_gn_w, ret_gn_b, rwkv_mu, rwkv_w0, rwkv_w_up, rwkv_a0, rwkv_a_up, rwkv_g_up, rwkv_k_k, rwkv_k_a, rwkv_r_k, rwkv_gn_w, rwkv_gn_b, w_br, w_out, ln_a_g, ln_a_b, ln_f_g, ln_f_b, peer_wq, peer_keys, peer_u, peer_v):
    n_ctx, ctx_len, d = x_prompt.shape
    n_lat, lat_len, _ = x_sample.shape
    m_ctx = n_ctx * ctx_len
    assert m_ctx % lat_len == 0 and n_lat == 2
    lat_blk0 = m_ctx // lat_len

    def row_of_tile(i, tm):
        return jnp.where(i < m_ctx // tm, 0, 1 + (i - m_ctx // tm) // (lat_len // tm))

    prep_tt = 256
    assert ctx_len == prep_tt

    def tiles_per_seq_of_tile(i):
        lat = i >= m_ctx // prep_tt
        per = jnp.where(lat, lat_len // prep_tt, 1)
        pos = jnp.where(lat, (i - m_ctx // prep_tt) % (lat_len // prep_tt), 0)
        return pos, per

    x = jnp.concatenate([x_prompt.reshape(m_ctx, d), x_sample.reshape(n_lat * lat_len, d)], axis=0)
    c8 = jnp.concatenate([c_ctx[None], c, jnp.zeros((8 - 1 - n_lat, d), F32)], axis=0)
    mod = _modulation(c8, w_mod, b_mod)
    rope_tabs = _rope_tables(lat_len)
    zeros_ret = jnp.zeros((n_ctx, 2, RET_HEADS, RET_DK, RET_DV), F32)
    zeros_rw = jnp.zeros((n_ctx, 2, RWKV_HEADS, RWKV_N, RWKV_N), F32)

    nk, nv, sr, sw = [], [], [], []
    for l in range(DEPTH):
        mod3 = mod[l].reshape(8, 1, 6 * d)
        w_in_p = _permute_in_proj(w_in[l]).astype(BF16)
        z = _lnmod_matmul(x, mod3, w_in_p, row_of_tile, 1024, 1152, 0, 1)

        o_na_c, k_cache, v_cache = _ctx_attention(z, n_ctx, ctx_len)
        o_na_l = _na_latent(z, cache_na_k[:, l], cache_na_v[:, l], _na_bias_table(na_rpb[l], lat_len // GRID_W),
                            n_lat, lat_len, lat_blk0)
        nk.append(k_cache)
        nv.append(v_cache)

        tables = _ret_decay_tables(ret_decay_logit[l])
        gn_w, gn_b = ret_gn_w[l][None], ret_gn_b[l][None]
        o_rt_c, s_ret = _retention(z, zeros_ret, tables, gn_w, gn_b, n_ctx, ctx_len, 0, None)
        o_rt_l, _ = _retention(z, state_ret[:, l], tables, gn_w, gn_b, n_lat, lat_len, lat_blk0, rope_tabs)
        sr.append(s_ret)

        mu = rwkv_mu[l]
        prm = {
            'mu_main': mu[None, :1536], 'mu_lora': mu[None, 1536:],
            'rwkv_w0': rwkv_w0[l][:, None], 'rwkv_w_up': rwkv_w_up[l].astype(BF16),
            'rwkv_a0': rwkv_a0[l][:, None], 'rwkv_a_up': rwkv_a_up[l].astype(BF16),
            'rwkv_g_up': rwkv_g_up[l].astype(BF16), 'rwkv_k_k': rwkv_k_k[l][None],
            'rwkv_k_a': rwkv_k_a[l][None], 'rwkv_r_k': rwkv_r_k[l].reshape(1, RWKV_WIDTH),
        }
        r, kap, v, lw, cs, kd, beta, g_rw, bonus = _rwkv_prep(z, prm, tiles_per_seq_of_tile, prep_tt)
        o_c, h_fin = _rwkv_scan(r, kap, v, lw, cs, kd, beta, zeros_rw, n_ctx, ctx_len, 0)
        h0_lat = jnp.swapaxes(state_rwkv[:, l], -1, -2)
        o_l, _ = _rwkv_scan(r, kap, v, lw, cs, kd, beta, h0_lat, n_lat, lat_len, lat_blk0)
        sw.append(jnp.swapaxes(h_fin, -1, -2))

        mp = {
            'rwkv_gn_w': rwkv_gn_w[l].reshape(RWKV_HEADS, 1, RWKV_N),
            'rwkv_gn_b': rwkv_gn_b[l].reshape(RWKV_HEADS, 1, RWKV_N),
            'w_br': w_br[l].astype(BF16), 'w_out': w_out[l].astype(BF16),
            'ln_a_g': ln_a_g[l][None], 'ln_a_b': ln_a_b[l][None],
        }
        x = _merge((o_c, o_l), (o_na_c, o_na_l), (o_rt_c, o_rt_l), bonus, g_rw, z, x, mod3, mp, row_of_tile, 256)

        wq_t = peer_wq[l].T.astype(BF16)
        ht, cut, s1, e0, e1 = _peer_gate(x, mod3, wq_t, peer_keys[l].astype(BF16), row_of_tile, 256)
        x = _peer_expert(ht, cut, s1, e0, e1, peer_u[l].astype(BF16), peer_v[l].T.astype(BF16), x, mod3,
                         ln_f_g[l][None], ln_f_b[l][None], row_of_tile, 512, 2048)

    dt = x_prompt.dtype
    y_prompt = x[:m_ctx].reshape(n_ctx, ctx_len, d)
    y_sample = x[m_ctx:].reshape(n_lat, lat_len, d)
    return (y_prompt, y_sample, jnp.stack(nk, axis=1).astype(dt), jnp.stack(nv, axis=1).astype(dt),
            jnp.stack(sr, axis=1).astype(dt), jnp.stack(sw, axis=1).astype(dt))
```

```python
import functools
import math

import jax
import jax.numpy as jnp
import numpy as np
from jax import lax
from jax.experimental import pallas as pl
from jax.experimental.pallas import tpu as pltpu

F32 = jnp.float32
BF16 = jnp.bfloat16

D_MODEL = 1024
DEPTH = 2
GRID_W = 64
NA_HEADS = 8
NA_HEAD_DIM = 64
WIN_R = 8
WIN_C = 16
NEG_INF = -1e30
RET_HEADS = 4
RET_DK = 64
RET_DV = 128
RET_CHUNK = 64
RET_GN_EPS = 1e-5
ROPE_BASE = 10000.0
RWKV_HEADS = 8
RWKV_N = 64
RWKV_WIDTH = 512
RWKV_DECAY_SCALE = 0.606531
RWKV_GN_EPS = 64e-5
RWKV_CHUNK = 64
PEER_HEADS = 8
PEER_NKEYS = 128
PEER_TOPK = 16
LN_EPS = 1e-5
DEEPNORM_ALPHA = (2 * DEPTH) ** 0.25

OFF_GATE = 0
OFF_NA = 3072
OFF_RET_QK = 4608
OFF_RET_V = 5120
OFF_RW_MAIN = 6144
OFF_RW_LORA = 7680
P_IN = 8064

VMEM_LIMIT_BYTES = 56 * 1024 * 1024


def _cparams(*sem):
    return pltpu.CompilerParams(dimension_semantics=sem, vmem_limit_bytes=VMEM_LIMIT_BYTES)


def _ln(x, eps=LN_EPS):
    mu = jnp.mean(x, axis=-1, keepdims=True)
    xc = x - mu
    var = jnp.mean(xc * xc, axis=-1, keepdims=True)
    return xc * lax.rsqrt(var + eps)


def _sigmoid(x):
    return 1.0 / (1.0 + jnp.exp(-x))


def _dot(a, b):
    return jnp.dot(a.astype(BF16), b.astype(BF16), preferred_element_type=F32)


def _dot_nt(a, b):
    return lax.dot_general(a.astype(BF16), b.astype(BF16), (((1,), (1,)), ((), ())),
                           preferred_element_type=F32)


def _dot_tn(a, b):
    return lax.dot_general(a.astype(BF16), b.astype(BF16), (((0,), (0,)), ((), ())),
                           preferred_element_type=F32)


def _lockstep(gens):
    out = [None] * len(gens)
    live = list(range(len(gens)))
    while live:
        for i in list(live):
            try:
                next(gens[i])
            except StopIteration as done:
                out[i] = done.value
                live.remove(i)
    return out


def _split3(x):
    hi = x.astype(BF16)
    r1 = x - hi.astype(F32)
    mid = r1.astype(BF16)
    lo = (r1 - mid.astype(F32)).astype(BF16)
    return hi, mid, lo


def _dot_exact_lhs(sel, x):
    hi, mid, lo = _split3(x)
    s = sel.astype(BF16)
    out = jnp.dot(s, lo, preferred_element_type=F32)
    out = out + jnp.dot(s, mid, preferred_element_type=F32)
    return out + jnp.dot(s, hi, preferred_element_type=F32)


def _dot_exact_rhs(x, sel):
    hi, mid, lo = _split3(x)
    s = sel.astype(BF16)
    out = jnp.dot(lo, s, preferred_element_type=F32)
    out = out + jnp.dot(mid, s, preferred_element_type=F32)
    return out + jnp.dot(hi, s, preferred_element_type=F32)


def _mod_kernel(c_ref, w_ref, b_ref, o_ref):
    c = c_ref[...]
    s = c * _sigmoid(c)
    o_ref[0] = _dot(s, w_ref[0]) + b_ref[0]


def _modulation(c8, w_mod, b_mod):
    tn = 1536
    n = w_mod.shape[-1]
    return pl.pallas_call(
        _mod_kernel,
        grid=(DEPTH, n // tn),
        in_specs=[pl.BlockSpec((8, D_MODEL), lambda l, j: (0, 0)),
                  pl.BlockSpec((1, D_MODEL, tn), lambda l, j: (l, 0, j)),
                  pl.BlockSpec((1, 1, tn), lambda l, j: (l, 0, j))],
        out_specs=pl.BlockSpec((1, 8, tn), lambda l, j: (l, 0, j)),
        out_shape=jax.ShapeDtypeStruct((DEPTH, 8, n), F32),
        compiler_params=_cparams("parallel", "parallel"),
        name="modulation",
    )(c8, w_mod, b_mod.reshape(DEPTH, 1, n))


def _lnmod_matmul_kernel(x_ref, sh_ref, sc_ref, w_ref, o_ref, h_ref):
    @pl.when(pl.program_id(1) == 0)
    def _():
        h = _ln(x_ref[...]) * (1.0 + sc_ref[0]) + sh_ref[0]
        h_ref[...] = h.astype(BF16)

    o_ref[...] = jnp.dot(h_ref[...], w_ref[...], preferred_element_type=F32)


def _lnmod_matmul(x, mod3, w, row_of_tile, tm, tn, sh_blk, sc_blk):
    m = x.shape[0]
    n = w.shape[1]
    return pl.pallas_call(
        _lnmod_matmul_kernel,
        grid=(m // tm, n // tn),
        in_specs=[pl.BlockSpec((tm, D_MODEL), lambda i, j: (i, 0)),
                  pl.BlockSpec((1, 1, D_MODEL), lambda i, j: (row_of_tile(i, tm), 0, sh_blk)),
                  pl.BlockSpec((1, 1, D_MODEL), lambda i, j: (row_of_tile(i, tm), 0, sc_blk)),
                  pl.BlockSpec((D_MODEL, tn), lambda i, j: (0, j))],
        out_specs=pl.BlockSpec((tm, tn), lambda i, j: (i, j)),
        out_shape=jax.ShapeDtypeStruct((m, n), F32),
        scratch_shapes=[pltpu.VMEM((tm, D_MODEL), BF16)],
        compiler_params=_cparams("parallel", "arbitrary"),
        name="adaln_in_proj",
    )(x, mod3, mod3, w)


def _ctx_attn_kernel(q_ref, k_ref, v_ref, o_ref, kc_ref, vc_ref):
    scale = NA_HEAD_DIM ** -0.5
    for hh in range(2):
        sl = slice(hh * NA_HEAD_DIM, (hh + 1) * NA_HEAD_DIM)
        k = k_ref[:, sl]
        v = v_ref[:, sl]
        kc_ref[0, hh] = k
        vc_ref[0, hh] = v
        s = _dot_nt(q_ref[:, sl], k) * scale
        m = jnp.max(s, axis=-1, keepdims=True)
        p = jnp.exp(s - m)
        l = jnp.sum(p, axis=-1, keepdims=True)
        o_ref[:, sl] = _dot(p, v) / l


def _ctx_attention(z, n_seq, seq_len):
    cb = OFF_NA // 128
    cache_spec = pl.BlockSpec((1, 2, seq_len, NA_HEAD_DIM), lambda b, h: (b, h, 0, 0))
    cache_shape = jax.ShapeDtypeStruct((n_seq, NA_HEADS, seq_len, NA_HEAD_DIM), F32)
    return pl.pallas_call(
        _ctx_attn_kernel,
        grid=(n_seq, NA_HEADS // 2),
        in_specs=[pl.BlockSpec((seq_len, 128), lambda b, h: (b, cb + h)),
                  pl.BlockSpec((seq_len, 128), lambda b, h: (b, cb + 4 + h)),
                  pl.BlockSpec((seq_len, 128), lambda b, h: (b, cb + 8 + h))],
        out_specs=[pl.BlockSpec((seq_len, 128), lambda b, h: (b, h)), cache_spec, cache_spec],
        out_shape=[jax.ShapeDtypeStruct((n_seq * seq_len, NA_HEADS * NA_HEAD_DIM), F32), cache_shape, cache_shape],
        compiler_params=_cparams("parallel", "parallel"),
        name="ctx_attention",
    )(z, z, z)


def _na_bias_kernel(rpb_ref, rowsel_ref, colsel_ref, neg_ref, o_ref):
    picked = _dot_exact_lhs(rowsel_ref[...], rpb_ref[0])
    o_ref[0] = _dot_exact_rhs(picked, colsel_ref[...]) + neg_ref[...]


def _na_bias_table(rpb, rows):
    kr = min(WIN_R, rows)
    nh, n_dr, n_dc = rpb.shape
    rep = np.array([0, 1, 2, 3, 4, rows - 3, rows - 2, rows - 1])
    start = np.clip(rep - kr // 2, 0, rows - kr)
    dr = start[:, None] - rep[:, None] + WIN_R - 1 + np.arange(kr)[None, :]
    qcol = np.arange(GRID_W)
    kcol = np.arange(GRID_W)
    q_start = np.clip(qcol - WIN_C // 2, 0, GRID_W - WIN_C)
    rel = kcol[None, :] - q_start[:, None]
    valid = (rel >= 0) & (rel < WIN_C)
    dc = np.clip(kcol[None, :] - qcol[:, None], -(WIN_C - 1), WIN_C - 1) + WIN_C - 1
    row_sel = np.zeros((8 * kr, 16), np.float32)
    row_sel[np.arange(8 * kr), dr.reshape(-1)] = 1.0
    col_sel = np.zeros((32, GRID_W * GRID_W), np.float32)
    qq, kk = np.nonzero(valid)
    col_sel[dc[qq, kk], qq * GRID_W + kk] = 1.0
    neg = np.where(valid, 0.0, NEG_INF).astype(np.float32).reshape(1, -1)
    rpb_p = jnp.pad(rpb.astype(F32), ((0, 0), (0, 16 - n_dr), (0, 32 - n_dc)))
    full = lambda a: pl.BlockSpec(a.shape, lambda h: (0,) * a.ndim)
    consts = [jnp.asarray(row_sel, BF16), jnp.asarray(col_sel, BF16), jnp.asarray(neg)]
    b = pl.pallas_call(
        _na_bias_kernel,
        grid=(nh,),
        in_specs=[pl.BlockSpec((1, 16, 32), lambda h: (h, 0, 0))] + [full(a) for a in consts],
        out_specs=pl.BlockSpec((1, 8 * kr, GRID_W * GRID_W), lambda h: (h, 0, 0)),
        out_shape=jax.ShapeDtypeStruct((nh, 8 * kr, GRID_W * GRID_W), F32),
        compiler_params=_cparams("parallel"),
        name="na_bias",
    )(rpb_p, *consts)
    b = b.reshape(nh, 8, kr, GRID_W, GRID_W)
    return b.transpose(0, 1, 3, 2, 4).reshape(nh, 8, GRID_W, kr * GRID_W)


NA_ROWS_PER_STEP = 2


def _na_latent_kernel(q_ref, k_ref, v_ref, ck_ref, cv_ref, b_ref, o_ref, *, rows):
    scale = NA_HEAD_DIM ** -0.5
    kr = min(WIN_R, rows)

    def query_row(r, hh):
        start = jnp.clip(r - kr // 2, 0, rows - kr)
        typ = jnp.where(r < 4, r, jnp.where(r > rows - 4, r - (rows - 8), 4))
        q0 = pl.multiple_of(r * GRID_W, GRID_W)
        k0 = pl.multiple_of(start * GRID_W, GRID_W)
        sl = slice(hh * NA_HEAD_DIM, (hh + 1) * NA_HEAD_DIM)
        q = q_ref[pl.ds(q0, GRID_W), sl]
        kw = k_ref[pl.ds(k0, kr * GRID_W), sl]
        vw = v_ref[pl.ds(k0, kr * GRID_W), sl]
        s_w = _dot_nt(q, kw) * scale + b_ref[hh, typ]
        s_c = _dot_nt(q, ck_ref[0, hh]) * scale
        yield
        m = jnp.maximum(jnp.max(s_w, axis=-1, keepdims=True), jnp.max(s_c, axis=-1, keepdims=True))
        p_w = jnp.exp(s_w - m)
        p_c = jnp.exp(s_c - m)
        l = jnp.sum(p_w, axis=-1, keepdims=True) + jnp.sum(p_c, axis=-1, keepdims=True)
        yield
        o_ref[pl.ds(q0, GRID_W), sl] = (_dot(p_w, vw) + _dot(p_c, cv_ref[0, hh])) / l

    def body(rp, carry):
        _lockstep([query_row(rp * NA_ROWS_PER_STEP + u, hh) for u in range(NA_ROWS_PER_STEP) for hh in range(2)])
        return carry

    lax.fori_loop(0, rows // NA_ROWS_PER_STEP, body, 0)


def _na_latent(z, ctx_k, ctx_v, bias, n_seq, seq_len, row_blk0):
    cb = OFF_NA // 128
    rows = seq_len // GRID_W
    past = ctx_k.shape[2]
    return pl.pallas_call(
        functools.partial(_na_latent_kernel, rows=rows),
        grid=(n_seq, NA_HEADS // 2),
        in_specs=[pl.BlockSpec((seq_len, 128), lambda b, h: (row_blk0 + b, cb + h)),
                  pl.BlockSpec((seq_len, 128), lambda b, h: (row_blk0 + b, cb + 4 + h)),
                  pl.BlockSpec((seq_len, 128), lambda b, h: (row_blk0 + b, cb + 8 + h)),
                  pl.BlockSpec((1, 2, past, NA_HEAD_DIM), lambda b, h: (b, h, 0, 0)),
                  pl.BlockSpec((1, 2, past, NA_HEAD_DIM), lambda b, h: (b, h, 0, 0)),
                  pl.BlockSpec((2, 8, GRID_W, bias.shape[-1]), lambda b, h: (h, 0, 0, 0))],
        out_specs=pl.BlockSpec((seq_len, 128), lambda b, h: (b, h)),
        out_shape=jax.ShapeDtypeStruct((n_seq * seq_len, NA_HEADS * NA_HEAD_DIM), F32),
        compiler_params=_cparams("parallel", "parallel"),
        name="na_latent",
    )(z, z, z, ctx_k, ctx_v, bias)


def _ret_decay_tables(decay_logit):
    c = RET_CHUNK
    lg = jnp.log(jax.nn.sigmoid(decay_logit.astype(F32)))[:, :, None, None]
    i = jnp.arange(c, dtype=F32)
    diff = i[:, None] - i[None, :]
    m_f = jnp.where(diff >= 0, jnp.exp(jnp.maximum(diff, 0.0) * lg[0]), 0.0)
    m_b = jnp.where(diff <= 0, jnp.exp(jnp.maximum(-diff, 0.0) * lg[1]), 0.0)
    mask = jnp.stack([m_f, m_b])
    ones = jnp.ones((1, 1, 1, RET_DK), F32)
    q_f = jnp.exp((i + 1.0)[None, :, None] * lg[0]) * ones[0]
    q_b = jnp.exp((c - i)[None, :, None] * lg[1]) * ones[0]
    k_f = jnp.exp((c - 1.0 - i)[None, :, None] * lg[0]) * ones[0]
    k_b = jnp.exp(i[None, :, None] * lg[1]) * ones[0]
    c_dec = jnp.exp(c * lg) * jnp.ones((1, 1, RET_DK, RET_DV), F32)
    return mask, jnp.stack([q_f, q_b]), jnp.stack([k_f, k_b]), c_dec


def _rope_tables(seq_len):
    quarter = RET_DK // 4
    pos = np.arange(seq_len)
    freqs = 1.0 / (ROPE_BASE ** (jnp.arange(quarter, dtype=F32) / quarter))
    a_row = jnp.asarray(pos // GRID_W, F32)[:, None] * freqs[None, :]
    a_col = jnp.asarray(pos % GRID_W, F32)[:, None] * freqs[None, :]
    cos = jnp.concatenate([jnp.cos(a_row)] * 2 + [jnp.cos(a_col)] * 2, axis=-1)
    sin = jnp.concatenate([-jnp.sin(a_row), jnp.sin(a_row), -jnp.sin(a_col), jnp.sin(a_col)], axis=-1)
    return jnp.tile(cos, (1, RET_HEADS)), jnp.tile(sin, (1, RET_HEADS))


def _retention_kernel(*refs, seq_len, rope):
    if rope:
        (q_ref, k_ref, v_ref, g_ref, s0_ref, mask_ref, qd_ref, kd_ref, cd_ref, gnw_ref, gnb_ref,
         cos_ref, sin_ref, o_ref, sfin_ref, qs_ref, ks_ref, acc_ref) = refs
    else:
        (q_ref, k_ref, v_ref, g_ref, s0_ref, mask_ref, qd_ref, kd_ref, cd_ref, gnw_ref, gnb_ref,
         o_ref, sfin_ref, qs_ref, ks_ref, acc_ref) = refs
    c = RET_CHUNK
    n = seq_len // c
    quarter = RET_DK // 4
    q = q_ref[...]
    k = k_ref[...] * (RET_DK ** -0.5)
    if rope:
        lane = lax.broadcasted_iota(jnp.int32, q.shape, 1)
        first = (lane % (2 * quarter)) < quarter
        width = q.shape[1]

        def rot(x):
            swapped = jnp.where(first, pltpu.roll(x, width - quarter, 1), pltpu.roll(x, quarter, 1))
            return x * cos_ref[...] + swapped * sin_ref[...]

        q = rot(q)
        k = rot(k)
    qs_ref[...] = q
    ks_ref[...] = k

    sfin_ref[...] = s0_ref[...]

    def chain(h, d, ci):
        ksl = slice(h * RET_DK, (h + 1) * RET_DK)
        vsl = slice(h * RET_DV, (h + 1) * RET_DV)
        cc = ci if d == 0 else n - 1 - ci
        rows = pl.ds(pl.multiple_of(cc * c, c), c)
        qc = qs_ref[rows, ksl]
        kc = ks_ref[rows, ksl]
        vc = v_ref[rows, vsl]
        att = _dot_nt(qc, kc) * mask_ref[d, h]
        s = sfin_ref[0, d, h]
        yield
        acc_ref[d, rows, vsl] = _dot(att, vc) + _dot(qc * qd_ref[d, h], s)
        yield
        sfin_ref[0, d, h] = s * cd_ref[d, h] + _dot_tn(kc * kd_ref[d, h], vc)

    def step(ci, carry):
        _lockstep([chain(h, d, ci) for h in range(RET_HEADS) for d in range(2)])
        return carry

    lax.fori_loop(0, n, step, 0)

    for h in range(RET_HEADS):
        vsl = slice(h * RET_DV, (h + 1) * RET_DV)
        y = _ln(acc_ref[0, :, vsl] + acc_ref[1, :, vsl], RET_GN_EPS) * gnw_ref[:, vsl] + gnb_ref[:, vsl]
        g = g_ref[:, vsl]
        o_ref[:, vsl] = y * (g * _sigmoid(g))


def _retention(z, s0, tables, gn_w, gn_b, n_seq, seq_len, row_blk0, rope_tabs):
    mask, qd, kd, cd = tables
    rope = rope_tabs is not None
    qk_w = RET_HEADS * RET_DK
    v_w = RET_HEADS * RET_DV
    full = lambda a: pl.BlockSpec(a.shape, lambda b: (0,) * a.ndim)
    in_specs = [pl.BlockSpec((seq_len, qk_w), lambda b: (row_blk0 + b, OFF_RET_QK // qk_w)),
                pl.BlockSpec((seq_len, qk_w), lambda b: (row_blk0 + b, OFF_RET_QK // qk_w + 1)),
                pl.BlockSpec((seq_len, v_w), lambda b: (row_blk0 + b, OFF_RET_V // v_w)),
                pl.BlockSpec((seq_len, v_w), lambda b: (row_blk0 + b, OFF_RET_V // v_w + 1)),
                pl.BlockSpec((1, 2, RET_HEADS, RET_DK, RET_DV), lambda b: (b, 0, 0, 0, 0)),
                full(mask), full(qd), full(kd), full(cd), full(gn_w), full(gn_b)]
    args = [z, z, z, z, s0, mask, qd, kd, cd, gn_w, gn_b]
    if rope:
        in_specs += [full(rope_tabs[0]), full(rope_tabs[1])]
        args += list(rope_tabs)
    return pl.pallas_call(
        functools.partial(_retention_kernel, seq_len=seq_len, rope=rope),
        grid=(n_seq,),
        in_specs=in_specs,
        out_specs=[pl.BlockSpec((seq_len, v_w), lambda b: (b, 0)),
                   pl.BlockSpec((1, 2, RET_HEADS, RET_DK, RET_DV), lambda b: (b, 0, 0, 0, 0))],
        out_shape=[jax.ShapeDtypeStruct((n_seq * seq_len, v_w), F32),
                   jax.ShapeDtypeStruct((n_seq, 2, RET_HEADS, RET_DK, RET_DV), F32)],
        scratch_shapes=[pltpu.VMEM((seq_len, qk_w), F32), pltpu.VMEM((seq_len, qk_w), F32),
                        pltpu.VMEM((2, seq_len, v_w), F32)],
        compiler_params=_cparams("parallel"),
        name="retention_rope" if rope else "retention",
    )(*args)


def _rwkv_prep_kernel(zm_ref, zl_ref, pm_ref, nm_ref, pl_ref, nl_ref, mum_ref, mul_ref,
                      w0_ref, wup_ref, a0_ref, aup_ref, gup_ref, kk_ref, ka_ref, rk_ref, ones_ref, tri_ref,
                      r_out, kap_out, v_out, lw_out, cs_out, kd_out, beta_out, g_out, bonus_out,
                      *, tiles_per_seq_of_tile):
    i = pl.program_id(0)
    pos, per = tiles_per_seq_of_tile(i)
    has_prev = (pos != 0).astype(F32)
    has_next = (pos != per - 1).astype(F32)
    tt = zm_ref.shape[0]

    def shift(z_ref, p_ref, n_ref, mu_ref):
        z = z_ref[...]
        row = lax.broadcasted_iota(jnp.int32, z.shape, 0)
        prev = jnp.where(row == 0, p_ref[7:8, :] * has_prev, pltpu.roll(z, 1, 0))
        nxt = jnp.where(row == tt - 1, n_ref[0:1, :] * has_next, pltpu.roll(z, tt - 1, 0))
        return z + mu_ref[...] * (0.5 * (prev + nxt) - z)

    zm = shift(zm_ref, pm_ref, nm_ref, mum_ref)
    zl = shift(zl_ref, pl_ref, nl_ref, mul_ref)
    w = RWKV_WIDTH
    r_c, k_c, v_c = zm[:, 0:w], zm[:, w:2 * w], zm[:, 2 * w:3 * w]
    ones = ones_ref[...]
    g_out[...] = _dot(_sigmoid(zl[:, 256:384]), gup_ref[...])
    kk = k_c * kk_ref[...]
    ss = _dot_exact_rhs(kk * kk, ones)
    kap = kk * lax.rsqrt(jnp.maximum(ss, 1e-24))
    bonus_out[...] = _dot_exact_rhs(r_c * k_c * rk_ref[...], ones) * v_c
    for h in range(RWKV_HEADS):
        sl = slice(h * RWKV_N, (h + 1) * RWKV_N)
        r_out[h] = r_c[:, sl]
        kap_out[h] = kap[:, sl]
        v_out[h] = v_c[:, sl]
    for d in range(2):
        wl = zl[:, d * 64:(d + 1) * 64]
        al = zl[:, 128 + d * 64:128 + (d + 1) * 64]
        lw = -RWKV_DECAY_SCALE * _sigmoid(w0_ref[d] + _dot(jnp.tanh(wl), wup_ref[d]))
        a = _sigmoid(a0_ref[d] + _dot(al, aup_ref[d]))
        k_d = k_c * (1.0 + (a - 1.0) * ka_ref[...])
        beta = kap * a
        cs = _dot_exact_lhs(tri_ref[d], lw)
        for h in range(RWKV_HEADS):
            sl = slice(h * RWKV_N, (h + 1) * RWKV_N)
            lw_out[d, h] = lw[:, sl]
            cs_out[d, h] = cs[:, sl]
            kd_out[d, h] = k_d[:, sl]
            beta_out[d, h] = beta[:, sl]


def _rwkv_prep(z, p, tiles_per_seq_of_tile, tt):
    m = z.shape[0]
    nt = m // tt
    w = RWKV_WIDTH
    mb, lb = OFF_RW_MAIN // (3 * w), OFF_RW_LORA // 384
    hm = tt // 8
    last8 = m // 8 - 1
    prev_idx = lambda i: jnp.maximum(i * hm - 1, 0)
    next_idx = lambda i: jnp.minimum((i + 1) * hm, last8)
    full = lambda a: pl.BlockSpec(a.shape, lambda i: (0,) * a.ndim)
    heads = jnp.arange(w) // RWKV_N
    ones = (heads[:, None] == heads[None, :]).astype(BF16)
    t = np.arange(tt)
    same = (t[:, None] // RWKV_CHUNK) == (t[None, :] // RWKV_CHUNK)
    tri = jnp.asarray(np.stack([same & (t[None, :] <= t[:, None]), same & (t[None, :] >= t[:, None])]), BF16)
    params = [p['mu_main'], p['mu_lora'], p['rwkv_w0'], p['rwkv_w_up'], p['rwkv_a0'], p['rwkv_a_up'],
              p['rwkv_g_up'], p['rwkv_k_k'], p['rwkv_k_a'], p['rwkv_r_k'], ones, tri]
    hd = lambda: jax.ShapeDtypeStruct((RWKV_HEADS, m, RWKV_N), F32)
    dhd = lambda: jax.ShapeDtypeStruct((2, RWKV_HEADS, m, RWKV_N), F32)
    tok = lambda: jax.ShapeDtypeStruct((m, w), F32)
    hd_spec = pl.BlockSpec((RWKV_HEADS, tt, RWKV_N), lambda i: (0, i, 0))
    dhd_spec = pl.BlockSpec((2, RWKV_HEADS, tt, RWKV_N), lambda i: (0, 0, i, 0))
    tok_spec = pl.BlockSpec((tt, w), lambda i: (i, 0))
    return pl.pallas_call(
        functools.partial(_rwkv_prep_kernel, tiles_per_seq_of_tile=tiles_per_seq_of_tile),
        grid=(nt,),
        in_specs=[pl.BlockSpec((tt, 3 * w), lambda i: (i, mb)),
                  pl.BlockSpec((tt, 384), lambda i: (i, lb)),
                  pl.BlockSpec((8, 3 * w), lambda i: (prev_idx(i), mb)),
                  pl.BlockSpec((8, 3 * w), lambda i: (next_idx(i), mb)),
                  pl.BlockSpec((8, 384), lambda i: (prev_idx(i), lb)),
                  pl.BlockSpec((8, 384), lambda i: (next_idx(i), lb))] + [full(a) for a in params],
        out_specs=[hd_spec, hd_spec, hd_spec, dhd_spec, dhd_spec, dhd_spec, dhd_spec, tok_spec, tok_spec],
        out_shape=[hd(), hd(), hd(), dhd(), dhd(), dhd(), dhd(), tok(), tok()],
        compiler_params=_cparams("parallel"),
        name="rwkv_prep",
    )(z, z, z, z, z, z, *params)


def _dot3(a, b, dims):
    a_hi = a.astype(BF16)
    b_hi = b.astype(BF16)
    a_lo = (a - a_hi.astype(F32)).astype(BF16)
    b_lo = (b - b_hi.astype(F32)).astype(BF16)
    dn = (dims, ((), ()))
    out = lax.dot_general(a_lo, b_hi, dn, preferred_element_type=F32)
    out = out + lax.dot_general(a_hi, b_lo, dn, preferred_element_type=F32)
    return out + lax.dot_general(a_hi, b_hi, dn, preferred_element_type=F32)


_NN = ((1,), (0,))
_NT = ((1,), (1,))
_TN = ((0,), (0,))


def _dot1(a, b, dims):
    return lax.dot_general(a.astype(BF16), b.astype(BF16), (dims, ((), ())), preferred_element_type=F32)


def _dot_rhs2(a, b, dims):
    a_hi = a.astype(BF16)
    b_hi = b.astype(BF16)
    b_lo = (b - b_hi.astype(F32)).astype(BF16)
    dn = (dims, ((), ()))
    return (lax.dot_general(a_hi, b_lo, dn, preferred_element_type=F32)
            + lax.dot_general(a_hi, b_hi, dn, preferred_element_type=F32))


_RWKV_MM = _dot1
RWKV_CHAINS = 16
RWKV_MAX_GROUP = 8


def _rwkv_scan_kernel(r_ref, kap_ref, v_ref, lw_ref, cs_ref, kd_ref, beta_ref, h0_ref, o_ref, hfin_ref, *,
                      seq_len, group, heads):
    c = RWKV_CHUNK
    n = seq_len // c
    d = pl.program_id(0)
    sgn = 1 - 2 * d
    ti = lax.broadcasted_iota(jnp.int32, (c, c), 0) * sgn
    tj = lax.broadcasted_iota(jnp.int32, (c, c), 1) * sgn
    strict = tj < ti
    incl = tj <= ti
    eye_f = (ti == tj).astype(F32)

    def chunk_terms(hh, cc):
        rows = pl.ds(pl.multiple_of(cc * c, c), c)
        r = r_ref[hh, rows, :]
        kap = kap_ref[hh, rows, :]
        v = v_ref[hh, rows, :]
        lw = lw_ref[0, hh, rows, :]
        cs = cs_ref[0, hh, rows, :]
        kd = kd_ref[0, hh, rows, :]
        beta = beta_ref[0, hh, rows, :]
        tot = jnp.sum(lw, axis=0, keepdims=True)
        kap_t = kap * jnp.exp(cs - lw)
        r_t = r * jnp.exp(cs)
        e_ncs = jnp.exp(-cs)
        k_t = kd * e_ncs
        b_t = beta * e_ncs
        e_end = jnp.exp(tot - cs)
        k_end = kd * e_end
        b_end = beta * e_end
        yield
        l_b = jnp.where(strict, _RWKV_MM(kap_t, b_t, _NT), 0.0)
        l_k = jnp.where(strict, _RWKV_MM(kap_t, k_t, _NT), 0.0)
        m_b = jnp.where(incl, _RWKV_MM(r_t, b_t, _NT), 0.0)
        m_k = jnp.where(incl, _RWKV_MM(r_t, k_t, _NT), 0.0)
        yield
        x = -l_b
        y = jnp.concatenate([kap_t, _RWKV_MM(l_k, v, _NN)], axis=1)
        yield
        y = y + _dot_rhs2(x, y, _NN)
        pw = x
        for _ in range(5):
            pw = _RWKV_MM(pw, pw, _NN)
            yield
            y = y + _dot_rhs2(pw, y, _NN)
        yield
        mby = _RWKV_MM(m_b, y, _NN)
        bty = _RWKV_MM(b_end, y, _TN)
        rr = r_t - mby[:, :RWKV_N]
        o0 = _RWKV_MM(m_k, v, _NN) - mby[:, RWKV_N:]
        pc = eye_f * jnp.exp(tot) - bty[:, :RWKV_N]
        qc = _RWKV_MM(k_end, v, _TN) - bty[:, RWKV_N:]
        return rows, rr, o0, pc, qc

    def step(gi, hstates):
        keys = [(hh, u) for u in range(group) for hh in range(heads)]
        terms = dict(zip(keys, _lockstep(
            [chunk_terms(hh, (gi * group + u) + d * (n - 1 - 2 * (gi * group + u))) for hh, u in keys])))
        hstates = list(hstates)
        for u in range(group):
            for hh in range(heads):
                rows, rr, o0, pc, qc = terms[(hh, u)]
                o_ref[0, hh, rows, :] = _dot3(rr, hstates[hh], _NN) + o0
                hstates[hh] = _dot3(pc, hstates[hh], _NN) + qc
        return tuple(hstates)

    fin = lax.fori_loop(0, n // group, step, tuple(h0_ref[0, 0, hh] for hh in range(heads)))
    for hh in range(heads):
        hfin_ref[0, 0, hh] = fin[hh]


def _rwkv_scan(r, kap, v, lw, cs, kd, beta, h0, n_seq, seq_len, row_blk0):
    group = min(seq_len // RWKV_CHUNK, RWKV_MAX_GROUP)
    hp = RWKV_CHAINS // group
    hd_spec = pl.BlockSpec((hp, seq_len, RWKV_N), lambda d, h, b: (h, row_blk0 + b, 0))
    dhd_spec = pl.BlockSpec((1, hp, seq_len, RWKV_N), lambda d, h, b: (d, h, row_blk0 + b, 0))
    st_spec = pl.BlockSpec((1, 1, hp, RWKV_N, RWKV_N), lambda d, h, b: (b, d, h, 0, 0))
    return pl.pallas_call(
        functools.partial(_rwkv_scan_kernel, seq_len=seq_len, group=group, heads=hp),
        grid=(2, RWKV_HEADS // hp, n_seq),
        in_specs=[hd_spec, hd_spec, hd_spec, dhd_spec, dhd_spec, dhd_spec, dhd_spec, st_spec],
        out_specs=[pl.BlockSpec((1, hp, seq_len, RWKV_N), lambda d, h, b: (d, h, b, 0)), st_spec],
        out_shape=[jax.ShapeDtypeStruct((2, RWKV_HEADS, n_seq * seq_len, RWKV_N), F32),
                   jax.ShapeDtypeStruct((n_seq, 2, RWKV_HEADS, RWKV_N, RWKV_N), F32)],
        compiler_params=_cparams("parallel", "parallel", "parallel"),
        name="rwkv_scan",
    )(r, kap, v, lw, cs, kd, beta, h0)


def _merge_kernel(oscan_c_ref, oscan_l_ref, ona_c_ref, ona_l_ref, ort_c_ref, ort_l_ref, bonus_ref, grw_ref,
                  ga_ref, gb_ref, gc_ref, x_ref, gt_ref, gnw_ref, gnb_ref, wbr_ref, wout_ref, lng_ref, lnb_ref,
                  o_ref, orw_ref, *, ctx_tiles):
    is_ctx = pl.program_id(0) < ctx_tiles
    pick = lambda c_val, l_val: jnp.where(is_ctx, c_val, l_val)
    for h in range(RWKV_HEADS):
        sl = slice(h * RWKV_N, (h + 1) * RWKV_N)
        o = pick(oscan_c_ref[0, h] + oscan_c_ref[1, h], oscan_l_ref[0, h] + oscan_l_ref[1, h])
        y = _ln(o, RWKV_GN_EPS) * gnw_ref[h] + gnb_ref[h]
        orw_ref[:, sl] = (y + bonus_ref[:, sl]) * grw_ref[:, sl]
    merged = (_sigmoid(ga_ref[...]) * _dot(pick(ona_c_ref[...], ona_l_ref[...]), wbr_ref[0])
              + _sigmoid(gb_ref[...]) * _dot(pick(ort_c_ref[...], ort_l_ref[...]), wbr_ref[1])
              + _sigmoid(gc_ref[...]) * _dot(orw_ref[...], wbr_ref[2]))
    t = _dot(merged, wout_ref[...])
    o_ref[...] = _ln(DEEPNORM_ALPHA * x_ref[...] + gt_ref[0] * t) * lng_ref[...] + lnb_ref[...]


def _merge(oscan, o_na, o_rt, bonus, g_rw, z, x, mod3, p, row_of_tile, tm):
    m = x.shape[0]
    bw = RWKV_WIDTH
    ctx_tiles = o_na[0].shape[0] // tm
    full = lambda a: pl.BlockSpec(a.shape, lambda i: (0,) * a.ndim)
    tok = pl.BlockSpec((tm, bw), lambda i: (i, 0))
    c_row = lambda i: jnp.minimum(i, ctx_tiles - 1)
    l_row = lambda i: jnp.maximum(i - ctx_tiles, 0)
    tok_c = pl.BlockSpec((tm, bw), lambda i: (c_row(i), 0))
    tok_l = pl.BlockSpec((tm, bw), lambda i: (l_row(i), 0))
    params = [p['rwkv_gn_w'], p['rwkv_gn_b'], p['w_br'], p['w_out'], p['ln_a_g'], p['ln_a_b']]
    return pl.pallas_call(
        functools.partial(_merge_kernel, ctx_tiles=ctx_tiles),
        grid=(m // tm,),
        in_specs=[pl.BlockSpec((2, RWKV_HEADS, tm, RWKV_N), lambda i: (0, 0, c_row(i), 0)),
                  pl.BlockSpec((2, RWKV_HEADS, tm, RWKV_N), lambda i: (0, 0, l_row(i), 0)),
                  tok_c, tok_l, tok_c, tok_l, tok, tok,
                  pl.BlockSpec((tm, D_MODEL), lambda i: (i, 0)),
                  pl.BlockSpec((tm, D_MODEL), lambda i: (i, 1)),
                  pl.BlockSpec((tm, D_MODEL), lambda i: (i, 2)),
                  pl.BlockSpec((tm, D_MODEL), lambda i: (i, 0)),
                  pl.BlockSpec((1, 1, D_MODEL), lambda i: (row_of_tile(i, tm), 0, 2))]
                 + [full(a) for a in params],
        out_specs=pl.BlockSpec((tm, D_MODEL), lambda i: (i, 0)),
        out_shape=jax.ShapeDtypeStruct((m, D_MODEL), F32),
        scratch_shapes=[pltpu.VMEM((tm, bw), F32)],
        compiler_params=_cparams("parallel"),
        name="merge_out_proj",
    )(*oscan, *o_na, *o_rt, bonus, g_rw, z, z, z, x, mod3, *params)


def _top_values(s, k):
    out = []
    for _ in range(k):
        m = jnp.max(s, axis=0, keepdims=True)
        out.append(m)
        s = jnp.where(s == m, -jnp.inf, s)
    return out


def _peer_gate_kernel(x_ref, sh_ref, sc_ref, wq_ref, keys_ref, ht_ref, cut_ref, s1_ref, e0_ref, e1_ref):
    h = _ln(x_ref[...]) * (1.0 + sc_ref[0]) + sh_ref[0]
    ht = h.T.astype(BF16)
    ht_ref[...] = ht
    qt = jnp.dot(wq_ref[...], ht, preferred_element_type=F32)
    half = keys_ref.shape[-1]
    for hd in range(PEER_HEADS):
        scores, tops = [], []
        for part in range(2):
            row0 = (hd * 2 + part) * half
            s = _dot(keys_ref[hd, part], qt[row0:row0 + half, :])
            scores.append(s)
            tops.append(_top_values(s, PEER_TOPK))
        pairs = [tops[0][a] + tops[1][b] for a in range(PEER_TOPK) for b in range(PEER_TOPK // (a + 1))]
        pairs += [jnp.full_like(pairs[0], -jnp.inf)] * (-len(pairs) % 8)
        best = _top_values(jnp.concatenate(pairs, axis=0), PEER_TOPK)
        zsum = jnp.exp(best[0] - best[0])
        for t in range(1, PEER_TOPK):
            zsum = zsum + jnp.exp(best[t] - best[0])
        tau = best[PEER_TOPK - 1]
        cut = jnp.full(scores[0].shape, jnp.inf, F32)
        for t1 in tops[1]:
            cut = jnp.where(scores[0] + t1 >= tau, t1, cut)
        cut_ref[hd] = cut
        s1_ref[hd] = scores[1]
        e0_ref[hd] = jnp.exp(scores[0] - tops[0][0])
        e1_ref[hd] = jnp.exp(scores[1] - tops[1][0]) / zsum


def _peer_gate(x, mod3, wq_t, keys, row_of_tile, tn):
    m = x.shape[0]
    full = lambda a: pl.BlockSpec(a.shape, lambda i: (0,) * a.ndim)
    sk = pl.BlockSpec((PEER_HEADS, PEER_NKEYS, tn), lambda i: (0, 0, i))
    sk_shape = jax.ShapeDtypeStruct((PEER_HEADS, PEER_NKEYS, m), F32)
    return pl.pallas_call(
        _peer_gate_kernel,
        grid=(m // tn,),
        in_specs=[pl.BlockSpec((tn, D_MODEL), lambda i: (i, 0)),
                  pl.BlockSpec((1, 1, D_MODEL), lambda i: (row_of_tile(i, tn), 0, 3)),
                  pl.BlockSpec((1, 1, D_MODEL), lambda i: (row_of_tile(i, tn), 0, 4)),
                  full(wq_t), full(keys)],
        out_specs=[pl.BlockSpec((D_MODEL, tn), lambda i: (0, i)), sk, sk, sk, sk],
        out_shape=[jax.ShapeDtypeStruct((D_MODEL, m), BF16), sk_shape, sk_shape, sk_shape, sk_shape],
        compiler_params=_cparams("parallel"),
        name="peer_gate",
    )(x, mod3, mod3, wq_t, keys)


def _gelu_exact(x):
    return 0.5 * x * (1.0 + lax.erf(x * (2.0 ** -0.5)))


PEER_SUB_ROWS = 32
PEER_SUB_LANES = 256


def _peer_expert_kernel(ht_ref, cut_ref, s1_ref, e0_ref, e1_ref, u_ref, vt_ref, x_ref, gt_ref,
                        lng_ref, lnb_ref, o_ref, acc_ref, st_ref, st2_ref, act_ref, *, groups):
    e = pl.program_id(1)
    tn = ht_ref.shape[1]

    @pl.when(e == 0)
    def _():
        acc_ref[...] = jnp.zeros_like(acc_ref)

    def pre_activations(g, dst_ref):
        rows = pl.ds(pl.multiple_of(g * PEER_NKEYS, PEER_NKEYS), PEER_NKEYS)
        dst_ref[...] = jnp.dot(u_ref[rows, :], ht_ref[...], preferred_element_type=F32)

    def gated_activations(g, src_ref):
        i = e * groups + g
        row0 = pl.multiple_of(g * PEER_NKEYS, PEER_NKEYS)
        cut_row = [cut_ref[hd, pl.ds(i, 1), :] for hd in range(PEER_HEADS)]
        e0_row = [e0_ref[hd, pl.ds(i, 1), :] for hd in range(PEER_HEADS)]
        for lb in range(tn // PEER_SUB_LANES):
            ls = slice(lb * PEER_SUB_LANES, (lb + 1) * PEER_SUB_LANES)
            cut = [c[:, ls] for c in cut_row]
            e0 = [c[:, ls] for c in e0_row]
            for rb in range(PEER_NKEYS // PEER_SUB_ROWS):
                js = slice(rb * PEER_SUB_ROWS, (rb + 1) * PEER_SUB_ROWS)
                gate = None
                for hd in range(PEER_HEADS):
                    term = jnp.where(s1_ref[hd, js, ls] >= cut[hd], e1_ref[hd, js, ls], 0.0) * e0[hd]
                    gate = term if gate is None else gate + term
                act = _gelu_exact(src_ref[js, ls]) * gate
                act_ref[pl.ds(row0 + rb * PEER_SUB_ROWS, PEER_SUB_ROWS), ls] = act.astype(BF16)

    pre_activations(0, st_ref)

    def pair(p, carry):
        gated_activations(2 * p, st_ref)
        pre_activations(2 * p + 1, st2_ref)
        gated_activations(2 * p + 1, st2_ref)
        pre_activations(jnp.minimum(2 * p + 2, groups - 1), st_ref)
        return carry

    lax.fori_loop(0, groups // 2, pair, 0)
    acc_ref[...] += jnp.dot(vt_ref[...], act_ref[...], preferred_element_type=F32)

    @pl.when(e == pl.num_programs(1) - 1)
    def _():
        f = acc_ref[...].T
        o_ref[...] = _ln(DEEPNORM_ALPHA * x_ref[...] + gt_ref[0] * f) * lng_ref[...] + lnb_ref[...]


def _peer_expert(ht, cut, s1, e0, e1, u, v_t, x, mod3, ln_g, ln_b, row_of_tile, tn, eb):
    m = x.shape[0]
    n_exp = u.shape[0]
    groups = eb // PEER_NKEYS
    sk = pl.BlockSpec((PEER_HEADS, PEER_NKEYS, tn), lambda i, e: (0, 0, i))
    full = lambda a: pl.BlockSpec(a.shape, lambda i, e: (0,) * a.ndim)
    return pl.pallas_call(
        functools.partial(_peer_expert_kernel, groups=groups),
        grid=(m // tn, n_exp // eb),
        in_specs=[pl.BlockSpec((D_MODEL, tn), lambda i, e: (0, i)), sk, sk, sk, sk,
                  pl.BlockSpec((eb, D_MODEL), lambda i, e: (e, 0)),
                  pl.BlockSpec((D_MODEL, eb), lambda i, e: (0, e)),
                  pl.BlockSpec((tn, D_MODEL), lambda i, e: (i, 0)),
                  pl.BlockSpec((1, 1, D_MODEL), lambda i, e: (row_of_tile(i, tn), 0, 5)),
                  full(ln_g), full(ln_b)],
        out_specs=pl.BlockSpec((tn, D_MODEL), lambda i, e: (i, 0)),
        out_shape=jax.ShapeDtypeStruct((m, D_MODEL), F32),
        scratch_shapes=[pltpu.VMEM((D_MODEL, tn), F32), pltpu.VMEM((PEER_NKEYS, tn), F32),
                        pltpu.VMEM((PEER_NKEYS, tn), F32), pltpu.VMEM((eb, tn), BF16)],
        compiler_params=_cparams("parallel", "arbitrary"),
        name="peer_experts",
    )(ht, cut, s1, e0, e1, u, v_t, x, mod3, ln_g, ln_b)


def _permute_in_proj(w_in):
    na, qk, rv = 3 * 512, 2 * 256, 2 * 512
    o_na, o_qk, o_rv, o_rw, o_g = 0, na, na + qk, na + qk + rv, na + qk + rv + 1920
    return jnp.concatenate([w_in[:, o_g:], w_in[:, o_na:o_qk], w_in[:, o_qk:o_rv], w_in[:, o_rv:o_rw],
                            w_in[:, o_rw:o_g]], axis=1)


def kernel(x_prompt, x_sample, cache_na_k, cache_na_v, state_ret, state_rwkv, c, c_ctx, w_mod, b_mod, w_in, na_rpb, ret_decay_logit, ret_gn_w, ret_gn_b, rwkv_mu, rwkv_w0, rwkv_w_up, rwkv_a0, rwkv_a_up, rwkv_g_up, rwkv_k_k, rwkv_k_a, rwkv_r_k, rwkv_gn_w, rwkv_gn_b, w_br, w_out, ln_a_g, ln_a_b, ln_f_g, ln_f_b, peer_wq, peer_keys, peer_u, peer_v):
    n_ctx, ctx_len, d = x_prompt.shape
    n_lat, lat_len, _ = x_sample.shape
    m_ctx = n_ctx * ctx_len
    assert m_ctx % lat_len == 0 and n_lat == 2
    lat_blk0 = m_ctx // lat_len

    def row_of_tile(i, tm):
        return jnp.where(i < m_ctx // tm, 0, 1 + (i - m_ctx // tm) // (lat_len // tm))

    prep_tt = 256
    assert ctx_len == prep_tt

    def tiles_per_seq_of_tile(i):
        lat = i >= m_ctx // prep_tt
        per = jnp.where(lat, lat_len // prep_tt, 1)
        pos = jnp.where(lat, (i - m_ctx // prep_tt) % (lat_len // prep_tt), 0)
        return pos, per

    x = jnp.concatenate([x_prompt.reshape(m_ctx, d), x_sample.reshape(n_lat * lat_len, d)], axis=0)
    c8 = jnp.concatenate([c_ctx[None], c, jnp.zeros((8 - 1 - n_lat, d), F32)], axis=0)
    mod = _modulation(c8, w_mod, b_mod)
    rope_tabs = _rope_tables(lat_len)
    zeros_ret = jnp.zeros((n_ctx, 2, RET_HEADS, RET_DK, RET_DV), F32)
    zeros_rw = jnp.zeros((n_ctx, 2, RWKV_HEADS, RWKV_N, RWKV_N), F32)

    nk, nv, sr, sw = [], [], [], []
    for l in range(DEPTH):
        mod3 = mod[l].reshape(8, 1, 6 * d)
        w_in_p = _permute_in_proj(w_in[l]).astype(BF16)
        z = _lnmod_matmul(x, mod3, w_in_p, row_of_tile, 1024, 1152, 0, 1)

        o_na_c, k_cache, v_cache = _ctx_attention(z, n_ctx, ctx_len)
        o_na_l = _na_latent(z, cache_na_k[:, l], cache_na_v[:, l], _na_bias_table(na_rpb[l], lat_len // GRID_W),
                            n_lat, lat_len, lat_blk0)
        nk.append(k_cache)
        nv.append(v_cache)

        tables = _ret_decay_tables(ret_decay_logit[l])
        gn_w, gn_b = ret_gn_w[l][None], ret_gn_b[l][None]
        o_rt_c, s_ret = _retention(z, zeros_ret, tables, gn_w, gn_b, n_ctx, ctx_len, 0, None)
        o_rt_l, _ = _retention(z, state_ret[:, l], tables, gn_w, gn_b, n_lat, lat_len, lat_blk0, rope_tabs)
        sr.append(s_ret)

        mu = rwkv_mu[l]
        prm = {
            'mu_main': mu[None, :1536], 'mu_lora': mu[None, 1536:],
            'rwkv_w0': rwkv_w0[l][:, None], 'rwkv_w_up': rwkv_w_up[l].astype(BF16),
            'rwkv_a0': rwkv_a0[l][:, None], 'rwkv_a_up': rwkv_a_up[l].astype(BF16),
            'rwkv_g_up': rwkv_g_up[l].astype(BF16), 'rwkv_k_k': rwkv_k_k[l][None],
            'rwkv_k_a': rwkv_k_a[l][None], 'rwkv_r_k': rwkv_r_k[l].reshape(1, RWKV_WIDTH),
        }
        r, kap, v, lw, cs, kd, beta, g_rw, bonus = _rwkv_prep(z, prm, tiles_per_seq_of_tile, prep_tt)
        o_c, h_fin = _rwkv_scan(r, kap, v, lw, cs, kd, beta, zeros_rw, n_ctx, ctx_len, 0)
        h0_lat = jnp.swapaxes(state_rwkv[:, l], -1, -2)
        o_l, _ = _rwkv_scan(r, kap, v, lw, cs, kd, beta, h0_lat, n_lat, lat_len, lat_blk0)
        sw.append(jnp.swapaxes(h_fin, -1, -2))

        mp = {
            'rwkv_gn_w': rwkv_gn_w[l].reshape(RWKV_HEADS, 1, RWKV_N),
            'rwkv_gn_b': rwkv_gn_b[l].reshape(RWKV_HEADS, 1, RWKV_N),
            'w_br': w_br[l].astype(BF16), 'w_out': w_out[l].astype(BF16),
            'ln_a_g': ln_a_g[l][None], 'ln_a_b': ln_a_b[l][None],
        }
        x = _merge((o_c, o_l), (o_na_c, o_na_l), (o_rt_c, o_rt_l), bonus, g_rw, z, x, mod3, mp, row_of_tile, 256)

        wq_t = peer_wq[l].T.astype(BF16)
        ht, cut, s1, e0, e1 = _peer_gate(x, mod3, wq_t, peer_keys[l].astype(BF16), row_of_tile, 256)
        x = _peer_expert(ht, cut, s1, e0, e1, peer_u[l].astype(BF16), peer_v[l].T.astype(BF16), x, mod3,
                         ln_f_g[l][None], ln_f_b[l][None], row_of_tile, 512, 2048)

    dt = x_prompt.dtype
    y_prompt = x[:m_ctx].reshape(n_ctx, ctx_len, d)
    y_sample = x[m_ctx:].reshape(n_lat, lat_len, d)
    return (y_prompt, y_sample, jnp.stack(nk, axis=1).astype(dt), jnp.stack(nv, axis=1).astype(dt),
            jnp.stack(sr, axis=1).astype(dt), jnp.stack(sw, axis=1).astype(dt))
```

```python
import functools
import math

import jax
import jax.numpy as jnp
import numpy as np
from jax import lax
from jax.experimental import pallas as pl
from jax.experimental.pallas import tpu as pltpu

F32 = jnp.float32
BF16 = jnp.bfloat16

D_MODEL = 1024
DEPTH = 2
GRID_W = 64
NA_HEADS = 8
NA_HEAD_DIM = 64
WIN_R = 8
WIN_C = 16
NEG_INF = -1e30
RET_HEADS = 4
RET_DK = 64
RET_DV = 128
RET_CHUNK = 64
RET_GN_EPS = 1e-5
ROPE_BASE = 10000.0
RWKV_HEADS = 8
RWKV_N = 64
RWKV_WIDTH = 512
RWKV_DECAY_SCALE = 0.606531
RWKV_GN_EPS = 64e-5
RWKV_CHUNK = 64
PEER_HEADS = 8
PEER_NKEYS = 128
PEER_TOPK = 16
LN_EPS = 1e-5
DEEPNORM_ALPHA = (2 * DEPTH) ** 0.25

OFF_GATE = 0
OFF_NA = 3072
OFF_RET_QK = 4608
OFF_RET_V = 5120
OFF_RW_MAIN = 6144
OFF_RW_LORA = 7680
P_IN = 8064

VMEM_LIMIT_BYTES = 56 * 1024 * 1024


def _cparams(*sem):
    return pltpu.CompilerParams(dimension_semantics=sem, vmem_limit_bytes=VMEM_LIMIT_BYTES)


def _ln(x, eps=LN_EPS):
    mu = jnp.mean(x, axis=-1, keepdims=True)
    xc = x - mu
    var = jnp.mean(xc * xc, axis=-1, keepdims=True)
    return xc * lax.rsqrt(var + eps)


def _sigmoid(x):
    return 1.0 / (1.0 + jnp.exp(-x))


def _dot(a, b):
    return jnp.dot(a.astype(BF16), b.astype(BF16), preferred_element_type=F32)


def _dot_nt(a, b):
    return lax.dot_general(a.astype(BF16), b.astype(BF16), (((1,), (1,)), ((), ())),
                           preferred_element_type=F32)


def _dot_tn(a, b):
    return lax.dot_general(a.astype(BF16), b.astype(BF16), (((0,), (0,)), ((), ())),
                           preferred_element_type=F32)


def _lockstep(gens):
    out = [None] * len(gens)
    live = list(range(len(gens)))
    while live:
        for i in list(live):
            try:
                next(gens[i])
            except StopIteration as done:
                out[i] = done.value
                live.remove(i)
    return out


def _split3(x):
    hi = x.astype(BF16)
    r1 = x - hi.astype(F32)
    mid = r1.astype(BF16)
    lo = (r1 - mid.astype(F32)).astype(BF16)
    return hi, mid, lo


def _dot_exact_lhs(sel, x):
    hi, mid, lo = _split3(x)
    s = sel.astype(BF16)
    out = jnp.dot(s, lo, preferred_element_type=F32)
    out = out + jnp.dot(s, mid, preferred_element_type=F32)
    return out + jnp.dot(s, hi, preferred_element_type=F32)


def _dot_exact_rhs(x, sel):
    hi, mid, lo = _split3(x)
    s = sel.astype(BF16)
    out = jnp.dot(lo, s, preferred_element_type=F32)
    out = out + jnp.dot(mid, s, preferred_element_type=F32)
    return out + jnp.dot(hi, s, preferred_element_type=F32)


def _mod_kernel(c_ref, w_ref, b_ref, o_ref):
    c = c_ref[...]
    s = c * _sigmoid(c)
    o_ref[0] = _dot(s, w_ref[0]) + b_ref[0]


def _modulation(c8, w_mod, b_mod):
    tn = 1536
    n = w_mod.shape[-1]
    return pl.pallas_call(
        _mod_kernel,
        grid=(DEPTH, n // tn),
        in_specs=[pl.BlockSpec((8, D_MODEL), lambda l, j: (0, 0)),
                  pl.BlockSpec((1, D_MODEL, tn), lambda l, j: (l, 0, j)),
                  pl.BlockSpec((1, 1, tn), lambda l, j: (l, 0, j))],
        out_specs=pl.BlockSpec((1, 8, tn), lambda l, j: (l, 0, j)),
        out_shape=jax.ShapeDtypeStruct((DEPTH, 8, n), F32),
        compiler_params=_cparams("parallel", "parallel"),
        name="modulation",
    )(c8, w_mod, b_mod.reshape(DEPTH, 1, n))


def _lnmod_matmul_kernel(x_ref, sh_ref, sc_ref, w_ref, o_ref, h_ref):
    @pl.when(pl.program_id(1) == 0)
    def _():
        h = _ln(x_ref[...]) * (1.0 + sc_ref[0]) + sh_ref[0]
        h_ref[...] = h.astype(BF16)

    o_ref[...] = jnp.dot(h_ref[...], w_ref[...], preferred_element_type=F32)


def _lnmod_matmul(x, mod3, w, row_of_tile, tm, tn, sh_blk, sc_blk):
    m = x.shape[0]
    n = w.shape[1]
    return pl.pallas_call(
        _lnmod_matmul_kernel,
        grid=(m // tm, n // tn),
        in_specs=[pl.BlockSpec((tm, D_MODEL), lambda i, j: (i, 0)),
                  pl.BlockSpec((1, 1, D_MODEL), lambda i, j: (row_of_tile(i, tm), 0, sh_blk)),
                  pl.BlockSpec((1, 1, D_MODEL), lambda i, j: (row_of_tile(i, tm), 0, sc_blk)),
                  pl.BlockSpec((D_MODEL, tn), lambda i, j: (0, j))],
        out_specs=pl.BlockSpec((tm, tn), lambda i, j: (i, j)),
        out_shape=jax.ShapeDtypeStruct((m, n), F32),
        scratch_shapes=[pltpu.VMEM((tm, D_MODEL), BF16)],
        compiler_params=_cparams("parallel", "arbitrary"),
        name="adaln_in_proj",
    )(x, mod3, mod3, w)


def _ctx_attn_kernel(q_ref, k_ref, v_ref, o_ref, kc_ref, vc_ref):
    scale = NA_HEAD_DIM ** -0.5
    for hh in range(2):
        sl = slice(hh * NA_HEAD_DIM, (hh + 1) * NA_HEAD_DIM)
        k = k_ref[:, sl]
        v = v_ref[:, sl]
        kc_ref[0, hh] = k
        vc_ref[0, hh] = v
        s = _dot_nt(q_ref[:, sl], k) * scale
        m = jnp.max(s, axis=-1, keepdims=True)
        p = jnp.exp(s - m)
        l = jnp.sum(p, axis=-1, keepdims=True)
        o_ref[:, sl] = _dot(p, v) / l


def _ctx_attention(z, n_seq, seq_len):
    cb = OFF_NA // 128
    cache_spec = pl.BlockSpec((1, 2, seq_len, NA_HEAD_DIM), lambda b, h: (b, h, 0, 0))
    cache_shape = jax.ShapeDtypeStruct((n_seq, NA_HEADS, seq_len, NA_HEAD_DIM), F32)
    return pl.pallas_call(
        _ctx_attn_kernel,
        grid=(n_seq, NA_HEADS // 2),
        in_specs=[pl.BlockSpec((seq_len, 128), lambda b, h: (b, cb + h)),
                  pl.BlockSpec((seq_len, 128), lambda b, h: (b, cb + 4 + h)),
                  pl.BlockSpec((seq_len, 128), lambda b, h: (b, cb + 8 + h))],
        out_specs=[pl.BlockSpec((seq_len, 128), lambda b, h: (b, h)), cache_spec, cache_spec],
        out_shape=[jax.ShapeDtypeStruct((n_seq * seq_len, NA_HEADS * NA_HEAD_DIM), F32), cache_shape, cache_shape],
        compiler_params=_cparams("parallel", "parallel"),
        name="ctx_attention",
    )(z, z, z)


def _na_bias_kernel(rpb_ref, rowsel_ref, colsel_ref, neg_ref, o_ref):
    picked = _dot_exact_lhs(rowsel_ref[...], rpb_ref[0])
    o_ref[0] = _dot_exact_rhs(picked, colsel_ref[...]) + neg_ref[...]


def _na_bias_table(rpb, rows):
    kr = min(WIN_R, rows)
    nh, n_dr, n_dc = rpb.shape
    rep = np.array([0, 1, 2, 3, 4, rows - 3, rows - 2, rows - 1])
    start = np.clip(rep - kr // 2, 0, rows - kr)
    dr = start[:, None] - rep[:, None] + WIN_R - 1 + np.arange(kr)[None, :]
    qcol = np.arange(GRID_W)
    kcol = np.arange(GRID_W)
    q_start = np.clip(qcol - WIN_C // 2, 0, GRID_W - WIN_C)
    rel = kcol[None, :] - q_start[:, None]
    valid = (rel >= 0) & (rel < WIN_C)
    dc = np.clip(kcol[None, :] - qcol[:, None], -(WIN_C - 1), WIN_C - 1) + WIN_C - 1
    row_sel = np.zeros((8 * kr, 16), np.float32)
    row_sel[np.arange(8 * kr), dr.reshape(-1)] = 1.0
    col_sel = np.zeros((32, GRID_W * GRID_W), np.float32)
    qq, kk = np.nonzero(valid)
    col_sel[dc[qq, kk], qq * GRID_W + kk] = 1.0
    neg = np.where(valid, 0.0, NEG_INF).astype(np.float32).reshape(1, -1)
    rpb_p = jnp.pad(rpb.astype(F32), ((0, 0), (0, 16 - n_dr), (0, 32 - n_dc)))
    full = lambda a: pl.BlockSpec(a.shape, lambda h: (0,) * a.ndim)
    consts = [jnp.asarray(row_sel, BF16), jnp.asarray(col_sel, BF16), jnp.asarray(neg)]
    b = pl.pallas_call(
        _na_bias_kernel,
        grid=(nh,),
        in_specs=[pl.BlockSpec((1, 16, 32), lambda h: (h, 0, 0))] + [full(a) for a in consts],
        out_specs=pl.BlockSpec((1, 8 * kr, GRID_W * GRID_W), lambda h: (h, 0, 0)),
        out_shape=jax.ShapeDtypeStruct((nh, 8 * kr, GRID_W * GRID_W), F32),
        compiler_params=_cparams("parallel"),
        name="na_bias",
    )(rpb_p, *consts)
    b = b.reshape(nh, 8, kr, GRID_W, GRID_W)
    return b.transpose(0, 1, 3, 2, 4).reshape(nh, 8, GRID_W, kr * GRID_W)


NA_ROWS_PER_STEP = 2


def _na_latent_kernel(q_ref, k_ref, v_ref, ck_ref, cv_ref, b_ref, o_ref, *, rows):
    scale = NA_HEAD_DIM ** -0.5
    kr = min(WIN_R, rows)

    def query_row(r, hh):
        start = jnp.clip(r - kr // 2, 0, rows - kr)
        typ = jnp.where(r < 4, r, jnp.where(r > rows - 4, r - (rows - 8), 4))
        q0 = pl.multiple_of(r * GRID_W, GRID_W)
        k0 = pl.multiple_of(start * GRID_W, GRID_W)
        sl = slice(hh * NA_HEAD_DIM, (hh + 1) * NA_HEAD_DIM)
        q = q_ref[pl.ds(q0, GRID_W), sl]
        kw = k_ref[pl.ds(k0, kr * GRID_W), sl]
        vw = v_ref[pl.ds(k0, kr * GRID_W), sl]
        s_w = _dot_nt(q, kw) * scale + b_ref[hh, typ]
        s_c = _dot_nt(q, ck_ref[0, hh]) * scale
        yield
        m = jnp.maximum(jnp.max(s_w, axis=-1, keepdims=True), jnp.max(s_c, axis=-1, keepdims=True))
        p_w = jnp.exp(s_w - m)
        p_c = jnp.exp(s_c - m)
        l = jnp.sum(p_w, axis=-1, keepdims=True) + jnp.sum(p_c, axis=-1, keepdims=True)
        yield
        o_ref[pl.ds(q0, GRID_W), sl] = (_dot(p_w, vw) + _dot(p_c, cv_ref[0, hh])) / l

    def body(rp, carry):
        _lockstep([query_row(rp * NA_ROWS_PER_STEP + u, hh) for u in range(NA_ROWS_PER_STEP) for hh in range(2)])
        return carry

    lax.fori_loop(0, rows // NA_ROWS_PER_STEP, body, 0)


def _na_latent(z, ctx_k, ctx_v, bias, n_seq, seq_len, row_blk0):
    cb = OFF_NA // 128
    rows = seq_len // GRID_W
    past = ctx_k.shape[2]
    return pl.pallas_call(
        functools.partial(_na_latent_kernel, rows=rows),
        grid=(n_seq, NA_HEADS // 2),
        in_specs=[pl.BlockSpec((seq_len, 128), lambda b, h: (row_blk0 + b, cb + h)),
                  pl.BlockSpec((seq_len, 128), lambda b, h: (row_blk0 + b, cb + 4 + h)),
                  pl.BlockSpec((seq_len, 128), lambda b, h: (row_blk0 + b, cb + 8 + h)),
                  pl.BlockSpec((1, 2, past, NA_HEAD_DIM), lambda b, h: (b, h, 0, 0)),
                  pl.BlockSpec((1, 2, past, NA_HEAD_DIM), lambda b, h: (b, h, 0, 0)),
                  pl.BlockSpec((2, 8, GRID_W, bias.shape[-1]), lambda b, h: (h, 0, 0, 0))],
        out_specs=pl.BlockSpec((seq_len, 128), lambda b, h: (b, h)),
        out_shape=jax.ShapeDtypeStruct((n_seq * seq_len, NA_HEADS * NA_HEAD_DIM), F32),
        compiler_params=_cparams("parallel", "parallel"),
        name="na_latent",
    )(z, z, z, ctx_k, ctx_v, bias)


def _ret_decay_tables(decay_logit):
    c = RET_CHUNK
    lg = jnp.log(jax.nn.sigmoid(decay_logit.astype(F32)))[:, :, None, None]
    i = jnp.arange(c, dtype=F32)
    diff = i[:, None] - i[None, :]
    m_f = jnp.where(diff >= 0, jnp.exp(jnp.maximum(diff, 0.0) * lg[0]), 0.0)
    m_b = jnp.where(diff <= 0, jnp.exp(jnp.maximum(-diff, 0.0) * lg[1]), 0.0)
    mask = jnp.stack([m_f, m_b])
    ones = jnp.ones((1, 1, 1, RET_DK), F32)
    q_f = jnp.exp((i + 1.0)[None, :, None] * lg[0]) * ones[0]
    q_b = jnp.exp((c - i)[None, :, None] * lg[1]) * ones[0]
    k_f = jnp.exp((c - 1.0 - i)[None, :, None] * lg[0]) * ones[0]
    k_b = jnp.exp(i[None, :, None] * lg[1]) * ones[0]
    c_dec = jnp.exp(c * lg) * jnp.ones((1, 1, RET_DK, RET_DV), F32)
    return mask, jnp.stack([q_f, q_b]), jnp.stack([k_f, k_b]), c_dec


def _rope_tables(seq_len):
    quarter = RET_DK // 4
    pos = np.arange(seq_len)
    freqs = 1.0 / (ROPE_BASE ** (jnp.arange(quarter, dtype=F32) / quarter))
    a_row = jnp.asarray(pos // GRID_W, F32)[:, None] * freqs[None, :]
    a_col = jnp.asarray(pos % GRID_W, F32)[:, None] * freqs[None, :]
    cos = jnp.concatenate([jnp.cos(a_row)] * 2 + [jnp.cos(a_col)] * 2, axis=-1)
    sin = jnp.concatenate([-jnp.sin(a_row), jnp.sin(a_row), -jnp.sin(a_col), jnp.sin(a_col)], axis=-1)
    return jnp.tile(cos, (1, RET_HEADS)), jnp.tile(sin, (1, RET_HEADS))


def _retention_kernel(*refs, seq_len, rope):
    if rope:
        (q_ref, k_ref, v_ref, g_ref, s0_ref, mask_ref, qd_ref, kd_ref, cd_ref, gnw_ref, gnb_ref,
         cos_ref, sin_ref, o_ref, sfin_ref, qs_ref, ks_ref, acc_ref) = refs
    else:
        (q_ref, k_ref, v_ref, g_ref, s0_ref, mask_ref, qd_ref, kd_ref, cd_ref, gnw_ref, gnb_ref,
         o_ref, sfin_ref, qs_ref, ks_ref, acc_ref) = refs
    c = RET_CHUNK
    n = seq_len // c
    quarter = RET_DK // 4
    q = q_ref[...]
    k = k_ref[...] * (RET_DK ** -0.5)
    if rope:
        lane = lax.broadcasted_iota(jnp.int32, q.shape, 1)
        first = (lane % (2 * quarter)) < quarter
        width = q.shape[1]

        def rot(x):
            swapped = jnp.where(first, pltpu.roll(x, width - quarter, 1), pltpu.roll(x, quarter, 1))
            return x * cos_ref[...] + swapped * sin_ref[...]

        q = rot(q)
        k = rot(k)
    qs_ref[...] = q
    ks_ref[...] = k

    sfin_ref[...] = s0_ref[...]

    def chain(h, d, ci):
        ksl = slice(h * RET_DK, (h + 1) * RET_DK)
        vsl = slice(h * RET_DV, (h + 1) * RET_DV)
        cc = ci if d == 0 else n - 1 - ci
        rows = pl.ds(pl.multiple_of(cc * c, c), c)
        qc = qs_ref[rows, ksl]
        kc = ks_ref[rows, ksl]
        vc = v_ref[rows, vsl]
        att = _dot_nt(qc, kc) * mask_ref[d, h]
        s = sfin_ref[0, d, h]
        yield
        acc_ref[d, rows, vsl] = _dot(att, vc) + _dot(qc * qd_ref[d, h], s)
        yield
        sfin_ref[0, d, h] = s * cd_ref[d, h] + _dot_tn(kc * kd_ref[d, h], vc)

    def step(ci, carry):
        _lockstep([chain(h, d, ci) for h in range(RET_HEADS) for d in range(2)])
        return carry

    lax.fori_loop(0, n, step, 0)

    for h in range(RET_HEADS):
        vsl = slice(h * RET_DV, (h + 1) * RET_DV)
        y = _ln(acc_ref[0, :, vsl] + acc_ref[1, :, vsl], RET_GN_EPS) * gnw_ref[:, vsl] + gnb_ref[:, vsl]
        g = g_ref[:, vsl]
        o_ref[:, vsl] = y * (g * _sigmoid(g))


def _retention(z, s0, tables, gn_w, gn_b, n_seq, seq_len, row_blk0, rope_tabs):
    mask, qd, kd, cd = tables
    rope = rope_tabs is not None
    qk_w = RET_HEADS * RET_DK
    v_w = RET_HEADS * RET_DV
    full = lambda a: pl.BlockSpec(a.shape, lambda b: (0,) * a.ndim)
    in_specs = [pl.BlockSpec((seq_len, qk_w), lambda b: (row_blk0 + b, OFF_RET_QK // qk_w)),
                pl.BlockSpec((seq_len, qk_w), lambda b: (row_blk0 + b, OFF_RET_QK // qk_w + 1)),
                pl.BlockSpec((seq_len, v_w), lambda b: (row_blk0 + b, OFF_RET_V // v_w)),
                pl.BlockSpec((seq_len, v_w), lambda b: (row_blk0 + b, OFF_RET_V // v_w + 1)),
                pl.BlockSpec((1, 2, RET_HEADS, RET_DK, RET_DV), lambda b: (b, 0, 0, 0, 0)),
                full(mask), full(qd), full(kd), full(cd), full(gn_w), full(gn_b)]
    args = [z, z, z, z, s0, mask, qd, kd, cd, gn_w, gn_b]
    if rope:
        in_specs += [full(rope_tabs[0]), full(rope_tabs[1])]
        args += list(rope_tabs)
    return pl.pallas_call(
        functools.partial(_retention_kernel, seq_len=seq_len, rope=rope),
        grid=(n_seq,),
        in_specs=in_specs,
        out_specs=[pl.BlockSpec((seq_len, v_w), lambda b: (b, 0)),
                   pl.BlockSpec((1, 2, RET_HEADS, RET_DK, RET_DV), lambda b: (b, 0, 0, 0, 0))],
        out_shape=[jax.ShapeDtypeStruct((n_seq * seq_len, v_w), F32),
                   jax.ShapeDtypeStruct((n_seq, 2, RET_HEADS, RET_DK, RET_DV), F32)],
        scratch_shapes=[pltpu.VMEM((seq_len, qk_w), F32), pltpu.VMEM((seq_len, qk_w), F32),
                        pltpu.VMEM((2, seq_len, v_w), F32)],
        compiler_params=_cparams("parallel"),
        name="retention_rope" if rope else "retention",
    )(*args)


def _rwkv_prep_kernel(zm_ref, zl_ref, pm_ref, nm_ref, pl_ref, nl_ref, mum_ref, mul_ref,
                      w0_ref, wup_ref, a0_ref, aup_ref, gup_ref, kk_ref, ka_ref, rk_ref, ones_ref, tri_ref,
                      r_out, kap_out, v_out, lw_out, cs_out, kd_out, beta_out, g_out, bonus_out,
                      *, tiles_per_seq_of_tile):
    i = pl.program_id(0)
    pos, per = tiles_per_seq_of_tile(i)
    has_prev = (pos != 0).astype(F32)
    has_next = (pos != per - 1).astype(F32)
    tt = zm_ref.shape[0]

    def shift(z_ref, p_ref, n_ref, mu_ref):
        z = z_ref[...]
        row = lax.broadcasted_iota(jnp.int32, z.shape, 0)
        prev = jnp.where(row == 0, p_ref[7:8, :] * has_prev, pltpu.roll(z, 1, 0))
        nxt = jnp.where(row == tt - 1, n_ref[0:1, :] * has_next, pltpu.roll(z, tt - 1, 0))
        return z + mu_ref[...] * (0.5 * (prev + nxt) - z)

    zm = shift(zm_ref, pm_ref, nm_ref, mum_ref)
    zl = shift(zl_ref, pl_ref, nl_ref, mul_ref)
    w = RWKV_WIDTH
    r_c, k_c, v_c = zm[:, 0:w], zm[:, w:2 * w], zm[:, 2 * w:3 * w]
    ones = ones_ref[...]
    g_out[...] = _dot(_sigmoid(zl[:, 256:384]), gup_ref[...])
    kk = k_c * kk_ref[...]
    ss = _dot_exact_rhs(kk * kk, ones)
    kap = kk * lax.rsqrt(jnp.maximum(ss, 1e-24))
    bonus_out[...] = _dot_exact_rhs(r_c * k_c * rk_ref[...], ones) * v_c
    for h in range(RWKV_HEADS):
        sl = slice(h * RWKV_N, (h + 1) * RWKV_N)
        r_out[h] = r_c[:, sl]
        kap_out[h] = kap[:, sl]
        v_out[h] = v_c[:, sl]
    for d in range(2):
        wl = zl[:, d * 64:(d + 1) * 64]
        al = zl[:, 128 + d * 64:128 + (d + 1) * 64]
        lw = -RWKV_DECAY_SCALE * _sigmoid(w0_ref[d] + _dot(jnp.tanh(wl), wup_ref[d]))
        a = _sigmoid(a0_ref[d] + _dot(al, aup_ref[d]))
        k_d = k_c * (1.0 + (a - 1.0) * ka_ref[...])
        beta = kap * a
        cs = _dot_exact_lhs(tri_ref[d], lw)
        for h in range(RWKV_HEADS):
            sl = slice(h * RWKV_N, (h + 1) * RWKV_N)
            lw_out[d, h] = lw[:, sl]
            cs_out[d, h] = cs[:, sl]
            kd_out[d, h] = k_d[:, sl]
            beta_out[d, h] = beta[:, sl]


def _rwkv_prep(z, p, tiles_per_seq_of_tile, tt):
    m = z.shape[0]
    nt = m // tt
    w = RWKV_WIDTH
    mb, lb = OFF_RW_MAIN // (3 * w), OFF_RW_LORA // 384
    hm = tt // 8
    last8 = m // 8 - 1
    prev_idx = lambda i: jnp.maximum(i * hm - 1, 0)
    next_idx = lambda i: jnp.minimum((i + 1) * hm, last8)
    full = lambda a: pl.BlockSpec(a.shape, lambda i: (0,) * a.ndim)
    heads = jnp.arange(w) // RWKV_N
    ones = (heads[:, None] == heads[None, :]).astype(BF16)
    t = np.arange(tt)
    same = (t[:, None] // RWKV_CHUNK) == (t[None, :] // RWKV_CHUNK)
    tri = jnp.asarray(np.stack([same & (t[None, :] <= t[:, None]), same & (t[None, :] >= t[:, None])]), BF16)
    params = [p['mu_main'], p['mu_lora'], p['rwkv_w0'], p['rwkv_w_up'], p['rwkv_a0'], p['rwkv_a_up'],
              p['rwkv_g_up'], p['rwkv_k_k'], p['rwkv_k_a'], p['rwkv_r_k'], ones, tri]
    hd = lambda: jax.ShapeDtypeStruct((RWKV_HEADS, m, RWKV_N), F32)
    dhd = lambda: jax.ShapeDtypeStruct((2, RWKV_HEADS, m, RWKV_N), F32)
    tok = lambda: jax.ShapeDtypeStruct((m, w), F32)
    hd_spec = pl.BlockSpec((RWKV_HEADS, tt, RWKV_N), lambda i: (0, i, 0))
    dhd_spec = pl.BlockSpec((2, RWKV_HEADS, tt, RWKV_N), lambda i: (0, 0, i, 0))
    tok_spec = pl.BlockSpec((tt, w), lambda i: (i, 0))
    return pl.pallas_call(
        functools.partial(_rwkv_prep_kernel, tiles_per_seq_of_tile=tiles_per_seq_of_tile),
        grid=(nt,),
        in_specs=[pl.BlockSpec((tt, 3 * w), lambda i: (i, mb)),
                  pl.BlockSpec((tt, 384), lambda i: (i, lb)),
                  pl.BlockSpec((8, 3 * w), lambda i: (prev_idx(i), mb)),
                  pl.BlockSpec((8, 3 * w), lambda i: (next_idx(i), mb)),
                  pl.BlockSpec((8, 384), lambda i: (prev_idx(i), lb)),
                  pl.BlockSpec((8, 384), lambda i: (next_idx(i), lb))] + [full(a) for a in params],
        out_specs=[hd_spec, hd_spec, hd_spec, dhd_spec, dhd_spec, dhd_spec, dhd_spec, tok_spec, tok_spec],
        out_shape=[hd(), hd(), hd(), dhd(), dhd(), dhd(), dhd(), tok(), tok()],
        compiler_params=_cparams("parallel"),
        name="rwkv_prep",
    )(z, z, z, z, z, z, *params)


def _dot3(a, b, dims):
    a_hi = a.astype(BF16)
    b_hi = b.astype(BF16)
    a_lo = (a - a_hi.astype(F32)).astype(BF16)
    b_lo = (b - b_hi.astype(F32)).astype(BF16)
    dn = (dims, ((), ()))
    out = lax.dot_general(a_lo, b_hi, dn, preferred_element_type=F32)
    out = out + lax.dot_general(a_hi, b_lo, dn, preferred_element_type=F32)
    return out + lax.dot_general(a_hi, b_hi, dn, preferred_element_type=F32)


_NN = ((1,), (0,))
_NT = ((1,), (1,))
_TN = ((0,), (0,))


def _dot1(a, b, dims):
    return lax.dot_general(a.astype(BF16), b.astype(BF16), (dims, ((), ())), preferred_element_type=F32)


def _dot_rhs2(a, b, dims):
    a_hi = a.astype(BF16)
    b_hi = b.astype(BF16)
    b_lo = (b - b_hi.astype(F32)).astype(BF16)
    dn = (dims, ((), ()))
    return (lax.dot_general(a_hi, b_lo, dn, preferred_element_type=F32)
            + lax.dot_general(a_hi, b_hi, dn, preferred_element_type=F32))


_RWKV_MM = _dot1
RWKV_CHAINS = 16
RWKV_MAX_GROUP = 8


def _rwkv_scan_kernel(r_ref, kap_ref, v_ref, lw_ref, cs_ref, kd_ref, beta_ref, h0_ref, o_ref, hfin_ref, *,
                      seq_len, group, heads):
    c = RWKV_CHUNK
    n = seq_len // c
    d = pl.program_id(0)
    sgn = 1 - 2 * d
    ti = lax.broadcasted_iota(jnp.int32, (c, c), 0) * sgn
    tj = lax.broadcasted_iota(jnp.int32, (c, c), 1) * sgn
    strict = tj < ti
    incl = tj <= ti
    eye_f = (ti == tj).astype(F32)

    def chunk_terms(hh, cc):
        rows = pl.ds(pl.multiple_of(cc * c, c), c)
        r = r_ref[hh, rows, :]
        kap = kap_ref[hh, rows, :]
        v = v_ref[hh, rows, :]
        lw = lw_ref[0, hh, rows, :]
        cs = cs_ref[0, hh, rows, :]
        kd = kd_ref[0, hh, rows, :]
        beta = beta_ref[0, hh, rows, :]
        tot = jnp.sum(lw, axis=0, keepdims=True)
        kap_t = kap * jnp.exp(cs - lw)
        r_t = r * jnp.exp(cs)
        e_ncs = jnp.exp(-cs)
        k_t = kd * e_ncs
        b_t = beta * e_ncs
        e_end = jnp.exp(tot - cs)
        k_end = kd * e_end
        b_end = beta * e_end
        yield
        l_b = jnp.where(strict, _RWKV_MM(kap_t, b_t, _NT), 0.0)
        l_k = jnp.where(strict, _RWKV_MM(kap_t, k_t, _NT), 0.0)
        m_b = jnp.where(incl, _RWKV_MM(r_t, b_t, _NT), 0.0)
        m_k = jnp.where(incl, _RWKV_MM(r_t, k_t, _NT), 0.0)
        yield
        x = -l_b
        y = jnp.concatenate([kap_t, _RWKV_MM(l_k, v, _NN)], axis=1)
        yield
        y = y + _dot_rhs2(x, y, _NN)
        pw = x
        for _ in range(5):
            pw = _RWKV_MM(pw, pw, _NN)
            yield
            y = y + _dot_rhs2(pw, y, _NN)
        yield
        mby = _RWKV_MM(m_b, y, _NN)
        bty = _RWKV_MM(b_end, y, _TN)
        rr = r_t - mby[:, :RWKV_N]
        o0 = _RWKV_MM(m_k, v, _NN) - mby[:, RWKV_N:]
        pc = eye_f * jnp.exp(tot) - bty[:, :RWKV_N]
        qc = _RWKV_MM(k_end, v, _TN) - bty[:, RWKV_N:]
        return rows, rr, o0, pc, qc

    def step(gi, hstates):
        keys = [(hh, u) for u in range(group) for hh in range(heads)]
        terms = dict(zip(keys, _lockstep(
            [chunk_terms(hh, (gi * group + u) + d * (n - 1 - 2 * (gi * group + u))) for hh, u in keys])))
        hstates = list(hstates)
        for u in range(group):
            for hh in range(heads):
                rows, rr, o0, pc, qc = terms[(hh, u)]
                o_ref[0, hh, rows, :] = _dot3(rr, hstates[hh], _NN) + o0
                hstates[hh] = _dot3(pc, hstates[hh], _NN) + qc
        return tuple(hstates)

    fin = lax.fori_loop(0, n // group, step, tuple(h0_ref[0, 0, hh] for hh in range(heads)))
    for hh in range(heads):
        hfin_ref[0, 0, hh] = fin[hh]


def _rwkv_scan(r, kap, v, lw, cs, kd, beta, h0, n_seq, seq_len, row_blk0):
    group = min(seq_len // RWKV_CHUNK, RWKV_MAX_GROUP)
    hp = RWKV_CHAINS // group
    hd_spec = pl.BlockSpec((hp, seq_len, RWKV_N), lambda d, h, b: (h, row_blk0 + b, 0))
    dhd_spec = pl.BlockSpec((1, hp, seq_len, RWKV_N), lambda d, h, b: (d, h, row_blk0 + b, 0))
    st_spec = pl.BlockSpec((1, 1, hp, RWKV_N, RWKV_N), lambda d, h, b: (b, d, h, 0, 0))
    return pl.pallas_call(
        functools.partial(_rwkv_scan_kernel, seq_len=seq_len, group=group, heads=hp),
        grid=(2, RWKV_HEADS // hp, n_seq),
        in_specs=[hd_spec, hd_spec, hd_spec, dhd_spec, dhd_spec, dhd_spec, dhd_spec, st_spec],
        out_specs=[pl.BlockSpec((1, hp, seq_len, RWKV_N), lambda d, h, b: (d, h, b, 0)), st_spec],
        out_shape=[jax.ShapeDtypeStruct((2, RWKV_HEADS, n_seq * seq_len, RWKV_N), F32),
                   jax.ShapeDtypeStruct((n_seq, 2, RWKV_HEADS, RWKV_N, RWKV_N), F32)],
        compiler_params=_cparams("parallel", "parallel", "parallel"),
        name="rwkv_scan",
    )(r, kap, v, lw, cs, kd, beta, h0)


def _merge_kernel(oscan_c_ref, oscan_l_ref, ona_c_ref, ona_l_ref, ort_c_ref, ort_l_ref, bonus_ref, grw_ref,
                  ga_ref, gb_ref, gc_ref, x_ref, gt_ref, gnw_ref, gnb_ref, wbr_ref, wout_ref, lng_ref, lnb_ref,
                  o_ref, orw_ref, *, ctx_tiles):
    is_ctx = pl.program_id(0) < ctx_tiles
    pick = lambda c_val, l_val: jnp.where(is_ctx, c_val, l_val)
    for h in range(RWKV_HEADS):
        sl = slice(h * RWKV_N, (h + 1) * RWKV_N)
        o = pick(oscan_c_ref[0, h] + oscan_c_ref[1, h], oscan_l_ref[0, h] + oscan_l_ref[1, h])
        y = _ln(o, RWKV_GN_EPS) * gnw_ref[h] + gnb_ref[h]
        orw_ref[:, sl] = (y + bonus_ref[:, sl]) * grw_ref[:, sl]
    merged = (_sigmoid(ga_ref[...]) * _dot(pick(ona_c_ref[...], ona_l_ref[...]), wbr_ref[0])
              + _sigmoid(gb_ref[...]) * _dot(pick(ort_c_ref[...], ort_l_ref[...]), wbr_ref[1])
              + _sigmoid(gc_ref[...]) * _dot(orw_ref[...], wbr_ref[2]))
    t = _dot(merged, wout_ref[...])
    o_ref[...] = _ln(DEEPNORM_ALPHA * x_ref[...] + gt_ref[0] * t) * lng_ref[...] + lnb_ref[...]


def _merge(oscan, o_na, o_rt, bonus, g_rw, z, x, mod3, p, row_of_tile, tm):
    m = x.shape[0]
    bw = RWKV_WIDTH
    ctx_tiles = o_na[0].shape[0] // tm
    full = lambda a: pl.BlockSpec(a.shape, lambda i: (0,) * a.ndim)
    tok = pl.BlockSpec((tm, bw), lambda i: (i, 0))
    c_row = lambda i: jnp.minimum(i, ctx_tiles - 1)
    l_row = lambda i: jnp.maximum(i - ctx_tiles, 0)
    tok_c = pl.BlockSpec((tm, bw), lambda i: (c_row(i), 0))
    tok_l = pl.BlockSpec((tm, bw), lambda i: (l_row(i), 0))
    params = [p['rwkv_gn_w'], p['rwkv_gn_b'], p['w_br'], p['w_out'], p['ln_a_g'], p['ln_a_b']]
    return pl.pallas_call(
        functools.partial(_merge_kernel, ctx_tiles=ctx_tiles),
        grid=(m // tm,),
        in_specs=[pl.BlockSpec((2, RWKV_HEADS, tm, RWKV_N), lambda i: (0, 0, c_row(i), 0)),
                  pl.BlockSpec((2, RWKV_HEADS, tm, RWKV_N), lambda i: (0, 0, l_row(i), 0)),
                  tok_c, tok_l, tok_c, tok_l, tok, tok,
                  pl.BlockSpec((tm, D_MODEL), lambda i: (i, 0)),
                  pl.BlockSpec((tm, D_MODEL), lambda i: (i, 1)),
                  pl.BlockSpec((tm, D_MODEL), lambda i: (i, 2)),
                  pl.BlockSpec((tm, D_MODEL), lambda i: (i, 0)),
                  pl.BlockSpec((1, 1, D_MODEL), lambda i: (row_of_tile(i, tm), 0, 2))]
                 + [full(a) for a in params],
        out_specs=pl.BlockSpec((tm, D_MODEL), lambda i: (i, 0)),
        out_shape=jax.ShapeDtypeStruct((m, D_MODEL), F32),
        scratch_shapes=[pltpu.VMEM((tm, bw), F32)],
        compiler_params=_cparams("parallel"),
        name="merge_out_proj",
    )(*oscan, *o_na, *o_rt, bonus, g_rw, z, z, z, x, mod3, *params)


def _top_values(s, k, with_rank=False):
    out = []
    rank = jnp.full(s.shape, float(k), F32)
    for b in range(k):
        m = jnp.max(s, axis=0, keepdims=True)
        out.append(m)
        hit = s == m
        if with_rank:
            rank = jnp.where(hit, float(b), rank)
        s = jnp.where(hit, -jnp.inf, s)
    return (out, rank) if with_rank else out


def _peer_gate_kernel(x_ref, sh_ref, sc_ref, wq_ref, keys_ref, hb_ref, cnt_ref, rank1_ref, e0_ref, e1_ref):
    h = _ln(x_ref[...]) * (1.0 + sc_ref[0]) + sh_ref[0]
    hb_ref[...] = h.astype(BF16)
    ht = h.T.astype(BF16)
    qt = jnp.dot(wq_ref[...], ht, preferred_element_type=F32)
    half = keys_ref.shape[-1]
    for hd in range(PEER_HEADS):
        scores, tops = [], []
        for part in range(2):
            row0 = (hd * 2 + part) * half
            s = _dot(keys_ref[hd, part], qt[row0:row0 + half, :])
            scores.append(s)
            if part == 0:
                tops.append(_top_values(s, PEER_TOPK))
            else:
                top1, rank1 = _top_values(s, PEER_TOPK, with_rank=True)
                tops.append(top1)
        pairs = [tops[0][a] + tops[1][b] for a in range(PEER_TOPK) for b in range(PEER_TOPK // (a + 1))]
        pairs += [jnp.full_like(pairs[0], -jnp.inf)] * (-len(pairs) % 8)
        best = _top_values(jnp.concatenate(pairs, axis=0), PEER_TOPK)
        zsum = jnp.exp(best[0] - best[0])
        for t in range(1, PEER_TOPK):
            zsum = zsum + jnp.exp(best[t] - best[0])
        tau = best[PEER_TOPK - 1]
        cnt = jnp.zeros(scores[0].shape, F32)
        for b, t1 in enumerate(tops[1]):
            cnt = jnp.where(scores[0] + t1 >= tau, float(b + 1), cnt)
        cnt_ref[hd] = cnt
        rank1_ref[hd] = rank1.astype(BF16)
        e0_ref[hd] = jnp.exp(scores[0] - tops[0][0])
        e1_ref[hd] = (jnp.exp(scores[1] - tops[1][0]) / zsum).astype(BF16)


def _peer_gate(x, mod3, wq_t, keys, row_of_tile, tn):
    m = x.shape[0]
    full = lambda a: pl.BlockSpec(a.shape, lambda i: (0,) * a.ndim)
    sk = pl.BlockSpec((PEER_HEADS, PEER_NKEYS, tn), lambda i: (0, 0, i))
    sk_f32 = jax.ShapeDtypeStruct((PEER_HEADS, PEER_NKEYS, m), F32)
    sk_bf16 = jax.ShapeDtypeStruct((PEER_HEADS, PEER_NKEYS, m), BF16)
    return pl.pallas_call(
        _peer_gate_kernel,
        grid=(m // tn,),
        in_specs=[pl.BlockSpec((tn, D_MODEL), lambda i: (i, 0)),
                  pl.BlockSpec((1, 1, D_MODEL), lambda i: (row_of_tile(i, tn), 0, 3)),
                  pl.BlockSpec((1, 1, D_MODEL), lambda i: (row_of_tile(i, tn), 0, 4)),
                  full(wq_t), full(keys)],
        out_specs=[pl.BlockSpec((tn, D_MODEL), lambda i: (i, 0)), sk, sk, sk, sk],
        out_shape=[jax.ShapeDtypeStruct((m, D_MODEL), BF16), sk_f32, sk_bf16, sk_f32, sk_bf16],
        compiler_params=_cparams("parallel"),
        name="peer_gate",
    )(x, mod3, mod3, wq_t, keys)


def _gelu_exact(x):
    return 0.5 * x * (1.0 + lax.erf(x * (2.0 ** -0.5)))


PEER_SUB_ROWS = 32
PEER_SUB_LANES = 256
PEER_GROUP_KEYS = 4
PEER_GROUP = PEER_GROUP_KEYS * PEER_NKEYS


def _peer_expert_kernel(hb_ref, cnt_ref, rank1_ref, e0_ref, e1_ref, u_ref, vt_ref, x_ref, gt_ref,
                        lng_ref, lnb_ref, o_ref, acc_ref, st_ref, st2_ref, act_ref, *, groups):
    e = pl.program_id(1)
    tn = hb_ref.shape[0]

    @pl.when(e == 0)
    def _():
        acc_ref[...] = jnp.zeros_like(acc_ref)

    def pre_activations(g, dst_ref):
        rows = pl.ds(pl.multiple_of(g * PEER_GROUP, PEER_GROUP), PEER_GROUP)
        dst_ref[...] = lax.dot_general(u_ref[rows, :], hb_ref[...], (_NT, ((), ())), preferred_element_type=F32)

    def gated_activations(g, src_ref):
        row0 = pl.multiple_of(g * PEER_GROUP, PEER_GROUP)
        for r in range(PEER_GROUP_KEYS):
            i = (e * groups + g) * PEER_GROUP_KEYS + r
            cnt_row = [cnt_ref[hd, pl.ds(i, 1), :] for hd in range(PEER_HEADS)]
            e0_row = [e0_ref[hd, pl.ds(i, 1), :] for hd in range(PEER_HEADS)]
            for lb in range(tn // PEER_SUB_LANES):
                ls = slice(lb * PEER_SUB_LANES, (lb + 1) * PEER_SUB_LANES)
                sub = (PEER_SUB_ROWS, PEER_SUB_LANES)
                cnt = [jnp.broadcast_to(c[:, ls].astype(BF16), sub) for c in cnt_row]
                e0 = [jnp.broadcast_to(c[:, ls].astype(BF16), sub) for c in e0_row]
                zero = jnp.zeros(sub, BF16)
                for rb in range(PEER_NKEYS // PEER_SUB_ROWS):
                    js = slice(rb * PEER_SUB_ROWS, (rb + 1) * PEER_SUB_ROWS)
                    gate = None
                    for hd in range(PEER_HEADS):
                        term = jnp.where(rank1_ref[hd, js, ls] < cnt[hd], e1_ref[hd, js, ls], zero) * e0[hd]
                        gate = term if gate is None else gate + term
                    off = r * PEER_NKEYS + rb * PEER_SUB_ROWS
                    act = _gelu_exact(src_ref[off:off + PEER_SUB_ROWS, ls]).astype(BF16) * gate
                    act_ref[pl.ds(row0 + off, PEER_SUB_ROWS), ls] = act

    pre_activations(0, st_ref)

    def pair(p, carry):
        gated_activations(2 * p, st_ref)
        pre_activations(2 * p + 1, st2_ref)
        gated_activations(2 * p + 1, st2_ref)
        pre_activations(jnp.minimum(2 * p + 2, groups - 1), st_ref)
        return carry

    lax.fori_loop(0, groups // 2, pair, 0)
    acc_ref[...] += jnp.dot(vt_ref[...], act_ref[...], preferred_element_type=F32)

    @pl.when(e == pl.num_programs(1) - 1)
    def _():
        f = acc_ref[...].T
        o_ref[...] = _ln(DEEPNORM_ALPHA * x_ref[...] + gt_ref[0] * f) * lng_ref[...] + lnb_ref[...]


def _peer_expert(hb, cnt, rank1, e0, e1, u, v_t, x, mod3, ln_g, ln_b, row_of_tile, tn, eb):
    m = x.shape[0]
    n_exp = u.shape[0]
    groups = eb // PEER_GROUP
    sk = pl.BlockSpec((PEER_HEADS, PEER_NKEYS, tn), lambda i, e: (0, 0, i))
    full = lambda a: pl.BlockSpec(a.shape, lambda i, e: (0,) * a.ndim)
    return pl.pallas_call(
        functools.partial(_peer_expert_kernel, groups=groups),
        grid=(m // tn, n_exp // eb),
        in_specs=[pl.BlockSpec((tn, D_MODEL), lambda i, e: (i, 0)), sk, sk, sk, sk,
                  pl.BlockSpec((eb, D_MODEL), lambda i, e: (e, 0)),
                  pl.BlockSpec((D_MODEL, eb), lambda i, e: (0, e)),
                  pl.BlockSpec((tn, D_MODEL), lambda i, e: (i, 0)),
                  pl.BlockSpec((1, 1, D_MODEL), lambda i, e: (row_of_tile(i, tn), 0, 5)),
                  full(ln_g), full(ln_b)],
        out_specs=pl.BlockSpec((tn, D_MODEL), lambda i, e: (i, 0)),
        out_shape=jax.ShapeDtypeStruct((m, D_MODEL), F32),
        scratch_shapes=[pltpu.VMEM((D_MODEL, tn), F32), pltpu.VMEM((PEER_GROUP, tn), F32),
                        pltpu.VMEM((PEER_GROUP, tn), F32), pltpu.VMEM((eb, tn), BF16)],
        compiler_params=_cparams("parallel", "arbitrary"),
        name="peer_experts",
    )(hb, cnt, rank1, e0, e1, u, v_t, x, mod3, ln_g, ln_b)


def _permute_in_proj(w_in):
    na, qk, rv = 3 * 512, 2 * 256, 2 * 512
    o_na, o_qk, o_rv, o_rw, o_g = 0, na, na + qk, na + qk + rv, na + qk + rv + 1920
    return jnp.concatenate([w_in[:, o_g:], w_in[:, o_na:o_qk], w_in[:, o_qk:o_rv], w_in[:, o_rv:o_rw],
                            w_in[:, o_rw:o_g]], axis=1)


def kernel(x_prompt, x_sample, cache_na_k, cache_na_v, state_ret, state_rwkv, c, c_ctx, w_mod, b_mod, w_in, na_rpb, ret_decay_logit, ret_gn_w, ret_gn_b, rwkv_mu, rwkv_w0, rwkv_w_up, rwkv_a0, rwkv_a_up, rwkv_g_up, rwkv_k_k, rwkv_k_a, rwkv_r_k, rwkv_gn_w, rwkv_gn_b, w_br, w_out, ln_a_g, ln_a_b, ln_f_g, ln_f_b, peer_wq, peer_keys, peer_u, peer_v):
    n_ctx, ctx_len, d = x_prompt.shape
    n_lat, lat_len, _ = x_sample.shape
    m_ctx = n_ctx * ctx_len
    assert m_ctx % lat_len == 0 and n_lat == 2
    lat_blk0 = m_ctx // lat_len

    def row_of_tile(i, tm):
        return jnp.where(i < m_ctx // tm, 0, 1 + (i - m_ctx // tm) // (lat_len // tm))

    prep_tt = 256
    assert ctx_len == prep_tt

    def tiles_per_seq_of_tile(i):
        lat = i >= m_ctx // prep_tt
        per = jnp.where(lat, lat_len // prep_tt, 1)
        pos = jnp.where(lat, (i - m_ctx // prep_tt) % (lat_len // prep_tt), 0)
        return pos, per

    x = jnp.concatenate([x_prompt.reshape(m_ctx, d), x_sample.reshape(n_lat * lat_len, d)], axis=0)
    c8 = jnp.concatenate([c_ctx[None], c, jnp.zeros((8 - 1 - n_lat, d), F32)], axis=0)
    mod = _modulation(c8, w_mod, b_mod)
    rope_tabs = _rope_tables(lat_len)
    zeros_ret = jnp.zeros((n_ctx, 2, RET_HEADS, RET_DK, RET_DV), F32)
    zeros_rw = jnp.zeros((n_ctx, 2, RWKV_HEADS, RWKV_N, RWKV_N), F32)

    nk, nv, sr, sw = [], [], [], []
    for l in range(DEPTH):
        mod3 = mod[l].reshape(8, 1, 6 * d)
        w_in_p = _permute_in_proj(w_in[l]).astype(BF16)
        z = _lnmod_matmul(x, mod3, w_in_p, row_of_tile, 1024, 1152, 0, 1)

        o_na_c, k_cache, v_cache = _ctx_attention(z, n_ctx, ctx_len)
        o_na_l = _na_latent(z, cache_na_k[:, l], cache_na_v[:, l], _na_bias_table(na_rpb[l], lat_len // GRID_W),
                            n_lat, lat_len, lat_blk0)
        nk.append(k_cache)
        nv.append(v_cache)

        tables = _ret_decay_tables(ret_decay_logit[l])
        gn_w, gn_b = ret_gn_w[l][None], ret_gn_b[l][None]
        o_rt_c, s_ret = _retention(z, zeros_ret, tables, gn_w, gn_b, n_ctx, ctx_len, 0, None)
        o_rt_l, _ = _retention(z, state_ret[:, l], tables, gn_w, gn_b, n_lat, lat_len, lat_blk0, rope_tabs)
        sr.append(s_ret)

        mu = rwkv_mu[l]
        prm = {
            'mu_main': mu[None, :1536], 'mu_lora': mu[None, 1536:],
            'rwkv_w0': rwkv_w0[l][:, None], 'rwkv_w_up': rwkv_w_up[l].astype(BF16),
            'rwkv_a0': rwkv_a0[l][:, None], 'rwkv_a_up': rwkv_a_up[l].astype(BF16),
            'rwkv_g_up': rwkv_g_up[l].astype(BF16), 'rwkv_k_k': rwkv_k_k[l][None],
            'rwkv_k_a': rwkv_k_a[l][None], 'rwkv_r_k': rwkv_r_k[l].reshape(1, RWKV_WIDTH),
        }
        r, kap, v, lw, cs, kd, beta, g_rw, bonus = _rwkv_prep(z, prm, tiles_per_seq_of_tile, prep_tt)
        o_c, h_fin = _rwkv_scan(r, kap, v, lw, cs, kd, beta, zeros_rw, n_ctx, ctx_len, 0)
        h0_lat = jnp.swapaxes(state_rwkv[:, l], -1, -2)
        o_l, _ = _rwkv_scan(r, kap, v, lw, cs, kd, beta, h0_lat, n_lat, lat_len, lat_blk0)
        sw.append(jnp.swapaxes(h_fin, -1, -2))

        mp = {
            'rwkv_gn_w': rwkv_gn_w[l].reshape(RWKV_HEADS, 1, RWKV_N),
            'rwkv_gn_b': rwkv_gn_b[l].reshape(RWKV_HEADS, 1, RWKV_N),
            'w_br': w_br[l].astype(BF16), 'w_out': w_out[l].astype(BF16),
            'ln_a_g': ln_a_g[l][None], 'ln_a_b': ln_a_b[l][None],
        }
        x = _merge((o_c, o_l), (o_na_c, o_na_l), (o_rt_c, o_rt_l), bonus, g_rw, z, x, mod3, mp, row_of_tile, 256)

        wq_t = peer_wq[l].T.astype(BF16)
        hb, cnt, rank1, e0, e1 = _peer_gate(x, mod3, wq_t, peer_keys[l].astype(BF16), row_of_tile, 256)
        x = _peer_expert(hb, cnt, rank1, e0, e1, peer_u[l].astype(BF16), peer_v[l].T.astype(BF16), x, mod3,
                         ln_f_g[l][None], ln_f_b[l][None], row_of_tile, 512, 2048)

    dt = x_prompt.dtype
    y_prompt = x[:m_ctx].reshape(n_ctx, ctx_len, d)
    y_sample = x[m_ctx:].reshape(n_lat, lat_len, d)
    return (y_prompt, y_sample, jnp.stack(nk, axis=1).astype(dt), jnp.stack(nv, axis=1).astype(dt),
            jnp.stack(sr, axis=1).astype(dt), jnp.stack(sw, axis=1).astype(dt))
```

```python
import functools
import math

import jax
import jax.numpy as jnp
import numpy as np
from jax import lax
from jax.experimental import pallas as pl
from jax.experimental.pallas import tpu as pltpu

F32 = jnp.float32
BF16 = jnp.bfloat16

D_MODEL = 1024
DEPTH = 2
GRID_W = 64
NA_HEADS = 8
NA_HEAD_DIM = 64
WIN_R = 8
WIN_C = 16
NEG_INF = -1e30
RET_HEADS = 4
RET_DK = 64
RET_DV = 128
RET_CHUNK = 64
RET_GN_EPS = 1e-5
ROPE_BASE = 10000.0
RWKV_HEADS = 8
RWKV_N = 64
RWKV_WIDTH = 512
RWKV_DECAY_SCALE = 0.606531
RWKV_GN_EPS = 64e-5
RWKV_CHUNK = 64
PEER_HEADS = 8
PEER_NKEYS = 128
PEER_TOPK = 16
LN_EPS = 1e-5
DEEPNORM_ALPHA = (2 * DEPTH) ** 0.25

OFF_GATE = 0
OFF_NA = 3072
OFF_RET_QK = 4608
OFF_RET_V = 5120
OFF_RW_MAIN = 6144
OFF_RW_LORA = 7680
P_IN = 8064

VMEM_LIMIT_BYTES = 56 * 1024 * 1024


def _cparams(*sem):
    return pltpu.CompilerParams(dimension_semantics=sem, vmem_limit_bytes=VMEM_LIMIT_BYTES)


def _ln(x, eps=LN_EPS):
    mu = jnp.mean(x, axis=-1, keepdims=True)
    xc = x - mu
    var = jnp.mean(xc * xc, axis=-1, keepdims=True)
    return xc * lax.rsqrt(var + eps)


def _sigmoid(x):
    return 1.0 / (1.0 + jnp.exp(-x))


def _dot(a, b):
    return jnp.dot(a.astype(BF16), b.astype(BF16), preferred_element_type=F32)


def _dot_nt(a, b):
    return lax.dot_general(a.astype(BF16), b.astype(BF16), (((1,), (1,)), ((), ())),
                           preferred_element_type=F32)


def _dot_tn(a, b):
    return lax.dot_general(a.astype(BF16), b.astype(BF16), (((0,), (0,)), ((), ())),
                           preferred_element_type=F32)


def _lockstep(gens):
    out = [None] * len(gens)
    live = list(range(len(gens)))
    while live:
        for i in list(live):
            try:
                next(gens[i])
            except StopIteration as done:
                out[i] = done.value
                live.remove(i)
    return out


def _split3(x):
    hi = x.astype(BF16)
    r1 = x - hi.astype(F32)
    mid = r1.astype(BF16)
    lo = (r1 - mid.astype(F32)).astype(BF16)
    return hi, mid, lo


def _dot_exact_lhs(sel, x):
    hi, mid, lo = _split3(x)
    s = sel.astype(BF16)
    out = jnp.dot(s, lo, preferred_element_type=F32)
    out = out + jnp.dot(s, mid, preferred_element_type=F32)
    return out + jnp.dot(s, hi, preferred_element_type=F32)


def _dot_exact_rhs(x, sel):
    hi, mid, lo = _split3(x)
    s = sel.astype(BF16)
    out = jnp.dot(lo, s, preferred_element_type=F32)
    out = out + jnp.dot(mid, s, preferred_element_type=F32)
    return out + jnp.dot(hi, s, preferred_element_type=F32)


def _mod_kernel(c_ref, w_ref, b_ref, o_ref):
    c = c_ref[...]
    s = c * _sigmoid(c)
    o_ref[0] = _dot(s, w_ref[0]) + b_ref[0]


def _modulation(c8, w_mod, b_mod):
    tn = 1536
    n = w_mod.shape[-1]
    return pl.pallas_call(
        _mod_kernel,
        grid=(DEPTH, n // tn),
        in_specs=[pl.BlockSpec((8, D_MODEL), lambda l, j: (0, 0)),
                  pl.BlockSpec((1, D_MODEL, tn), lambda l, j: (l, 0, j)),
                  pl.BlockSpec((1, 1, tn), lambda l, j: (l, 0, j))],
        out_specs=pl.BlockSpec((1, 8, tn), lambda l, j: (l, 0, j)),
        out_shape=jax.ShapeDtypeStruct((DEPTH, 8, n), F32),
        compiler_params=_cparams("parallel", "parallel"),
        name="modulation",
    )(c8, w_mod, b_mod.reshape(DEPTH, 1, n))


def _lnmod_matmul_kernel(x_ref, sh_ref, sc_ref, w_ref, o_ref, h_ref):
    @pl.when(pl.program_id(1) == 0)
    def _():
        h = _ln(x_ref[...]) * (1.0 + sc_ref[0]) + sh_ref[0]
        h_ref[...] = h.astype(BF16)

    o_ref[...] = jnp.dot(h_ref[...], w_ref[...], preferred_element_type=F32)


def _lnmod_matmul(x, mod3, w, row_of_tile, tm, tn, sh_blk, sc_blk):
    m = x.shape[0]
    n = w.shape[1]
    return pl.pallas_call(
        _lnmod_matmul_kernel,
        grid=(m // tm, n // tn),
        in_specs=[pl.BlockSpec((tm, D_MODEL), lambda i, j: (i, 0)),
                  pl.BlockSpec((1, 1, D_MODEL), lambda i, j: (row_of_tile(i, tm), 0, sh_blk)),
                  pl.BlockSpec((1, 1, D_MODEL), lambda i, j: (row_of_tile(i, tm), 0, sc_blk)),
                  pl.BlockSpec((D_MODEL, tn), lambda i, j: (0, j))],
        out_specs=pl.BlockSpec((tm, tn), lambda i, j: (i, j)),
        out_shape=jax.ShapeDtypeStruct((m, n), F32),
        scratch_shapes=[pltpu.VMEM((tm, D_MODEL), BF16)],
        compiler_params=_cparams("parallel", "arbitrary"),
        name="adaln_in_proj",
    )(x, mod3, mod3, w)


def _ctx_attn_kernel(q_ref, k_ref, v_ref, o_ref, kc_ref, vc_ref):
    scale = NA_HEAD_DIM ** -0.5
    half = q_ref.shape[0] // 2

    def head(hh, q0):
        sl = slice(hh * NA_HEAD_DIM, (hh + 1) * NA_HEAD_DIM)
        k = k_ref[:, sl]
        v = v_ref[:, sl]
        if q0 == 0:
            kc_ref[0, hh] = k
            vc_ref[0, hh] = v
        s = _dot_nt(q_ref[q0:q0 + half, sl], k) * scale
        yield
        m = jnp.max(s, axis=-1, keepdims=True)
        p = jnp.exp(s - m)
        l = jnp.sum(p, axis=-1, keepdims=True)
        yield
        o_ref[q0:q0 + half, sl] = _dot(p, v) / l

    _lockstep([head(hh, q0) for hh in range(2) for q0 in (0, half)])


def _ctx_attention(z, n_seq, seq_len):
    cb = OFF_NA // 128
    cache_spec = pl.BlockSpec((1, 2, seq_len, NA_HEAD_DIM), lambda b, h: (b, h, 0, 0))
    cache_shape = jax.ShapeDtypeStruct((n_seq, NA_HEADS, seq_len, NA_HEAD_DIM), F32)
    return pl.pallas_call(
        _ctx_attn_kernel,
        grid=(n_seq, NA_HEADS // 2),
        in_specs=[pl.BlockSpec((seq_len, 128), lambda b, h: (b, cb + h)),
                  pl.BlockSpec((seq_len, 128), lambda b, h: (b, cb + 4 + h)),
                  pl.BlockSpec((seq_len, 128), lambda b, h: (b, cb + 8 + h))],
        out_specs=[pl.BlockSpec((seq_len, 128), lambda b, h: (b, h)), cache_spec, cache_spec],
        out_shape=[jax.ShapeDtypeStruct((n_seq * seq_len, NA_HEADS * NA_HEAD_DIM), F32), cache_shape, cache_shape],
        compiler_params=_cparams("parallel", "parallel"),
        name="ctx_attention",
    )(z, z, z)


def _na_bias_kernel(rpb_ref, rowsel_ref, colsel_ref, neg_ref, o_ref):
    picked = _dot_exact_lhs(rowsel_ref[...], rpb_ref[0])
    o_ref[0] = _dot_exact_rhs(picked, colsel_ref[...]) + neg_ref[...]


def _na_bias_table(rpb, rows):
    kr = min(WIN_R, rows)
    nh, n_dr, n_dc = rpb.shape
    rep = np.array([0, 1, 2, 3, 4, rows - 3, rows - 2, rows - 1])
    start = np.clip(rep - kr // 2, 0, rows - kr)
    dr = start[:, None] - rep[:, None] + WIN_R - 1 + np.arange(kr)[None, :]
    qcol = np.arange(GRID_W)
    kcol = np.arange(GRID_W)
    q_start = np.clip(qcol - WIN_C // 2, 0, GRID_W - WIN_C)
    rel = kcol[None, :] - q_start[:, None]
    valid = (rel >= 0) & (rel < WIN_C)
    dc = np.clip(kcol[None, :] - qcol[:, None], -(WIN_C - 1), WIN_C - 1) + WIN_C - 1
    row_sel = np.zeros((8 * kr, 16), np.float32)
    row_sel[np.arange(8 * kr), dr.reshape(-1)] = 1.0
    col_sel = np.zeros((32, GRID_W * GRID_W), np.float32)
    qq, kk = np.nonzero(valid)
    col_sel[dc[qq, kk], qq * GRID_W + kk] = 1.0
    neg = np.where(valid, 0.0, NEG_INF).astype(np.float32).reshape(1, -1)
    rpb_p = jnp.pad(rpb.astype(F32), ((0, 0), (0, 16 - n_dr), (0, 32 - n_dc)))
    full = lambda a: pl.BlockSpec(a.shape, lambda h: (0,) * a.ndim)
    consts = [jnp.asarray(row_sel, BF16), jnp.asarray(col_sel, BF16), jnp.asarray(neg)]
    b = pl.pallas_call(
        _na_bias_kernel,
        grid=(nh,),
        in_specs=[pl.BlockSpec((1, 16, 32), lambda h: (h, 0, 0))] + [full(a) for a in consts],
        out_specs=pl.BlockSpec((1, 8 * kr, GRID_W * GRID_W), lambda h: (h, 0, 0)),
        out_shape=jax.ShapeDtypeStruct((nh, 8 * kr, GRID_W * GRID_W), F32),
        compiler_params=_cparams("parallel"),
        name="na_bias",
    )(rpb_p, *consts)
    b = b.reshape(nh, 8, kr, GRID_W, GRID_W)
    return b.transpose(0, 1, 3, 2, 4).reshape(nh, 8, GRID_W, kr * GRID_W)


NA_ROWS_PER_STEP = 4


def _na_latent_kernel(q_ref, k_ref, v_ref, ck_ref, cv_ref, b_ref, o_ref, *, rows):
    scale = NA_HEAD_DIM ** -0.5
    kr = min(WIN_R, rows)

    def query_row(r, hh):
        start = jnp.clip(r - kr // 2, 0, rows - kr)
        typ = jnp.where(r < 4, r, jnp.where(r > rows - 4, r - (rows - 8), 4))
        q0 = pl.multiple_of(r * GRID_W, GRID_W)
        k0 = pl.multiple_of(start * GRID_W, GRID_W)
        sl = slice(hh * NA_HEAD_DIM, (hh + 1) * NA_HEAD_DIM)
        q = q_ref[pl.ds(q0, GRID_W), sl]
        kw = k_ref[pl.ds(k0, kr * GRID_W), sl]
        vw = v_ref[pl.ds(k0, kr * GRID_W), sl]
        s_w = _dot_nt(q, kw) * scale + b_ref[hh, typ]
        s_c = _dot_nt(q, ck_ref[0, hh]) * scale
        yield
        m = jnp.maximum(jnp.max(s_w, axis=-1, keepdims=True), jnp.max(s_c, axis=-1, keepdims=True))
        p_w = jnp.exp(s_w - m)
        p_c = jnp.exp(s_c - m)
        l = jnp.sum(p_w, axis=-1, keepdims=True) + jnp.sum(p_c, axis=-1, keepdims=True)
        yield
        o_ref[pl.ds(q0, GRID_W), sl] = (_dot(p_w, vw) + _dot(p_c, cv_ref[0, hh])) / l

    def body(rp, carry):
        _lockstep([query_row(rp * NA_ROWS_PER_STEP + u, hh) for u in range(NA_ROWS_PER_STEP) for hh in range(2)])
        return carry

    lax.fori_loop(0, rows // NA_ROWS_PER_STEP, body, 0)


def _na_latent(z, ctx_k, ctx_v, bias, n_seq, seq_len, row_blk0):
    cb = OFF_NA // 128
    rows = seq_len // GRID_W
    past = ctx_k.shape[2]
    return pl.pallas_call(
        functools.partial(_na_latent_kernel, rows=rows),
        grid=(n_seq, NA_HEADS // 2),
        in_specs=[pl.BlockSpec((seq_len, 128), lambda b, h: (row_blk0 + b, cb + h)),
                  pl.BlockSpec((seq_len, 128), lambda b, h: (row_blk0 + b, cb + 4 + h)),
                  pl.BlockSpec((seq_len, 128), lambda b, h: (row_blk0 + b, cb + 8 + h)),
                  pl.BlockSpec((1, 2, past, NA_HEAD_DIM), lambda b, h: (b, h, 0, 0)),
                  pl.BlockSpec((1, 2, past, NA_HEAD_DIM), lambda b, h: (b, h, 0, 0)),
                  pl.BlockSpec((2, 8, GRID_W, bias.shape[-1]), lambda b, h: (h, 0, 0, 0))],
        out_specs=pl.BlockSpec((seq_len, 128), lambda b, h: (b, h)),
        out_shape=jax.ShapeDtypeStruct((n_seq * seq_len, NA_HEADS * NA_HEAD_DIM), F32),
        compiler_params=_cparams("parallel", "parallel"),
        name="na_latent",
    )(z, z, z, ctx_k, ctx_v, bias)


def _ret_decay_tables(decay_logit):
    c = RET_CHUNK
    lg = jnp.log(jax.nn.sigmoid(decay_logit.astype(F32)))[:, :, None, None]
    i = jnp.arange(c, dtype=F32)
    diff = i[:, None] - i[None, :]
    m_f = jnp.where(diff >= 0, jnp.exp(jnp.maximum(diff, 0.0) * lg[0]), 0.0)
    m_b = jnp.where(diff <= 0, jnp.exp(jnp.maximum(-diff, 0.0) * lg[1]), 0.0)
    mask = jnp.stack([m_f, m_b])
    ones = jnp.ones((1, 1, 1, RET_DK), F32)
    q_f = jnp.exp((i + 1.0)[None, :, None] * lg[0]) * ones[0]
    q_b = jnp.exp((c - i)[None, :, None] * lg[1]) * ones[0]
    k_f = jnp.exp((c - 1.0 - i)[None, :, None] * lg[0]) * ones[0]
    k_b = jnp.exp(i[None, :, None] * lg[1]) * ones[0]
    c_dec = jnp.exp(c * lg) * jnp.ones((1, 1, RET_DK, RET_DV), F32)
    return mask, jnp.stack([q_f, q_b]), jnp.stack([k_f, k_b]), c_dec


def _rope_tables(seq_len):
    quarter = RET_DK // 4
    pos = np.arange(seq_len)
    freqs = 1.0 / (ROPE_BASE ** (jnp.arange(quarter, dtype=F32) / quarter))
    a_row = jnp.asarray(pos // GRID_W, F32)[:, None] * freqs[None, :]
    a_col = jnp.asarray(pos % GRID_W, F32)[:, None] * freqs[None, :]
    cos = jnp.concatenate([jnp.cos(a_row)] * 2 + [jnp.cos(a_col)] * 2, axis=-1)
    sin = jnp.concatenate([-jnp.sin(a_row), jnp.sin(a_row), -jnp.sin(a_col), jnp.sin(a_col)], axis=-1)
    return jnp.tile(cos, (1, RET_HEADS)), jnp.tile(sin, (1, RET_HEADS))


def _retention_kernel(*refs, seq_len, rope):
    if rope:
        (q_ref, k_ref, v_ref, g_ref, s0_ref, mask_ref, qd_ref, kd_ref, cd_ref, gnw_ref, gnb_ref,
         cos_ref, sin_ref, o_ref, sfin_ref, qs_ref, ks_ref, acc_ref) = refs
    else:
        (q_ref, k_ref, v_ref, g_ref, s0_ref, mask_ref, qd_ref, kd_ref, cd_ref, gnw_ref, gnb_ref,
         o_ref, sfin_ref, qs_ref, ks_ref, acc_ref) = refs
    c = RET_CHUNK
    n = seq_len // c
    quarter = RET_DK // 4
    q = q_ref[...]
    k = k_ref[...] * (RET_DK ** -0.5)
    if rope:
        lane = lax.broadcasted_iota(jnp.int32, q.shape, 1)
        first = (lane % (2 * quarter)) < quarter
        width = q.shape[1]

        def rot(x):
            swapped = jnp.where(first, pltpu.roll(x, width - quarter, 1), pltpu.roll(x, quarter, 1))
            return x * cos_ref[...] + swapped * sin_ref[...]

        q = rot(q)
        k = rot(k)
    qs_ref[...] = q
    ks_ref[...] = k

    sfin_ref[...] = s0_ref[...]

    def chain(h, d, ci):
        ksl = slice(h * RET_DK, (h + 1) * RET_DK)
        vsl = slice(h * RET_DV, (h + 1) * RET_DV)
        cc = ci if d == 0 else n - 1 - ci
        rows = pl.ds(pl.multiple_of(cc * c, c), c)
        qc = qs_ref[rows, ksl]
        kc = ks_ref[rows, ksl]
        vc = v_ref[rows, vsl]
        att = _dot_nt(qc, kc) * mask_ref[d, h]
        s = sfin_ref[0, d, h]
        yield
        acc_ref[d, rows, vsl] = _dot(att, vc) + _dot(qc * qd_ref[d, h], s)
        yield
        sfin_ref[0, d, h] = s * cd_ref[d, h] + _dot_tn(kc * kd_ref[d, h], vc)

    def step(ci, carry):
        _lockstep([chain(h, d, ci) for h in range(RET_HEADS) for d in range(2)])
        return carry

    lax.fori_loop(0, n, step, 0)

    for h in range(RET_HEADS):
        vsl = slice(h * RET_DV, (h + 1) * RET_DV)
        y = _ln(acc_ref[0, :, vsl] + acc_ref[1, :, vsl], RET_GN_EPS) * gnw_ref[:, vsl] + gnb_ref[:, vsl]
        g = g_ref[:, vsl]
        o_ref[:, vsl] = y * (g * _sigmoid(g))


def _retention(z, s0, tables, gn_w, gn_b, n_seq, seq_len, row_blk0, rope_tabs):
    mask, qd, kd, cd = tables
    rope = rope_tabs is not None
    qk_w = RET_HEADS * RET_DK
    v_w = RET_HEADS * RET_DV
    full = lambda a: pl.BlockSpec(a.shape, lambda b: (0,) * a.ndim)
    in_specs = [pl.BlockSpec((seq_len, qk_w), lambda b: (row_blk0 + b, OFF_RET_QK // qk_w)),
                pl.BlockSpec((seq_len, qk_w), lambda b: (row_blk0 + b, OFF_RET_QK // qk_w + 1)),
                pl.BlockSpec((seq_len, v_w), lambda b: (row_blk0 + b, OFF_RET_V // v_w)),
                pl.BlockSpec((seq_len, v_w), lambda b: (row_blk0 + b, OFF_RET_V // v_w + 1)),
                pl.BlockSpec((1, 2, RET_HEADS, RET_DK, RET_DV), lambda b: (b, 0, 0, 0, 0)),
                full(mask), full(qd), full(kd), full(cd), full(gn_w), full(gn_b)]
    args = [z, z, z, z, s0, mask, qd, kd, cd, gn_w, gn_b]
    if rope:
        in_specs += [full(rope_tabs[0]), full(rope_tabs[1])]
        args += list(rope_tabs)
    return pl.pallas_call(
        functools.partial(_retention_kernel, seq_len=seq_len, rope=rope),
        grid=(n_seq,),
        in_specs=in_specs,
        out_specs=[pl.BlockSpec((seq_len, v_w), lambda b: (b, 0)),
                   pl.BlockSpec((1, 2, RET_HEADS, RET_DK, RET_DV), lambda b: (b, 0, 0, 0, 0))],
        out_shape=[jax.ShapeDtypeStruct((n_seq * seq_len, v_w), F32),
                   jax.ShapeDtypeStruct((n_seq, 2, RET_HEADS, RET_DK, RET_DV), F32)],
        scratch_shapes=[pltpu.VMEM((seq_len, qk_w), F32), pltpu.VMEM((seq_len, qk_w), F32),
                        pltpu.VMEM((2, seq_len, v_w), F32)],
        compiler_params=_cparams("parallel"),
        name="retention_rope" if rope else "retention",
    )(*args)


def _rwkv_prep_kernel(zm_ref, zl_ref, pm_ref, nm_ref, pl_ref, nl_ref, mum_ref, mul_ref,
                      w0_ref, wup_ref, a0_ref, aup_ref, gup_ref, kk_ref, ka_ref, rk_ref, ones_ref, tri_ref,
                      r_out, kap_out, v_out, lw_out, cs_out, kd_out, beta_out, g_out, bonus_out,
                      *, tiles_per_seq_of_tile):
    i = pl.program_id(0)
    pos, per = tiles_per_seq_of_tile(i)
    has_prev = (pos != 0).astype(F32)
    has_next = (pos != per - 1).astype(F32)
    tt = zm_ref.shape[0]

    def shift(z_ref, p_ref, n_ref, mu_ref):
        z = z_ref[...]
        row = lax.broadcasted_iota(jnp.int32, z.shape, 0)
        prev = jnp.where(row == 0, p_ref[7:8, :] * has_prev, pltpu.roll(z, 1, 0))
        nxt = jnp.where(row == tt - 1, n_ref[0:1, :] * has_next, pltpu.roll(z, tt - 1, 0))
        return z + mu_ref[...] * (0.5 * (prev + nxt) - z)

    zm = shift(zm_ref, pm_ref, nm_ref, mum_ref)
    zl = shift(zl_ref, pl_ref, nl_ref, mul_ref)
    w = RWKV_WIDTH
    r_c, k_c, v_c = zm[:, 0:w], zm[:, w:2 * w], zm[:, 2 * w:3 * w]
    ones = ones_ref[...]
    g_out[...] = _dot(_sigmoid(zl[:, 256:384]), gup_ref[...])
    kk = k_c * kk_ref[...]
    ss = _dot_exact_rhs(kk * kk, ones)
    kap = kk * lax.rsqrt(jnp.maximum(ss, 1e-24))
    bonus_out[...] = _dot_exact_rhs(r_c * k_c * rk_ref[...], ones) * v_c
    for h in range(RWKV_HEADS):
        sl = slice(h * RWKV_N, (h + 1) * RWKV_N)
        r_out[h] = r_c[:, sl]
        kap_out[h] = kap[:, sl]
        v_out[h] = v_c[:, sl]
    for d in range(2):
        wl = zl[:, d * 64:(d + 1) * 64]
        al = zl[:, 128 + d * 64:128 + (d + 1) * 64]
        lw = -RWKV_DECAY_SCALE * _sigmoid(w0_ref[d] + _dot(jnp.tanh(wl), wup_ref[d]))
        a = _sigmoid(a0_ref[d] + _dot(al, aup_ref[d]))
        k_d = k_c * (1.0 + (a - 1.0) * ka_ref[...])
        beta = kap * a
        cs = _dot_exact_lhs(tri_ref[d], lw)
        for h in range(RWKV_HEADS):
            sl = slice(h * RWKV_N, (h + 1) * RWKV_N)
            lw_out[d, h] = lw[:, sl]
            cs_out[d, h] = cs[:, sl]
            kd_out[d, h] = k_d[:, sl]
            beta_out[d, h] = beta[:, sl]


def _rwkv_prep(z, p, tiles_per_seq_of_tile, tt):
    m = z.shape[0]
    nt = m // tt
    w = RWKV_WIDTH
    mb, lb = OFF_RW_MAIN // (3 * w), OFF_RW_LORA // 384
    hm = tt // 8
    last8 = m // 8 - 1
    prev_idx = lambda i: jnp.maximum(i * hm - 1, 0)
    next_idx = lambda i: jnp.minimum((i + 1) * hm, last8)
    full = lambda a: pl.BlockSpec(a.shape, lambda i: (0,) * a.ndim)
    heads = jnp.arange(w) // RWKV_N
    ones = (heads[:, None] == heads[None, :]).astype(BF16)
    t = np.arange(tt)
    same = (t[:, None] // RWKV_CHUNK) == (t[None, :] // RWKV_CHUNK)
    tri = jnp.asarray(np.stack([same & (t[None, :] <= t[:, None]), same & (t[None, :] >= t[:, None])]), BF16)
    params = [p['mu_main'], p['mu_lora'], p['rwkv_w0'], p['rwkv_w_up'], p['rwkv_a0'], p['rwkv_a_up'],
              p['rwkv_g_up'], p['rwkv_k_k'], p['rwkv_k_a'], p['rwkv_r_k'], ones, tri]
    hd = lambda: jax.ShapeDtypeStruct((RWKV_HEADS, m, RWKV_N), F32)
    dhd = lambda: jax.ShapeDtypeStruct((2, RWKV_HEADS, m, RWKV_N), F32)
    tok = lambda: jax.ShapeDtypeStruct((m, w), F32)
    hd_spec = pl.BlockSpec((RWKV_HEADS, tt, RWKV_N), lambda i: (0, i, 0))
    dhd_spec = pl.BlockSpec((2, RWKV_HEADS, tt, RWKV_N), lambda i: (0, 0, i, 0))
    tok_spec = pl.BlockSpec((tt, w), lambda i: (i, 0))
    return pl.pallas_call(
        functools.partial(_rwkv_prep_kernel, tiles_per_seq_of_tile=tiles_per_seq_of_tile),
        grid=(nt,),
        in_specs=[pl.BlockSpec((tt, 3 * w), lambda i: (i, mb)),
                  pl.BlockSpec((tt, 384), lambda i: (i, lb)),
                  pl.BlockSpec((8, 3 * w), lambda i: (prev_idx(i), mb)),
                  pl.BlockSpec((8, 3 * w), lambda i: (next_idx(i), mb)),
                  pl.BlockSpec((8, 384), lambda i: (prev_idx(i), lb)),
                  pl.BlockSpec((8, 384), lambda i: (next_idx(i), lb))] + [full(a) for a in params],
        out_specs=[hd_spec, hd_spec, hd_spec, dhd_spec, dhd_spec, dhd_spec, dhd_spec, tok_spec, tok_spec],
        out_shape=[hd(), hd(), hd(), dhd(), dhd(), dhd(), dhd(), tok(), tok()],
        compiler_params=_cparams("parallel"),
        name="rwkv_prep",
    )(z, z, z, z, z, z, *params)


def _dot3(a, b, dims):
    a_hi = a.astype(BF16)
    b_hi = b.astype(BF16)
    a_lo = (a - a_hi.astype(F32)).astype(BF16)
    b_lo = (b - b_hi.astype(F32)).astype(BF16)
    dn = (dims, ((), ()))
    out = lax.dot_general(a_lo, b_hi, dn, preferred_element_type=F32)
    out = out + lax.dot_general(a_hi, b_lo, dn, preferred_element_type=F32)
    return out + lax.dot_general(a_hi, b_hi, dn, preferred_element_type=F32)


_NN = ((1,), (0,))
_NT = ((1,), (1,))
_TN = ((0,), (0,))


def _dot1(a, b, dims):
    return lax.dot_general(a.astype(BF16), b.astype(BF16), (dims, ((), ())), preferred_element_type=F32)


def _dot_rhs2(a, b, dims):
    a_hi = a.astype(BF16)
    b_hi = b.astype(BF16)
    b_lo = (b - b_hi.astype(F32)).astype(BF16)
    dn = (dims, ((), ()))
    return (lax.dot_general(a_hi, b_lo, dn, preferred_element_type=F32)
            + lax.dot_general(a_hi, b_hi, dn, preferred_element_type=F32))


_RWKV_MM = _dot1
RWKV_CHAINS = 16
RWKV_MAX_GROUP = 8


def _rwkv_scan_kernel(r_ref, kap_ref, v_ref, lw_ref, cs_ref, kd_ref, beta_ref, h0_ref, o_ref, hfin_ref, *,
                      seq_len, group, heads):
    c = RWKV_CHUNK
    n = seq_len // c
    d = pl.program_id(0)
    sgn = 1 - 2 * d
    ti = lax.broadcasted_iota(jnp.int32, (c, c), 0) * sgn
    tj = lax.broadcasted_iota(jnp.int32, (c, c), 1) * sgn
    strict = tj < ti
    incl = tj <= ti
    eye_f = (ti == tj).astype(F32)

    def chunk_terms(hh, cc):
        rows = pl.ds(pl.multiple_of(cc * c, c), c)
        r = r_ref[hh, rows, :]
        kap = kap_ref[hh, rows, :]
        v = v_ref[hh, rows, :]
        lw = lw_ref[0, hh, rows, :]
        cs = cs_ref[0, hh, rows, :]
        kd = kd_ref[0, hh, rows, :]
        beta = beta_ref[0, hh, rows, :]
        tot = jnp.sum(lw, axis=0, keepdims=True)
        kap_t = kap * jnp.exp(cs - lw)
        r_t = r * jnp.exp(cs)
        e_ncs = jnp.exp(-cs)
        k_t = kd * e_ncs
        b_t = beta * e_ncs
        e_end = jnp.exp(tot - cs)
        k_end = kd * e_end
        b_end = beta * e_end
        yield
        p = _RWKV_MM(jnp.concatenate([kap_t, r_t], axis=0), jnp.concatenate([b_t, k_t], axis=0), _NT)
        l_b = jnp.where(strict, p[:c, :c], 0.0)
        l_k = jnp.where(strict, p[:c, c:], 0.0)
        m_b = jnp.where(incl, p[c:, :c], 0.0)
        m_k = jnp.where(incl, p[c:, c:], 0.0)
        yield
        x = -l_b
        y = jnp.concatenate([kap_t, _RWKV_MM(l_k, v, _NN)], axis=1)
        yield
        y = y + _dot_rhs2(x, y, _NN)
        pw = x
        for _ in range(5):
            pw = _RWKV_MM(pw, pw, _NN)
            yield
            y = y + _dot_rhs2(pw, y, _NN)
        yield
        mby = _RWKV_MM(m_b, y, _NN)
        bty = _RWKV_MM(b_end, y, _TN)
        rr = r_t - mby[:, :RWKV_N]
        o0 = _RWKV_MM(m_k, v, _NN) - mby[:, RWKV_N:]
        pc = eye_f * jnp.exp(tot) - bty[:, :RWKV_N]
        qc = _RWKV_MM(k_end, v, _TN) - bty[:, RWKV_N:]
        return rows, rr, o0, pc, qc

    def step(gi, hstates):
        keys = [(hh, u) for u in range(group) for hh in range(heads)]
        terms = dict(zip(keys, _lockstep(
            [chunk_terms(hh, (gi * group + u) + d * (n - 1 - 2 * (gi * group + u))) for hh, u in keys])))
        hstates = list(hstates)
        for u in range(group):
            for hh in range(heads):
                rows, rr, o0, pc, qc = terms[(hh, u)]
                o_ref[0, hh, rows, :] = _dot3(rr, hstates[hh], _NN) + o0
                hstates[hh] = _dot3(pc, hstates[hh], _NN) + qc
        return tuple(hstates)

    fin = lax.fori_loop(0, n // group, step, tuple(h0_ref[0, 0, hh] for hh in range(heads)))
    for hh in range(heads):
        hfin_ref[0, 0, hh] = fin[hh]


def _rwkv_scan(r, kap, v, lw, cs, kd, beta, h0, n_seq, seq_len, row_blk0):
    group = min(seq_len // RWKV_CHUNK, RWKV_MAX_GROUP)
    hp = RWKV_CHAINS // group
    hd_spec = pl.BlockSpec((hp, seq_len, RWKV_N), lambda d, h, b: (h, row_blk0 + b, 0))
    dhd_spec = pl.BlockSpec((1, hp, seq_len, RWKV_N), lambda d, h, b: (d, h, row_blk0 + b, 0))
    st_spec = pl.BlockSpec((1, 1, hp, RWKV_N, RWKV_N), lambda d, h, b: (b, d, h, 0, 0))
    return pl.pallas_call(
        functools.partial(_rwkv_scan_kernel, seq_len=seq_len, group=group, heads=hp),
        grid=(2, RWKV_HEADS // hp, n_seq),
        in_specs=[hd_spec, hd_spec, hd_spec, dhd_spec, dhd_spec, dhd_spec, dhd_spec, st_spec],
        out_specs=[pl.BlockSpec((1, hp, seq_len, RWKV_N), lambda d, h, b: (d, h, b, 0)), st_spec],
        out_shape=[jax.ShapeDtypeStruct((2, RWKV_HEADS, n_seq * seq_len, RWKV_N), F32),
                   jax.ShapeDtypeStruct((n_seq, 2, RWKV_HEADS, RWKV_N, RWKV_N), F32)],
        compiler_params=_cparams("parallel", "parallel", "parallel"),
        name="rwkv_scan",
    )(r, kap, v, lw, cs, kd, beta, h0)


def _merge_kernel(oscan_c_ref, oscan_l_ref, ona_c_ref, ona_l_ref, ort_c_ref, ort_l_ref, bonus_ref, grw_ref,
                  ga_ref, gb_ref, gc_ref, x_ref, gt_ref, gnw_ref, gnb_ref, wbr_ref, wout_ref, lng_ref, lnb_ref,
                  o_ref, orw_ref, *, ctx_tiles):
    is_ctx = pl.program_id(0) < ctx_tiles
    pick = lambda c_val, l_val: jnp.where(is_ctx, c_val, l_val)
    for h in range(RWKV_HEADS):
        sl = slice(h * RWKV_N, (h + 1) * RWKV_N)
        o = pick(oscan_c_ref[0, h] + oscan_c_ref[1, h], oscan_l_ref[0, h] + oscan_l_ref[1, h])
        y = _ln(o, RWKV_GN_EPS) * gnw_ref[h] + gnb_ref[h]
        orw_ref[:, sl] = (y + bonus_ref[:, sl]) * grw_ref[:, sl]
    merged = (_sigmoid(ga_ref[...]) * _dot(pick(ona_c_ref[...], ona_l_ref[...]), wbr_ref[0])
              + _sigmoid(gb_ref[...]) * _dot(pick(ort_c_ref[...], ort_l_ref[...]), wbr_ref[1])
              + _sigmoid(gc_ref[...]) * _dot(orw_ref[...], wbr_ref[2]))
    t = _dot(merged, wout_ref[...])
    o_ref[...] = _ln(DEEPNORM_ALPHA * x_ref[...] + gt_ref[0] * t) * lng_ref[...] + lnb_ref[...]


def _merge(oscan, o_na, o_rt, bonus, g_rw, z, x, mod3, p, row_of_tile, tm):
    m = x.shape[0]
    bw = RWKV_WIDTH
    ctx_tiles = o_na[0].shape[0] // tm
    full = lambda a: pl.BlockSpec(a.shape, lambda i: (0,) * a.ndim)
    tok = pl.BlockSpec((tm, bw), lambda i: (i, 0))
    c_row = lambda i: jnp.minimum(i, ctx_tiles - 1)
    l_row = lambda i: jnp.maximum(i - ctx_tiles, 0)
    tok_c = pl.BlockSpec((tm, bw), lambda i: (c_row(i), 0))
    tok_l = pl.BlockSpec((tm, bw), lambda i: (l_row(i), 0))
    params = [p['rwkv_gn_w'], p['rwkv_gn_b'], p['w_br'], p['w_out'], p['ln_a_g'], p['ln_a_b']]
    return pl.pallas_call(
        functools.partial(_merge_kernel, ctx_tiles=ctx_tiles),
        grid=(m // tm,),
        in_specs=[pl.BlockSpec((2, RWKV_HEADS, tm, RWKV_N), lambda i: (0, 0, c_row(i), 0)),
                  pl.BlockSpec((2, RWKV_HEADS, tm, RWKV_N), lambda i: (0, 0, l_row(i), 0)),
                  tok_c, tok_l, tok_c, tok_l, tok, tok,
                  pl.BlockSpec((tm, D_MODEL), lambda i: (i, 0)),
                  pl.BlockSpec((tm, D_MODEL), lambda i: (i, 1)),
                  pl.BlockSpec((tm, D_MODEL), lambda i: (i, 2)),
                  pl.BlockSpec((tm, D_MODEL), lambda i: (i, 0)),
                  pl.BlockSpec((1, 1, D_MODEL), lambda i: (row_of_tile(i, tm), 0, 2))]
                 + [full(a) for a in params],
        out_specs=pl.BlockSpec((tm, D_MODEL), lambda i: (i, 0)),
        out_shape=jax.ShapeDtypeStruct((m, D_MODEL), F32),
        scratch_shapes=[pltpu.VMEM((tm, bw), F32)],
        compiler_params=_cparams("parallel"),
        name="merge_out_proj",
    )(*oscan, *o_na, *o_rt, bonus, g_rw, z, z, z, x, mod3, *params)


def _top_values(s, k, with_rank=False):
    out = []
    rank = jnp.full(s.shape, float(k), F32)
    for b in range(k):
        m = jnp.max(s, axis=0, keepdims=True)
        out.append(m)
        hit = s == m
        if with_rank:
            rank = jnp.where(hit, float(b), rank)
        s = jnp.where(hit, -jnp.inf, s)
    return (out, rank) if with_rank else out


def _peer_gate_kernel(x_ref, sh_ref, sc_ref, wq_ref, keys_ref, hb_ref, cnt_ref, rank1_ref, e0_ref, e1_ref):
    h = _ln(x_ref[...]) * (1.0 + sc_ref[0]) + sh_ref[0]
    hb_ref[...] = h.astype(BF16)
    ht = h.T.astype(BF16)
    qt = jnp.dot(wq_ref[...], ht, preferred_element_type=F32)
    half = keys_ref.shape[-1]
    for hd in range(PEER_HEADS):
        scores, tops = [], []
        for part in range(2):
            row0 = (hd * 2 + part) * half
            s = _dot(keys_ref[hd, part], qt[row0:row0 + half, :])
            scores.append(s)
            if part == 0:
                tops.append(_top_values(s, PEER_TOPK))
            else:
                top1, rank1 = _top_values(s, PEER_TOPK, with_rank=True)
                tops.append(top1)
        pairs = [tops[0][a] + tops[1][b] for a in range(PEER_TOPK) for b in range(PEER_TOPK // (a + 1))]
        pairs += [jnp.full_like(pairs[0], -jnp.inf)] * (-len(pairs) % 8)
        best = _top_values(jnp.concatenate(pairs, axis=0), PEER_TOPK)
        zsum = jnp.exp(best[0] - best[0])
        for t in range(1, PEER_TOPK):
            zsum = zsum + jnp.exp(best[t] - best[0])
        tau = best[PEER_TOPK - 1]
        cnt = jnp.zeros(scores[0].shape, F32)
        for b, t1 in enumerate(tops[1]):
            cnt = jnp.where(scores[0] + t1 >= tau, float(b + 1), cnt)
        cnt_ref[hd] = cnt
        rank1_ref[hd] = rank1.astype(BF16)
        e0_ref[hd] = jnp.exp(scores[0] - tops[0][0])
        e1_ref[hd] = (jnp.exp(scores[1] - tops[1][0]) / zsum).astype(BF16)


def _peer_gate(x, mod3, wq_t, keys, row_of_tile, tn):
    m = x.shape[0]
    full = lambda a: pl.BlockSpec(a.shape, lambda i: (0,) * a.ndim)
    sk = pl.BlockSpec((PEER_HEADS, PEER_NKEYS, tn), lambda i: (0, 0, i))
    sk_f32 = jax.ShapeDtypeStruct((PEER_HEADS, PEER_NKEYS, m), F32)
    sk_bf16 = jax.ShapeDtypeStruct((PEER_HEADS, PEER_NKEYS, m), BF16)
    return pl.pallas_call(
        _peer_gate_kernel,
        grid=(m // tn,),
        in_specs=[pl.BlockSpec((tn, D_MODEL), lambda i: (i, 0)),
                  pl.BlockSpec((1, 1, D_MODEL), lambda i: (row_of_tile(i, tn), 0, 3)),
                  pl.BlockSpec((1, 1, D_MODEL), lambda i: (row_of_tile(i, tn), 0, 4)),
                  full(wq_t), full(keys)],
        out_specs=[pl.BlockSpec((tn, D_MODEL), lambda i: (i, 0)), sk, sk, sk, sk],
        out_shape=[jax.ShapeDtypeStruct((m, D_MODEL), BF16), sk_f32, sk_bf16, sk_f32, sk_bf16],
        compiler_params=_cparams("parallel"),
        name="peer_gate",
    )(x, mod3, mod3, wq_t, keys)


def _gelu_exact(x):
    return 0.5 * x * (1.0 + lax.erf(x * (2.0 ** -0.5)))


PEER_SUB_ROWS = 32
PEER_SUB_LANES = 256
PEER_GROUP_KEYS = 4
PEER_GROUP = PEER_GROUP_KEYS * PEER_NKEYS


def _peer_expert_kernel(hb_ref, cnt_ref, rank1_ref, e0_ref, e1_ref, u_ref, vt_ref, x_ref, gt_ref,
                        lng_ref, lnb_ref, o_ref, acc_ref, st_ref, st2_ref, act_ref, *, groups):
    e = pl.program_id(1)
    tn = hb_ref.shape[0]

    @pl.when(e == 0)
    def _():
        acc_ref[...] = jnp.zeros_like(acc_ref)

    def pre_activations(g, dst_ref):
        rows = pl.ds(pl.multiple_of(g * PEER_GROUP, PEER_GROUP), PEER_GROUP)
        dst_ref[...] = lax.dot_general(u_ref[rows, :], hb_ref[...], (_NT, ((), ())), preferred_element_type=F32)

    def gated_activations(g, src_ref):
        row0 = pl.multiple_of(g * PEER_GROUP, PEER_GROUP)
        for r in range(PEER_GROUP_KEYS):
            i = (e * groups + g) * PEER_GROUP_KEYS + r
            cnt_row = [cnt_ref[hd, pl.ds(i, 1), :] for hd in range(PEER_HEADS)]
            e0_row = [e0_ref[hd, pl.ds(i, 1), :] for hd in range(PEER_HEADS)]
            for lb in range(tn // PEER_SUB_LANES):
                ls = slice(lb * PEER_SUB_LANES, (lb + 1) * PEER_SUB_LANES)
                sub = (PEER_SUB_ROWS, PEER_SUB_LANES)
                cnt = [jnp.broadcast_to(c[:, ls].astype(BF16), sub) for c in cnt_row]
                e0 = [jnp.broadcast_to(c[:, ls].astype(BF16), sub) for c in e0_row]
                zero = jnp.zeros(sub, BF16)
                for rb in range(PEER_NKEYS // PEER_SUB_ROWS):
                    js = slice(rb * PEER_SUB_ROWS, (rb + 1) * PEER_SUB_ROWS)
                    gate = None
                    for hd in range(PEER_HEADS):
                        term = jnp.where(rank1_ref[hd, js, ls] < cnt[hd], e1_ref[hd, js, ls], zero) * e0[hd]
                        gate = term if gate is None else gate + term
                    off = r * PEER_NKEYS + rb * PEER_SUB_ROWS
                    act = _gelu_exact(src_ref[off:off + PEER_SUB_ROWS, ls]).astype(BF16) * gate
                    act_ref[pl.ds(row0 + off, PEER_SUB_ROWS), ls] = act

    pre_activations(0, st_ref)

    def pair(p, carry):
        gated_activations(2 * p, st_ref)
        pre_activations(2 * p + 1, st2_ref)
        gated_activations(2 * p + 1, st2_ref)
        pre_activations(jnp.minimum(2 * p + 2, groups - 1), st_ref)
        return carry

    lax.fori_loop(0, groups // 2, pair, 0)
    acc_ref[...] += jnp.dot(vt_ref[...], act_ref[...], preferred_element_type=F32)

    @pl.when(e == pl.num_programs(1) - 1)
    def _():
        f = acc_ref[...].T
        o_ref[...] = _ln(DEEPNORM_ALPHA * x_ref[...] + gt_ref[0] * f) * lng_ref[...] + lnb_ref[...]


def _peer_expert(hb, cnt, rank1, e0, e1, u, v_t, x, mod3, ln_g, ln_b, row_of_tile, tn, eb):
    m = x.shape[0]
    n_exp = u.shape[0]
    groups = eb // PEER_GROUP
    sk = pl.BlockSpec((PEER_HEADS, PEER_NKEYS, tn), lambda i, e: (0, 0, i))
    full = lambda a: pl.BlockSpec(a.shape, lambda i, e: (0,) * a.ndim)
    return pl.pallas_call(
        functools.partial(_peer_expert_kernel, groups=groups),
        grid=(m // tn, n_exp // eb),
        in_specs=[pl.BlockSpec((tn, D_MODEL), lambda i, e: (i, 0)), sk, sk, sk, sk,
                  pl.BlockSpec((eb, D_MODEL), lambda i, e: (e, 0)),
                  pl.BlockSpec((D_MODEL, eb), lambda i, e: (0, e)),
                  pl.BlockSpec((tn, D_MODEL), lambda i, e: (i, 0)),
                  pl.BlockSpec((1, 1, D_MODEL), lambda i, e: (row_of_tile(i, tn), 0, 5)),
                  full(ln_g), full(ln_b)],
        out_specs=pl.BlockSpec((tn, D_MODEL), lambda i, e: (i, 0)),
        out_shape=jax.ShapeDtypeStruct((m, D_MODEL), F32),
        scratch_shapes=[pltpu.VMEM((D_MODEL, tn), F32), pltpu.VMEM((PEER_GROUP, tn), F32),
                        pltpu.VMEM((PEER_GROUP, tn), F32), pltpu.VMEM((eb, tn), BF16)],
        compiler_params=_cparams("parallel", "arbitrary"),
        name="peer_experts",
    )(hb, cnt, rank1, e0, e1, u, v_t, x, mod3, ln_g, ln_b)


def _permute_in_proj(w_in):
    na, qk, rv = 3 * 512, 2 * 256, 2 * 512
    o_na, o_qk, o_rv, o_rw, o_g = 0, na, na + qk, na + qk + rv, na + qk + rv + 1920
    return jnp.concatenate([w_in[:, o_g:], w_in[:, o_na:o_qk], w_in[:, o_qk:o_rv], w_in[:, o_rv:o_rw],
                            w_in[:, o_rw:o_g]], axis=1)


def kernel(x_prompt, x_sample, cache_na_k, cache_na_v, state_ret, state_rwkv, c, c_ctx, w_mod, b_mod, w_in, na_rpb, ret_decay_logit, ret_gn_w, ret_gn_b, rwkv_mu, rwkv_w0, rwkv_w_up, rwkv_a0, rwkv_a_up, rwkv_g_up, rwkv_k_k, rwkv_k_a, rwkv_r_k, rwkv_gn_w, rwkv_gn_b, w_br, w_out, ln_a_g, ln_a_b, ln_f_g, ln_f_b, peer_wq, peer_keys, peer_u, peer_v):
    n_ctx, ctx_len, d = x_prompt.shape
    n_lat, lat_len, _ = x_sample.shape
    m_ctx = n_ctx * ctx_len
    assert m_ctx % lat_len == 0 and n_lat == 2
    lat_blk0 = m_ctx // lat_len

    def row_of_tile(i, tm):
        return jnp.where(i < m_ctx // tm, 0, 1 + (i - m_ctx // tm) // (lat_len // tm))

    prep_tt = 256
    assert ctx_len == prep_tt

    def tiles_per_seq_of_tile(i):
        lat = i >= m_ctx // prep_tt
        per = jnp.where(lat, lat_len // prep_tt, 1)
        pos = jnp.where(lat, (i - m_ctx // prep_tt) % (lat_len // prep_tt), 0)
        return pos, per

    x = jnp.concatenate([x_prompt.reshape(m_ctx, d), x_sample.reshape(n_lat * lat_len, d)], axis=0)
    c8 = jnp.concatenate([c_ctx[None], c, jnp.zeros((8 - 1 - n_lat, d), F32)], axis=0)
    mod = _modulation(c8, w_mod, b_mod)
    rope_tabs = _rope_tables(lat_len)
    zeros_ret = jnp.zeros((n_ctx, 2, RET_HEADS, RET_DK, RET_DV), F32)
    zeros_rw = jnp.zeros((n_ctx, 2, RWKV_HEADS, RWKV_N, RWKV_N), F32)

    nk, nv, sr, sw = [], [], [], []
    for l in range(DEPTH):
        mod3 = mod[l].reshape(8, 1, 6 * d)
        w_in_p = _permute_in_proj(w_in[l]).astype(BF16)
        z = _lnmod_matmul(x, mod3, w_in_p, row_of_tile, 1024, 1152, 0, 1)

        o_na_c, k_cache, v_cache = _ctx_attention(z, n_ctx, ctx_len)
        o_na_l = _na_latent(z, cache_na_k[:, l], cache_na_v[:, l], _na_bias_table(na_rpb[l], lat_len // GRID_W),
                            n_lat, lat_len, lat_blk0)
        nk.append(k_cache)
        nv.append(v_cache)

        tables = _ret_decay_tables(ret_decay_logit[l])
        gn_w, gn_b = ret_gn_w[l][None], ret_gn_b[l][None]
        o_rt_c, s_ret = _retention(z, zeros_ret, tables, gn_w, gn_b, n_ctx, ctx_len, 0, None)
        o_rt_l, _ = _retention(z, state_ret[:, l], tables, gn_w, gn_b, n_lat, lat_len, lat_blk0, rope_tabs)
        sr.append(s_ret)

        mu = rwkv_mu[l]
        prm = {
            'mu_main': mu[None, :1536], 'mu_lora': mu[None, 1536:],
            'rwkv_w0': rwkv_w0[l][:, None], 'rwkv_w_up': rwkv_w_up[l].astype(BF16),
            'rwkv_a0': rwkv_a0[l][:, None], 'rwkv_a_up': rwkv_a_up[l].astype(BF16),
            'rwkv_g_up': rwkv_g_up[l].astype(BF16), 'rwkv_k_k': rwkv_k_k[l][None],
            'rwkv_k_a': rwkv_k_a[l][None], 'rwkv_r_k': rwkv_r_k[l].reshape(1, RWKV_WIDTH),
        }
        r, kap, v, lw, cs, kd, beta, g_rw, bonus = _rwkv_prep(z, prm, tiles_per_seq_of_tile, prep_tt)
        o_c, h_fin = _rwkv_scan(r, kap, v, lw, cs, kd, beta, zeros_rw, n_ctx, ctx_len, 0)
        h0_lat = jnp.swapaxes(state_rwkv[:, l], -1, -2)
        o_l, _ = _rwkv_scan(r, kap, v, lw, cs, kd, beta, h0_lat, n_lat, lat_len, lat_blk0)
        sw.append(jnp.swapaxes(h_fin, -1, -2))

        mp = {
            'rwkv_gn_w': rwkv_gn_w[l].reshape(RWKV_HEADS, 1, RWKV_N),
            'rwkv_gn_b': rwkv_gn_b[l].reshape(RWKV_HEADS, 1, RWKV_N),
            'w_br': w_br[l].astype(BF16), 'w_out': w_out[l].astype(BF16),
            'ln_a_g': ln_a_g[l][None], 'ln_a_b': ln_a_b[l][None],
        }
        x = _merge((o_c, o_l), (o_na_c, o_na_l), (o_rt_c, o_rt_l), bonus, g_rw, z, x, mod3, mp, row_of_tile, 256)

        wq_t = peer_wq[l].T.astype(BF16)
        hb, cnt, rank1, e0, e1 = _peer_gate(x, mod3, wq_t, peer_keys[l].astype(BF16), row_of_tile, 256)
        x = _peer_expert(hb, cnt, rank1, e0, e1, peer_u[l].astype(BF16), peer_v[l].T.astype(BF16), x, mod3,
                         ln_f_g[l][None], ln_f_b[l][None], row_of_tile, 512, 2048)

    dt = x_prompt.dtype
    y_prompt = x[:m_ctx].reshape(n_ctx, ctx_len, d)
    y_sample = x[m_ctx:].reshape(n_lat, lat_len, d)
    return (y_prompt, y_sample, jnp.stack(nk, axis=1).astype(dt), jnp.stack(nv, axis=1).astype(dt),
            jnp.stack(sr, axis=1).astype(dt), jnp.stack(sw, axis=1).astype(dt))
```

```python
import functools
import math

import jax
import jax.numpy as jnp
import numpy as np
from jax import lax
from jax.experimental import pallas as pl
from jax.experimental.pallas import tpu as pltpu

F32 = jnp.float32
BF16 = jnp.bfloat16

D_MODEL = 1024
DEPTH = 2
GRID_W = 64
NA_HEADS = 8
NA_HEAD_DIM = 64
WIN_R = 8
WIN_C = 16
NEG_INF = -1e30
RET_HEADS = 4
RET_DK = 64
RET_DV = 128
RET_CHUNK = 64
RET_GN_EPS = 1e-5
ROPE_BASE = 10000.0
RWKV_HEADS = 8
RWKV_N = 64
RWKV_WIDTH = 512
RWKV_DECAY_SCALE = 0.606531
RWKV_GN_EPS = 64e-5
RWKV_CHUNK = 64
PEER_HEADS = 8
PEER_NKEYS = 128
PEER_TOPK = 16
LN_EPS = 1e-5
DEEPNORM_ALPHA = (2 * DEPTH) ** 0.25

OFF_GATE = 0
OFF_NA = 3072
OFF_RET_QK = 4608
OFF_RET_V = 5120
OFF_RW_MAIN = 6144
OFF_RW_LORA = 7680
P_IN = 8064

VMEM_LIMIT_BYTES = 56 * 1024 * 1024


def _cparams(*sem):
    return pltpu.CompilerParams(dimension_semantics=sem, vmem_limit_bytes=VMEM_LIMIT_BYTES)


def _ln(x, eps=LN_EPS):
    mu = jnp.mean(x, axis=-1, keepdims=True)
    xc = x - mu
    var = jnp.mean(xc * xc, axis=-1, keepdims=True)
    return xc * lax.rsqrt(var + eps)


def _sigmoid(x):
    return 1.0 / (1.0 + jnp.exp(-x))


def _dot(a, b):
    return jnp.dot(a.astype(BF16), b.astype(BF16), preferred_element_type=F32)


def _dot_nt(a, b):
    return lax.dot_general(a.astype(BF16), b.astype(BF16), (((1,), (1,)), ((), ())),
                           preferred_element_type=F32)


def _dot_tn(a, b):
    return lax.dot_general(a.astype(BF16), b.astype(BF16), (((0,), (0,)), ((), ())),
                           preferred_element_type=F32)


def _lockstep(gens):
    out = [None] * len(gens)
    live = list(range(len(gens)))
    while live:
        for i in list(live):
            try:
                next(gens[i])
            except StopIteration as done:
                out[i] = done.value
                live.remove(i)
    return out


def _split3(x):
    hi = x.astype(BF16)
    r1 = x - hi.astype(F32)
    mid = r1.astype(BF16)
    lo = (r1 - mid.astype(F32)).astype(BF16)
    return hi, mid, lo


def _dot_exact_lhs(sel, x):
    hi, mid, lo = _split3(x)
    s = sel.astype(BF16)
    out = jnp.dot(s, lo, preferred_element_type=F32)
    out = out + jnp.dot(s, mid, preferred_element_type=F32)
    return out + jnp.dot(s, hi, preferred_element_type=F32)


def _dot_exact_rhs(x, sel):
    hi, mid, lo = _split3(x)
    s = sel.astype(BF16)
    out = jnp.dot(lo, s, preferred_element_type=F32)
    out = out + jnp.dot(mid, s, preferred_element_type=F32)
    return out + jnp.dot(hi, s, preferred_element_type=F32)


def _mod_kernel(c_ref, w_ref, b_ref, o_ref):
    c = c_ref[...]
    s = c * _sigmoid(c)
    o_ref[0] = _dot(s, w_ref[0]) + b_ref[0]


def _modulation(c8, w_mod, b_mod):
    tn = 1536
    n = w_mod.shape[-1]
    return pl.pallas_call(
        _mod_kernel,
        grid=(DEPTH, n // tn),
        in_specs=[pl.BlockSpec((8, D_MODEL), lambda l, j: (0, 0)),
                  pl.BlockSpec((1, D_MODEL, tn), lambda l, j: (l, 0, j)),
                  pl.BlockSpec((1, 1, tn), lambda l, j: (l, 0, j))],
        out_specs=pl.BlockSpec((1, 8, tn), lambda l, j: (l, 0, j)),
        out_shape=jax.ShapeDtypeStruct((DEPTH, 8, n), F32),
        compiler_params=_cparams("parallel", "parallel"),
        name="modulation",
    )(c8, w_mod, b_mod.reshape(DEPTH, 1, n))


def _lnmod_matmul_kernel(x_ref, sh_ref, sc_ref, w_ref, o_ref, h_ref):
    @pl.when(pl.program_id(1) == 0)
    def _():
        h = _ln(x_ref[...]) * (1.0 + sc_ref[0]) + sh_ref[0]
        h_ref[...] = h.astype(BF16)

    o_ref[...] = jnp.dot(h_ref[...], w_ref[...], preferred_element_type=F32)


def _lnmod_matmul(x, mod3, w, row_of_tile, tm, tn, sh_blk, sc_blk):
    m = x.shape[0]
    n = w.shape[1]
    return pl.pallas_call(
        _lnmod_matmul_kernel,
        grid=(m // tm, n // tn),
        in_specs=[pl.BlockSpec((tm, D_MODEL), lambda i, j: (i, 0)),
                  pl.BlockSpec((1, 1, D_MODEL), lambda i, j: (row_of_tile(i, tm), 0, sh_blk)),
                  pl.BlockSpec((1, 1, D_MODEL), lambda i, j: (row_of_tile(i, tm), 0, sc_blk)),
                  pl.BlockSpec((D_MODEL, tn), lambda i, j: (0, j))],
        out_specs=pl.BlockSpec((tm, tn), lambda i, j: (i, j)),
        out_shape=jax.ShapeDtypeStruct((m, n), F32),
        scratch_shapes=[pltpu.VMEM((tm, D_MODEL), BF16)],
        compiler_params=_cparams("parallel", "arbitrary"),
        name="adaln_in_proj",
    )(x, mod3, mod3, w)


def _ctx_attn_kernel(q_ref, k_ref, v_ref, o_ref, kc_ref, vc_ref):
    scale = NA_HEAD_DIM ** -0.5
    half = q_ref.shape[0] // 2

    def head(hh, q0):
        sl = slice(hh * NA_HEAD_DIM, (hh + 1) * NA_HEAD_DIM)
        k = k_ref[:, sl]
        v = v_ref[:, sl]
        if q0 == 0:
            kc_ref[0, hh] = k
            vc_ref[0, hh] = v
        s = _dot_nt(q_ref[q0:q0 + half, sl], k) * scale
        yield
        m = jnp.max(s, axis=-1, keepdims=True)
        p = jnp.exp(s - m)
        l = jnp.sum(p, axis=-1, keepdims=True)
        yield
        o_ref[q0:q0 + half, sl] = _dot(p, v) / l

    _lockstep([head(hh, q0) for hh in range(2) for q0 in (0, half)])


def _ctx_attention(z, n_seq, seq_len):
    cb = OFF_NA // 128
    cache_spec = pl.BlockSpec((1, 2, seq_len, NA_HEAD_DIM), lambda b, h: (b, h, 0, 0))
    cache_shape = jax.ShapeDtypeStruct((n_seq, NA_HEADS, seq_len, NA_HEAD_DIM), F32)
    return pl.pallas_call(
        _ctx_attn_kernel,
        grid=(n_seq, NA_HEADS // 2),
        in_specs=[pl.BlockSpec((seq_len, 128), lambda b, h: (b, cb + h)),
                  pl.BlockSpec((seq_len, 128), lambda b, h: (b, cb + 4 + h)),
                  pl.BlockSpec((seq_len, 128), lambda b, h: (b, cb + 8 + h))],
        out_specs=[pl.BlockSpec((seq_len, 128), lambda b, h: (b, h)), cache_spec, cache_spec],
        out_shape=[jax.ShapeDtypeStruct((n_seq * seq_len, NA_HEADS * NA_HEAD_DIM), F32), cache_shape, cache_shape],
        compiler_params=_cparams("parallel", "parallel"),
        name="ctx_attention",
    )(z, z, z)


def _na_bias_kernel(rpb_ref, rowsel_ref, colsel_ref, neg_ref, o_ref):
    picked = _dot_exact_lhs(rowsel_ref[...], rpb_ref[0])
    o_ref[0] = _dot_exact_rhs(picked, colsel_ref[...]) + neg_ref[...]


def _na_bias_table(rpb, rows):
    kr = min(WIN_R, rows)
    nh, n_dr, n_dc = rpb.shape
    rep = np.array([0, 1, 2, 3, 4, rows - 3, rows - 2, rows - 1])
    start = np.clip(rep - kr // 2, 0, rows - kr)
    dr = start[:, None] - rep[:, None] + WIN_R - 1 + np.arange(kr)[None, :]
    qcol = np.arange(GRID_W)
    kcol = np.arange(GRID_W)
    q_start = np.clip(qcol - WIN_C // 2, 0, GRID_W - WIN_C)
    rel = kcol[None, :] - q_start[:, None]
    valid = (rel >= 0) & (rel < WIN_C)
    dc = np.clip(kcol[None, :] - qcol[:, None], -(WIN_C - 1), WIN_C - 1) + WIN_C - 1
    row_sel = np.zeros((8 * kr, 16), np.float32)
    row_sel[np.arange(8 * kr), dr.reshape(-1)] = 1.0
    col_sel = np.zeros((32, GRID_W * GRID_W), np.float32)
    qq, kk = np.nonzero(valid)
    col_sel[dc[qq, kk], qq * GRID_W + kk] = 1.0
    neg = np.where(valid, 0.0, NEG_INF).astype(np.float32).reshape(1, -1)
    rpb_p = jnp.pad(rpb.astype(F32), ((0, 0), (0, 16 - n_dr), (0, 32 - n_dc)))
    full = lambda a: pl.BlockSpec(a.shape, lambda h: (0,) * a.ndim)
    consts = [jnp.asarray(row_sel, BF16), jnp.asarray(col_sel, BF16), jnp.asarray(neg)]
    b = pl.pallas_call(
        _na_bias_kernel,
        grid=(nh,),
        in_specs=[pl.BlockSpec((1, 16, 32), lambda h: (h, 0, 0))] + [full(a) for a in consts],
        out_specs=pl.BlockSpec((1, 8 * kr, GRID_W * GRID_W), lambda h: (h, 0, 0)),
        out_shape=jax.ShapeDtypeStruct((nh, 8 * kr, GRID_W * GRID_W), F32),
        compiler_params=_cparams("parallel"),
        name="na_bias",
    )(rpb_p, *consts)
    b = b.reshape(nh, 8, kr, GRID_W, GRID_W)
    return b.transpose(0, 1, 3, 2, 4).reshape(nh, 8, GRID_W, kr * GRID_W)


NA_ROWS_PER_STEP = 4


def _na_latent_kernel(q_ref, k_ref, v_ref, ck_ref, cv_ref, b_ref, o_ref, *, rows):
    scale = NA_HEAD_DIM ** -0.5
    kr = min(WIN_R, rows)

    def query_row(r, hh):
        start = jnp.clip(r - kr // 2, 0, rows - kr)
        typ = jnp.where(r < 4, r, jnp.where(r > rows - 4, r - (rows - 8), 4))
        q0 = pl.multiple_of(r * GRID_W, GRID_W)
        k0 = pl.multiple_of(start * GRID_W, GRID_W)
        sl = slice(hh * NA_HEAD_DIM, (hh + 1) * NA_HEAD_DIM)
        q = q_ref[pl.ds(q0, GRID_W), sl]
        kw = k_ref[pl.ds(k0, kr * GRID_W), sl]
        vw = v_ref[pl.ds(k0, kr * GRID_W), sl]
        s_w = _dot_nt(q, kw) * scale + b_ref[hh, typ]
        s_c = _dot_nt(q, ck_ref[0, hh]) * scale
        yield
        m = jnp.maximum(jnp.max(s_w, axis=-1, keepdims=True), jnp.max(s_c, axis=-1, keepdims=True))
        p_w = jnp.exp(s_w - m)
        p_c = jnp.exp(s_c - m)
        l = jnp.sum(p_w, axis=-1, keepdims=True) + jnp.sum(p_c, axis=-1, keepdims=True)
        yield
        o_ref[pl.ds(q0, GRID_W), sl] = (_dot(p_w, vw) + _dot(p_c, cv_ref[0, hh])) / l

    def body(rp, carry):
        _lockstep([query_row(rp * NA_ROWS_PER_STEP + u, hh) for u in range(NA_ROWS_PER_STEP) for hh in range(2)])
        return carry

    lax.fori_loop(0, rows // NA_ROWS_PER_STEP, body, 0)


def _na_latent(z, ctx_k, ctx_v, bias, n_seq, seq_len, row_blk0):
    cb = OFF_NA // 128
    rows = seq_len // GRID_W
    past = ctx_k.shape[2]
    return pl.pallas_call(
        functools.partial(_na_latent_kernel, rows=rows),
        grid=(n_seq, NA_HEADS // 2),
        in_specs=[pl.BlockSpec((seq_len, 128), lambda b, h: (row_blk0 + b, cb + h)),
                  pl.BlockSpec((seq_len, 128), lambda b, h: (row_blk0 + b, cb + 4 + h)),
                  pl.BlockSpec((seq_len, 128), lambda b, h: (row_blk0 + b, cb + 8 + h)),
                  pl.BlockSpec((1, 2, past, NA_HEAD_DIM), lambda b, h: (b, h, 0, 0)),
                  pl.BlockSpec((1, 2, past, NA_HEAD_DIM), lambda b, h: (b, h, 0, 0)),
                  pl.BlockSpec((2, 8, GRID_W, bias.shape[-1]), lambda b, h: (h, 0, 0, 0))],
        out_specs=pl.BlockSpec((seq_len, 128), lambda b, h: (b, h)),
        out_shape=jax.ShapeDtypeStruct((n_seq * seq_len, NA_HEADS * NA_HEAD_DIM), F32),
        compiler_params=_cparams("parallel", "parallel"),
        name="na_latent",
    )(z, z, z, ctx_k, ctx_v, bias)


def _ret_decay_tables(decay_logit):
    c = RET_CHUNK
    lg = jnp.log(jax.nn.sigmoid(decay_logit.astype(F32)))[:, :, None, None]
    i = jnp.arange(c, dtype=F32)
    diff = i[:, None] - i[None, :]
    m_f = jnp.where(diff >= 0, jnp.exp(jnp.maximum(diff, 0.0) * lg[0]), 0.0)
    m_b = jnp.where(diff <= 0, jnp.exp(jnp.maximum(-diff, 0.0) * lg[1]), 0.0)
    mask = jnp.stack([m_f, m_b])
    ones = jnp.ones((1, 1, 1, RET_DK), F32)
    q_f = jnp.exp((i + 1.0)[None, :, None] * lg[0]) * ones[0]
    q_b = jnp.exp((c - i)[None, :, None] * lg[1]) * ones[0]
    k_f = jnp.exp((c - 1.0 - i)[None, :, None] * lg[0]) * ones[0]
    k_b = jnp.exp(i[None, :, None] * lg[1]) * ones[0]
    c_dec = jnp.exp(c * lg) * jnp.ones((1, 1, RET_DK, RET_DV), F32)
    return mask, jnp.stack([q_f, q_b]), jnp.stack([k_f, k_b]), c_dec


def _rope_tables(seq_len):
    quarter = RET_DK // 4
    pos = np.arange(seq_len)
    freqs = 1.0 / (ROPE_BASE ** (jnp.arange(quarter, dtype=F32) / quarter))
    a_row = jnp.asarray(pos // GRID_W, F32)[:, None] * freqs[None, :]
    a_col = jnp.asarray(pos % GRID_W, F32)[:, None] * freqs[None, :]
    cos = jnp.concatenate([jnp.cos(a_row)] * 2 + [jnp.cos(a_col)] * 2, axis=-1)
    sin = jnp.concatenate([-jnp.sin(a_row), jnp.sin(a_row), -jnp.sin(a_col), jnp.sin(a_col)], axis=-1)
    return jnp.tile(cos, (1, RET_HEADS)), jnp.tile(sin, (1, RET_HEADS))


def _retention_kernel(*refs, seq_len, rope):
    if rope:
        (q_ref, k_ref, v_ref, g_ref, s0_ref, mask_ref, qd_ref, kd_ref, cd_ref, gnw_ref, gnb_ref,
         cos_ref, sin_ref, o_ref, sfin_ref, qs_ref, ks_ref, acc_ref) = refs
    else:
        (q_ref, k_ref, v_ref, g_ref, s0_ref, mask_ref, qd_ref, kd_ref, cd_ref, gnw_ref, gnb_ref,
         o_ref, sfin_ref, qs_ref, ks_ref, acc_ref) = refs
    c = RET_CHUNK
    n = seq_len // c
    quarter = RET_DK // 4
    q = q_ref[...]
    k = k_ref[...] * (RET_DK ** -0.5)
    if rope:
        lane = lax.broadcasted_iota(jnp.int32, q.shape, 1)
        first = (lane % (2 * quarter)) < quarter
        width = q.shape[1]

        def rot(x):
            swapped = jnp.where(first, pltpu.roll(x, width - quarter, 1), pltpu.roll(x, quarter, 1))
            return x * cos_ref[...] + swapped * sin_ref[...]

        q = rot(q)
        k = rot(k)
    qs_ref[...] = q
    ks_ref[...] = k

    sfin_ref[...] = s0_ref[...]

    def chain(h, d, ci):
        ksl = slice(h * RET_DK, (h + 1) * RET_DK)
        vsl = slice(h * RET_DV, (h + 1) * RET_DV)
        cc = ci if d == 0 else n - 1 - ci
        rows = pl.ds(pl.multiple_of(cc * c, c), c)
        qc = qs_ref[rows, ksl]
        kc = ks_ref[rows, ksl]
        vc = v_ref[rows, vsl]
        att = _dot_nt(qc, kc) * mask_ref[d, h]
        s = sfin_ref[0, d, h]
        yield
        acc_ref[d, rows, vsl] = _dot(att, vc) + _dot(qc * qd_ref[d, h], s)
        yield
        sfin_ref[0, d, h] = s * cd_ref[d, h] + _dot_tn(kc * kd_ref[d, h], vc)

    def step(ci, carry):
        _lockstep([chain(h, d, ci) for h in range(RET_HEADS) for d in range(2)])
        return carry

    lax.fori_loop(0, n, step, 0)

    for h in range(RET_HEADS):
        vsl = slice(h * RET_DV, (h + 1) * RET_DV)
        y = _ln(acc_ref[0, :, vsl] + acc_ref[1, :, vsl], RET_GN_EPS) * gnw_ref[:, vsl] + gnb_ref[:, vsl]
        g = g_ref[:, vsl]
        o_ref[:, vsl] = y * (g * _sigmoid(g))


def _retention(z, s0, tables, gn_w, gn_b, n_seq, seq_len, row_blk0, rope_tabs):
    mask, qd, kd, cd = tables
    rope = rope_tabs is not None
    qk_w = RET_HEADS * RET_DK
    v_w = RET_HEADS * RET_DV
    full = lambda a: pl.BlockSpec(a.shape, lambda b: (0,) * a.ndim)
    in_specs = [pl.BlockSpec((seq_len, qk_w), lambda b: (row_blk0 + b, OFF_RET_QK // qk_w)),
                pl.BlockSpec((seq_len, qk_w), lambda b: (row_blk0 + b, OFF_RET_QK // qk_w + 1)),
                pl.BlockSpec((seq_len, v_w), lambda b: (row_blk0 + b, OFF_RET_V // v_w)),
                pl.BlockSpec((seq_len, v_w), lambda b: (row_blk0 + b, OFF_RET_V // v_w + 1)),
                pl.BlockSpec((1, 2, RET_HEADS, RET_DK, RET_DV), lambda b: (b, 0, 0, 0, 0)),
                full(mask), full(qd), full(kd), full(cd), full(gn_w), full(gn_b)]
    args = [z, z, z, z, s0, mask, qd, kd, cd, gn_w, gn_b]
    if rope:
        in_specs += [full(rope_tabs[0]), full(rope_tabs[1])]
        args += list(rope_tabs)
    return pl.pallas_call(
        functools.partial(_retention_kernel, seq_len=seq_len, rope=rope),
        grid=(n_seq,),
        in_specs=in_specs,
        out_specs=[pl.BlockSpec((seq_len, v_w), lambda b: (b, 0)),
                   pl.BlockSpec((1, 2, RET_HEADS, RET_DK, RET_DV), lambda b: (b, 0, 0, 0, 0))],
        out_shape=[jax.ShapeDtypeStruct((n_seq * seq_len, v_w), F32),
                   jax.ShapeDtypeStruct((n_seq, 2, RET_HEADS, RET_DK, RET_DV), F32)],
        scratch_shapes=[pltpu.VMEM((seq_len, qk_w), F32), pltpu.VMEM((seq_len, qk_w), F32),
                        pltpu.VMEM((2, seq_len, v_w), F32)],
        compiler_params=_cparams("parallel"),
        name="retention_rope" if rope else "retention",
    )(*args)


def _rwkv_prep_kernel(zm_ref, zl_ref, pm_ref, nm_ref, pl_ref, nl_ref, mum_ref, mul_ref,
                      w0_ref, wup_ref, a0_ref, aup_ref, gup_ref, kk_ref, ka_ref, rk_ref, ones_ref, tri_ref,
                      r_out, kap_out, v_out, lw_out, cs_out, kd_out, beta_out, g_out, bonus_out,
                      *, tiles_per_seq_of_tile):
    i = pl.program_id(0)
    pos, per = tiles_per_seq_of_tile(i)
    has_prev = (pos != 0).astype(F32)
    has_next = (pos != per - 1).astype(F32)
    tt = zm_ref.shape[0]

    def shift(z_ref, p_ref, n_ref, mu_ref):
        z = z_ref[...]
        row = lax.broadcasted_iota(jnp.int32, z.shape, 0)
        prev = jnp.where(row == 0, p_ref[7:8, :] * has_prev, pltpu.roll(z, 1, 0))
        nxt = jnp.where(row == tt - 1, n_ref[0:1, :] * has_next, pltpu.roll(z, tt - 1, 0))
        return z + mu_ref[...] * (0.5 * (prev + nxt) - z)

    zm = shift(zm_ref, pm_ref, nm_ref, mum_ref)
    zl = shift(zl_ref, pl_ref, nl_ref, mul_ref)
    w = RWKV_WIDTH
    r_c, k_c, v_c = zm[:, 0:w], zm[:, w:2 * w], zm[:, 2 * w:3 * w]
    ones = ones_ref[...]
    g_out[...] = _dot(_sigmoid(zl[:, 256:384]), gup_ref[...])
    kk = k_c * kk_ref[...]
    ss = _dot_exact_rhs(kk * kk, ones)
    kap = kk * lax.rsqrt(jnp.maximum(ss, 1e-24))
    bonus_out[...] = _dot_exact_rhs(r_c * k_c * rk_ref[...], ones) * v_c
    for h in range(RWKV_HEADS):
        sl = slice(h * RWKV_N, (h + 1) * RWKV_N)
        r_out[h] = r_c[:, sl]
        kap_out[h] = kap[:, sl]
        v_out[h] = v_c[:, sl]
    for d in range(2):
        wl = zl[:, d * 64:(d + 1) * 64]
        al = zl[:, 128 + d * 64:128 + (d + 1) * 64]
        lw = -RWKV_DECAY_SCALE * _sigmoid(w0_ref[d] + _dot(jnp.tanh(wl), wup_ref[d]))
        a = _sigmoid(a0_ref[d] + _dot(al, aup_ref[d]))
        k_d = k_c * (1.0 + (a - 1.0) * ka_ref[...])
        beta = kap * a
        cs = _dot_exact_lhs(tri_ref[d], lw)
        for h in range(RWKV_HEADS):
            sl = slice(h * RWKV_N, (h + 1) * RWKV_N)
            lw_out[d, h] = lw[:, sl]
            cs_out[d, h] = cs[:, sl]
            kd_out[d, h] = k_d[:, sl]
            beta_out[d, h] = beta[:, sl]


def _rwkv_prep(z, p, tiles_per_seq_of_tile, tt):
    m = z.shape[0]
    nt = m // tt
    w = RWKV_WIDTH
    mb, lb = OFF_RW_MAIN // (3 * w), OFF_RW_LORA // 384
    hm = tt // 8
    last8 = m // 8 - 1
    prev_idx = lambda i: jnp.maximum(i * hm - 1, 0)
    next_idx = lambda i: jnp.minimum((i + 1) * hm, last8)
    full = lambda a: pl.BlockSpec(a.shape, lambda i: (0,) * a.ndim)
    heads = jnp.arange(w) // RWKV_N
    ones = (heads[:, None] == heads[None, :]).astype(BF16)
    t = np.arange(tt)
    same = (t[:, None] // RWKV_CHUNK) == (t[None, :] // RWKV_CHUNK)
    tri = jnp.asarray(np.stack([same & (t[None, :] <= t[:, None]), same & (t[None, :] >= t[:, None])]), BF16)
    params = [p['mu_main'], p['mu_lora'], p['rwkv_w0'], p['rwkv_w_up'], p['rwkv_a0'], p['rwkv_a_up'],
              p['rwkv_g_up'], p['rwkv_k_k'], p['rwkv_k_a'], p['rwkv_r_k'], ones, tri]
    hd = lambda: jax.ShapeDtypeStruct((RWKV_HEADS, m, RWKV_N), F32)
    dhd = lambda: jax.ShapeDtypeStruct((2, RWKV_HEADS, m, RWKV_N), F32)
    tok = lambda: jax.ShapeDtypeStruct((m, w), F32)
    hd_spec = pl.BlockSpec((RWKV_HEADS, tt, RWKV_N), lambda i: (0, i, 0))
    dhd_spec = pl.BlockSpec((2, RWKV_HEADS, tt, RWKV_N), lambda i: (0, 0, i, 0))
    tok_spec = pl.BlockSpec((tt, w), lambda i: (i, 0))
    return pl.pallas_call(
        functools.partial(_rwkv_prep_kernel, tiles_per_seq_of_tile=tiles_per_seq_of_tile),
        grid=(nt,),
        in_specs=[pl.BlockSpec((tt, 3 * w), lambda i: (i, mb)),
                  pl.BlockSpec((tt, 384), lambda i: (i, lb)),
                  pl.BlockSpec((8, 3 * w), lambda i: (prev_idx(i), mb)),
                  pl.BlockSpec((8, 3 * w), lambda i: (next_idx(i), mb)),
                  pl.BlockSpec((8, 384), lambda i: (prev_idx(i), lb)),
                  pl.BlockSpec((8, 384), lambda i: (next_idx(i), lb))] + [full(a) for a in params],
        out_specs=[hd_spec, hd_spec, hd_spec, dhd_spec, dhd_spec, dhd_spec, dhd_spec, tok_spec, tok_spec],
        out_shape=[hd(), hd(), hd(), dhd(), dhd(), dhd(), dhd(), tok(), tok()],
        compiler_params=_cparams("parallel"),
        name="rwkv_prep",
    )(z, z, z, z, z, z, *params)


def _dot3(a, b, dims):
    a_hi = a.astype(BF16)
    b_hi = b.astype(BF16)
    a_lo = (a - a_hi.astype(F32)).astype(BF16)
    b_lo = (b - b_hi.astype(F32)).astype(BF16)
    dn = (dims, ((), ()))
    out = lax.dot_general(a_lo, b_hi, dn, preferred_element_type=F32)
    out = out + lax.dot_general(a_hi, b_lo, dn, preferred_element_type=F32)
    return out + lax.dot_general(a_hi, b_hi, dn, preferred_element_type=F32)


_NN = ((1,), (0,))
_NT = ((1,), (1,))
_TN = ((0,), (0,))


def _dot1(a, b, dims):
    return lax.dot_general(a.astype(BF16), b.astype(BF16), (dims, ((), ())), preferred_element_type=F32)


def _dot_rhs2(a, b, dims):
    a_hi = a.astype(BF16)
    b_hi = b.astype(BF16)
    b_lo = (b - b_hi.astype(F32)).astype(BF16)
    dn = (dims, ((), ()))
    return (lax.dot_general(a_hi, b_lo, dn, preferred_element_type=F32)
            + lax.dot_general(a_hi, b_hi, dn, preferred_element_type=F32))


_RWKV_MM = _dot1
RWKV_CHAINS = 16
RWKV_MAX_GROUP = 8


def _rwkv_scan_kernel(r_ref, kap_ref, v_ref, lw_ref, cs_ref, kd_ref, beta_ref, h0_ref, o_ref, hfin_ref, *,
                      seq_len, group, heads):
    c = RWKV_CHUNK
    n = seq_len // c
    d = pl.program_id(0)
    sgn = 1 - 2 * d
    ti = lax.broadcasted_iota(jnp.int32, (c, c), 0) * sgn
    tj = lax.broadcasted_iota(jnp.int32, (c, c), 1) * sgn
    strict = tj < ti
    incl = tj <= ti
    eye_f = (ti == tj).astype(F32)

    def chunk_terms(hh, cc):
        rows = pl.ds(pl.multiple_of(cc * c, c), c)
        r = r_ref[hh, rows, :]
        kap = kap_ref[hh, rows, :]
        v = v_ref[hh, rows, :]
        lw = lw_ref[0, hh, rows, :]
        cs = cs_ref[0, hh, rows, :]
        kd = kd_ref[0, hh, rows, :]
        beta = beta_ref[0, hh, rows, :]
        tot = jnp.sum(lw, axis=0, keepdims=True)
        kap_t = kap * jnp.exp(cs - lw)
        r_t = r * jnp.exp(cs)
        e_ncs = jnp.exp(-cs)
        k_t = kd * e_ncs
        b_t = beta * e_ncs
        e_end = jnp.exp(tot - cs)
        k_end = kd * e_end
        b_end = beta * e_end
        yield
        p = _RWKV_MM(jnp.concatenate([kap_t, r_t], axis=0), jnp.concatenate([b_t, k_t], axis=0), _NT)
        l_b = jnp.where(strict, p[:c, :c], 0.0)
        l_k = jnp.where(strict, p[:c, c:], 0.0)
        m_b = jnp.where(incl, p[c:, :c], 0.0)
        m_k = jnp.where(incl, p[c:, c:], 0.0)
        yield
        x = -l_b
        y = jnp.concatenate([kap_t, _RWKV_MM(l_k, v, _NN)], axis=1)
        yield
        y = y + _dot_rhs2(x, y, _NN)
        pw = x
        for _ in range(5):
            pw = _RWKV_MM(pw, pw, _NN)
            yield
            y = y + _dot_rhs2(pw, y, _NN)
        yield
        mby = _RWKV_MM(m_b, y, _NN)
        bty = _RWKV_MM(b_end, y, _TN)
        rr = r_t - mby[:, :RWKV_N]
        o0 = _RWKV_MM(m_k, v, _NN) - mby[:, RWKV_N:]
        pc = eye_f * jnp.exp(tot) - bty[:, :RWKV_N]
        qc = _RWKV_MM(k_end, v, _TN) - bty[:, RWKV_N:]
        return rows, rr, o0, pc, qc

    def step(gi, hstates):
        keys = [(hh, u) for u in range(group) for hh in range(heads)]
        terms = dict(zip(keys, _lockstep(
            [chunk_terms(hh, (gi * group + u) + d * (n - 1 - 2 * (gi * group + u))) for hh, u in keys])))
        hstates = list(hstates)
        for u in range(group):
            for hh in range(heads):
                rows, rr, o0, pc, qc = terms[(hh, u)]
                o_ref[0, hh, rows, :] = _dot3(rr, hstates[hh], _NN) + o0
                hstates[hh] = _dot3(pc, hstates[hh], _NN) + qc
        return tuple(hstates)

    fin = lax.fori_loop(0, n // group, step, tuple(h0_ref[0, 0, hh] for hh in range(heads)))
    for hh in range(heads):
        hfin_ref[0, 0, hh] = fin[hh]


def _rwkv_scan(r, kap, v, lw, cs, kd, beta, h0, n_seq, seq_len, row_blk0):
    group = min(seq_len // RWKV_CHUNK, RWKV_MAX_GROUP)
    hp = RWKV_CHAINS // group
    hd_spec = pl.BlockSpec((hp, seq_len, RWKV_N), lambda d, h, b: (h, row_blk0 + b, 0))
    dhd_spec = pl.BlockSpec((1, hp, seq_len, RWKV_N), lambda d, h, b: (d, h, row_blk0 + b, 0))
    st_spec = pl.BlockSpec((1, 1, hp, RWKV_N, RWKV_N), lambda d, h, b: (b, d, h, 0, 0))
    return pl.pallas_call(
        functools.partial(_rwkv_scan_kernel, seq_len=seq_len, group=group, heads=hp),
        grid=(2, RWKV_HEADS // hp, n_seq),
        in_specs=[hd_spec, hd_spec, hd_spec, dhd_spec, dhd_spec, dhd_spec, dhd_spec, st_spec],
        out_specs=[pl.BlockSpec((1, hp, seq_len, RWKV_N), lambda d, h, b: (d, h, b, 0)), st_spec],
        out_shape=[jax.ShapeDtypeStruct((2, RWKV_HEADS, n_seq * seq_len, RWKV_N), F32),
                   jax.ShapeDtypeStruct((n_seq, 2, RWKV_HEADS, RWKV_N, RWKV_N), F32)],
        compiler_params=_cparams("parallel", "parallel", "parallel"),
        name="rwkv_scan",
    )(r, kap, v, lw, cs, kd, beta, h0)


def _merge_kernel(oscan_c_ref, oscan_l_ref, ona_c_ref, ona_l_ref, ort_c_ref, ort_l_ref, bonus_ref, grw_ref,
                  ga_ref, gb_ref, gc_ref, x_ref, gt_ref, gnw_ref, gnb_ref, wbr_ref, wout_ref, lng_ref, lnb_ref,
                  o_ref, orw_ref, *, ctx_tiles):
    is_ctx = pl.program_id(0) < ctx_tiles
    pick = lambda c_val, l_val: jnp.where(is_ctx, c_val, l_val)
    for h in range(RWKV_HEADS):
        sl = slice(h * RWKV_N, (h + 1) * RWKV_N)
        o = pick(oscan_c_ref[0, h] + oscan_c_ref[1, h], oscan_l_ref[0, h] + oscan_l_ref[1, h])
        y = _ln(o, RWKV_GN_EPS) * gnw_ref[h] + gnb_ref[h]
        orw_ref[:, sl] = (y + bonus_ref[:, sl]) * grw_ref[:, sl]
    merged = (_sigmoid(ga_ref[...]) * _dot(pick(ona_c_ref[...], ona_l_ref[...]), wbr_ref[0])
              + _sigmoid(gb_ref[...]) * _dot(pick(ort_c_ref[...], ort_l_ref[...]), wbr_ref[1])
              + _sigmoid(gc_ref[...]) * _dot(orw_ref[...], wbr_ref[2]))
    t = _dot(merged, wout_ref[...])
    o_ref[...] = _ln(DEEPNORM_ALPHA * x_ref[...] + gt_ref[0] * t) * lng_ref[...] + lnb_ref[...]


def _merge(oscan, o_na, o_rt, bonus, g_rw, z, x, mod3, p, row_of_tile, tm):
    m = x.shape[0]
    bw = RWKV_WIDTH
    ctx_tiles = o_na[0].shape[0] // tm
    full = lambda a: pl.BlockSpec(a.shape, lambda i: (0,) * a.ndim)
    tok = pl.BlockSpec((tm, bw), lambda i: (i, 0))
    c_row = lambda i: jnp.minimum(i, ctx_tiles - 1)
    l_row = lambda i: jnp.maximum(i - ctx_tiles, 0)
    tok_c = pl.BlockSpec((tm, bw), lambda i: (c_row(i), 0))
    tok_l = pl.BlockSpec((tm, bw), lambda i: (l_row(i), 0))
    params = [p['rwkv_gn_w'], p['rwkv_gn_b'], p['w_br'], p['w_out'], p['ln_a_g'], p['ln_a_b']]
    return pl.pallas_call(
        functools.partial(_merge_kernel, ctx_tiles=ctx_tiles),
        grid=(m // tm,),
        in_specs=[pl.BlockSpec((2, RWKV_HEADS, tm, RWKV_N), lambda i: (0, 0, c_row(i), 0)),
                  pl.BlockSpec((2, RWKV_HEADS, tm, RWKV_N), lambda i: (0, 0, l_row(i), 0)),
                  tok_c, tok_l, tok_c, tok_l, tok, tok,
                  pl.BlockSpec((tm, D_MODEL), lambda i: (i, 0)),
                  pl.BlockSpec((tm, D_MODEL), lambda i: (i, 1)),
                  pl.BlockSpec((tm, D_MODEL), lambda i: (i, 2)),
                  pl.BlockSpec((tm, D_MODEL), lambda i: (i, 0)),
                  pl.BlockSpec((1, 1, D_MODEL), lambda i: (row_of_tile(i, tm), 0, 2))]
                 + [full(a) for a in params],
        out_specs=pl.BlockSpec((tm, D_MODEL), lambda i: (i, 0)),
        out_shape=jax.ShapeDtypeStruct((m, D_MODEL), F32),
        scratch_shapes=[pltpu.VMEM((tm, bw), F32)],
        compiler_params=_cparams("parallel"),
        name="merge_out_proj",
    )(*oscan, *o_na, *o_rt, bonus, g_rw, z, z, z, x, mod3, *params)


def _top_values(s, k, with_rank=False):
    out = []
    rank = jnp.full(s.shape, float(k), F32)
    for b in range(k):
        m = jnp.max(s, axis=0, keepdims=True)
        out.append(m)
        hit = s == m
        if with_rank:
            rank = jnp.where(hit, float(b), rank)
        s = jnp.where(hit, -jnp.inf, s)
    return (out, rank) if with_rank else out


def _peer_gate_kernel(x_ref, sh_ref, sc_ref, wq_ref, keys_ref, hb_ref, cnt_ref, rank1_ref, e0_ref, e1_ref):
    h = _ln(x_ref[...]) * (1.0 + sc_ref[0]) + sh_ref[0]
    hb_ref[...] = h.astype(BF16)
    ht = h.T.astype(BF16)
    qt = jnp.dot(wq_ref[...], ht, preferred_element_type=F32)
    half = keys_ref.shape[-1]
    for hd in range(PEER_HEADS):
        scores, tops = [], []
        for part in range(2):
            row0 = (hd * 2 + part) * half
            s = _dot(keys_ref[hd, part], qt[row0:row0 + half, :])
            scores.append(s)
            if part == 0:
                tops.append(_top_values(s, PEER_TOPK))
            else:
                top1, rank1 = _top_values(s, PEER_TOPK, with_rank=True)
                tops.append(top1)
        pairs = [tops[0][a] + tops[1][b] for a in range(PEER_TOPK) for b in range(PEER_TOPK // (a + 1))]
        pairs += [jnp.full_like(pairs[0], -jnp.inf)] * (-len(pairs) % 8)
        best = _top_values(jnp.concatenate(pairs, axis=0), PEER_TOPK)
        zsum = jnp.exp(best[0] - best[0])
        for t in range(1, PEER_TOPK):
            zsum = zsum + jnp.exp(best[t] - best[0])
        tau = best[PEER_TOPK - 1]
        cnt = jnp.zeros(scores[0].shape, F32)
        for b, t1 in enumerate(tops[1]):
            cnt = jnp.where(scores[0] + t1 >= tau, float(b + 1), cnt)
        cnt_ref[hd] = cnt
        rank1_ref[hd] = rank1.astype(BF16)
        e0_ref[hd] = jnp.exp(scores[0] - tops[0][0])
        e1_ref[hd] = (jnp.exp(scores[1] - tops[1][0]) / zsum).astype(BF16)


def _peer_gate(x, mod3, wq_t, keys, row_of_tile, tn):
    m = x.shape[0]
    full = lambda a: pl.BlockSpec(a.shape, lambda i: (0,) * a.ndim)
    sk = pl.BlockSpec((PEER_HEADS, PEER_NKEYS, tn), lambda i: (0, 0, i))
    sk_f32 = jax.ShapeDtypeStruct((PEER_HEADS, PEER_NKEYS, m), F32)
    sk_bf16 = jax.ShapeDtypeStruct((PEER_HEADS, PEER_NKEYS, m), BF16)
    return pl.pallas_call(
        _peer_gate_kernel,
        grid=(m // tn,),
        in_specs=[pl.BlockSpec((tn, D_MODEL), lambda i: (i, 0)),
                  pl.BlockSpec((1, 1, D_MODEL), lambda i: (row_of_tile(i, tn), 0, 3)),
                  pl.BlockSpec((1, 1, D_MODEL), lambda i: (row_of_tile(i, tn), 0, 4)),
                  full(wq_t), full(keys)],
        out_specs=[pl.BlockSpec((tn, D_MODEL), lambda i: (i, 0)), sk, sk, sk, sk],
        out_shape=[jax.ShapeDtypeStruct((m, D_MODEL), BF16), sk_f32, sk_bf16, sk_f32, sk_bf16],
        compiler_params=_cparams("parallel"),
        name="peer_gate",
    )(x, mod3, mod3, wq_t, keys)


def _gelu_exact(x):
    return 0.5 * x * (1.0 + lax.erf(x * (2.0 ** -0.5)))


PEER_SUB_ROWS = 32
PEER_SUB_LANES = 256
PEER_GROUP_KEYS = 8
PEER_GROUP = PEER_GROUP_KEYS * PEER_NKEYS


def _peer_expert_kernel(hb_ref, cnt_ref, rank1_ref, e0_ref, e1_ref, u_ref, vt_ref, x_ref, gt_ref,
                        lng_ref, lnb_ref, o_ref, acc_ref, st_ref, st2_ref, act_ref, *, groups):
    e = pl.program_id(1)
    tn = hb_ref.shape[0]

    @pl.when(e == 0)
    def _():
        acc_ref[...] = jnp.zeros_like(acc_ref)

    def pre_activations(g, dst_ref):
        rows = pl.ds(pl.multiple_of(g * PEER_GROUP, PEER_GROUP), PEER_GROUP)
        dst_ref[...] = lax.dot_general(u_ref[rows, :], hb_ref[...], (_NT, ((), ())), preferred_element_type=F32)

    def gated_activations(g, src_ref):
        row0 = pl.multiple_of(g * PEER_GROUP, PEER_GROUP)
        for r in range(PEER_GROUP_KEYS):
            i = (e * groups + g) * PEER_GROUP_KEYS + r
            cnt_row = [cnt_ref[hd, pl.ds(i, 1), :] for hd in range(PEER_HEADS)]
            e0_row = [e0_ref[hd, pl.ds(i, 1), :] for hd in range(PEER_HEADS)]
            for lb in range(tn // PEER_SUB_LANES):
                ls = slice(lb * PEER_SUB_LANES, (lb + 1) * PEER_SUB_LANES)
                sub = (PEER_SUB_ROWS, PEER_SUB_LANES)
                cnt = [jnp.broadcast_to(c[:, ls].astype(BF16), sub) for c in cnt_row]
                e0 = [jnp.broadcast_to(c[:, ls].astype(BF16), sub) for c in e0_row]
                zero = jnp.zeros(sub, BF16)
                for rb in range(PEER_NKEYS // PEER_SUB_ROWS):
                    js = slice(rb * PEER_SUB_ROWS, (rb + 1) * PEER_SUB_ROWS)
                    gate = None
                    for hd in range(PEER_HEADS):
                        term = jnp.where(rank1_ref[hd, js, ls] < cnt[hd], e1_ref[hd, js, ls], zero) * e0[hd]
                        gate = term if gate is None else gate + term
                    off = r * PEER_NKEYS + rb * PEER_SUB_ROWS
                    act = _gelu_exact(src_ref[off:off + PEER_SUB_ROWS, ls]).astype(BF16) * gate
                    act_ref[pl.ds(row0 + off, PEER_SUB_ROWS), ls] = act

    scratch = (st_ref, st2_ref)
    pre_activations(0, scratch[0])
    for g in range(groups):
        if g + 1 < groups:
            pre_activations(g + 1, scratch[(g + 1) % 2])
        gated_activations(g, scratch[g % 2])
        cols = slice(g * PEER_GROUP, (g + 1) * PEER_GROUP)
        acc_ref[...] += jnp.dot(vt_ref[:, cols], act_ref[cols, :], preferred_element_type=F32)

    @pl.when(e == pl.num_programs(1) - 1)
    def _():
        f = acc_ref[...].T
        o_ref[...] = _ln(DEEPNORM_ALPHA * x_ref[...] + gt_ref[0] * f) * lng_ref[...] + lnb_ref[...]


def _peer_expert(hb, cnt, rank1, e0, e1, u, v_t, x, mod3, ln_g, ln_b, row_of_tile, tn, eb):
    m = x.shape[0]
    n_exp = u.shape[0]
    groups = eb // PEER_GROUP
    sk = pl.BlockSpec((PEER_HEADS, PEER_NKEYS, tn), lambda i, e: (0, 0, i))
    full = lambda a: pl.BlockSpec(a.shape, lambda i, e: (0,) * a.ndim)
    return pl.pallas_call(
        functools.partial(_peer_expert_kernel, groups=groups),
        grid=(m // tn, n_exp // eb),
        in_specs=[pl.BlockSpec((tn, D_MODEL), lambda i, e: (i, 0)), sk, sk, sk, sk,
                  pl.BlockSpec((eb, D_MODEL), lambda i, e: (e, 0)),
                  pl.BlockSpec((D_MODEL, eb), lambda i, e: (0, e)),
                  pl.BlockSpec((tn, D_MODEL), lambda i, e: (i, 0)),
                  pl.BlockSpec((1, 1, D_MODEL), lambda i, e: (row_of_tile(i, tn), 0, 5)),
                  full(ln_g), full(ln_b)],
        out_specs=pl.BlockSpec((tn, D_MODEL), lambda i, e: (i, 0)),
        out_shape=jax.ShapeDtypeStruct((m, D_MODEL), F32),
        scratch_shapes=[pltpu.VMEM((D_MODEL, tn), F32), pltpu.VMEM((PEER_GROUP, tn), F32),
                        pltpu.VMEM((PEER_GROUP, tn), F32), pltpu.VMEM((eb, tn), BF16)],
        compiler_params=_cparams("parallel", "arbitrary"),
        name="peer_experts",
    )(hb, cnt, rank1, e0, e1, u, v_t, x, mod3, ln_g, ln_b)


def _permute_in_proj(w_in):
    na, qk, rv = 3 * 512, 2 * 256, 2 * 512
    o_na, o_qk, o_rv, o_rw, o_g = 0, na, na + qk, na + qk + rv, na + qk + rv + 1920
    return jnp.concatenate([w_in[:, o_g:], w_in[:, o_na:o_qk], w_in[:, o_qk:o_rv], w_in[:, o_rv:o_rw],
                            w_in[:, o_rw:o_g]], axis=1)


def kernel(x_prompt, x_sample, cache_na_k, cache_na_v, state_ret, state_rwkv, c, c_ctx, w_mod, b_mod, w_in, na_rpb, ret_decay_logit, ret_gn_w, ret_gn_b, rwkv_mu, rwkv_w0, rwkv_w_up, rwkv_a0, rwkv_a_up, rwkv_g_up, rwkv_k_k, rwkv_k_a, rwkv_r_k, rwkv_gn_w, rwkv_gn_b, w_br, w_out, ln_a_g, ln_a_b, ln_f_g, ln_f_b, peer_wq, peer_keys, peer_u, peer_v):
    n_ctx, ctx_len, d = x_prompt.shape
    n_lat, lat_len, _ = x_sample.shape
    m_ctx = n_ctx * ctx_len
    assert m_ctx % lat_len == 0 and n_lat == 2
    lat_blk0 = m_ctx // lat_len

    def row_of_tile(i, tm):
        return jnp.where(i < m_ctx // tm, 0, 1 + (i - m_ctx // tm) // (lat_len // tm))

    prep_tt = 256
    assert ctx_len == prep_tt

    def tiles_per_seq_of_tile(i):
        lat = i >= m_ctx // prep_tt
        per = jnp.where(lat, lat_len // prep_tt, 1)
        pos = jnp.where(lat, (i - m_ctx // prep_tt) % (lat_len // prep_tt), 0)
        return pos, per

    x = jnp.concatenate([x_prompt.reshape(m_ctx, d), x_sample.reshape(n_lat * lat_len, d)], axis=0)
    c8 = jnp.concatenate([c_ctx[None], c, jnp.zeros((8 - 1 - n_lat, d), F32)], axis=0)
    mod = _modulation(c8, w_mod, b_mod)
    rope_tabs = _rope_tables(lat_len)
    zeros_ret = jnp.zeros((n_ctx, 2, RET_HEADS, RET_DK, RET_DV), F32)
    zeros_rw = jnp.zeros((n_ctx, 2, RWKV_HEADS, RWKV_N, RWKV_N), F32)

    nk, nv, sr, sw = [], [], [], []
    for l in range(DEPTH):
        mod3 = mod[l].reshape(8, 1, 6 * d)
        w_in_p = _permute_in_proj(w_in[l]).astype(BF16)
        z = _lnmod_matmul(x, mod3, w_in_p, row_of_tile, 1024, 1152, 0, 1)

        o_na_c, k_cache, v_cache = _ctx_attention(z, n_ctx, ctx_len)
        o_na_l = _na_latent(z, cache_na_k[:, l], cache_na_v[:, l], _na_bias_table(na_rpb[l], lat_len // GRID_W),
                            n_lat, lat_len, lat_blk0)
        nk.append(k_cache)
        nv.append(v_cache)

        tables = _ret_decay_tables(ret_decay_logit[l])
        gn_w, gn_b = ret_gn_w[l][None], ret_gn_b[l][None]
        o_rt_c, s_ret = _retention(z, zeros_ret, tables, gn_w, gn_b, n_ctx, ctx_len, 0, None)
        o_rt_l, _ = _retention(z, state_ret[:, l], tables, gn_w, gn_b, n_lat, lat_len, lat_blk0, rope_tabs)
        sr.append(s_ret)

        mu = rwkv_mu[l]
        prm = {
            'mu_main': mu[None, :1536], 'mu_lora': mu[None, 1536:],
            'rwkv_w0': rwkv_w0[l][:, None], 'rwkv_w_up': rwkv_w_up[l].astype(BF16),
            'rwkv_a0': rwkv_a0[l][:, None], 'rwkv_a_up': rwkv_a_up[l].astype(BF16),
            'rwkv_g_up': rwkv_g_up[l].astype(BF16), 'rwkv_k_k': rwkv_k_k[l][None],
            'rwkv_k_a': rwkv_k_a[l][None], 'rwkv_r_k': rwkv_r_k[l].reshape(1, RWKV_WIDTH),
        }
        r, kap, v, lw, cs, kd, beta, g_rw, bonus = _rwkv_prep(z, prm, tiles_per_seq_of_tile, prep_tt)
        o_c, h_fin = _rwkv_scan(r, kap, v, lw, cs, kd, beta, zeros_rw, n_ctx, ctx_len, 0)
        h0_lat = jnp.swapaxes(state_rwkv[:, l], -1, -2)
        o_l, _ = _rwkv_scan(r, kap, v, lw, cs, kd, beta, h0_lat, n_lat, lat_len, lat_blk0)
        sw.append(jnp.swapaxes(h_fin, -1, -2))

        mp = {
            'rwkv_gn_w': rwkv_gn_w[l].reshape(RWKV_HEADS, 1, RWKV_N),
            'rwkv_gn_b': rwkv_gn_b[l].reshape(RWKV_HEADS, 1, RWKV_N),
            'w_br': w_br[l].astype(BF16), 'w_out': w_out[l].astype(BF16),
            'ln_a_g': ln_a_g[l][None], 'ln_a_b': ln_a_b[l][None],
        }
        x = _merge((o_c, o_l), (o_na_c, o_na_l), (o_rt_c, o_rt_l), bonus, g_rw, z, x, mod3, mp, row_of_tile, 256)

        wq_t = peer_wq[l].T.astype(BF16)
        hb, cnt, rank1, e0, e1 = _peer_gate(x, mod3, wq_t, peer_keys[l].astype(BF16), row_of_tile, 256)
        x = _peer_expert(hb, cnt, rank1, e0, e1, peer_u[l].astype(BF16), peer_v[l].T.astype(BF16), x, mod3,
                         ln_f_g[l][None], ln_f_b[l][None], row_of_tile, 512, 2048)

    dt = x_prompt.dtype
    y_prompt = x[:m_ctx].reshape(n_ctx, ctx_len, d)
    y_sample = x[m_ctx:].reshape(n_lat, lat_len, d)
    return (y_prompt, y_sample, jnp.stack(nk, axis=1).astype(dt), jnp.stack(nv, axis=1).astype(dt),
            jnp.stack(sr, axis=1).astype(dt), jnp.stack(sw, axis=1).astype(dt))
```

```python
import functools
import math

import jax
import jax.numpy as jnp
import numpy as np
from jax import lax
from jax.experimental import pallas as pl
from jax.experimental.pallas import tpu as pltpu

F32 = jnp.float32
BF16 = jnp.bfloat16

D_MODEL = 1024
DEPTH = 2
GRID_W = 64
NA_HEADS = 8
NA_HEAD_DIM = 64
WIN_R = 8
WIN_C = 16
NEG_INF = -1e30
RET_HEADS = 4
RET_DK = 64
RET_DV = 128
RET_CHUNK = 64
RET_GN_EPS = 1e-5
ROPE_BASE = 10000.0
RWKV_HEADS = 8
RWKV_N = 64
RWKV_WIDTH = 512
RWKV_DECAY_SCALE = 0.606531
RWKV_GN_EPS = 64e-5
RWKV_CHUNK = 64
PEER_HEADS = 8
PEER_NKEYS = 128
PEER_TOPK = 16
LN_EPS = 1e-5
DEEPNORM_ALPHA = (2 * DEPTH) ** 0.25

OFF_GATE = 0
OFF_NA = 3072
OFF_RET_QK = 4608
OFF_RET_V = 5120
OFF_RW_MAIN = 6144
OFF_RW_LORA = 7680
P_IN = 8064
P_IN_PADDED = 8192

VMEM_LIMIT_BYTES = 56 * 1024 * 1024


def _cparams(*sem):
    return pltpu.CompilerParams(dimension_semantics=sem, vmem_limit_bytes=VMEM_LIMIT_BYTES)


def _ln(x, eps=LN_EPS):
    mu = jnp.mean(x, axis=-1, keepdims=True)
    xc = x - mu
    var = jnp.mean(xc * xc, axis=-1, keepdims=True)
    return xc * lax.rsqrt(var + eps)


def _sigmoid(x):
    return 1.0 / (1.0 + jnp.exp(-x))


def _dot(a, b):
    return jnp.dot(a.astype(BF16), b.astype(BF16), preferred_element_type=F32)


def _dot_nt(a, b):
    return lax.dot_general(a.astype(BF16), b.astype(BF16), (((1,), (1,)), ((), ())),
                           preferred_element_type=F32)


def _dot_tn(a, b):
    return lax.dot_general(a.astype(BF16), b.astype(BF16), (((0,), (0,)), ((), ())),
                           preferred_element_type=F32)


def _lockstep(gens):
    out = [None] * len(gens)
    live = list(range(len(gens)))
    while live:
        for i in list(live):
            try:
                next(gens[i])
            except StopIteration as done:
                out[i] = done.value
                live.remove(i)
    return out


def _split3(x):
    hi = x.astype(BF16)
    r1 = x - hi.astype(F32)
    mid = r1.astype(BF16)
    lo = (r1 - mid.astype(F32)).astype(BF16)
    return hi, mid, lo


def _dot_exact_lhs(sel, x):
    hi, mid, lo = _split3(x)
    s = sel.astype(BF16)
    out = jnp.dot(s, lo, preferred_element_type=F32)
    out = out + jnp.dot(s, mid, preferred_element_type=F32)
    return out + jnp.dot(s, hi, preferred_element_type=F32)


def _dot_exact_rhs(x, sel):
    hi, mid, lo = _split3(x)
    s = sel.astype(BF16)
    out = jnp.dot(lo, s, preferred_element_type=F32)
    out = out + jnp.dot(mid, s, preferred_element_type=F32)
    return out + jnp.dot(hi, s, preferred_element_type=F32)


def _mod_kernel(c_ref, w_ref, b_ref, o_ref):
    c = c_ref[...]
    s = c * _sigmoid(c)
    o_ref[0] = _dot(s, w_ref[0]) + b_ref[0]


def _modulation(c8, w_mod, b_mod):
    tn = 1536
    n = w_mod.shape[-1]
    return pl.pallas_call(
        _mod_kernel,
        grid=(DEPTH, n // tn),
        in_specs=[pl.BlockSpec((8, D_MODEL), lambda l, j: (0, 0)),
                  pl.BlockSpec((1, D_MODEL, tn), lambda l, j: (l, 0, j)),
                  pl.BlockSpec((1, 1, tn), lambda l, j: (l, 0, j))],
        out_specs=pl.BlockSpec((1, 8, tn), lambda l, j: (l, 0, j)),
        out_shape=jax.ShapeDtypeStruct((DEPTH, 8, n), F32),
        compiler_params=_cparams("parallel", "parallel"),
        name="modulation",
    )(c8, w_mod, b_mod.reshape(DEPTH, 1, n))


def _lnmod_matmul_kernel(x_ref, sh_ref, sc_ref, w_ref, o_ref, h_ref):
    @pl.when(pl.program_id(1) == 0)
    def _():
        h = _ln(x_ref[...]) * (1.0 + sc_ref[0]) + sh_ref[0]
        h_ref[...] = h.astype(BF16)

    o_ref[...] = jnp.dot(h_ref[...], w_ref[...], preferred_element_type=F32)


def _lnmod_matmul(x, mod3, w, row_of_tile, tm, tn, sh_blk, sc_blk):
    m = x.shape[0]
    n = w.shape[1]
    return pl.pallas_call(
        _lnmod_matmul_kernel,
        grid=(m // tm, n // tn),
        in_specs=[pl.BlockSpec((tm, D_MODEL), lambda i, j: (i, 0)),
                  pl.BlockSpec((1, 1, D_MODEL), lambda i, j: (row_of_tile(i, tm), 0, sh_blk)),
                  pl.BlockSpec((1, 1, D_MODEL), lambda i, j: (row_of_tile(i, tm), 0, sc_blk)),
                  pl.BlockSpec((D_MODEL, tn), lambda i, j: (0, j))],
        out_specs=pl.BlockSpec((tm, tn), lambda i, j: (i, j)),
        out_shape=jax.ShapeDtypeStruct((m, n), F32),
        scratch_shapes=[pltpu.VMEM((tm, D_MODEL), BF16)],
        compiler_params=_cparams("parallel", "arbitrary"),
        name="adaln_in_proj",
    )(x, mod3, mod3, w)


def _ctx_attn_kernel(q_ref, k_ref, v_ref, o_ref, kc_ref, vc_ref):
    scale = NA_HEAD_DIM ** -0.5
    half = q_ref.shape[0] // 2

    def head(hh, q0):
        sl = slice(hh * NA_HEAD_DIM, (hh + 1) * NA_HEAD_DIM)
        k = k_ref[:, sl]
        v = v_ref[:, sl]
        if q0 == 0:
            kc_ref[0, hh] = k
            vc_ref[0, hh] = v
        s = _dot_nt(q_ref[q0:q0 + half, sl], k) * scale
        yield
        m = jnp.max(s, axis=-1, keepdims=True)
        p = jnp.exp(s - m)
        l = jnp.sum(p, axis=-1, keepdims=True)
        yield
        o_ref[q0:q0 + half, sl] = _dot(p, v) / l

    _lockstep([head(hh, q0) for hh in range(2) for q0 in (0, half)])


def _ctx_attention(z, n_seq, seq_len):
    cb = OFF_NA // 128
    cache_spec = pl.BlockSpec((1, 2, seq_len, NA_HEAD_DIM), lambda b, h: (b, h, 0, 0))
    cache_shape = jax.ShapeDtypeStruct((n_seq, NA_HEADS, seq_len, NA_HEAD_DIM), F32)
    return pl.pallas_call(
        _ctx_attn_kernel,
        grid=(n_seq, NA_HEADS // 2),
        in_specs=[pl.BlockSpec((seq_len, 128), lambda b, h: (b, cb + h)),
                  pl.BlockSpec((seq_len, 128), lambda b, h: (b, cb + 4 + h)),
                  pl.BlockSpec((seq_len, 128), lambda b, h: (b, cb + 8 + h))],
        out_specs=[pl.BlockSpec((seq_len, 128), lambda b, h: (b, h)), cache_spec, cache_spec],
        out_shape=[jax.ShapeDtypeStruct((n_seq * seq_len, NA_HEADS * NA_HEAD_DIM), F32), cache_shape, cache_shape],
        compiler_params=_cparams("parallel", "parallel"),
        name="ctx_attention",
    )(z, z, z)


def _na_bias_kernel(rpb_ref, rowsel_ref, colsel_ref, neg_ref, o_ref):
    picked = _dot_exact_lhs(rowsel_ref[...], rpb_ref[0])
    o_ref[0] = _dot_exact_rhs(picked, colsel_ref[...]) + neg_ref[...]


def _na_bias_table(rpb, rows):
    kr = min(WIN_R, rows)
    nh, n_dr, n_dc = rpb.shape
    rep = np.array([0, 1, 2, 3, 4, rows - 3, rows - 2, rows - 1])
    start = np.clip(rep - kr // 2, 0, rows - kr)
    dr = start[:, None] - rep[:, None] + WIN_R - 1 + np.arange(kr)[None, :]
    qcol = np.arange(GRID_W)
    kcol = np.arange(GRID_W)
    q_start = np.clip(qcol - WIN_C // 2, 0, GRID_W - WIN_C)
    rel = kcol[None, :] - q_start[:, None]
    valid = (rel >= 0) & (rel < WIN_C)
    dc = np.clip(kcol[None, :] - qcol[:, None], -(WIN_C - 1), WIN_C - 1) + WIN_C - 1
    row_sel = np.zeros((8 * kr, 16), np.float32)
    row_sel[np.arange(8 * kr), dr.reshape(-1)] = 1.0
    col_sel = np.zeros((32, GRID_W * GRID_W), np.float32)
    qq, kk = np.nonzero(valid)
    col_sel[dc[qq, kk], qq * GRID_W + kk] = 1.0
    neg = np.where(valid, 0.0, NEG_INF).astype(np.float32).reshape(1, -1)
    rpb_p = jnp.pad(rpb.astype(F32), ((0, 0), (0, 16 - n_dr), (0, 32 - n_dc)))
    full = lambda a: pl.BlockSpec(a.shape, lambda h: (0,) * a.ndim)
    consts = [jnp.asarray(row_sel, BF16), jnp.asarray(col_sel, BF16), jnp.asarray(neg)]
    b = pl.pallas_call(
        _na_bias_kernel,
        grid=(nh,),
        in_specs=[pl.BlockSpec((1, 16, 32), lambda h: (h, 0, 0))] + [full(a) for a in consts],
        out_specs=pl.BlockSpec((1, 8 * kr, GRID_W * GRID_W), lambda h: (h, 0, 0)),
        out_shape=jax.ShapeDtypeStruct((nh, 8 * kr, GRID_W * GRID_W), F32),
        compiler_params=_cparams("parallel"),
        name="na_bias",
    )(rpb_p, *consts)
    b = b.reshape(nh, 8, kr, GRID_W, GRID_W)
    return b.transpose(0, 1, 3, 2, 4).reshape(nh, 8, GRID_W, kr * GRID_W)


NA_ROWS_PER_STEP = 4


def _na_latent_kernel(q_ref, k_ref, v_ref, ck_ref, cv_ref, b_ref, o_ref, *, rows):
    scale = NA_HEAD_DIM ** -0.5
    kr = min(WIN_R, rows)

    def query_row(r, hh):
        start = jnp.clip(r - kr // 2, 0, rows - kr)
        typ = jnp.where(r < 4, r, jnp.where(r > rows - 4, r - (rows - 8), 4))
        q0 = pl.multiple_of(r * GRID_W, GRID_W)
        k0 = pl.multiple_of(start * GRID_W, GRID_W)
        sl = slice(hh * NA_HEAD_DIM, (hh + 1) * NA_HEAD_DIM)
        q = q_ref[pl.ds(q0, GRID_W), sl]
        kw = k_ref[pl.ds(k0, kr * GRID_W), sl]
        vw = v_ref[pl.ds(k0, kr * GRID_W), sl]
        s_w = _dot_nt(q, kw) * scale + b_ref[hh, typ]
        s_c = _dot_nt(q, ck_ref[0, hh]) * scale
        yield
        m = jnp.maximum(jnp.max(s_w, axis=-1, keepdims=True), jnp.max(s_c, axis=-1, keepdims=True))
        p_w = jnp.exp(s_w - m)
        p_c = jnp.exp(s_c - m)
        l = jnp.sum(p_w, axis=-1, keepdims=True) + jnp.sum(p_c, axis=-1, keepdims=True)
        yield
        o_ref[pl.ds(q0, GRID_W), sl] = (_dot(p_w, vw) + _dot(p_c, cv_ref[0, hh])) / l

    def body(rp, carry):
        _lockstep([query_row(rp * NA_ROWS_PER_STEP + u, hh) for u in range(NA_ROWS_PER_STEP) for hh in range(2)])
        return carry

    lax.fori_loop(0, rows // NA_ROWS_PER_STEP, body, 0)


def _na_latent(z, ctx_k, ctx_v, bias, n_seq, seq_len, row_blk0):
    cb = OFF_NA // 128
    rows = seq_len // GRID_W
    past = ctx_k.shape[2]
    return pl.pallas_call(
        functools.partial(_na_latent_kernel, rows=rows),
        grid=(n_seq, NA_HEADS // 2),
        in_specs=[pl.BlockSpec((seq_len, 128), lambda b, h: (row_blk0 + b, cb + h)),
                  pl.BlockSpec((seq_len, 128), lambda b, h: (row_blk0 + b, cb + 4 + h)),
                  pl.BlockSpec((seq_len, 128), lambda b, h: (row_blk0 + b, cb + 8 + h)),
                  pl.BlockSpec((1, 2, past, NA_HEAD_DIM), lambda b, h: (b, h, 0, 0)),
                  pl.BlockSpec((1, 2, past, NA_HEAD_DIM), lambda b, h: (b, h, 0, 0)),
                  pl.BlockSpec((2, 8, GRID_W, bias.shape[-1]), lambda b, h: (h, 0, 0, 0))],
        out_specs=pl.BlockSpec((seq_len, 128), lambda b, h: (b, h)),
        out_shape=jax.ShapeDtypeStruct((n_seq * seq_len, NA_HEADS * NA_HEAD_DIM), F32),
        compiler_params=_cparams("parallel", "parallel"),
        name="na_latent",
    )(z, z, z, ctx_k, ctx_v, bias)


def _ret_decay_tables(decay_logit):
    c = RET_CHUNK
    lg = jnp.log(jax.nn.sigmoid(decay_logit.astype(F32)))[:, :, None, None]
    i = jnp.arange(c, dtype=F32)
    diff = i[:, None] - i[None, :]
    m_f = jnp.where(diff >= 0, jnp.exp(jnp.maximum(diff, 0.0) * lg[0]), 0.0)
    m_b = jnp.where(diff <= 0, jnp.exp(jnp.maximum(-diff, 0.0) * lg[1]), 0.0)
    mask = jnp.stack([m_f, m_b])
    ones = jnp.ones((1, 1, 1, RET_DK), F32)
    q_f = jnp.exp((i + 1.0)[None, :, None] * lg[0]) * ones[0]
    q_b = jnp.exp((c - i)[None, :, None] * lg[1]) * ones[0]
    k_f = jnp.exp((c - 1.0 - i)[None, :, None] * lg[0]) * ones[0]
    k_b = jnp.exp(i[None, :, None] * lg[1]) * ones[0]
    c_dec = jnp.exp(c * lg) * jnp.ones((1, 1, RET_DK, RET_DV), F32)
    return mask, jnp.stack([q_f, q_b]), jnp.stack([k_f, k_b]), c_dec


def _rope_tables(seq_len):
    quarter = RET_DK // 4
    pos = np.arange(seq_len)
    freqs = 1.0 / (ROPE_BASE ** (jnp.arange(quarter, dtype=F32) / quarter))
    a_row = jnp.asarray(pos // GRID_W, F32)[:, None] * freqs[None, :]
    a_col = jnp.asarray(pos % GRID_W, F32)[:, None] * freqs[None, :]
    cos = jnp.concatenate([jnp.cos(a_row)] * 2 + [jnp.cos(a_col)] * 2, axis=-1)
    sin = jnp.concatenate([-jnp.sin(a_row), jnp.sin(a_row), -jnp.sin(a_col), jnp.sin(a_col)], axis=-1)
    return jnp.tile(cos, (1, RET_HEADS)), jnp.tile(sin, (1, RET_HEADS))


def _retention_kernel(*refs, seq_len, rope):
    if rope:
        (q_ref, k_ref, v_ref, g_ref, s0_ref, mask_ref, qd_ref, kd_ref, cd_ref, gnw_ref, gnb_ref,
         cos_ref, sin_ref, o_ref, sfin_ref, qs_ref, ks_ref, acc_ref) = refs
    else:
        (q_ref, k_ref, v_ref, g_ref, s0_ref, mask_ref, qd_ref, kd_ref, cd_ref, gnw_ref, gnb_ref,
         o_ref, sfin_ref, qs_ref, ks_ref, acc_ref) = refs
    c = RET_CHUNK
    n = seq_len // c
    quarter = RET_DK // 4
    q = q_ref[...]
    k = k_ref[...] * (RET_DK ** -0.5)
    if rope:
        lane = lax.broadcasted_iota(jnp.int32, q.shape, 1)
        first = (lane % (2 * quarter)) < quarter
        width = q.shape[1]

        def rot(x):
            swapped = jnp.where(first, pltpu.roll(x, width - quarter, 1), pltpu.roll(x, quarter, 1))
            return x * cos_ref[...] + swapped * sin_ref[...]

        q = rot(q)
        k = rot(k)
    qs_ref[...] = q
    ks_ref[...] = k

    sfin_ref[...] = s0_ref[...]

    def chain(h, d, ci):
        ksl = slice(h * RET_DK, (h + 1) * RET_DK)
        vsl = slice(h * RET_DV, (h + 1) * RET_DV)
        cc = ci if d == 0 else n - 1 - ci
        rows = pl.ds(pl.multiple_of(cc * c, c), c)
        qc = qs_ref[rows, ksl]
        kc = ks_ref[rows, ksl]
        vc = v_ref[rows, vsl]
        att = _dot_nt(qc, kc) * mask_ref[d, h]
        s = sfin_ref[0, d, h]
        yield
        acc_ref[d, rows, vsl] = _dot(att, vc) + _dot(qc * qd_ref[d, h], s)
        yield
        sfin_ref[0, d, h] = s * cd_ref[d, h] + _dot_tn(kc * kd_ref[d, h], vc)

    def step(ci, carry):
        _lockstep([chain(h, d, ci) for h in range(RET_HEADS) for d in range(2)])
        return carry

    lax.fori_loop(0, n, step, 0)

    for h in range(RET_HEADS):
        vsl = slice(h * RET_DV, (h + 1) * RET_DV)
        y = _ln(acc_ref[0, :, vsl] + acc_ref[1, :, vsl], RET_GN_EPS) * gnw_ref[:, vsl] + gnb_ref[:, vsl]
        g = g_ref[:, vsl]
        o_ref[:, vsl] = y * (g * _sigmoid(g))


def _retention(z, s0, tables, gn_w, gn_b, n_seq, seq_len, row_blk0, rope_tabs):
    mask, qd, kd, cd = tables
    rope = rope_tabs is not None
    qk_w = RET_HEADS * RET_DK
    v_w = RET_HEADS * RET_DV
    full = lambda a: pl.BlockSpec(a.shape, lambda b: (0,) * a.ndim)
    in_specs = [pl.BlockSpec((seq_len, qk_w), lambda b: (row_blk0 + b, OFF_RET_QK // qk_w)),
                pl.BlockSpec((seq_len, qk_w), lambda b: (row_blk0 + b, OFF_RET_QK // qk_w + 1)),
                pl.BlockSpec((seq_len, v_w), lambda b: (row_blk0 + b, OFF_RET_V // v_w)),
                pl.BlockSpec((seq_len, v_w), lambda b: (row_blk0 + b, OFF_RET_V // v_w + 1)),
                pl.BlockSpec((1, 2, RET_HEADS, RET_DK, RET_DV), lambda b: (b, 0, 0, 0, 0)),
                full(mask), full(qd), full(kd), full(cd), full(gn_w), full(gn_b)]
    args = [z, z, z, z, s0, mask, qd, kd, cd, gn_w, gn_b]
    if rope:
        in_specs += [full(rope_tabs[0]), full(rope_tabs[1])]
        args += list(rope_tabs)
    return pl.pallas_call(
        functools.partial(_retention_kernel, seq_len=seq_len, rope=rope),
        grid=(n_seq,),
        in_specs=in_specs,
        out_specs=[pl.BlockSpec((seq_len, v_w), lambda b: (b, 0)),
                   pl.BlockSpec((1, 2, RET_HEADS, RET_DK, RET_DV), lambda b: (b, 0, 0, 0, 0))],
        out_shape=[jax.ShapeDtypeStruct((n_seq * seq_len, v_w), F32),
                   jax.ShapeDtypeStruct((n_seq, 2, RET_HEADS, RET_DK, RET_DV), F32)],
        scratch_shapes=[pltpu.VMEM((seq_len, qk_w), F32), pltpu.VMEM((seq_len, qk_w), F32),
                        pltpu.VMEM((2, seq_len, v_w), F32)],
        compiler_params=_cparams("parallel"),
        name="retention_rope" if rope else "retention",
    )(*args)


def _rwkv_prep_kernel(zm_ref, zl_ref, pm_ref, nm_ref, pl_ref, nl_ref, mum_ref, mul_ref,
                      w0_ref, wup_ref, a0_ref, aup_ref, gup_ref, kk_ref, ka_ref, rk_ref, ones_ref, tri_ref,
                      r_out, kap_out, v_out, lw_out, cs_out, kd_out, beta_out, g_out, bonus_out,
                      *, tiles_per_seq_of_tile):
    i = pl.program_id(0)
    pos, per = tiles_per_seq_of_tile(i)
    has_prev = (pos != 0).astype(F32)
    has_next = (pos != per - 1).astype(F32)
    tt = zm_ref.shape[0]

    def shift(z_ref, p_ref, n_ref, mu_ref):
        z = z_ref[...]
        row = lax.broadcasted_iota(jnp.int32, z.shape, 0)
        prev = jnp.where(row == 0, p_ref[7:8, :] * has_prev, pltpu.roll(z, 1, 0))
        nxt = jnp.where(row == tt - 1, n_ref[0:1, :] * has_next, pltpu.roll(z, tt - 1, 0))
        return z + mu_ref[...] * (0.5 * (prev + nxt) - z)

    zm = shift(zm_ref, pm_ref, nm_ref, mum_ref)
    zl = shift(zl_ref, pl_ref, nl_ref, mul_ref)
    w = RWKV_WIDTH
    r_c, k_c, v_c = zm[:, 0:w], zm[:, w:2 * w], zm[:, 2 * w:3 * w]
    ones = ones_ref[...]
    g_out[...] = _dot(_sigmoid(zl[:, 256:384]), gup_ref[...])
    kk = k_c * kk_ref[...]
    ss = _dot_exact_rhs(kk * kk, ones)
    kap = kk * lax.rsqrt(jnp.maximum(ss, 1e-24))
    bonus_out[...] = _dot_exact_rhs(r_c * k_c * rk_ref[...], ones) * v_c
    for h in range(RWKV_HEADS):
        sl = slice(h * RWKV_N, (h + 1) * RWKV_N)
        r_out[h] = r_c[:, sl]
        kap_out[h] = kap[:, sl]
        v_out[h] = v_c[:, sl]
    for d in range(2):
        wl = zl[:, d * 64:(d + 1) * 64]
        al = zl[:, 128 + d * 64:128 + (d + 1) * 64]
        lw = -RWKV_DECAY_SCALE * _sigmoid(w0_ref[d] + _dot(jnp.tanh(wl), wup_ref[d]))
        a = _sigmoid(a0_ref[d] + _dot(al, aup_ref[d]))
        k_d = k_c * (1.0 + (a - 1.0) * ka_ref[...])
        beta = kap * a
        cs = _dot_exact_lhs(tri_ref[d], lw)
        for h in range(RWKV_HEADS):
            sl = slice(h * RWKV_N, (h + 1) * RWKV_N)
            lw_out[d, h] = lw[:, sl]
            cs_out[d, h] = cs[:, sl]
            kd_out[d, h] = k_d[:, sl]
            beta_out[d, h] = beta[:, sl]


def _rwkv_prep(z, p, tiles_per_seq_of_tile, tt):
    m = z.shape[0]
    nt = m // tt
    w = RWKV_WIDTH
    mb, lb = OFF_RW_MAIN // (3 * w), OFF_RW_LORA // 384
    hm = tt // 8
    last8 = m // 8 - 1
    prev_idx = lambda i: jnp.maximum(i * hm - 1, 0)
    next_idx = lambda i: jnp.minimum((i + 1) * hm, last8)
    full = lambda a: pl.BlockSpec(a.shape, lambda i: (0,) * a.ndim)
    heads = jnp.arange(w) // RWKV_N
    ones = (heads[:, None] == heads[None, :]).astype(BF16)
    t = np.arange(tt)
    same = (t[:, None] // RWKV_CHUNK) == (t[None, :] // RWKV_CHUNK)
    tri = jnp.asarray(np.stack([same & (t[None, :] <= t[:, None]), same & (t[None, :] >= t[:, None])]), BF16)
    params = [p['mu_main'], p['mu_lora'], p['rwkv_w0'], p['rwkv_w_up'], p['rwkv_a0'], p['rwkv_a_up'],
              p['rwkv_g_up'], p['rwkv_k_k'], p['rwkv_k_a'], p['rwkv_r_k'], ones, tri]
    hd = lambda: jax.ShapeDtypeStruct((RWKV_HEADS, m, RWKV_N), F32)
    dhd = lambda: jax.ShapeDtypeStruct((2, RWKV_HEADS, m, RWKV_N), F32)
    tok = lambda: jax.ShapeDtypeStruct((m, w), F32)
    hd_spec = pl.BlockSpec((RWKV_HEADS, tt, RWKV_N), lambda i: (0, i, 0))
    dhd_spec = pl.BlockSpec((2, RWKV_HEADS, tt, RWKV_N), lambda i: (0, 0, i, 0))
    tok_spec = pl.BlockSpec((tt, w), lambda i: (i, 0))
    return pl.pallas_call(
        functools.partial(_rwkv_prep_kernel, tiles_per_seq_of_tile=tiles_per_seq_of_tile),
        grid=(nt,),
        in_specs=[pl.BlockSpec((tt, 3 * w), lambda i: (i, mb)),
                  pl.BlockSpec((tt, 384), lambda i: (i, lb)),
                  pl.BlockSpec((8, 3 * w), lambda i: (prev_idx(i), mb)),
                  pl.BlockSpec((8, 3 * w), lambda i: (next_idx(i), mb)),
                  pl.BlockSpec((8, 384), lambda i: (prev_idx(i), lb)),
                  pl.BlockSpec((8, 384), lambda i: (next_idx(i), lb))] + [full(a) for a in params],
        out_specs=[hd_spec, hd_spec, hd_spec, dhd_spec, dhd_spec, dhd_spec, dhd_spec, tok_spec, tok_spec],
        out_shape=[hd(), hd(), hd(), dhd(), dhd(), dhd(), dhd(), tok(), tok()],
        compiler_params=_cparams("parallel"),
        name="rwkv_prep",
    )(z, z, z, z, z, z, *params)


def _dot3(a, b, dims):
    a_hi = a.astype(BF16)
    b_hi = b.astype(BF16)
    a_lo = (a - a_hi.astype(F32)).astype(BF16)
    b_lo = (b - b_hi.astype(F32)).astype(BF16)
    dn = (dims, ((), ()))
    out = lax.dot_general(a_lo, b_hi, dn, preferred_element_type=F32)
    out = out + lax.dot_general(a_hi, b_lo, dn, preferred_element_type=F32)
    return out + lax.dot_general(a_hi, b_hi, dn, preferred_element_type=F32)


_NN = ((1,), (0,))
_NT = ((1,), (1,))
_TN = ((0,), (0,))


def _dot1(a, b, dims):
    return lax.dot_general(a.astype(BF16), b.astype(BF16), (dims, ((), ())), preferred_element_type=F32)


def _dot_rhs2(a, b, dims):
    assert dims == _NN
    b_hi = b.astype(BF16)
    b_lo = (b - b_hi.astype(F32)).astype(BF16)
    out = jnp.dot(a.astype(BF16), jnp.concatenate([b_hi, b_lo], axis=1), preferred_element_type=F32)
    n = b.shape[1]
    return out[:, n:] + out[:, :n]


_RWKV_MM = _dot1
RWKV_CHAINS = 16
RWKV_MAX_GROUP = 8


def _rwkv_scan_kernel(r_ref, kap_ref, v_ref, lw_ref, cs_ref, kd_ref, beta_ref, h0_ref, o_ref, hfin_ref, *,
                      seq_len, group, heads):
    c = RWKV_CHUNK
    n = seq_len // c
    d = pl.program_id(0)
    sgn = 1 - 2 * d
    ti = lax.broadcasted_iota(jnp.int32, (c, c), 0) * sgn
    tj = lax.broadcasted_iota(jnp.int32, (c, c), 1) * sgn
    strict = tj < ti
    incl = tj <= ti
    eye_f = (ti == tj).astype(F32)

    def chunk_terms(hh, cc):
        rows = pl.ds(pl.multiple_of(cc * c, c), c)
        r = r_ref[hh, rows, :]
        kap = kap_ref[hh, rows, :]
        v = v_ref[hh, rows, :]
        lw = lw_ref[0, hh, rows, :]
        cs = cs_ref[0, hh, rows, :]
        kd = kd_ref[0, hh, rows, :]
        beta = beta_ref[0, hh, rows, :]
        tot = jnp.sum(lw, axis=0, keepdims=True)
        kap_t = kap * jnp.exp(cs - lw)
        r_t = r * jnp.exp(cs)
        e_ncs = jnp.exp(-cs)
        k_t = kd * e_ncs
        b_t = beta * e_ncs
        e_end = jnp.exp(tot - cs)
        k_end = kd * e_end
        b_end = beta * e_end
        yield
        p = _RWKV_MM(jnp.concatenate([kap_t, r_t], axis=0), jnp.concatenate([b_t, k_t], axis=0), _NT)
        l_b = jnp.where(strict, p[:c, :c], 0.0)
        l_k = jnp.where(strict, p[:c, c:], 0.0)
        m_b = jnp.where(incl, p[c:, :c], 0.0)
        m_k = jnp.where(incl, p[c:, c:], 0.0)
        yield
        x = -l_b
        y = jnp.concatenate([kap_t, _RWKV_MM(l_k, v, _NN)], axis=1)
        yield
        y = y + _dot_rhs2(x, y, _NN)
        pw = x
        for _ in range(5):
            pw = _RWKV_MM(pw, pw, _NN)
            yield
            y = y + _dot_rhs2(pw, y, _NN)
        yield
        mby = _RWKV_MM(m_b, y, _NN)
        bty = _RWKV_MM(b_end, y, _TN)
        rr = r_t - mby[:, :RWKV_N]
        o0 = _RWKV_MM(m_k, v, _NN) - mby[:, RWKV_N:]
        pc = eye_f * jnp.exp(tot) - bty[:, :RWKV_N]
        qc = _RWKV_MM(k_end, v, _TN) - bty[:, RWKV_N:]
        return rows, rr, o0, pc, qc

    def step(gi, hstates):
        keys = [(hh, u) for u in range(group) for hh in range(heads)]
        terms = dict(zip(keys, _lockstep(
            [chunk_terms(hh, (gi * group + u) + d * (n - 1 - 2 * (gi * group + u))) for hh, u in keys])))
        hstates = list(hstates)
        for u in range(group):
            for hh in range(heads):
                rows, rr, o0, pc, qc = terms[(hh, u)]
                o_ref[0, hh, rows, :] = _dot3(rr, hstates[hh], _NN) + o0
                hstates[hh] = _dot3(pc, hstates[hh], _NN) + qc
        return tuple(hstates)

    fin = lax.fori_loop(0, n // group, step, tuple(h0_ref[0, 0, hh] for hh in range(heads)))
    for hh in range(heads):
        hfin_ref[0, 0, hh] = fin[hh]


def _rwkv_scan(r, kap, v, lw, cs, kd, beta, h0, n_seq, seq_len, row_blk0):
    group = min(seq_len // RWKV_CHUNK, RWKV_MAX_GROUP)
    hp = RWKV_CHAINS // group
    hd_spec = pl.BlockSpec((hp, seq_len, RWKV_N), lambda d, h, b: (h, row_blk0 + b, 0))
    dhd_spec = pl.BlockSpec((1, hp, seq_len, RWKV_N), lambda d, h, b: (d, h, row_blk0 + b, 0))
    st_spec = pl.BlockSpec((1, 1, hp, RWKV_N, RWKV_N), lambda d, h, b: (b, d, h, 0, 0))
    return pl.pallas_call(
        functools.partial(_rwkv_scan_kernel, seq_len=seq_len, group=group, heads=hp),
        grid=(2, RWKV_HEADS // hp, n_seq),
        in_specs=[hd_spec, hd_spec, hd_spec, dhd_spec, dhd_spec, dhd_spec, dhd_spec, st_spec],
        out_specs=[pl.BlockSpec((1, hp, seq_len, RWKV_N), lambda d, h, b: (d, h, b, 0)), st_spec],
        out_shape=[jax.ShapeDtypeStruct((2, RWKV_HEADS, n_seq * seq_len, RWKV_N), F32),
                   jax.ShapeDtypeStruct((n_seq, 2, RWKV_HEADS, RWKV_N, RWKV_N), F32)],
        compiler_params=_cparams("parallel", "parallel", "parallel"),
        name="rwkv_scan",
    )(r, kap, v, lw, cs, kd, beta, h0)


def _merge_kernel(oscan_c_ref, oscan_l_ref, ona_c_ref, ona_l_ref, ort_c_ref, ort_l_ref, bonus_ref, grw_ref,
                  ga_ref, gb_ref, gc_ref, x_ref, gt_ref, gnw_ref, gnb_ref, wbr_ref, wout_ref, lng_ref, lnb_ref,
                  o_ref, orw_ref, *, ctx_tiles):
    is_ctx = pl.program_id(0) < ctx_tiles
    pick = lambda c_val, l_val: jnp.where(is_ctx, c_val, l_val)
    for h in range(RWKV_HEADS):
        sl = slice(h * RWKV_N, (h + 1) * RWKV_N)
        o = pick(oscan_c_ref[0, h] + oscan_c_ref[1, h], oscan_l_ref[0, h] + oscan_l_ref[1, h])
        y = _ln(o, RWKV_GN_EPS) * gnw_ref[h] + gnb_ref[h]
        orw_ref[:, sl] = (y + bonus_ref[:, sl]) * grw_ref[:, sl]
    merged = (_sigmoid(ga_ref[...]) * _dot(pick(ona_c_ref[...], ona_l_ref[...]), wbr_ref[0])
              + _sigmoid(gb_ref[...]) * _dot(pick(ort_c_ref[...], ort_l_ref[...]), wbr_ref[1])
              + _sigmoid(gc_ref[...]) * _dot(orw_ref[...], wbr_ref[2]))
    t = _dot(merged, wout_ref[...])
    o_ref[...] = _ln(DEEPNORM_ALPHA * x_ref[...] + gt_ref[0] * t) * lng_ref[...] + lnb_ref[...]


def _merge(oscan, o_na, o_rt, bonus, g_rw, z, x, mod3, p, row_of_tile, tm):
    m = x.shape[0]
    bw = RWKV_WIDTH
    ctx_tiles = o_na[0].shape[0] // tm
    full = lambda a: pl.BlockSpec(a.shape, lambda i: (0,) * a.ndim)
    tok = pl.BlockSpec((tm, bw), lambda i: (i, 0))
    c_row = lambda i: jnp.minimum(i, ctx_tiles - 1)
    l_row = lambda i: jnp.maximum(i - ctx_tiles, 0)
    tok_c = pl.BlockSpec((tm, bw), lambda i: (c_row(i), 0))
    tok_l = pl.BlockSpec((tm, bw), lambda i: (l_row(i), 0))
    params = [p['rwkv_gn_w'], p['rwkv_gn_b'], p['w_br'], p['w_out'], p['ln_a_g'], p['ln_a_b']]
    return pl.pallas_call(
        functools.partial(_merge_kernel, ctx_tiles=ctx_tiles),
        grid=(m // tm,),
        in_specs=[pl.BlockSpec((2, RWKV_HEADS, tm, RWKV_N), lambda i: (0, 0, c_row(i), 0)),
                  pl.BlockSpec((2, RWKV_HEADS, tm, RWKV_N), lambda i: (0, 0, l_row(i), 0)),
                  tok_c, tok_l, tok_c, tok_l, tok, tok,
                  pl.BlockSpec((tm, D_MODEL), lambda i: (i, 0)),
                  pl.BlockSpec((tm, D_MODEL), lambda i: (i, 1)),
                  pl.BlockSpec((tm, D_MODEL), lambda i: (i, 2)),
                  pl.BlockSpec((tm, D_MODEL), lambda i: (i, 0)),
                  pl.BlockSpec((1, 1, D_MODEL), lambda i: (row_of_tile(i, tm), 0, 2))]
                 + [full(a) for a in params],
        out_specs=pl.BlockSpec((tm, D_MODEL), lambda i: (i, 0)),
        out_shape=jax.ShapeDtypeStruct((m, D_MODEL), F32),
        scratch_shapes=[pltpu.VMEM((tm, bw), F32)],
        compiler_params=_cparams("parallel"),
        name="merge_out_proj",
    )(*oscan, *o_na, *o_rt, bonus, g_rw, z, z, z, x, mod3, *params)


def _top_values(s, k, with_rank=False):
    out = []
    rank = jnp.full(s.shape, float(k), F32)
    for b in range(k):
        m = jnp.max(s, axis=0, keepdims=True)
        out.append(m)
        hit = s == m
        if with_rank:
            rank = jnp.where(hit, float(b), rank)
        s = jnp.where(hit, -jnp.inf, s)
    return (out, rank) if with_rank else out


def _peer_gate_kernel(x_ref, sh_ref, sc_ref, wq_ref, keys_ref, hb_ref, cnt_ref, rank1_ref, e0_ref, e1_ref):
    h = _ln(x_ref[...]) * (1.0 + sc_ref[0]) + sh_ref[0]
    hb_ref[...] = h.astype(BF16)
    ht = h.T.astype(BF16)
    qt = jnp.dot(wq_ref[...], ht, preferred_element_type=F32)
    half = keys_ref.shape[-1]
    for hd in range(PEER_HEADS):
        scores, tops = [], []
        for part in range(2):
            row0 = (hd * 2 + part) * half
            s = _dot(keys_ref[hd, part], qt[row0:row0 + half, :])
            scores.append(s)
            if part == 0:
                tops.append(_top_values(s, PEER_TOPK))
            else:
                top1, rank1 = _top_values(s, PEER_TOPK, with_rank=True)
                tops.append(top1)
        pairs = [tops[0][a] + tops[1][b] for a in range(PEER_TOPK) for b in range(PEER_TOPK // (a + 1))]
        pairs += [jnp.full_like(pairs[0], -jnp.inf)] * (-len(pairs) % 8)
        best = _top_values(jnp.concatenate(pairs, axis=0), PEER_TOPK)
        zsum = jnp.exp(best[0] - best[0])
        for t in range(1, PEER_TOPK):
            zsum = zsum + jnp.exp(best[t] - best[0])
        tau = best[PEER_TOPK - 1]
        cnt = jnp.zeros(scores[0].shape, F32)
        for b, t1 in enumerate(tops[1]):
            cnt = jnp.where(scores[0] + t1 >= tau, float(b + 1), cnt)
        cnt_ref[hd] = cnt
        rank1_ref[hd] = rank1.astype(BF16)
        e0_ref[hd] = jnp.exp(scores[0] - tops[0][0])
        e1_ref[hd] = (jnp.exp(scores[1] - tops[1][0]) / zsum).astype(BF16)


def _peer_gate(x, mod3, wq_t, keys, row_of_tile, tn):
    m = x.shape[0]
    full = lambda a: pl.BlockSpec(a.shape, lambda i: (0,) * a.ndim)
    sk = pl.BlockSpec((PEER_HEADS, PEER_NKEYS, tn), lambda i: (0, 0, i))
    sk_f32 = jax.ShapeDtypeStruct((PEER_HEADS, PEER_NKEYS, m), F32)
    sk_bf16 = jax.ShapeDtypeStruct((PEER_HEADS, PEER_NKEYS, m), BF16)
    return pl.pallas_call(
        _peer_gate_kernel,
        grid=(m // tn,),
        in_specs=[pl.BlockSpec((tn, D_MODEL), lambda i: (i, 0)),
                  pl.BlockSpec((1, 1, D_MODEL), lambda i: (row_of_tile(i, tn), 0, 3)),
                  pl.BlockSpec((1, 1, D_MODEL), lambda i: (row_of_tile(i, tn), 0, 4)),
                  full(wq_t), full(keys)],
        out_specs=[pl.BlockSpec((tn, D_MODEL), lambda i: (i, 0)), sk, sk, sk, sk],
        out_shape=[jax.ShapeDtypeStruct((m, D_MODEL), BF16), sk_f32, sk_bf16, sk_f32, sk_bf16],
        compiler_params=_cparams("parallel"),
        name="peer_gate",
    )(x, mod3, mod3, wq_t, keys)


def _gelu_exact(x):
    return 0.5 * x * (1.0 + lax.erf(x * (2.0 ** -0.5)))


PEER_SUB_ROWS = 32
PEER_SUB_LANES = 256
PEER_GROUP_KEYS = 8
PEER_GROUP = PEER_GROUP_KEYS * PEER_NKEYS


def _peer_expert_kernel(hb_ref, cnt_ref, rank1_ref, e0_ref, e1_ref, u_ref, vt_ref, x_ref, gt_ref,
                        lng_ref, lnb_ref, o_ref, acc_ref, st_ref, st2_ref, act_ref, *, groups):
    e = pl.program_id(1)
    tn = hb_ref.shape[0]

    @pl.when(e == 0)
    def _():
        acc_ref[...] = jnp.zeros_like(acc_ref)

    def pre_activations(g, dst_ref):
        rows = pl.ds(pl.multiple_of(g * PEER_GROUP, PEER_GROUP), PEER_GROUP)
        dst_ref[...] = lax.dot_general(u_ref[rows, :], hb_ref[...], (_NT, ((), ())), preferred_element_type=F32)

    def gated_activations(g, src_ref):
        row0 = pl.multiple_of(g * PEER_GROUP, PEER_GROUP)
        for r in range(PEER_GROUP_KEYS):
            i = (e * groups + g) * PEER_GROUP_KEYS + r
            cnt_row = [cnt_ref[hd, pl.ds(i, 1), :] for hd in range(PEER_HEADS)]
            e0_row = [e0_ref[hd, pl.ds(i, 1), :] for hd in range(PEER_HEADS)]
            for lb in range(tn // PEER_SUB_LANES):
                ls = slice(lb * PEER_SUB_LANES, (lb + 1) * PEER_SUB_LANES)
                sub = (PEER_SUB_ROWS, PEER_SUB_LANES)
                cnt = [jnp.broadcast_to(c[:, ls].astype(BF16), sub) for c in cnt_row]
                e0 = [jnp.broadcast_to(c[:, ls].astype(BF16), sub) for c in e0_row]
                zero = jnp.zeros(sub, BF16)
                for rb in range(PEER_NKEYS // PEER_SUB_ROWS):
                    js = slice(rb * PEER_SUB_ROWS, (rb + 1) * PEER_SUB_ROWS)
                    gate = None
                    for hd in range(PEER_HEADS):
                        term = jnp.where(rank1_ref[hd, js, ls] < cnt[hd], e1_ref[hd, js, ls], zero) * e0[hd]
                        gate = term if gate is None else gate + term
                    off = r * PEER_NKEYS + rb * PEER_SUB_ROWS
                    act = _gelu_exact(src_ref[off:off + PEER_SUB_ROWS, ls]).astype(BF16) * gate
                    act_ref[pl.ds(row0 + off, PEER_SUB_ROWS), ls] = act

    scratch = (st_ref, st2_ref)
    pre_activations(0, scratch[0])
    for g in range(groups):
        if g + 1 < groups:
            pre_activations(g + 1, scratch[(g + 1) % 2])
        gated_activations(g, scratch[g % 2])
        cols = slice(g * PEER_GROUP, (g + 1) * PEER_GROUP)
        acc_ref[...] += jnp.dot(vt_ref[:, cols], act_ref[cols, :], preferred_element_type=F32)

    @pl.when(e == pl.num_programs(1) - 1)
    def _():
        f = acc_ref[...].T
        o_ref[...] = _ln(DEEPNORM_ALPHA * x_ref[...] + gt_ref[0] * f) * lng_ref[...] + lnb_ref[...]


def _peer_expert(hb, cnt, rank1, e0, e1, u, v_t, x, mod3, ln_g, ln_b, row_of_tile, tn, eb):
    m = x.shape[0]
    n_exp = u.shape[0]
    groups = eb // PEER_GROUP
    sk = pl.BlockSpec((PEER_HEADS, PEER_NKEYS, tn), lambda i, e: (0, 0, i))
    full = lambda a: pl.BlockSpec(a.shape, lambda i, e: (0,) * a.ndim)
    return pl.pallas_call(
        functools.partial(_peer_expert_kernel, groups=groups),
        grid=(m // tn, n_exp // eb),
        in_specs=[pl.BlockSpec((tn, D_MODEL), lambda i, e: (i, 0)), sk, sk, sk, sk,
                  pl.BlockSpec((eb, D_MODEL), lambda i, e: (e, 0)),
                  pl.BlockSpec((D_MODEL, eb), lambda i, e: (0, e)),
                  pl.BlockSpec((tn, D_MODEL), lambda i, e: (i, 0)),
                  pl.BlockSpec((1, 1, D_MODEL), lambda i, e: (row_of_tile(i, tn), 0, 5)),
                  full(ln_g), full(ln_b)],
        out_specs=pl.BlockSpec((tn, D_MODEL), lambda i, e: (i, 0)),
        out_shape=jax.ShapeDtypeStruct((m, D_MODEL), F32),
        scratch_shapes=[pltpu.VMEM((D_MODEL, tn), F32), pltpu.VMEM((PEER_GROUP, tn), F32),
                        pltpu.VMEM((PEER_GROUP, tn), F32), pltpu.VMEM((eb, tn), BF16)],
        compiler_params=_cparams("parallel", "arbitrary"),
        name="peer_experts",
    )(hb, cnt, rank1, e0, e1, u, v_t, x, mod3, ln_g, ln_b)


def _permute_in_proj(w_in):
    na, qk, rv = 3 * 512, 2 * 256, 2 * 512
    o_na, o_qk, o_rv, o_rw, o_g = 0, na, na + qk, na + qk + rv, na + qk + rv + 1920
    pad = jnp.zeros((w_in.shape[0], P_IN_PADDED - P_IN), w_in.dtype)
    return jnp.concatenate([w_in[:, o_g:], w_in[:, o_na:o_qk], w_in[:, o_qk:o_rv], w_in[:, o_rv:o_rw],
                            w_in[:, o_rw:o_g], pad], axis=1)


def kernel(x_prompt, x_sample, cache_na_k, cache_na_v, state_ret, state_rwkv, c, c_ctx, w_mod, b_mod, w_in, na_rpb, ret_decay_logit, ret_gn_w, ret_gn_b, rwkv_mu, rwkv_w0, rwkv_w_up, rwkv_a0, rwkv_a_up, rwkv_g_up, rwkv_k_k, rwkv_k_a, rwkv_r_k, rwkv_gn_w, rwkv_gn_b, w_br, w_out, ln_a_g, ln_a_b, ln_f_g, ln_f_b, peer_wq, peer_keys, peer_u, peer_v):
    n_ctx, ctx_len, d = x_prompt.shape
    n_lat, lat_len, _ = x_sample.shape
    m_ctx = n_ctx * ctx_len
    assert m_ctx % lat_len == 0 and n_lat == 2
    lat_blk0 = m_ctx // lat_len

    def row_of_tile(i, tm):
        return jnp.where(i < m_ctx // tm, 0, 1 + (i - m_ctx // tm) // (lat_len // tm))

    prep_tt = 256
    assert ctx_len == prep_tt

    def tiles_per_seq_of_tile(i):
        lat = i >= m_ctx // prep_tt
        per = jnp.where(lat, lat_len // prep_tt, 1)
        pos = jnp.where(lat, (i - m_ctx // prep_tt) % (lat_len // prep_tt), 0)
        return pos, per

    x = jnp.concatenate([x_prompt.reshape(m_ctx, d), x_sample.reshape(n_lat * lat_len, d)], axis=0)
    c8 = jnp.concatenate([c_ctx[None], c, jnp.zeros((8 - 1 - n_lat, d), F32)], axis=0)
    mod = _modulation(c8, w_mod, b_mod)
    rope_tabs = _rope_tables(lat_len)
    zeros_ret = jnp.zeros((n_ctx, 2, RET_HEADS, RET_DK, RET_DV), F32)
    zeros_rw = jnp.zeros((n_ctx, 2, RWKV_HEADS, RWKV_N, RWKV_N), F32)

    nk, nv, sr, sw = [], [], [], []
    for l in range(DEPTH):
        mod3 = mod[l].reshape(8, 1, 6 * d)
        w_in_p = _permute_in_proj(w_in[l]).astype(BF16)
        z = _lnmod_matmul(x, mod3, w_in_p, row_of_tile, 1024, 2048, 0, 1)

        o_na_c, k_cache, v_cache = _ctx_attention(z, n_ctx, ctx_len)
        o_na_l = _na_latent(z, cache_na_k[:, l], cache_na_v[:, l], _na_bias_table(na_rpb[l], lat_len // GRID_W),
                            n_lat, lat_len, lat_blk0)
        nk.append(k_cache)
        nv.append(v_cache)

        tables = _ret_decay_tables(ret_decay_logit[l])
        gn_w, gn_b = ret_gn_w[l][None], ret_gn_b[l][None]
        o_rt_c, s_ret = _retention(z, zeros_ret, tables, gn_w, gn_b, n_ctx, ctx_len, 0, None)
        o_rt_l, _ = _retention(z, state_ret[:, l], tables, gn_w, gn_b, n_lat, lat_len, lat_blk0, rope_tabs)
        sr.append(s_ret)

        mu = rwkv_mu[l]
        prm = {
            'mu_main': mu[None, :1536], 'mu_lora': mu[None, 1536:],
            'rwkv_w0': rwkv_w0[l][:, None], 'rwkv_w_up': rwkv_w_up[l].astype(BF16),
            'rwkv_a0': rwkv_a0[l][:, None], 'rwkv_a_up': rwkv_a_up[l].astype(BF16),
            'rwkv_g_up': rwkv_g_up[l].astype(BF16), 'rwkv_k_k': rwkv_k_k[l][None],
            'rwkv_k_a': rwkv_k_a[l][None], 'rwkv_r_k': rwkv_r_k[l].reshape(1, RWKV_WIDTH),
        }
        r, kap, v, lw, cs, kd, beta, g_rw, bonus = _rwkv_prep(z, prm, tiles_per_seq_of_tile, prep_tt)
        o_c, h_fin = _rwkv_scan(r, kap, v, lw, cs, kd, beta, zeros_rw, n_ctx, ctx_len, 0)
        h0_lat = jnp.swapaxes(state_rwkv[:, l], -1, -2)
        o_l, _ = _rwkv_scan(r, kap, v, lw, cs, kd, beta, h0_lat, n_lat, lat_len, lat_blk0)
        sw.append(jnp.swapaxes(h_fin, -1, -2))

        mp = {
            'rwkv_gn_w': rwkv_gn_w[l].reshape(RWKV_HEADS, 1, RWKV_N),
            'rwkv_gn_b': rwkv_gn_b[l].reshape(RWKV_HEADS, 1, RWKV_N),
            'w_br': w_br[l].astype(BF16), 'w_out': w_out[l].astype(BF16),
            'ln_a_g': ln_a_g[l][None], 'ln_a_b': ln_a_b[l][None],
        }
        x = _merge((o_c, o_l), (o_na_c, o_na_l), (o_rt_c, o_rt_l), bonus, g_rw, z, x, mod3, mp, row_of_tile, 256)

        wq_t = peer_wq[l].T.astype(BF16)
        hb, cnt, rank1, e0, e1 = _peer_gate(x, mod3, wq_t, peer_keys[l].astype(BF16), row_of_tile, 256)
        x = _peer_expert(hb, cnt, rank1, e0, e1, peer_u[l].astype(BF16), peer_v[l].T.astype(BF16), x, mod3,
                         ln_f_g[l][None], ln_f_b[l][None], row_of_tile, 512, 2048)

    dt = x_prompt.dtype
    y_prompt = x[:m_ctx].reshape(n_ctx, ctx_len, d)
    y_sample = x[m_ctx:].reshape(n_lat, lat_len, d)
    return (y_prompt, y_sample, jnp.stack(nk, axis=1).astype(dt), jnp.stack(nv, axis=1).astype(dt),
            jnp.stack(sr, axis=1).astype(dt), jnp.stack(sw, axis=1).astype(dt))
```

```python
import functools
import math

import jax
import jax.numpy as jnp
import numpy as np
from jax import lax
from jax.experimental import pallas as pl
from jax.experimental.pallas import tpu as pltpu

F32 = jnp.float32
BF16 = jnp.bfloat16

D_MODEL = 1024
DEPTH = 2
GRID_W = 64
NA_HEADS = 8
NA_HEAD_DIM = 64
WIN_R = 8
WIN_C = 16
NEG_INF = -1e30
RET_HEADS = 4
RET_DK = 64
RET_DV = 128
RET_CHUNK = 64
RET_GN_EPS = 1e-5
ROPE_BASE = 10000.0
RWKV_HEADS = 8
RWKV_N = 64
RWKV_WIDTH = 512
RWKV_DECAY_SCALE = 0.606531
RWKV_GN_EPS = 64e-5
RWKV_CHUNK = 64
PEER_HEADS = 8
PEER_NKEYS = 128
PEER_TOPK = 16
LN_EPS = 1e-5
DEEPNORM_ALPHA = (2 * DEPTH) ** 0.25

OFF_GATE = 0
OFF_NA = 3072
OFF_RET_QK = 4608
OFF_RET_V = 5120
OFF_RW_MAIN = 6144
OFF_RW_LORA = 7680
P_IN = 8064
P_IN_PADDED = 8192

VMEM_LIMIT_BYTES = 56 * 1024 * 1024


def _cparams(*sem):
    return pltpu.CompilerParams(dimension_semantics=sem, vmem_limit_bytes=VMEM_LIMIT_BYTES)


def _ln(x, eps=LN_EPS):
    mu = jnp.mean(x, axis=-1, keepdims=True)
    xc = x - mu
    var = jnp.mean(xc * xc, axis=-1, keepdims=True)
    return xc * lax.rsqrt(var + eps)


def _sigmoid(x):
    return 1.0 / (1.0 + jnp.exp(-x))


def _dot(a, b):
    return jnp.dot(a.astype(BF16), b.astype(BF16), preferred_element_type=F32)


def _dot_nt(a, b):
    return lax.dot_general(a.astype(BF16), b.astype(BF16), (((1,), (1,)), ((), ())),
                           preferred_element_type=F32)


def _dot_tn(a, b):
    return lax.dot_general(a.astype(BF16), b.astype(BF16), (((0,), (0,)), ((), ())),
                           preferred_element_type=F32)


def _lockstep(gens):
    out = [None] * len(gens)
    live = list(range(len(gens)))
    while live:
        for i in list(live):
            try:
                next(gens[i])
            except StopIteration as done:
                out[i] = done.value
                live.remove(i)
    return out


def _split3(x):
    hi = x.astype(BF16)
    r1 = x - hi.astype(F32)
    mid = r1.astype(BF16)
    lo = (r1 - mid.astype(F32)).astype(BF16)
    return hi, mid, lo


def _dot_exact_lhs(sel, x):
    hi, mid, lo = _split3(x)
    s = sel.astype(BF16)
    out = jnp.dot(s, lo, preferred_element_type=F32)
    out = out + jnp.dot(s, mid, preferred_element_type=F32)
    return out + jnp.dot(s, hi, preferred_element_type=F32)


def _dot_exact_rhs(x, sel):
    hi, mid, lo = _split3(x)
    s = sel.astype(BF16)
    out = jnp.dot(lo, s, preferred_element_type=F32)
    out = out + jnp.dot(mid, s, preferred_element_type=F32)
    return out + jnp.dot(hi, s, preferred_element_type=F32)


def _mod_kernel(c_ref, w_ref, b_ref, o_ref):
    c = c_ref[...]
    s = c * _sigmoid(c)
    o_ref[0] = _dot(s, w_ref[0]) + b_ref[0]


def _modulation(c8, w_mod, b_mod):
    tn = 1536
    n = w_mod.shape[-1]
    return pl.pallas_call(
        _mod_kernel,
        grid=(DEPTH, n // tn),
        in_specs=[pl.BlockSpec((8, D_MODEL), lambda l, j: (0, 0)),
                  pl.BlockSpec((1, D_MODEL, tn), lambda l, j: (l, 0, j)),
                  pl.BlockSpec((1, 1, tn), lambda l, j: (l, 0, j))],
        out_specs=pl.BlockSpec((1, 8, tn), lambda l, j: (l, 0, j)),
        out_shape=jax.ShapeDtypeStruct((DEPTH, 8, n), F32),
        compiler_params=_cparams("parallel", "parallel"),
        name="modulation",
    )(c8, w_mod, b_mod.reshape(DEPTH, 1, n))


def _lnmod_matmul_kernel(x_ref, sh_ref, sc_ref, w_ref, o_ref, h_ref):
    @pl.when(pl.program_id(1) == 0)
    def _():
        h = _ln(x_ref[...]) * (1.0 + sc_ref[0]) + sh_ref[0]
        h_ref[...] = h.astype(BF16)

    o_ref[...] = jnp.dot(h_ref[...], w_ref[...], preferred_element_type=F32)


def _lnmod_matmul(x, mod3, w, row_of_tile, tm, tn, sh_blk, sc_blk):
    m = x.shape[0]
    n = w.shape[1]
    return pl.pallas_call(
        _lnmod_matmul_kernel,
        grid=(m // tm, n // tn),
        in_specs=[pl.BlockSpec((tm, D_MODEL), lambda i, j: (i, 0)),
                  pl.BlockSpec((1, 1, D_MODEL), lambda i, j: (row_of_tile(i, tm), 0, sh_blk)),
                  pl.BlockSpec((1, 1, D_MODEL), lambda i, j: (row_of_tile(i, tm), 0, sc_blk)),
                  pl.BlockSpec((D_MODEL, tn), lambda i, j: (0, j))],
        out_specs=pl.BlockSpec((tm, tn), lambda i, j: (i, j)),
        out_shape=jax.ShapeDtypeStruct((m, n), F32),
        scratch_shapes=[pltpu.VMEM((tm, D_MODEL), BF16)],
        compiler_params=_cparams("parallel", "arbitrary"),
        name="adaln_in_proj",
    )(x, mod3, mod3, w)


def _ctx_attn_kernel(q_ref, k_ref, v_ref, o_ref, kc_ref, vc_ref):
    scale = NA_HEAD_DIM ** -0.5
    half = q_ref.shape[0] // 2

    def head(hh, q0):
        sl = slice(hh * NA_HEAD_DIM, (hh + 1) * NA_HEAD_DIM)
        k = k_ref[:, sl]
        v = v_ref[:, sl]
        if q0 == 0:
            kc_ref[0, hh] = k
            vc_ref[0, hh] = v
        s = _dot_nt(q_ref[q0:q0 + half, sl], k) * scale
        yield
        m = jnp.max(s, axis=-1, keepdims=True)
        p = jnp.exp(s - m)
        l = jnp.sum(p, axis=-1, keepdims=True)
        yield
        o_ref[q0:q0 + half, sl] = _dot(p, v) / l

    _lockstep([head(hh, q0) for hh in range(2) for q0 in (0, half)])


def _ctx_attention(z, n_seq, seq_len):
    cb = OFF_NA // 128
    cache_spec = pl.BlockSpec((1, 2, seq_len, NA_HEAD_DIM), lambda b, h: (b, h, 0, 0))
    cache_shape = jax.ShapeDtypeStruct((n_seq, NA_HEADS, seq_len, NA_HEAD_DIM), F32)
    return pl.pallas_call(
        _ctx_attn_kernel,
        grid=(n_seq, NA_HEADS // 2),
        in_specs=[pl.BlockSpec((seq_len, 128), lambda b, h: (b, cb + h)),
                  pl.BlockSpec((seq_len, 128), lambda b, h: (b, cb + 4 + h)),
                  pl.BlockSpec((seq_len, 128), lambda b, h: (b, cb + 8 + h))],
        out_specs=[pl.BlockSpec((seq_len, 128), lambda b, h: (b, h)), cache_spec, cache_spec],
        out_shape=[jax.ShapeDtypeStruct((n_seq * seq_len, NA_HEADS * NA_HEAD_DIM), F32), cache_shape, cache_shape],
        compiler_params=_cparams("parallel", "parallel"),
        name="ctx_attention",
    )(z, z, z)


def _na_bias_kernel(rpb_ref, rowsel_ref, colsel_ref, neg_ref, o_ref):
    picked = _dot_exact_lhs(rowsel_ref[...], rpb_ref[0])
    o_ref[0] = _dot_exact_rhs(picked, colsel_ref[...]) + neg_ref[...]


def _na_bias_table(rpb, rows):
    kr = min(WIN_R, rows)
    nh, n_dr, n_dc = rpb.shape
    rep = np.array([0, 1, 2, 3, 4, rows - 3, rows - 2, rows - 1])
    start = np.clip(rep - kr // 2, 0, rows - kr)
    dr = start[:, None] - rep[:, None] + WIN_R - 1 + np.arange(kr)[None, :]
    qcol = np.arange(GRID_W)
    kcol = np.arange(GRID_W)
    q_start = np.clip(qcol - WIN_C // 2, 0, GRID_W - WIN_C)
    rel = kcol[None, :] - q_start[:, None]
    valid = (rel >= 0) & (rel < WIN_C)
    dc = np.clip(kcol[None, :] - qcol[:, None], -(WIN_C - 1), WIN_C - 1) + WIN_C - 1
    row_sel = np.zeros((8 * kr, 16), np.float32)
    row_sel[np.arange(8 * kr), dr.reshape(-1)] = 1.0
    col_sel = np.zeros((32, GRID_W * GRID_W), np.float32)
    qq, kk = np.nonzero(valid)
    col_sel[dc[qq, kk], qq * GRID_W + kk] = 1.0
    neg = np.where(valid, 0.0, NEG_INF).astype(np.float32).reshape(1, -1)
    rpb_p = jnp.pad(rpb.astype(F32), ((0, 0), (0, 16 - n_dr), (0, 32 - n_dc)))
    full = lambda a: pl.BlockSpec(a.shape, lambda h: (0,) * a.ndim)
    consts = [jnp.asarray(row_sel, BF16), jnp.asarray(col_sel, BF16), jnp.asarray(neg)]
    b = pl.pallas_call(
        _na_bias_kernel,
        grid=(nh,),
        in_specs=[pl.BlockSpec((1, 16, 32), lambda h: (h, 0, 0))] + [full(a) for a in consts],
        out_specs=pl.BlockSpec((1, 8 * kr, GRID_W * GRID_W), lambda h: (h, 0, 0)),
        out_shape=jax.ShapeDtypeStruct((nh, 8 * kr, GRID_W * GRID_W), F32),
        compiler_params=_cparams("parallel"),
        name="na_bias",
    )(rpb_p, *consts)
    b = b.reshape(nh, 8, kr, GRID_W, GRID_W)
    return b.transpose(0, 1, 3, 2, 4).reshape(nh, 8, GRID_W, kr * GRID_W)


NA_ROWS_PER_STEP = 4


def _na_latent_kernel(q_ref, k_ref, v_ref, ck_ref, cv_ref, b_ref, o_ref, *, rows):
    scale = NA_HEAD_DIM ** -0.5
    kr = min(WIN_R, rows)

    def query_row(r, hh):
        start = jnp.clip(r - kr // 2, 0, rows - kr)
        typ = jnp.where(r < 4, r, jnp.where(r > rows - 4, r - (rows - 8), 4))
        q0 = pl.multiple_of(r * GRID_W, GRID_W)
        k0 = pl.multiple_of(start * GRID_W, GRID_W)
        sl = slice(hh * NA_HEAD_DIM, (hh + 1) * NA_HEAD_DIM)
        q = q_ref[pl.ds(q0, GRID_W), sl]
        kw = k_ref[pl.ds(k0, kr * GRID_W), sl]
        vw = v_ref[pl.ds(k0, kr * GRID_W), sl]
        s_w = _dot_nt(q, kw) * scale + b_ref[hh, typ]
        s_c = _dot_nt(q, ck_ref[0, hh]) * scale
        yield
        m = jnp.maximum(jnp.max(s_w, axis=-1, keepdims=True), jnp.max(s_c, axis=-1, keepdims=True))
        p_w = jnp.exp(s_w - m)
        p_c = jnp.exp(s_c - m)
        l = jnp.sum(p_w, axis=-1, keepdims=True) + jnp.sum(p_c, axis=-1, keepdims=True)
        yield
        o_ref[pl.ds(q0, GRID_W), sl] = (_dot(p_w, vw) + _dot(p_c, cv_ref[0, hh])) / l

    def body(rp, carry):
        _lockstep([query_row(rp * NA_ROWS_PER_STEP + u, hh) for u in range(NA_ROWS_PER_STEP) for hh in range(2)])
        return carry

    lax.fori_loop(0, rows // NA_ROWS_PER_STEP, body, 0)


def _na_latent(z, ctx_k, ctx_v, bias, n_seq, seq_len, row_blk0):
    cb = OFF_NA // 128
    rows = seq_len // GRID_W
    past = ctx_k.shape[2]
    return pl.pallas_call(
        functools.partial(_na_latent_kernel, rows=rows),
        grid=(n_seq, NA_HEADS // 2),
        in_specs=[pl.BlockSpec((seq_len, 128), lambda b, h: (row_blk0 + b, cb + h)),
                  pl.BlockSpec((seq_len, 128), lambda b, h: (row_blk0 + b, cb + 4 + h)),
                  pl.BlockSpec((seq_len, 128), lambda b, h: (row_blk0 + b, cb + 8 + h)),
                  pl.BlockSpec((1, 2, past, NA_HEAD_DIM), lambda b, h: (b, h, 0, 0)),
                  pl.BlockSpec((1, 2, past, NA_HEAD_DIM), lambda b, h: (b, h, 0, 0)),
                  pl.BlockSpec((2, 8, GRID_W, bias.shape[-1]), lambda b, h: (h, 0, 0, 0))],
        out_specs=pl.BlockSpec((seq_len, 128), lambda b, h: (b, h)),
        out_shape=jax.ShapeDtypeStruct((n_seq * seq_len, NA_HEADS * NA_HEAD_DIM), F32),
        compiler_params=_cparams("parallel", "parallel"),
        name="na_latent",
    )(z, z, z, ctx_k, ctx_v, bias)


def _ret_decay_tables(decay_logit):
    c = RET_CHUNK
    lg = jnp.log(jax.nn.sigmoid(decay_logit.astype(F32)))[:, :, None, None]
    i = jnp.arange(c, dtype=F32)
    diff = i[:, None] - i[None, :]
    m_f = jnp.where(diff >= 0, jnp.exp(jnp.maximum(diff, 0.0) * lg[0]), 0.0)
    m_b = jnp.where(diff <= 0, jnp.exp(jnp.maximum(-diff, 0.0) * lg[1]), 0.0)
    mask = jnp.stack([m_f, m_b])
    ones = jnp.ones((1, 1, 1, RET_DK), F32)
    q_f = jnp.exp((i + 1.0)[None, :, None] * lg[0]) * ones[0]
    q_b = jnp.exp((c - i)[None, :, None] * lg[1]) * ones[0]
    k_f = jnp.exp((c - 1.0 - i)[None, :, None] * lg[0]) * ones[0]
    k_b = jnp.exp(i[None, :, None] * lg[1]) * ones[0]
    c_dec = jnp.exp(c * lg) * jnp.ones((1, 1, RET_DK, RET_DV), F32)
    return mask, jnp.stack([q_f, q_b]), jnp.stack([k_f, k_b]), c_dec


def _rope_tables(seq_len):
    quarter = RET_DK // 4
    pos = np.arange(seq_len)
    freqs = 1.0 / (ROPE_BASE ** (jnp.arange(quarter, dtype=F32) / quarter))
    a_row = jnp.asarray(pos // GRID_W, F32)[:, None] * freqs[None, :]
    a_col = jnp.asarray(pos % GRID_W, F32)[:, None] * freqs[None, :]
    cos = jnp.concatenate([jnp.cos(a_row)] * 2 + [jnp.cos(a_col)] * 2, axis=-1)
    sin = jnp.concatenate([-jnp.sin(a_row), jnp.sin(a_row), -jnp.sin(a_col), jnp.sin(a_col)], axis=-1)
    return jnp.tile(cos, (1, RET_HEADS)), jnp.tile(sin, (1, RET_HEADS))


def _retention_kernel(*refs, seq_len, rope):
    if rope:
        (q_ref, k_ref, v_ref, g_ref, s0_ref, mask_ref, qd_ref, kd_ref, cd_ref, gnw_ref, gnb_ref,
         cos_ref, sin_ref, o_ref, sfin_ref, qs_ref, ks_ref, acc_ref) = refs
    else:
        (q_ref, k_ref, v_ref, g_ref, s0_ref, mask_ref, qd_ref, kd_ref, cd_ref, gnw_ref, gnb_ref,
         o_ref, sfin_ref, qs_ref, ks_ref, acc_ref) = refs
    c = RET_CHUNK
    n = seq_len // c
    quarter = RET_DK // 4
    q = q_ref[...]
    k = k_ref[...] * (RET_DK ** -0.5)
    if rope:
        lane = lax.broadcasted_iota(jnp.int32, q.shape, 1)
        first = (lane % (2 * quarter)) < quarter
        width = q.shape[1]

        def rot(x):
            swapped = jnp.where(first, pltpu.roll(x, width - quarter, 1), pltpu.roll(x, quarter, 1))
            return x * cos_ref[...] + swapped * sin_ref[...]

        q = rot(q)
        k = rot(k)
    qs_ref[...] = q
    ks_ref[...] = k

    sfin_ref[...] = s0_ref[...]

    def chain(h, d, ci):
        ksl = slice(h * RET_DK, (h + 1) * RET_DK)
        vsl = slice(h * RET_DV, (h + 1) * RET_DV)
        cc = ci if d == 0 else n - 1 - ci
        rows = pl.ds(pl.multiple_of(cc * c, c), c)
        qc = qs_ref[rows, ksl]
        kc = ks_ref[rows, ksl]
        vc = v_ref[rows, vsl]
        att = _dot_nt(qc, kc) * mask_ref[d, h]
        s = sfin_ref[0, d, h]
        yield
        acc_ref[d, rows, vsl] = _dot(att, vc) + _dot(qc * qd_ref[d, h], s)
        yield
        sfin_ref[0, d, h] = s * cd_ref[d, h] + _dot_tn(kc * kd_ref[d, h], vc)

    def step(ci, carry):
        _lockstep([chain(h, d, ci) for h in range(RET_HEADS) for d in range(2)])
        return carry

    lax.fori_loop(0, n, step, 0)

    for h in range(RET_HEADS):
        vsl = slice(h * RET_DV, (h + 1) * RET_DV)
        y = _ln(acc_ref[0, :, vsl] + acc_ref[1, :, vsl], RET_GN_EPS) * gnw_ref[:, vsl] + gnb_ref[:, vsl]
        g = g_ref[:, vsl]
        o_ref[:, vsl] = y * (g * _sigmoid(g))


def _retention(z, s0, tables, gn_w, gn_b, n_seq, seq_len, row_blk0, rope_tabs):
    mask, qd, kd, cd = tables
    rope = rope_tabs is not None
    qk_w = RET_HEADS * RET_DK
    v_w = RET_HEADS * RET_DV
    full = lambda a: pl.BlockSpec(a.shape, lambda b: (0,) * a.ndim)
    in_specs = [pl.BlockSpec((seq_len, qk_w), lambda b: (row_blk0 + b, OFF_RET_QK // qk_w)),
                pl.BlockSpec((seq_len, qk_w), lambda b: (row_blk0 + b, OFF_RET_QK // qk_w + 1)),
                pl.BlockSpec((seq_len, v_w), lambda b: (row_blk0 + b, OFF_RET_V // v_w)),
                pl.BlockSpec((seq_len, v_w), lambda b: (row_blk0 + b, OFF_RET_V // v_w + 1)),
                pl.BlockSpec((1, 2, RET_HEADS, RET_DK, RET_DV), lambda b: (b, 0, 0, 0, 0)),
                full(mask), full(qd), full(kd), full(cd), full(gn_w), full(gn_b)]
    args = [z, z, z, z, s0, mask, qd, kd, cd, gn_w, gn_b]
    if rope:
        in_specs += [full(rope_tabs[0]), full(rope_tabs[1])]
        args += list(rope_tabs)
    return pl.pallas_call(
        functools.partial(_retention_kernel, seq_len=seq_len, rope=rope),
        grid=(n_seq,),
        in_specs=in_specs,
        out_specs=[pl.BlockSpec((seq_len, v_w), lambda b: (b, 0)),
                   pl.BlockSpec((1, 2, RET_HEADS, RET_DK, RET_DV), lambda b: (b, 0, 0, 0, 0))],
        out_shape=[jax.ShapeDtypeStruct((n_seq * seq_len, v_w), F32),
                   jax.ShapeDtypeStruct((n_seq, 2, RET_HEADS, RET_DK, RET_DV), F32)],
        scratch_shapes=[pltpu.VMEM((seq_len, qk_w), F32), pltpu.VMEM((seq_len, qk_w), F32),
                        pltpu.VMEM((2, seq_len, v_w), F32)],
        compiler_params=_cparams("parallel"),
        name="retention_rope" if rope else "retention",
    )(*args)


def _rwkv_prep_kernel(zm_ref, zl_ref, pm_ref, nm_ref, pl_ref, nl_ref, mum_ref, mul_ref,
                      w0_ref, wup_ref, a0_ref, aup_ref, gup_ref, kk_ref, ka_ref, rk_ref, ones_ref, tri_ref,
                      r_out, kap_out, v_out, lw_out, cs_out, kd_out, beta_out, g_out, bonus_out,
                      *, tiles_per_seq_of_tile):
    i = pl.program_id(0)
    pos, per = tiles_per_seq_of_tile(i)
    has_prev = (pos != 0).astype(F32)
    has_next = (pos != per - 1).astype(F32)
    tt = zm_ref.shape[0]

    def shift(z_ref, p_ref, n_ref, mu_ref):
        z = z_ref[...]
        row = lax.broadcasted_iota(jnp.int32, z.shape, 0)
        prev = jnp.where(row == 0, p_ref[7:8, :] * has_prev, pltpu.roll(z, 1, 0))
        nxt = jnp.where(row == tt - 1, n_ref[0:1, :] * has_next, pltpu.roll(z, tt - 1, 0))
        return z + mu_ref[...] * (0.5 * (prev + nxt) - z)

    zm = shift(zm_ref, pm_ref, nm_ref, mum_ref)
    zl = shift(zl_ref, pl_ref, nl_ref, mul_ref)
    w = RWKV_WIDTH
    r_c, k_c, v_c = zm[:, 0:w], zm[:, w:2 * w], zm[:, 2 * w:3 * w]
    ones = ones_ref[...]
    g_out[...] = _dot(_sigmoid(zl[:, 256:384]), gup_ref[...])
    kk = k_c * kk_ref[...]
    ss = _dot_exact_rhs(kk * kk, ones)
    kap = kk * lax.rsqrt(jnp.maximum(ss, 1e-24))
    bonus_out[...] = _dot_exact_rhs(r_c * k_c * rk_ref[...], ones) * v_c
    r_out[...] = r_c
    kap_out[...] = kap
    v_out[...] = v_c
    for d in range(2):
        wl = zl[:, d * 64:(d + 1) * 64]
        al = zl[:, 128 + d * 64:128 + (d + 1) * 64]
        lw = -RWKV_DECAY_SCALE * _sigmoid(w0_ref[d] + _dot(jnp.tanh(wl), wup_ref[d]))
        a = _sigmoid(a0_ref[d] + _dot(al, aup_ref[d]))
        k_d = k_c * (1.0 + (a - 1.0) * ka_ref[...])
        beta = kap * a
        cs = _dot_exact_lhs(tri_ref[d], lw)
        lw_out[d] = lw
        cs_out[d] = cs
        kd_out[d] = k_d
        beta_out[d] = beta


def _rwkv_prep(z, p, tiles_per_seq_of_tile, tt):
    m = z.shape[0]
    nt = m // tt
    w = RWKV_WIDTH
    mb, lb = OFF_RW_MAIN // (3 * w), OFF_RW_LORA // 384
    hm = tt // 8
    last8 = m // 8 - 1
    prev_idx = lambda i: jnp.maximum(i * hm - 1, 0)
    next_idx = lambda i: jnp.minimum((i + 1) * hm, last8)
    full = lambda a: pl.BlockSpec(a.shape, lambda i: (0,) * a.ndim)
    heads = jnp.arange(w) // RWKV_N
    ones = (heads[:, None] == heads[None, :]).astype(BF16)
    t = np.arange(tt)
    same = (t[:, None] // RWKV_CHUNK) == (t[None, :] // RWKV_CHUNK)
    tri = jnp.asarray(np.stack([same & (t[None, :] <= t[:, None]), same & (t[None, :] >= t[:, None])]), BF16)
    params = [p['mu_main'], p['mu_lora'], p['rwkv_w0'], p['rwkv_w_up'], p['rwkv_a0'], p['rwkv_a_up'],
              p['rwkv_g_up'], p['rwkv_k_k'], p['rwkv_k_a'], p['rwkv_r_k'], ones, tri]
    hd = lambda: jax.ShapeDtypeStruct((m, w), F32)
    dhd = lambda: jax.ShapeDtypeStruct((2, m, w), F32)
    tok = lambda: jax.ShapeDtypeStruct((m, w), F32)
    hd_spec = pl.BlockSpec((tt, w), lambda i: (i, 0))
    dhd_spec = pl.BlockSpec((2, tt, w), lambda i: (0, i, 0))
    tok_spec = pl.BlockSpec((tt, w), lambda i: (i, 0))
    return pl.pallas_call(
        functools.partial(_rwkv_prep_kernel, tiles_per_seq_of_tile=tiles_per_seq_of_tile),
        grid=(nt,),
        in_specs=[pl.BlockSpec((tt, 3 * w), lambda i: (i, mb)),
                  pl.BlockSpec((tt, 384), lambda i: (i, lb)),
                  pl.BlockSpec((8, 3 * w), lambda i: (prev_idx(i), mb)),
                  pl.BlockSpec((8, 3 * w), lambda i: (next_idx(i), mb)),
                  pl.BlockSpec((8, 384), lambda i: (prev_idx(i), lb)),
                  pl.BlockSpec((8, 384), lambda i: (next_idx(i), lb))] + [full(a) for a in params],
        out_specs=[hd_spec, hd_spec, hd_spec, dhd_spec, dhd_spec, dhd_spec, dhd_spec, tok_spec, tok_spec],
        out_shape=[hd(), hd(), hd(), dhd(), dhd(), dhd(), dhd(), tok(), tok()],
        compiler_params=_cparams("parallel"),
        name="rwkv_prep",
    )(z, z, z, z, z, z, *params)


def _dot3(a, b, dims):
    a_hi = a.astype(BF16)
    b_hi = b.astype(BF16)
    a_lo = (a - a_hi.astype(F32)).astype(BF16)
    b_lo = (b - b_hi.astype(F32)).astype(BF16)
    dn = (dims, ((), ()))
    out = lax.dot_general(a_lo, b_hi, dn, preferred_element_type=F32)
    out = out + lax.dot_general(a_hi, b_lo, dn, preferred_element_type=F32)
    return out + lax.dot_general(a_hi, b_hi, dn, preferred_element_type=F32)


_NN = ((1,), (0,))
_NT = ((1,), (1,))
_TN = ((0,), (0,))


def _dot1(a, b, dims):
    return lax.dot_general(a.astype(BF16), b.astype(BF16), (dims, ((), ())), preferred_element_type=F32)


def _dot_rhs2(a, b, dims):
    assert dims == _NN
    b_hi = b.astype(BF16)
    b_lo = (b - b_hi.astype(F32)).astype(BF16)
    out = jnp.dot(a.astype(BF16), jnp.concatenate([b_hi, b_lo], axis=1), preferred_element_type=F32)
    n = b.shape[1]
    return out[:, n:] + out[:, :n]


_RWKV_MM = _dot1
RWKV_CHAINS = 16
RWKV_MAX_GROUP = 8


def _rwkv_scan_kernel(r_ref, kap_ref, v_ref, lw_ref, cs_ref, kd_ref, beta_ref, h0_ref, o_ref, hfin_ref, *,
                      seq_len, group, heads):
    c = RWKV_CHUNK
    n = seq_len // c
    d = pl.program_id(0)
    sgn = 1 - 2 * d
    ti = lax.broadcasted_iota(jnp.int32, (c, c), 0) * sgn
    tj = lax.broadcasted_iota(jnp.int32, (c, c), 1) * sgn
    strict = tj < ti
    incl = tj <= ti
    eye_f = (ti == tj).astype(F32)

    def chunk_terms(hh, cc):
        rows = pl.ds(pl.multiple_of(cc * c, c), c)
        hl = slice(hh * RWKV_N, (hh + 1) * RWKV_N)
        r = r_ref[rows, hl]
        kap = kap_ref[rows, hl]
        v = v_ref[rows, hl]
        lw = lw_ref[0, rows, hl]
        cs = cs_ref[0, rows, hl]
        kd = kd_ref[0, rows, hl]
        beta = beta_ref[0, rows, hl]
        tot = jnp.sum(lw, axis=0, keepdims=True)
        kap_t = kap * jnp.exp(cs - lw)
        r_t = r * jnp.exp(cs)
        e_ncs = jnp.exp(-cs)
        k_t = kd * e_ncs
        b_t = beta * e_ncs
        e_end = jnp.exp(tot - cs)
        k_end = kd * e_end
        b_end = beta * e_end
        yield
        p = _RWKV_MM(jnp.concatenate([kap_t, r_t], axis=0), jnp.concatenate([b_t, k_t], axis=0), _NT)
        l_b = jnp.where(strict, p[:c, :c], 0.0)
        l_k = jnp.where(strict, p[:c, c:], 0.0)
        m_b = jnp.where(incl, p[c:, :c], 0.0)
        m_k = jnp.where(incl, p[c:, c:], 0.0)
        yield
        x = -l_b
        y = jnp.concatenate([kap_t, _RWKV_MM(l_k, v, _NN)], axis=1)
        yield
        y = y + _dot_rhs2(x, y, _NN)
        pw = x
        for _ in range(5):
            pw = _RWKV_MM(pw, pw, _NN)
            yield
            y = y + _dot_rhs2(pw, y, _NN)
        yield
        mby = _RWKV_MM(m_b, y, _NN)
        bty = _RWKV_MM(b_end, y, _TN)
        rr = r_t - mby[:, :RWKV_N]
        o0 = _RWKV_MM(m_k, v, _NN) - mby[:, RWKV_N:]
        pc = eye_f * jnp.exp(tot) - bty[:, :RWKV_N]
        qc = _RWKV_MM(k_end, v, _TN) - bty[:, RWKV_N:]
        return rows, rr, o0, pc, qc

    def step(gi, hstates):
        keys = [(hh, u) for u in range(group) for hh in range(heads)]
        terms = dict(zip(keys, _lockstep(
            [chunk_terms(hh, (gi * group + u) + d * (n - 1 - 2 * (gi * group + u))) for hh, u in keys])))
        hstates = list(hstates)
        for u in range(group):
            for hh in range(heads):
                rows, rr, o0, pc, qc = terms[(hh, u)]
                o_ref[0, hh, rows, :] = _dot3(rr, hstates[hh], _NN) + o0
                hstates[hh] = _dot3(pc, hstates[hh], _NN) + qc
        return tuple(hstates)

    fin = lax.fori_loop(0, n // group, step, tuple(h0_ref[0, 0, hh] for hh in range(heads)))
    for hh in range(heads):
        hfin_ref[0, 0, hh] = fin[hh]


def _rwkv_scan(r, kap, v, lw, cs, kd, beta, h0, n_seq, seq_len, row_blk0):
    group = min(seq_len // RWKV_CHUNK, RWKV_MAX_GROUP)
    hp = RWKV_CHAINS // group
    hd_spec = pl.BlockSpec((seq_len, hp * RWKV_N), lambda d, h, b: (row_blk0 + b, h))
    dhd_spec = pl.BlockSpec((1, seq_len, hp * RWKV_N), lambda d, h, b: (d, row_blk0 + b, h))
    st_spec = pl.BlockSpec((1, 1, hp, RWKV_N, RWKV_N), lambda d, h, b: (b, d, h, 0, 0))
    return pl.pallas_call(
        functools.partial(_rwkv_scan_kernel, seq_len=seq_len, group=group, heads=hp),
        grid=(2, RWKV_HEADS // hp, n_seq),
        in_specs=[hd_spec, hd_spec, hd_spec, dhd_spec, dhd_spec, dhd_spec, dhd_spec, st_spec],
        out_specs=[pl.BlockSpec((1, hp, seq_len, RWKV_N), lambda d, h, b: (d, h, b, 0)), st_spec],
        out_shape=[jax.ShapeDtypeStruct((2, RWKV_HEADS, n_seq * seq_len, RWKV_N), F32),
                   jax.ShapeDtypeStruct((n_seq, 2, RWKV_HEADS, RWKV_N, RWKV_N), F32)],
        compiler_params=_cparams("parallel", "parallel", "parallel"),
        name="rwkv_scan",
    )(r, kap, v, lw, cs, kd, beta, h0)


def _merge_kernel(oscan_c_ref, oscan_l_ref, ona_c_ref, ona_l_ref, ort_c_ref, ort_l_ref, bonus_ref, grw_ref,
                  ga_ref, gb_ref, gc_ref, x_ref, gt_ref, gnw_ref, gnb_ref, wbr_ref, wout_ref, lng_ref, lnb_ref,
                  o_ref, orw_ref, *, ctx_tiles):
    is_ctx = pl.program_id(0) < ctx_tiles
    pick = lambda c_val, l_val: jnp.where(is_ctx, c_val, l_val)
    for h in range(RWKV_HEADS):
        sl = slice(h * RWKV_N, (h + 1) * RWKV_N)
        o = pick(oscan_c_ref[0, h] + oscan_c_ref[1, h], oscan_l_ref[0, h] + oscan_l_ref[1, h])
        y = _ln(o, RWKV_GN_EPS) * gnw_ref[h] + gnb_ref[h]
        orw_ref[:, sl] = (y + bonus_ref[:, sl]) * grw_ref[:, sl]
    merged = (_sigmoid(ga_ref[...]) * _dot(pick(ona_c_ref[...], ona_l_ref[...]), wbr_ref[0])
              + _sigmoid(gb_ref[...]) * _dot(pick(ort_c_ref[...], ort_l_ref[...]), wbr_ref[1])
              + _sigmoid(gc_ref[...]) * _dot(orw_ref[...], wbr_ref[2]))
    t = _dot(merged, wout_ref[...])
    o_ref[...] = _ln(DEEPNORM_ALPHA * x_ref[...] + gt_ref[0] * t) * lng_ref[...] + lnb_ref[...]


def _merge(oscan, o_na, o_rt, bonus, g_rw, z, x, mod3, p, row_of_tile, tm):
    m = x.shape[0]
    bw = RWKV_WIDTH
    ctx_tiles = o_na[0].shape[0] // tm
    full = lambda a: pl.BlockSpec(a.shape, lambda i: (0,) * a.ndim)
    tok = pl.BlockSpec((tm, bw), lambda i: (i, 0))
    c_row = lambda i: jnp.minimum(i, ctx_tiles - 1)
    l_row = lambda i: jnp.maximum(i - ctx_tiles, 0)
    tok_c = pl.BlockSpec((tm, bw), lambda i: (c_row(i), 0))
    tok_l = pl.BlockSpec((tm, bw), lambda i: (l_row(i), 0))
    params = [p['rwkv_gn_w'], p['rwkv_gn_b'], p['w_br'], p['w_out'], p['ln_a_g'], p['ln_a_b']]
    return pl.pallas_call(
        functools.partial(_merge_kernel, ctx_tiles=ctx_tiles),
        grid=(m // tm,),
        in_specs=[pl.BlockSpec((2, RWKV_HEADS, tm, RWKV_N), lambda i: (0, 0, c_row(i), 0)),
                  pl.BlockSpec((2, RWKV_HEADS, tm, RWKV_N), lambda i: (0, 0, l_row(i), 0)),
                  tok_c, tok_l, tok_c, tok_l, tok, tok,
                  pl.BlockSpec((tm, D_MODEL), lambda i: (i, 0)),
                  pl.BlockSpec((tm, D_MODEL), lambda i: (i, 1)),
                  pl.BlockSpec((tm, D_MODEL), lambda i: (i, 2)),
                  pl.BlockSpec((tm, D_MODEL), lambda i: (i, 0)),
                  pl.BlockSpec((1, 1, D_MODEL), lambda i: (row_of_tile(i, tm), 0, 2))]
                 + [full(a) for a in params],
        out_specs=pl.BlockSpec((tm, D_MODEL), lambda i: (i, 0)),
        out_shape=jax.ShapeDtypeStruct((m, D_MODEL), F32),
        scratch_shapes=[pltpu.VMEM((tm, bw), F32)],
        compiler_params=_cparams("parallel"),
        name="merge_out_proj",
    )(*oscan, *o_na, *o_rt, bonus, g_rw, z, z, z, x, mod3, *params)


def _top_values(s, k, with_rank=False):
    out = []
    rank = jnp.full(s.shape, float(k), F32)
    for b in range(k):
        m = jnp.max(s, axis=0, keepdims=True)
        out.append(m)
        hit = s == m
        if with_rank:
            rank = jnp.where(hit, float(b), rank)
        s = jnp.where(hit, -jnp.inf, s)
    return (out, rank) if with_rank else out


def _peer_gate_kernel(x_ref, sh_ref, sc_ref, wq_ref, keys_ref, hb_ref, cnt_ref, rank1_ref, e0_ref, e1_ref):
    h = _ln(x_ref[...]) * (1.0 + sc_ref[0]) + sh_ref[0]
    hb_ref[...] = h.astype(BF16)
    ht = h.T.astype(BF16)
    qt = jnp.dot(wq_ref[...], ht, preferred_element_type=F32)
    half = keys_ref.shape[-1]
    for hd in range(PEER_HEADS):
        scores, tops = [], []
        for part in range(2):
            row0 = (hd * 2 + part) * half
            s = _dot(keys_ref[hd, part], qt[row0:row0 + half, :])
            scores.append(s)
            if part == 0:
                tops.append(_top_values(s, PEER_TOPK))
            else:
                top1, rank1 = _top_values(s, PEER_TOPK, with_rank=True)
                tops.append(top1)
        pairs = [tops[0][a] + tops[1][b] for a in range(PEER_TOPK) for b in range(PEER_TOPK // (a + 1))]
        pairs += [jnp.full_like(pairs[0], -jnp.inf)] * (-len(pairs) % 8)
        best = _top_values(jnp.concatenate(pairs, axis=0), PEER_TOPK)
        zsum = jnp.exp(best[0] - best[0])
        for t in range(1, PEER_TOPK):
            zsum = zsum + jnp.exp(best[t] - best[0])
        tau = best[PEER_TOPK - 1]
        cnt = jnp.zeros(scores[0].shape, F32)
        for b, t1 in enumerate(tops[1]):
            cnt = jnp.where(scores[0] + t1 >= tau, float(b + 1), cnt)
        cnt_ref[hd] = cnt
        rank1_ref[hd] = rank1.astype(BF16)
        e0_ref[hd] = jnp.exp(scores[0] - tops[0][0])
        e1_ref[hd] = (jnp.exp(scores[1] - tops[1][0]) / zsum).astype(BF16)


def _peer_gate(x, mod3, wq_t, keys, row_of_tile, tn):
    m = x.shape[0]
    full = lambda a: pl.BlockSpec(a.shape, lambda i: (0,) * a.ndim)
    sk = pl.BlockSpec((PEER_HEADS, PEER_NKEYS, tn), lambda i: (0, 0, i))
    sk_f32 = jax.ShapeDtypeStruct((PEER_HEADS, PEER_NKEYS, m), F32)
    sk_bf16 = jax.ShapeDtypeStruct((PEER_HEADS, PEER_NKEYS, m), BF16)
    return pl.pallas_call(
        _peer_gate_kernel,
        grid=(m // tn,),
        in_specs=[pl.BlockSpec((tn, D_MODEL), lambda i: (i, 0)),
                  pl.BlockSpec((1, 1, D_MODEL), lambda i: (row_of_tile(i, tn), 0, 3)),
                  pl.BlockSpec((1, 1, D_MODEL), lambda i: (row_of_tile(i, tn), 0, 4)),
                  full(wq_t), full(keys)],
        out_specs=[pl.BlockSpec((tn, D_MODEL), lambda i: (i, 0)), sk, sk, sk, sk],
        out_shape=[jax.ShapeDtypeStruct((m, D_MODEL), BF16), sk_f32, sk_bf16, sk_f32, sk_bf16],
        compiler_params=_cparams("parallel"),
        name="peer_gate",
    )(x, mod3, mod3, wq_t, keys)


def _gelu_exact(x):
    return 0.5 * x * (1.0 + lax.erf(x * (2.0 ** -0.5)))


PEER_SUB_ROWS = 32
PEER_SUB_LANES = 256
PEER_GROUP_KEYS = 8
PEER_GROUP = PEER_GROUP_KEYS * PEER_NKEYS


def _peer_expert_kernel(hb_ref, cnt_ref, rank1_ref, e0_ref, e1_ref, u_ref, vt_ref, x_ref, gt_ref,
                        lng_ref, lnb_ref, o_ref, acc_ref, st_ref, st2_ref, act_ref, *, groups):
    e = pl.program_id(1)
    tn = hb_ref.shape[0]

    @pl.when(e == 0)
    def _():
        acc_ref[...] = jnp.zeros_like(acc_ref)

    def pre_activations(g, dst_ref):
        rows = pl.ds(pl.multiple_of(g * PEER_GROUP, PEER_GROUP), PEER_GROUP)
        dst_ref[...] = lax.dot_general(u_ref[rows, :], hb_ref[...], (_NT, ((), ())), preferred_element_type=F32)

    def gated_activations(g, src_ref):
        row0 = pl.multiple_of(g * PEER_GROUP, PEER_GROUP)
        for r in range(PEER_GROUP_KEYS):
            i = (e * groups + g) * PEER_GROUP_KEYS + r
            cnt_row = [cnt_ref[hd, pl.ds(i, 1), :] for hd in range(PEER_HEADS)]
            e0_row = [e0_ref[hd, pl.ds(i, 1), :] for hd in range(PEER_HEADS)]
            for lb in range(tn // PEER_SUB_LANES):
                ls = slice(lb * PEER_SUB_LANES, (lb + 1) * PEER_SUB_LANES)
                sub = (PEER_SUB_ROWS, PEER_SUB_LANES)
                cnt = [jnp.broadcast_to(c[:, ls].astype(BF16), sub) for c in cnt_row]
                e0 = [jnp.broadcast_to(c[:, ls].astype(BF16), sub) for c in e0_row]
                zero = jnp.zeros(sub, BF16)
                for rb in range(PEER_NKEYS // PEER_SUB_ROWS):
                    js = slice(rb * PEER_SUB_ROWS, (rb + 1) * PEER_SUB_ROWS)
                    gate = None
                    for hd in range(PEER_HEADS):
                        term = jnp.where(rank1_ref[hd, js, ls] < cnt[hd], e1_ref[hd, js, ls], zero) * e0[hd]
                        gate = term if gate is None else gate + term
                    off = r * PEER_NKEYS + rb * PEER_SUB_ROWS
                    act = _gelu_exact(src_ref[off:off + PEER_SUB_ROWS, ls]).astype(BF16) * gate
                    act_ref[pl.ds(row0 + off, PEER_SUB_ROWS), ls] = act

    scratch = (st_ref, st2_ref)
    pre_activations(0, scratch[0])
    for g in range(groups):
        if g + 1 < groups:
            pre_activations(g + 1, scratch[(g + 1) % 2])
        gated_activations(g, scratch[g % 2])
        cols = slice(g * PEER_GROUP, (g + 1) * PEER_GROUP)
        acc_ref[...] += jnp.dot(vt_ref[:, cols], act_ref[cols, :], preferred_element_type=F32)

    @pl.when(e == pl.num_programs(1) - 1)
    def _():
        f = acc_ref[...].T
        o_ref[...] = _ln(DEEPNORM_ALPHA * x_ref[...] + gt_ref[0] * f) * lng_ref[...] + lnb_ref[...]


def _peer_expert(hb, cnt, rank1, e0, e1, u, v_t, x, mod3, ln_g, ln_b, row_of_tile, tn, eb):
    m = x.shape[0]
    n_exp = u.shape[0]
    groups = eb // PEER_GROUP
    sk = pl.BlockSpec((PEER_HEADS, PEER_NKEYS, tn), lambda i, e: (0, 0, i))
    full = lambda a: pl.BlockSpec(a.shape, lambda i, e: (0,) * a.ndim)
    return pl.pallas_call(
        functools.partial(_peer_expert_kernel, groups=groups),
        grid=(m // tn, n_exp // eb),
        in_specs=[pl.BlockSpec((tn, D_MODEL), lambda i, e: (i, 0)), sk, sk, sk, sk,
                  pl.BlockSpec((eb, D_MODEL), lambda i, e: (e, 0)),
                  pl.BlockSpec((D_MODEL, eb), lambda i, e: (0, e)),
                  pl.BlockSpec((tn, D_MODEL), lambda i, e: (i, 0)),
                  pl.BlockSpec((1, 1, D_MODEL), lambda i, e: (row_of_tile(i, tn), 0, 5)),
                  full(ln_g), full(ln_b)],
        out_specs=pl.BlockSpec((tn, D_MODEL), lambda i, e: (i, 0)),
        out_shape=jax.ShapeDtypeStruct((m, D_MODEL), F32),
        scratch_shapes=[pltpu.VMEM((D_MODEL, tn), F32), pltpu.VMEM((PEER_GROUP, tn), F32),
                        pltpu.VMEM((PEER_GROUP, tn), F32), pltpu.VMEM((eb, tn), BF16)],
        compiler_params=_cparams("parallel", "arbitrary"),
        name="peer_experts",
    )(hb, cnt, rank1, e0, e1, u, v_t, x, mod3, ln_g, ln_b)


def _permute_in_proj(w_in):
    na, qk, rv = 3 * 512, 2 * 256, 2 * 512
    o_na, o_qk, o_rv, o_rw, o_g = 0, na, na + qk, na + qk + rv, na + qk + rv + 1920
    pad = jnp.zeros((w_in.shape[0], P_IN_PADDED - P_IN), w_in.dtype)
    return jnp.concatenate([w_in[:, o_g:], w_in[:, o_na:o_qk], w_in[:, o_qk:o_rv], w_in[:, o_rv:o_rw],
                            w_in[:, o_rw:o_g], pad], axis=1)


def kernel(x_prompt, x_sample, cache_na_k, cache_na_v, state_ret, state_rwkv, c, c_ctx, w_mod, b_mod, w_in, na_rpb, ret_decay_logit, ret_gn_w, ret_gn_b, rwkv_mu, rwkv_w0, rwkv_w_up, rwkv_a0, rwkv_a_up, rwkv_g_up, rwkv_k_k, rwkv_k_a, rwkv_r_k, rwkv_gn_w, rwkv_gn_b, w_br, w_out, ln_a_g, ln_a_b, ln_f_g, ln_f_b, peer_wq, peer_keys, peer_u, peer_v):
    n_ctx, ctx_len, d = x_prompt.shape
    n_lat, lat_len, _ = x_sample.shape
    m_ctx = n_ctx * ctx_len
    assert m_ctx % lat_len == 0 and n_lat == 2
    lat_blk0 = m_ctx // lat_len

    def row_of_tile(i, tm):
        return jnp.where(i < m_ctx // tm, 0, 1 + (i - m_ctx // tm) // (lat_len // tm))

    prep_tt = 256
    assert ctx_len == prep_tt

    def tiles_per_seq_of_tile(i):
        lat = i >= m_ctx // prep_tt
        per = jnp.where(lat, lat_len // prep_tt, 1)
        pos = jnp.where(lat, (i - m_ctx // prep_tt) % (lat_len // prep_tt), 0)
        return pos, per

    x = jnp.concatenate([x_prompt.reshape(m_ctx, d), x_sample.reshape(n_lat * lat_len, d)], axis=0)
    c8 = jnp.concatenate([c_ctx[None], c, jnp.zeros((8 - 1 - n_lat, d), F32)], axis=0)
    mod = _modulation(c8, w_mod, b_mod)
    rope_tabs = _rope_tables(lat_len)
    zeros_ret = jnp.zeros((n_ctx, 2, RET_HEADS, RET_DK, RET_DV), F32)
    zeros_rw = jnp.zeros((n_ctx, 2, RWKV_HEADS, RWKV_N, RWKV_N), F32)

    nk, nv, sr, sw = [], [], [], []
    for l in range(DEPTH):
        mod3 = mod[l].reshape(8, 1, 6 * d)
        w_in_p = _permute_in_proj(w_in[l]).astype(BF16)
        z = _lnmod_matmul(x, mod3, w_in_p, row_of_tile, 1024, 2048, 0, 1)

        o_na_c, k_cache, v_cache = _ctx_attention(z, n_ctx, ctx_len)
        o_na_l = _na_latent(z, cache_na_k[:, l], cache_na_v[:, l], _na_bias_table(na_rpb[l], lat_len // GRID_W),
                            n_lat, lat_len, lat_blk0)
        nk.append(k_cache)
        nv.append(v_cache)

        tables = _ret_decay_tables(ret_decay_logit[l])
        gn_w, gn_b = ret_gn_w[l][None], ret_gn_b[l][None]
        o_rt_c, s_ret = _retention(z, zeros_ret, tables, gn_w, gn_b, n_ctx, ctx_len, 0, None)
        o_rt_l, _ = _retention(z, state_ret[:, l], tables, gn_w, gn_b, n_lat, lat_len, lat_blk0, rope_tabs)
        sr.append(s_ret)

        mu = rwkv_mu[l]
        prm = {
            'mu_main': mu[None, :1536], 'mu_lora': mu[None, 1536:],
            'rwkv_w0': rwkv_w0[l][:, None], 'rwkv_w_up': rwkv_w_up[l].astype(BF16),
            'rwkv_a0': rwkv_a0[l][:, None], 'rwkv_a_up': rwkv_a_up[l].astype(BF16),
            'rwkv_g_up': rwkv_g_up[l].astype(BF16), 'rwkv_k_k': rwkv_k_k[l][None],
            'rwkv_k_a': rwkv_k_a[l][None], 'rwkv_r_k': rwkv_r_k[l].reshape(1, RWKV_WIDTH),
        }
        r, kap, v, lw, cs, kd, beta, g_rw, bonus = _rwkv_prep(z, prm, tiles_per_seq_of_tile, prep_tt)
        o_c, h_fin = _rwkv_scan(r, kap, v, lw, cs, kd, beta, zeros_rw, n_ctx, ctx_len, 0)
        h0_lat = jnp.swapaxes(state_rwkv[:, l], -1, -2)
        o_l, _ = _rwkv_scan(r, kap, v, lw, cs, kd, beta, h0_lat, n_lat, lat_len, lat_blk0)
        sw.append(jnp.swapaxes(h_fin, -1, -2))

        mp = {
            'rwkv_gn_w': rwkv_gn_w[l].reshape(RWKV_HEADS, 1, RWKV_N),
            'rwkv_gn_b': rwkv_gn_b[l].reshape(RWKV_HEADS, 1, RWKV_N),
            'w_br': w_br[l].astype(BF16), 'w_out': w_out[l].astype(BF16),
            'ln_a_g': ln_a_g[l][None], 'ln_a_b': ln_a_b[l][None],
        }
        x = _merge((o_c, o_l), (o_na_c, o_na_l), (o_rt_c, o_rt_l), bonus, g_rw, z, x, mod3, mp, row_of_tile, 256)

        wq_t = peer_wq[l].T.astype(BF16)
        hb, cnt, rank1, e0, e1 = _peer_gate(x, mod3, wq_t, peer_keys[l].astype(BF16), row_of_tile, 256)
        x = _peer_expert(hb, cnt, rank1, e0, e1, peer_u[l].astype(BF16), peer_v[l].T.astype(BF16), x, mod3,
                         ln_f_g[l][None], ln_f_b[l][None], row_of_tile, 512, 2048)

    dt = x_prompt.dtype
    y_prompt = x[:m_ctx].reshape(n_ctx, ctx_len, d)
    y_sample = x[m_ctx:].reshape(n_lat, lat_len, d)
    return (y_prompt, y_sample, jnp.stack(nk, axis=1).astype(dt), jnp.stack(nv, axis=1).astype(dt),
            jnp.stack(sr, axis=1).astype(dt), jnp.stack(sw, axis=1).astype(dt))
```

```python
import functools
import math

import jax
import jax.numpy as jnp
import numpy as np
from jax import lax
from jax.experimental import pallas as pl
from jax.experimental.pallas import tpu as pltpu

F32 = jnp.float32
BF16 = jnp.bfloat16

D_MODEL = 1024
DEPTH = 2
GRID_W = 64
NA_HEADS = 8
NA_HEAD_DIM = 64
WIN_R = 8
WIN_C = 16
NEG_INF = -1e30
RET_HEADS = 4
RET_DK = 64
RET_DV = 128
RET_CHUNK = 64
RET_GN_EPS = 1e-5
ROPE_BASE = 10000.0
RWKV_HEADS = 8
RWKV_N = 64
RWKV_WIDTH = 512
RWKV_DECAY_SCALE = 0.606531
RWKV_GN_EPS = 64e-5
RWKV_CHUNK = 64
PEER_HEADS = 8
PEER_NKEYS = 128
PEER_TOPK = 16
LN_EPS = 1e-5
DEEPNORM_ALPHA = (2 * DEPTH) ** 0.25

OFF_GATE = 0
OFF_NA = 3072
OFF_RET_QK = 4608
OFF_RET_V = 5120
OFF_RW_MAIN = 6144
OFF_RW_LORA = 7680
P_IN = 8064
P_IN_PADDED = 8192

VMEM_LIMIT_BYTES = 56 * 1024 * 1024


def _cparams(*sem):
    return pltpu.CompilerParams(dimension_semantics=sem, vmem_limit_bytes=VMEM_LIMIT_BYTES)


def _ln(x, eps=LN_EPS):
    mu = jnp.mean(x, axis=-1, keepdims=True)
    xc = x - mu
    var = jnp.mean(xc * xc, axis=-1, keepdims=True)
    return xc * lax.rsqrt(var + eps)


def _sigmoid(x):
    return 1.0 / (1.0 + jnp.exp(-x))


def _dot(a, b):
    return jnp.dot(a.astype(BF16), b.astype(BF16), preferred_element_type=F32)


def _dot_nt(a, b):
    return lax.dot_general(a.astype(BF16), b.astype(BF16), (((1,), (1,)), ((), ())),
                           preferred_element_type=F32)


def _dot_tn(a, b):
    return lax.dot_general(a.astype(BF16), b.astype(BF16), (((0,), (0,)), ((), ())),
                           preferred_element_type=F32)


def _lockstep(gens):
    out = [None] * len(gens)
    live = list(range(len(gens)))
    while live:
        for i in list(live):
            try:
                next(gens[i])
            except StopIteration as done:
                out[i] = done.value
                live.remove(i)
    return out


def _split3(x):
    hi = x.astype(BF16)
    r1 = x - hi.astype(F32)
    mid = r1.astype(BF16)
    lo = (r1 - mid.astype(F32)).astype(BF16)
    return hi, mid, lo


def _dot_exact_lhs(sel, x):
    hi, mid, lo = _split3(x)
    s = sel.astype(BF16)
    out = jnp.dot(s, lo, preferred_element_type=F32)
    out = out + jnp.dot(s, mid, preferred_element_type=F32)
    return out + jnp.dot(s, hi, preferred_element_type=F32)


def _dot_exact_rhs(x, sel):
    hi, mid, lo = _split3(x)
    s = sel.astype(BF16)
    out = jnp.dot(lo, s, preferred_element_type=F32)
    out = out + jnp.dot(mid, s, preferred_element_type=F32)
    return out + jnp.dot(hi, s, preferred_element_type=F32)


def _mod_kernel(c_ref, w_ref, b_ref, o_ref):
    c = c_ref[...]
    s = c * _sigmoid(c)
    o_ref[0] = _dot(s, w_ref[0]) + b_ref[0]


def _modulation(c8, w_mod, b_mod):
    tn = 1536
    n = w_mod.shape[-1]
    return pl.pallas_call(
        _mod_kernel,
        grid=(DEPTH, n // tn),
        in_specs=[pl.BlockSpec((8, D_MODEL), lambda l, j: (0, 0)),
                  pl.BlockSpec((1, D_MODEL, tn), lambda l, j: (l, 0, j)),
                  pl.BlockSpec((1, 1, tn), lambda l, j: (l, 0, j))],
        out_specs=pl.BlockSpec((1, 8, tn), lambda l, j: (l, 0, j)),
        out_shape=jax.ShapeDtypeStruct((DEPTH, 8, n), F32),
        compiler_params=_cparams("parallel", "parallel"),
        name="modulation",
    )(c8, w_mod, b_mod.reshape(DEPTH, 1, n))


def _lnmod_matmul_kernel(x_ref, sh_ref, sc_ref, w_ref, o_ref, h_ref):
    @pl.when(pl.program_id(1) == 0)
    def _():
        h = _ln(x_ref[...]) * (1.0 + sc_ref[0]) + sh_ref[0]
        h_ref[...] = h.astype(BF16)

    o_ref[...] = jnp.dot(h_ref[...], w_ref[...], preferred_element_type=F32)


def _lnmod_matmul(x, mod3, w, row_of_tile, tm, tn, sh_blk, sc_blk):
    m = x.shape[0]
    n = w.shape[1]
    return pl.pallas_call(
        _lnmod_matmul_kernel,
        grid=(m // tm, n // tn),
        in_specs=[pl.BlockSpec((tm, D_MODEL), lambda i, j: (i, 0)),
                  pl.BlockSpec((1, 1, D_MODEL), lambda i, j: (row_of_tile(i, tm), 0, sh_blk)),
                  pl.BlockSpec((1, 1, D_MODEL), lambda i, j: (row_of_tile(i, tm), 0, sc_blk)),
                  pl.BlockSpec((D_MODEL, tn), lambda i, j: (0, j))],
        out_specs=pl.BlockSpec((tm, tn), lambda i, j: (i, j)),
        out_shape=jax.ShapeDtypeStruct((m, n), F32),
        scratch_shapes=[pltpu.VMEM((tm, D_MODEL), BF16)],
        compiler_params=_cparams("parallel", "arbitrary"),
        name="adaln_in_proj",
    )(x, mod3, mod3, w)


def _ctx_attn_kernel(q_ref, k_ref, v_ref, o_ref, kc_ref, vc_ref):
    scale = NA_HEAD_DIM ** -0.5
    half = q_ref.shape[0] // 2

    def head(hh, q0):
        sl = slice(hh * NA_HEAD_DIM, (hh + 1) * NA_HEAD_DIM)
        k = k_ref[:, sl]
        v = v_ref[:, sl]
        if q0 == 0:
            kc_ref[0, hh] = k
            vc_ref[0, hh] = v
        s = _dot_nt(q_ref[q0:q0 + half, sl], k) * scale
        yield
        m = jnp.max(s, axis=-1, keepdims=True)
        p = jnp.exp(s - m)
        l = jnp.sum(p, axis=-1, keepdims=True)
        yield
        o_ref[q0:q0 + half, sl] = _dot(p, v) / l

    _lockstep([head(hh, q0) for hh in range(2) for q0 in (0, half)])


def _ctx_attention(z, n_seq, seq_len):
    cb = OFF_NA // 128
    cache_spec = pl.BlockSpec((1, 2, seq_len, NA_HEAD_DIM), lambda b, h: (b, h, 0, 0))
    cache_shape = jax.ShapeDtypeStruct((n_seq, NA_HEADS, seq_len, NA_HEAD_DIM), F32)
    return pl.pallas_call(
        _ctx_attn_kernel,
        grid=(n_seq, NA_HEADS // 2),
        in_specs=[pl.BlockSpec((seq_len, 128), lambda b, h: (b, cb + h)),
                  pl.BlockSpec((seq_len, 128), lambda b, h: (b, cb + 4 + h)),
                  pl.BlockSpec((seq_len, 128), lambda b, h: (b, cb + 8 + h))],
        out_specs=[pl.BlockSpec((seq_len, 128), lambda b, h: (b, h)), cache_spec, cache_spec],
        out_shape=[jax.ShapeDtypeStruct((n_seq * seq_len, NA_HEADS * NA_HEAD_DIM), F32), cache_shape, cache_shape],
        compiler_params=_cparams("parallel", "parallel"),
        name="ctx_attention",
    )(z, z, z)


def _na_bias_kernel(rpb_ref, rowsel_ref, colsel_ref, neg_ref, o_ref):
    picked = _dot_exact_lhs(rowsel_ref[...], rpb_ref[0])
    o_ref[0] = _dot_exact_rhs(picked, colsel_ref[...]) + neg_ref[...]


def _na_bias_table(rpb, rows):
    kr = min(WIN_R, rows)
    nh, n_dr, n_dc = rpb.shape
    rep = np.array([0, 1, 2, 3, 4, rows - 3, rows - 2, rows - 1])
    start = np.clip(rep - kr // 2, 0, rows - kr)
    dr = start[:, None] - rep[:, None] + WIN_R - 1 + np.arange(kr)[None, :]
    qcol = np.arange(GRID_W)
    kcol = np.arange(GRID_W)
    q_start = np.clip(qcol - WIN_C // 2, 0, GRID_W - WIN_C)
    rel = kcol[None, :] - q_start[:, None]
    valid = (rel >= 0) & (rel < WIN_C)
    dc = np.clip(kcol[None, :] - qcol[:, None], -(WIN_C - 1), WIN_C - 1) + WIN_C - 1
    row_sel = np.zeros((8 * kr, 16), np.float32)
    row_sel[np.arange(8 * kr), dr.reshape(-1)] = 1.0
    col_sel = np.zeros((32, GRID_W * GRID_W), np.float32)
    qq, kk = np.nonzero(valid)
    col_sel[dc[qq, kk], qq * GRID_W + kk] = 1.0
    neg = np.where(valid, 0.0, NEG_INF).astype(np.float32).reshape(1, -1)
    rpb_p = jnp.pad(rpb.astype(F32), ((0, 0), (0, 16 - n_dr), (0, 32 - n_dc)))
    full = lambda a: pl.BlockSpec(a.shape, lambda h: (0,) * a.ndim)
    consts = [jnp.asarray(row_sel, BF16), jnp.asarray(col_sel, BF16), jnp.asarray(neg)]
    b = pl.pallas_call(
        _na_bias_kernel,
        grid=(nh,),
        in_specs=[pl.BlockSpec((1, 16, 32), lambda h: (h, 0, 0))] + [full(a) for a in consts],
        out_specs=pl.BlockSpec((1, 8 * kr, GRID_W * GRID_W), lambda h: (h, 0, 0)),
        out_shape=jax.ShapeDtypeStruct((nh, 8 * kr, GRID_W * GRID_W), F32),
        compiler_params=_cparams("parallel"),
        name="na_bias",
    )(rpb_p, *consts)
    b = b.reshape(nh, 8, kr, GRID_W, GRID_W)
    return b.transpose(0, 1, 3, 2, 4).reshape(nh, 8, GRID_W, kr * GRID_W)


NA_ROWS_PER_STEP = 4


def _na_latent_kernel(q_ref, k_ref, v_ref, ck_ref, cv_ref, b_ref, o_ref, *, rows):
    scale = NA_HEAD_DIM ** -0.5
    kr = min(WIN_R, rows)

    def query_row(r, hh):
        start = jnp.clip(r - kr // 2, 0, rows - kr)
        typ = jnp.where(r < 4, r, jnp.where(r > rows - 4, r - (rows - 8), 4))
        q0 = pl.multiple_of(r * GRID_W, GRID_W)
        k0 = pl.multiple_of(start * GRID_W, GRID_W)
        sl = slice(hh * NA_HEAD_DIM, (hh + 1) * NA_HEAD_DIM)
        q = q_ref[pl.ds(q0, GRID_W), sl]
        kw = k_ref[pl.ds(k0, kr * GRID_W), sl]
        vw = v_ref[pl.ds(k0, kr * GRID_W), sl]
        s_w = _dot_nt(q, kw) * scale + b_ref[hh, typ]
        s_c = _dot_nt(q, ck_ref[0, hh]) * scale
        yield
        m = jnp.maximum(jnp.max(s_w, axis=-1, keepdims=True), jnp.max(s_c, axis=-1, keepdims=True))
        p_w = jnp.exp(s_w - m)
        p_c = jnp.exp(s_c - m)
        l = jnp.sum(p_w, axis=-1, keepdims=True) + jnp.sum(p_c, axis=-1, keepdims=True)
        yield
        o_ref[pl.ds(q0, GRID_W), sl] = (_dot(p_w, vw) + _dot(p_c, cv_ref[0, hh])) / l

    def body(rp, carry):
        _lockstep([query_row(rp * NA_ROWS_PER_STEP + u, hh) for u in range(NA_ROWS_PER_STEP) for hh in range(2)])
        return carry

    lax.fori_loop(0, rows // NA_ROWS_PER_STEP, body, 0)


def _na_latent(z, ctx_k, ctx_v, bias, n_seq, seq_len, row_blk0):
    cb = OFF_NA // 128
    rows = seq_len // GRID_W
    past = ctx_k.shape[2]
    return pl.pallas_call(
        functools.partial(_na_latent_kernel, rows=rows),
        grid=(n_seq, NA_HEADS // 2),
        in_specs=[pl.BlockSpec((seq_len, 128), lambda b, h: (row_blk0 + b, cb + h)),
                  pl.BlockSpec((seq_len, 128), lambda b, h: (row_blk0 + b, cb + 4 + h)),
                  pl.BlockSpec((seq_len, 128), lambda b, h: (row_blk0 + b, cb + 8 + h)),
                  pl.BlockSpec((1, 2, past, NA_HEAD_DIM), lambda b, h: (b, h, 0, 0)),
                  pl.BlockSpec((1, 2, past, NA_HEAD_DIM), lambda b, h: (b, h, 0, 0)),
                  pl.BlockSpec((2, 8, GRID_W, bias.shape[-1]), lambda b, h: (h, 0, 0, 0))],
        out_specs=pl.BlockSpec((seq_len, 128), lambda b, h: (b, h)),
        out_shape=jax.ShapeDtypeStruct((n_seq * seq_len, NA_HEADS * NA_HEAD_DIM), F32),
        compiler_params=_cparams("parallel", "parallel"),
        name="na_latent",
    )(z, z, z, ctx_k, ctx_v, bias)


def _ret_decay_tables(decay_logit):
    c = RET_CHUNK
    lg = jnp.log(jax.nn.sigmoid(decay_logit.astype(F32)))[:, :, None, None]
    i = jnp.arange(c, dtype=F32)
    diff = i[:, None] - i[None, :]
    m_f = jnp.where(diff >= 0, jnp.exp(jnp.maximum(diff, 0.0) * lg[0]), 0.0)
    m_b = jnp.where(diff <= 0, jnp.exp(jnp.maximum(-diff, 0.0) * lg[1]), 0.0)
    mask = jnp.stack([m_f, m_b])
    ones = jnp.ones((1, 1, 1, RET_DK), F32)
    q_f = jnp.exp((i + 1.0)[None, :, None] * lg[0]) * ones[0]
    q_b = jnp.exp((c - i)[None, :, None] * lg[1]) * ones[0]
    k_f = jnp.exp((c - 1.0 - i)[None, :, None] * lg[0]) * ones[0]
    k_b = jnp.exp(i[None, :, None] * lg[1]) * ones[0]
    c_dec = jnp.exp(c * lg) * jnp.ones((1, 1, RET_DK, RET_DV), F32)
    return mask, jnp.stack([q_f, q_b]), jnp.stack([k_f, k_b]), c_dec


def _rope_tables(seq_len):
    quarter = RET_DK // 4
    pos = np.arange(seq_len)
    freqs = 1.0 / (ROPE_BASE ** (jnp.arange(quarter, dtype=F32) / quarter))
    a_row = jnp.asarray(pos // GRID_W, F32)[:, None] * freqs[None, :]
    a_col = jnp.asarray(pos % GRID_W, F32)[:, None] * freqs[None, :]
    cos = jnp.concatenate([jnp.cos(a_row)] * 2 + [jnp.cos(a_col)] * 2, axis=-1)
    sin = jnp.concatenate([-jnp.sin(a_row), jnp.sin(a_row), -jnp.sin(a_col), jnp.sin(a_col)], axis=-1)
    return jnp.tile(cos, (1, RET_HEADS)), jnp.tile(sin, (1, RET_HEADS))


def _retention_kernel(*refs, seq_len, rope):
    if rope:
        (q_ref, k_ref, v_ref, g_ref, s0_ref, mask_ref, qd_ref, kd_ref, cd_ref, gnw_ref, gnb_ref,
         cos_ref, sin_ref, o_ref, sfin_ref, qs_ref, ks_ref, acc_ref) = refs
    else:
        (q_ref, k_ref, v_ref, g_ref, s0_ref, mask_ref, qd_ref, kd_ref, cd_ref, gnw_ref, gnb_ref,
         o_ref, sfin_ref, qs_ref, ks_ref, acc_ref) = refs
    c = RET_CHUNK
    n = seq_len // c
    quarter = RET_DK // 4
    q = q_ref[...]
    k = k_ref[...] * (RET_DK ** -0.5)
    if rope:
        lane = lax.broadcasted_iota(jnp.int32, q.shape, 1)
        first = (lane % (2 * quarter)) < quarter
        width = q.shape[1]

        def rot(x):
            swapped = jnp.where(first, pltpu.roll(x, width - quarter, 1), pltpu.roll(x, quarter, 1))
            return x * cos_ref[...] + swapped * sin_ref[...]

        q = rot(q)
        k = rot(k)
    qs_ref[...] = q
    ks_ref[...] = k

    sfin_ref[...] = s0_ref[...]

    def chain(h, d, ci):
        ksl = slice(h * RET_DK, (h + 1) * RET_DK)
        vsl = slice(h * RET_DV, (h + 1) * RET_DV)
        cc = ci if d == 0 else n - 1 - ci
        rows = pl.ds(pl.multiple_of(cc * c, c), c)
        qc = qs_ref[rows, ksl]
        kc = ks_ref[rows, ksl]
        vc = v_ref[rows, vsl]
        att = _dot_nt(qc, kc) * mask_ref[d, h]
        s = sfin_ref[0, d, h]
        yield
        acc_ref[d, rows, vsl] = _dot(att, vc) + _dot(qc * qd_ref[d, h], s)
        yield
        sfin_ref[0, d, h] = s * cd_ref[d, h] + _dot_tn(kc * kd_ref[d, h], vc)

    def step(ci, carry):
        _lockstep([chain(h, d, ci) for h in range(RET_HEADS) for d in range(2)])
        return carry

    lax.fori_loop(0, n, step, 0)

    for h in range(RET_HEADS):
        vsl = slice(h * RET_DV, (h + 1) * RET_DV)
        y = _ln(acc_ref[0, :, vsl] + acc_ref[1, :, vsl], RET_GN_EPS) * gnw_ref[:, vsl] + gnb_ref[:, vsl]
        g = g_ref[:, vsl]
        o_ref[:, vsl] = y * (g * _sigmoid(g))


def _retention(z, s0, tables, gn_w, gn_b, n_seq, seq_len, row_blk0, rope_tabs):
    mask, qd, kd, cd = tables
    rope = rope_tabs is not None
    qk_w = RET_HEADS * RET_DK
    v_w = RET_HEADS * RET_DV
    full = lambda a: pl.BlockSpec(a.shape, lambda b: (0,) * a.ndim)
    in_specs = [pl.BlockSpec((seq_len, qk_w), lambda b: (row_blk0 + b, OFF_RET_QK // qk_w)),
                pl.BlockSpec((seq_len, qk_w), lambda b: (row_blk0 + b, OFF_RET_QK // qk_w + 1)),
                pl.BlockSpec((seq_len, v_w), lambda b: (row_blk0 + b, OFF_RET_V // v_w)),
                pl.BlockSpec((seq_len, v_w), lambda b: (row_blk0 + b, OFF_RET_V // v_w + 1)),
                pl.BlockSpec((1, 2, RET_HEADS, RET_DK, RET_DV), lambda b: (b, 0, 0, 0, 0)),
                full(mask), full(qd), full(kd), full(cd), full(gn_w), full(gn_b)]
    args = [z, z, z, z, s0, mask, qd, kd, cd, gn_w, gn_b]
    if rope:
        in_specs += [full(rope_tabs[0]), full(rope_tabs[1])]
        args += list(rope_tabs)
    return pl.pallas_call(
        functools.partial(_retention_kernel, seq_len=seq_len, rope=rope),
        grid=(n_seq,),
        in_specs=in_specs,
        out_specs=[pl.BlockSpec((seq_len, v_w), lambda b: (b, 0)),
                   pl.BlockSpec((1, 2, RET_HEADS, RET_DK, RET_DV), lambda b: (b, 0, 0, 0, 0))],
        out_shape=[jax.ShapeDtypeStruct((n_seq * seq_len, v_w), F32),
                   jax.ShapeDtypeStruct((n_seq, 2, RET_HEADS, RET_DK, RET_DV), F32)],
        scratch_shapes=[pltpu.VMEM((seq_len, qk_w), F32), pltpu.VMEM((seq_len, qk_w), F32),
                        pltpu.VMEM((2, seq_len, v_w), F32)],
        compiler_params=_cparams("parallel"),
        name="retention_rope" if rope else "retention",
    )(*args)


def _rwkv_prep_kernel(zm_ref, zl_ref, pm_ref, nm_ref, pl_ref, nl_ref, mum_ref, mul_ref,
                      w0_ref, wup_ref, a0_ref, aup_ref, gup_ref, kk_ref, ka_ref, rk_ref, ones_ref, tri_ref,
                      r_out, kap_out, v_out, lw_out, cs_out, kd_out, beta_out, g_out, bonus_out,
                      *, tiles_per_seq_of_tile):
    i = pl.program_id(0)
    pos, per = tiles_per_seq_of_tile(i)
    has_prev = (pos != 0).astype(F32)
    has_next = (pos != per - 1).astype(F32)
    tt = zm_ref.shape[0]

    def shift(z_ref, p_ref, n_ref, mu_ref):
        z = z_ref[...]
        row = lax.broadcasted_iota(jnp.int32, z.shape, 0)
        prev = jnp.where(row == 0, p_ref[7:8, :] * has_prev, pltpu.roll(z, 1, 0))
        nxt = jnp.where(row == tt - 1, n_ref[0:1, :] * has_next, pltpu.roll(z, tt - 1, 0))
        return z + mu_ref[...] * (0.5 * (prev + nxt) - z)

    zm = shift(zm_ref, pm_ref, nm_ref, mum_ref)
    zl = shift(zl_ref, pl_ref, nl_ref, mul_ref)
    w = RWKV_WIDTH
    r_c, k_c, v_c = zm[:, 0:w], zm[:, w:2 * w], zm[:, 2 * w:3 * w]
    ones = ones_ref[...]
    g_out[...] = _dot(_sigmoid(zl[:, 256:384]), gup_ref[...])
    kk = k_c * kk_ref[...]
    ss = _dot_exact_rhs(kk * kk, ones)
    kap = kk * lax.rsqrt(jnp.maximum(ss, 1e-24))
    bonus_out[...] = _dot_exact_rhs(r_c * k_c * rk_ref[...], ones) * v_c
    r_out[...] = r_c
    kap_out[...] = kap
    v_out[...] = v_c
    for d in range(2):
        wl = zl[:, d * 64:(d + 1) * 64]
        al = zl[:, 128 + d * 64:128 + (d + 1) * 64]
        lw = -RWKV_DECAY_SCALE * _sigmoid(w0_ref[d] + _dot(jnp.tanh(wl), wup_ref[d]))
        a = _sigmoid(a0_ref[d] + _dot(al, aup_ref[d]))
        k_d = k_c * (1.0 + (a - 1.0) * ka_ref[...])
        beta = kap * a
        cs = _dot_exact_lhs(tri_ref[d], lw)
        lw_out[d] = lw
        cs_out[d] = cs
        kd_out[d] = k_d
        beta_out[d] = beta


def _rwkv_prep(z, p, tiles_per_seq_of_tile, tt):
    m = z.shape[0]
    nt = m // tt
    w = RWKV_WIDTH
    mb, lb = OFF_RW_MAIN // (3 * w), OFF_RW_LORA // 384
    hm = tt // 8
    last8 = m // 8 - 1
    prev_idx = lambda i: jnp.maximum(i * hm - 1, 0)
    next_idx = lambda i: jnp.minimum((i + 1) * hm, last8)
    full = lambda a: pl.BlockSpec(a.shape, lambda i: (0,) * a.ndim)
    heads = jnp.arange(w) // RWKV_N
    ones = (heads[:, None] == heads[None, :]).astype(BF16)
    t = np.arange(tt)
    same = (t[:, None] // RWKV_CHUNK) == (t[None, :] // RWKV_CHUNK)
    tri = jnp.asarray(np.stack([same & (t[None, :] <= t[:, None]), same & (t[None, :] >= t[:, None])]), BF16)
    params = [p['mu_main'], p['mu_lora'], p['rwkv_w0'], p['rwkv_w_up'], p['rwkv_a0'], p['rwkv_a_up'],
              p['rwkv_g_up'], p['rwkv_k_k'], p['rwkv_k_a'], p['rwkv_r_k'], ones, tri]
    hd = lambda: jax.ShapeDtypeStruct((m, w), F32)
    dhd = lambda: jax.ShapeDtypeStruct((2, m, w), F32)
    tok = lambda: jax.ShapeDtypeStruct((m, w), F32)
    hd_spec = pl.BlockSpec((tt, w), lambda i: (i, 0))
    dhd_spec = pl.BlockSpec((2, tt, w), lambda i: (0, i, 0))
    tok_spec = pl.BlockSpec((tt, w), lambda i: (i, 0))
    return pl.pallas_call(
        functools.partial(_rwkv_prep_kernel, tiles_per_seq_of_tile=tiles_per_seq_of_tile),
        grid=(nt,),
        in_specs=[pl.BlockSpec((tt, 3 * w), lambda i: (i, mb)),
                  pl.BlockSpec((tt, 384), lambda i: (i, lb)),
                  pl.BlockSpec((8, 3 * w), lambda i: (prev_idx(i), mb)),
                  pl.BlockSpec((8, 3 * w), lambda i: (next_idx(i), mb)),
                  pl.BlockSpec((8, 384), lambda i: (prev_idx(i), lb)),
                  pl.BlockSpec((8, 384), lambda i: (next_idx(i), lb))] + [full(a) for a in params],
        out_specs=[hd_spec, hd_spec, hd_spec, dhd_spec, dhd_spec, dhd_spec, dhd_spec, tok_spec, tok_spec],
        out_shape=[hd(), hd(), hd(), dhd(), dhd(), dhd(), dhd(), tok(), tok()],
        compiler_params=_cparams("parallel"),
        name="rwkv_prep",
    )(z, z, z, z, z, z, *params)


def _dot3(a, b, dims):
    a_hi = a.astype(BF16)
    b_hi = b.astype(BF16)
    a_lo = (a - a_hi.astype(F32)).astype(BF16)
    b_lo = (b - b_hi.astype(F32)).astype(BF16)
    dn = (dims, ((), ()))
    out = lax.dot_general(a_lo, b_hi, dn, preferred_element_type=F32)
    out = out + lax.dot_general(a_hi, b_lo, dn, preferred_element_type=F32)
    return out + lax.dot_general(a_hi, b_hi, dn, preferred_element_type=F32)


_NN = ((1,), (0,))
_NT = ((1,), (1,))
_TN = ((0,), (0,))


def _dot1(a, b, dims):
    return lax.dot_general(a.astype(BF16), b.astype(BF16), (dims, ((), ())), preferred_element_type=F32)


def _dot_rhs2(a, b, dims):
    assert dims == _NN
    b_hi = b.astype(BF16)
    b_lo = (b - b_hi.astype(F32)).astype(BF16)
    out = jnp.dot(a.astype(BF16), jnp.concatenate([b_hi, b_lo], axis=1), preferred_element_type=F32)
    n = b.shape[1]
    return out[:, n:] + out[:, :n]


_RWKV_MM = _dot1
RWKV_CHAINS = 16
RWKV_MAX_GROUP = 8


def _rwkv_scan_kernel(r_ref, kap_ref, v_ref, lw_ref, cs_ref, kd_ref, beta_ref, h0_ref, o_ref, hfin_ref, *,
                      seq_len, group, heads):
    c = RWKV_CHUNK
    n = seq_len // c
    d = pl.program_id(0)
    sgn = 1 - 2 * d
    ti = lax.broadcasted_iota(jnp.int32, (c, c), 0) * sgn
    tj = lax.broadcasted_iota(jnp.int32, (c, c), 1) * sgn
    strict = tj < ti
    incl = tj <= ti
    eye_f = (ti == tj).astype(F32)

    def chunk_terms(hh, cc):
        rows = pl.ds(pl.multiple_of(cc * c, c), c)
        hl = slice(hh * RWKV_N, (hh + 1) * RWKV_N)
        r = r_ref[rows, hl]
        kap = kap_ref[rows, hl]
        v = v_ref[rows, hl]
        lw = lw_ref[0, rows, hl]
        cs = cs_ref[0, rows, hl]
        kd = kd_ref[0, rows, hl]
        beta = beta_ref[0, rows, hl]
        tot = jnp.sum(lw, axis=0, keepdims=True)
        kap_t = kap * jnp.exp(cs - lw)
        r_t = r * jnp.exp(cs)
        e_ncs = jnp.exp(-cs)
        k_t = kd * e_ncs
        b_t = beta * e_ncs
        e_end = jnp.exp(tot - cs)
        k_end = kd * e_end
        b_end = beta * e_end
        yield
        p = _RWKV_MM(jnp.concatenate([kap_t, r_t], axis=0), jnp.concatenate([b_t, k_t], axis=0), _NT)
        l_b = jnp.where(strict, p[:c, :c], 0.0)
        l_k = jnp.where(strict, p[:c, c:], 0.0)
        m_b = jnp.where(incl, p[c:, :c], 0.0)
        m_k = jnp.where(incl, p[c:, c:], 0.0)
        yield
        x = -l_b
        y = jnp.concatenate([kap_t, _RWKV_MM(l_k, v, _NN)], axis=1)
        yield
        y = y + _dot_rhs2(x, y, _NN)
        pw = x
        for _ in range(5):
            pw = _RWKV_MM(pw, pw, _NN)
            yield
            y = y + _dot_rhs2(pw, y, _NN)
        yield
        mby = _RWKV_MM(m_b, y, _NN)
        bty = _RWKV_MM(b_end, y, _TN)
        rr = r_t - mby[:, :RWKV_N]
        o0 = _RWKV_MM(m_k, v, _NN) - mby[:, RWKV_N:]
        pc = eye_f * jnp.exp(tot) - bty[:, :RWKV_N]
        qc = _RWKV_MM(k_end, v, _TN) - bty[:, RWKV_N:]
        return rows, rr, o0, pc, qc

    def step(gi, hstates):
        keys = [(hh, u) for u in range(group) for hh in range(heads)]
        terms = dict(zip(keys, _lockstep(
            [chunk_terms(hh, (gi * group + u) + d * (n - 1 - 2 * (gi * group + u))) for hh, u in keys])))
        hstates = list(hstates)
        for u in range(group):
            for hh in range(heads):
                rows, rr, o0, pc, qc = terms[(hh, u)]
                o_ref[0, rows, hh * RWKV_N:(hh + 1) * RWKV_N] = _dot3(rr, hstates[hh], _NN) + o0
                hstates[hh] = _dot3(pc, hstates[hh], _NN) + qc
        return tuple(hstates)

    fin = lax.fori_loop(0, n // group, step, tuple(h0_ref[0, 0, hh] for hh in range(heads)))
    for hh in range(heads):
        hfin_ref[0, 0, hh] = fin[hh]


def _rwkv_scan(r, kap, v, lw, cs, kd, beta, h0, n_seq, seq_len, row_blk0):
    group = min(seq_len // RWKV_CHUNK, RWKV_MAX_GROUP)
    hp = RWKV_CHAINS // group
    hd_spec = pl.BlockSpec((seq_len, hp * RWKV_N), lambda d, h, b: (row_blk0 + b, h))
    dhd_spec = pl.BlockSpec((1, seq_len, hp * RWKV_N), lambda d, h, b: (d, row_blk0 + b, h))
    st_spec = pl.BlockSpec((1, 1, hp, RWKV_N, RWKV_N), lambda d, h, b: (b, d, h, 0, 0))
    return pl.pallas_call(
        functools.partial(_rwkv_scan_kernel, seq_len=seq_len, group=group, heads=hp),
        grid=(2, RWKV_HEADS // hp, n_seq),
        in_specs=[hd_spec, hd_spec, hd_spec, dhd_spec, dhd_spec, dhd_spec, dhd_spec, st_spec],
        out_specs=[pl.BlockSpec((1, seq_len, hp * RWKV_N), lambda d, h, b: (d, b, h)), st_spec],
        out_shape=[jax.ShapeDtypeStruct((2, n_seq * seq_len, RWKV_WIDTH), F32),
                   jax.ShapeDtypeStruct((n_seq, 2, RWKV_HEADS, RWKV_N, RWKV_N), F32)],
        compiler_params=_cparams("parallel", "parallel", "parallel"),
        name="rwkv_scan",
    )(r, kap, v, lw, cs, kd, beta, h0)


def _merge_kernel(oscan_c_ref, oscan_l_ref, ona_c_ref, ona_l_ref, ort_c_ref, ort_l_ref, bonus_ref, grw_ref,
                  ga_ref, gb_ref, gc_ref, x_ref, gt_ref, gnw_ref, gnb_ref, wbr_ref, wout_ref, lng_ref, lnb_ref,
                  o_ref, orw_ref, *, ctx_tiles):
    is_ctx = pl.program_id(0) < ctx_tiles
    pick = lambda c_val, l_val: jnp.where(is_ctx, c_val, l_val)
    for h in range(RWKV_HEADS):
        sl = slice(h * RWKV_N, (h + 1) * RWKV_N)
        o = pick(oscan_c_ref[0, :, sl] + oscan_c_ref[1, :, sl], oscan_l_ref[0, :, sl] + oscan_l_ref[1, :, sl])
        y = _ln(o, RWKV_GN_EPS) * gnw_ref[h] + gnb_ref[h]
        orw_ref[:, sl] = (y + bonus_ref[:, sl]) * grw_ref[:, sl]
    merged = (_sigmoid(ga_ref[...]) * _dot(pick(ona_c_ref[...], ona_l_ref[...]), wbr_ref[0])
              + _sigmoid(gb_ref[...]) * _dot(pick(ort_c_ref[...], ort_l_ref[...]), wbr_ref[1])
              + _sigmoid(gc_ref[...]) * _dot(orw_ref[...], wbr_ref[2]))
    t = _dot(merged, wout_ref[...])
    o_ref[...] = _ln(DEEPNORM_ALPHA * x_ref[...] + gt_ref[0] * t) * lng_ref[...] + lnb_ref[...]


def _merge(oscan, o_na, o_rt, bonus, g_rw, z, x, mod3, p, row_of_tile, tm):
    m = x.shape[0]
    bw = RWKV_WIDTH
    ctx_tiles = o_na[0].shape[0] // tm
    full = lambda a: pl.BlockSpec(a.shape, lambda i: (0,) * a.ndim)
    tok = pl.BlockSpec((tm, bw), lambda i: (i, 0))
    c_row = lambda i: jnp.minimum(i, ctx_tiles - 1)
    l_row = lambda i: jnp.maximum(i - ctx_tiles, 0)
    tok_c = pl.BlockSpec((tm, bw), lambda i: (c_row(i), 0))
    tok_l = pl.BlockSpec((tm, bw), lambda i: (l_row(i), 0))
    params = [p['rwkv_gn_w'], p['rwkv_gn_b'], p['w_br'], p['w_out'], p['ln_a_g'], p['ln_a_b']]
    return pl.pallas_call(
        functools.partial(_merge_kernel, ctx_tiles=ctx_tiles),
        grid=(m // tm,),
        in_specs=[pl.BlockSpec((2, tm, bw), lambda i: (0, c_row(i), 0)),
                  pl.BlockSpec((2, tm, bw), lambda i: (0, l_row(i), 0)),
                  tok_c, tok_l, tok_c, tok_l, tok, tok,
                  pl.BlockSpec((tm, D_MODEL), lambda i: (i, 0)),
                  pl.BlockSpec((tm, D_MODEL), lambda i: (i, 1)),
                  pl.BlockSpec((tm, D_MODEL), lambda i: (i, 2)),
                  pl.BlockSpec((tm, D_MODEL), lambda i: (i, 0)),
                  pl.BlockSpec((1, 1, D_MODEL), lambda i: (row_of_tile(i, tm), 0, 2))]
                 + [full(a) for a in params],
        out_specs=pl.BlockSpec((tm, D_MODEL), lambda i: (i, 0)),
        out_shape=jax.ShapeDtypeStruct((m, D_MODEL), F32),
        scratch_shapes=[pltpu.VMEM((tm, bw), F32)],
        compiler_params=_cparams("parallel"),
        name="merge_out_proj",
    )(*oscan, *o_na, *o_rt, bonus, g_rw, z, z, z, x, mod3, *params)


def _top_values(s, k, with_rank=False):
    out = []
    rank = jnp.full(s.shape, float(k), F32)
    for b in range(k):
        m = jnp.max(s, axis=0, keepdims=True)
        out.append(m)
        hit = s == m
        if with_rank:
            rank = jnp.where(hit, float(b), rank)
        s = jnp.where(hit, -jnp.inf, s)
    return (out, rank) if with_rank else out


def _peer_gate_kernel(x_ref, sh_ref, sc_ref, wq_ref, keys_ref, hb_ref, cnt_ref, rank1_ref, e0_ref, e1_ref):
    h = _ln(x_ref[...]) * (1.0 + sc_ref[0]) + sh_ref[0]
    hb_ref[...] = h.astype(BF16)
    ht = h.T.astype(BF16)
    qt = jnp.dot(wq_ref[...], ht, preferred_element_type=F32)
    half = keys_ref.shape[-1]
    for hd in range(PEER_HEADS):
        scores, tops = [], []
        for part in range(2):
            row0 = (hd * 2 + part) * half
            s = _dot(keys_ref[hd, part], qt[row0:row0 + half, :])
            scores.append(s)
            if part == 0:
                tops.append(_top_values(s, PEER_TOPK))
            else:
                top1, rank1 = _top_values(s, PEER_TOPK, with_rank=True)
                tops.append(top1)
        pairs = [tops[0][a] + tops[1][b] for a in range(PEER_TOPK) for b in range(PEER_TOPK // (a + 1))]
        pairs += [jnp.full_like(pairs[0], -jnp.inf)] * (-len(pairs) % 8)
        best = _top_values(jnp.concatenate(pairs, axis=0), PEER_TOPK)
        zsum = jnp.exp(best[0] - best[0])
        for t in range(1, PEER_TOPK):
            zsum = zsum + jnp.exp(best[t] - best[0])
        tau = best[PEER_TOPK - 1]
        cnt = jnp.zeros(scores[0].shape, F32)
        for b, t1 in enumerate(tops[1]):
            cnt = jnp.where(scores[0] + t1 >= tau, float(b + 1), cnt)
        cnt_ref[hd] = cnt
        rank1_ref[hd] = rank1.astype(BF16)
        e0_ref[hd] = jnp.exp(scores[0] - tops[0][0])
        e1_ref[hd] = (jnp.exp(scores[1] - tops[1][0]) / zsum).astype(BF16)


def _peer_gate(x, mod3, wq_t, keys, row_of_tile, tn):
    m = x.shape[0]
    full = lambda a: pl.BlockSpec(a.shape, lambda i: (0,) * a.ndim)
    sk = pl.BlockSpec((PEER_HEADS, PEER_NKEYS, tn), lambda i: (0, 0, i))
    sk_f32 = jax.ShapeDtypeStruct((PEER_HEADS, PEER_NKEYS, m), F32)
    sk_bf16 = jax.ShapeDtypeStruct((PEER_HEADS, PEER_NKEYS, m), BF16)
    return pl.pallas_call(
        _peer_gate_kernel,
        grid=(m // tn,),
        in_specs=[pl.BlockSpec((tn, D_MODEL), lambda i: (i, 0)),
                  pl.BlockSpec((1, 1, D_MODEL), lambda i: (row_of_tile(i, tn), 0, 3)),
                  pl.BlockSpec((1, 1, D_MODEL), lambda i: (row_of_tile(i, tn), 0, 4)),
                  full(wq_t), full(keys)],
        out_specs=[pl.BlockSpec((tn, D_MODEL), lambda i: (i, 0)), sk, sk, sk, sk],
        out_shape=[jax.ShapeDtypeStruct((m, D_MODEL), BF16), sk_f32, sk_bf16, sk_f32, sk_bf16],
        compiler_params=_cparams("parallel"),
        name="peer_gate",
    )(x, mod3, mod3, wq_t, keys)


def _gelu_exact(x):
    return 0.5 * x * (1.0 + lax.erf(x * (2.0 ** -0.5)))


PEER_SUB_ROWS = 32
PEER_SUB_LANES = 256
PEER_GROUP_KEYS = 8
PEER_GROUP = PEER_GROUP_KEYS * PEER_NKEYS


def _peer_expert_kernel(hb_ref, cnt_ref, rank1_ref, e0_ref, e1_ref, u_ref, vt_ref, x_ref, gt_ref,
                        lng_ref, lnb_ref, o_ref, acc_ref, st_ref, st2_ref, act_ref, *, groups):
    e = pl.program_id(1)
    tn = hb_ref.shape[0]

    @pl.when(e == 0)
    def _():
        acc_ref[...] = jnp.zeros_like(acc_ref)

    def pre_activations(g, dst_ref):
        rows = pl.ds(pl.multiple_of(g * PEER_GROUP, PEER_GROUP), PEER_GROUP)
        dst_ref[...] = lax.dot_general(u_ref[rows, :], hb_ref[...], (_NT, ((), ())), preferred_element_type=F32)

    def gated_activations(g, src_ref):
        row0 = pl.multiple_of(g * PEER_GROUP, PEER_GROUP)
        for r in range(PEER_GROUP_KEYS):
            i = (e * groups + g) * PEER_GROUP_KEYS + r
            cnt_row = [cnt_ref[hd, pl.ds(i, 1), :] for hd in range(PEER_HEADS)]
            e0_row = [e0_ref[hd, pl.ds(i, 1), :] for hd in range(PEER_HEADS)]
            for lb in range(tn // PEER_SUB_LANES):
                ls = slice(lb * PEER_SUB_LANES, (lb + 1) * PEER_SUB_LANES)
                sub = (PEER_SUB_ROWS, PEER_SUB_LANES)
                cnt = [jnp.broadcast_to(c[:, ls].astype(BF16), sub) for c in cnt_row]
                e0 = [jnp.broadcast_to(c[:, ls].astype(BF16), sub) for c in e0_row]
                zero = jnp.zeros(sub, BF16)
                for rb in range(PEER_NKEYS // PEER_SUB_ROWS):
                    js = slice(rb * PEER_SUB_ROWS, (rb + 1) * PEER_SUB_ROWS)
                    gate = None
                    for hd in range(PEER_HEADS):
                        term = jnp.where(rank1_ref[hd, js, ls] < cnt[hd], e1_ref[hd, js, ls], zero) * e0[hd]
                        gate = term if gate is None else gate + term
                    off = r * PEER_NKEYS + rb * PEER_SUB_ROWS
                    act = _gelu_exact(src_ref[off:off + PEER_SUB_ROWS, ls]).astype(BF16) * gate
                    act_ref[pl.ds(row0 + off, PEER_SUB_ROWS), ls] = act

    scratch = (st_ref, st2_ref)
    pre_activations(0, scratch[0])
    for g in range(groups):
        if g + 1 < groups:
            pre_activations(g + 1, scratch[(g + 1) % 2])
        gated_activations(g, scratch[g % 2])
        cols = slice(g * PEER_GROUP, (g + 1) * PEER_GROUP)
        acc_ref[...] += jnp.dot(vt_ref[:, cols], act_ref[cols, :], preferred_element_type=F32)

    @pl.when(e == pl.num_programs(1) - 1)
    def _():
        f = acc_ref[...].T
        o_ref[...] = _ln(DEEPNORM_ALPHA * x_ref[...] + gt_ref[0] * f) * lng_ref[...] + lnb_ref[...]


def _peer_expert(hb, cnt, rank1, e0, e1, u, v_t, x, mod3, ln_g, ln_b, row_of_tile, tn, eb):
    m = x.shape[0]
    n_exp = u.shape[0]
    groups = eb // PEER_GROUP
    sk = pl.BlockSpec((PEER_HEADS, PEER_NKEYS, tn), lambda i, e: (0, 0, i))
    full = lambda a: pl.BlockSpec(a.shape, lambda i, e: (0,) * a.ndim)
    return pl.pallas_call(
        functools.partial(_peer_expert_kernel, groups=groups),
        grid=(m // tn, n_exp // eb),
        in_specs=[pl.BlockSpec((tn, D_MODEL), lambda i, e: (i, 0)), sk, sk, sk, sk,
                  pl.BlockSpec((eb, D_MODEL), lambda i, e: (e, 0)),
                  pl.BlockSpec((D_MODEL, eb), lambda i, e: (0, e)),
                  pl.BlockSpec((tn, D_MODEL), lambda i, e: (i, 0)),
                  pl.BlockSpec((1, 1, D_MODEL), lambda i, e: (row_of_tile(i, tn), 0, 5)),
                  full(ln_g), full(ln_b)],
        out_specs=pl.BlockSpec((tn, D_MODEL), lambda i, e: (i, 0)),
        out_shape=jax.ShapeDtypeStruct((m, D_MODEL), F32),
        scratch_shapes=[pltpu.VMEM((D_MODEL, tn), F32), pltpu.VMEM((PEER_GROUP, tn), F32),
                        pltpu.VMEM((PEER_GROUP, tn), F32), pltpu.VMEM((eb, tn), BF16)],
        compiler_params=_cparams("parallel", "arbitrary"),
        name="peer_experts",
    )(hb, cnt, rank1, e0, e1, u, v_t, x, mod3, ln_g, ln_b)


def _permute_in_proj(w_in):
    na, qk, rv = 3 * 512, 2 * 256, 2 * 512
    o_na, o_qk, o_rv, o_rw, o_g = 0, na, na + qk, na + qk + rv, na + qk + rv + 1920
    pad = jnp.zeros((w_in.shape[0], P_IN_PADDED - P_IN), w_in.dtype)
    return jnp.concatenate([w_in[:, o_g:], w_in[:, o_na:o_qk], w_in[:, o_qk:o_rv], w_in[:, o_rv:o_rw],
                            w_in[:, o_rw:o_g], pad], axis=1)


def kernel(x_prompt, x_sample, cache_na_k, cache_na_v, state_ret, state_rwkv, c, c_ctx, w_mod, b_mod, w_in, na_rpb, ret_decay_logit, ret_gn_w, ret_gn_b, rwkv_mu, rwkv_w0, rwkv_w_up, rwkv_a0, rwkv_a_up, rwkv_g_up, rwkv_k_k, rwkv_k_a, rwkv_r_k, rwkv_gn_w, rwkv_gn_b, w_br, w_out, ln_a_g, ln_a_b, ln_f_g, ln_f_b, peer_wq, peer_keys, peer_u, peer_v):
    n_ctx, ctx_len, d = x_prompt.shape
    n_lat, lat_len, _ = x_sample.shape
    m_ctx = n_ctx * ctx_len
    assert m_ctx % lat_len == 0 and n_lat == 2
    lat_blk0 = m_ctx // lat_len

    def row_of_tile(i, tm):
        return jnp.where(i < m_ctx // tm, 0, 1 + (i - m_ctx // tm) // (lat_len // tm))

    prep_tt = 256
    assert ctx_len == prep_tt

    def tiles_per_seq_of_tile(i):
        lat = i >= m_ctx // prep_tt
        per = jnp.where(lat, lat_len // prep_tt, 1)
        pos = jnp.where(lat, (i - m_ctx // prep_tt) % (lat_len // prep_tt), 0)
        return pos, per

    x = jnp.concatenate([x_prompt.reshape(m_ctx, d), x_sample.reshape(n_lat * lat_len, d)], axis=0)
    c8 = jnp.concatenate([c_ctx[None], c, jnp.zeros((8 - 1 - n_lat, d), F32)], axis=0)
    mod = _modulation(c8, w_mod, b_mod)
    rope_tabs = _rope_tables(lat_len)
    zeros_ret = jnp.zeros((n_ctx, 2, RET_HEADS, RET_DK, RET_DV), F32)
    zeros_rw = jnp.zeros((n_ctx, 2, RWKV_HEADS, RWKV_N, RWKV_N), F32)

    nk, nv, sr, sw = [], [], [], []
    for l in range(DEPTH):
        mod3 = mod[l].reshape(8, 1, 6 * d)
        w_in_p = _permute_in_proj(w_in[l]).astype(BF16)
        z = _lnmod_matmul(x, mod3, w_in_p, row_of_tile, 1024, 2048, 0, 1)

        o_na_c, k_cache, v_cache = _ctx_attention(z, n_ctx, ctx_len)
        o_na_l = _na_latent(z, cache_na_k[:, l], cache_na_v[:, l], _na_bias_table(na_rpb[l], lat_len // GRID_W),
                            n_lat, lat_len, lat_blk0)
        nk.append(k_cache)
        nv.append(v_cache)

        tables = _ret_decay_tables(ret_decay_logit[l])
        gn_w, gn_b = ret_gn_w[l][None], ret_gn_b[l][None]
        o_rt_c, s_ret = _retention(z, zeros_ret, tables, gn_w, gn_b, n_ctx, ctx_len, 0, None)
        o_rt_l, _ = _retention(z, state_ret[:, l], tables, gn_w, gn_b, n_lat, lat_len, lat_blk0, rope_tabs)
        sr.append(s_ret)

        mu = rwkv_mu[l]
        prm = {
            'mu_main': mu[None, :1536], 'mu_lora': mu[None, 1536:],
            'rwkv_w0': rwkv_w0[l][:, None], 'rwkv_w_up': rwkv_w_up[l].astype(BF16),
            'rwkv_a0': rwkv_a0[l][:, None], 'rwkv_a_up': rwkv_a_up[l].astype(BF16),
            'rwkv_g_up': rwkv_g_up[l].astype(BF16), 'rwkv_k_k': rwkv_k_k[l][None],
            'rwkv_k_a': rwkv_k_a[l][None], 'rwkv_r_k': rwkv_r_k[l].reshape(1, RWKV_WIDTH),
        }
        r, kap, v, lw, cs, kd, beta, g_rw, bonus = _rwkv_prep(z, prm, tiles_per_seq_of_tile, prep_tt)
        o_c, h_fin = _rwkv_scan(r, kap, v, lw, cs, kd, beta, zeros_rw, n_ctx, ctx_len, 0)
        h0_lat = jnp.swapaxes(state_rwkv[:, l], -1, -2)
        o_l, _ = _rwkv_scan(r, kap, v, lw, cs, kd, beta, h0_lat, n_lat, lat_len, lat_blk0)
        sw.append(jnp.swapaxes(h_fin, -1, -2))

        mp = {
            'rwkv_gn_w': rwkv_gn_w[l].reshape(RWKV_HEADS, 1, RWKV_N),
            'rwkv_gn_b': rwkv_gn_b[l].reshape(RWKV_HEADS, 1, RWKV_N),
            'w_br': w_br[l].astype(BF16), 'w_out': w_out[l].astype(BF16),
            'ln_a_g': ln_a_g[l][None], 'ln_a_b': ln_a_b[l][None],
        }
        x = _merge((o_c, o_l), (o_na_c, o_na_l), (o_rt_c, o_rt_l), bonus, g_rw, z, x, mod3, mp, row_of_tile, 256)

        wq_t = peer_wq[l].T.astype(BF16)
        hb, cnt, rank1, e0, e1 = _peer_gate(x, mod3, wq_t, peer_keys[l].astype(BF16), row_of_tile, 256)
        x = _peer_expert(hb, cnt, rank1, e0, e1, peer_u[l].astype(BF16), peer_v[l].T.astype(BF16), x, mod3,
                         ln_f_g[l][None], ln_f_b[l][None], row_of_tile, 512, 2048)

    dt = x_prompt.dtype
    y_prompt = x[:m_ctx].reshape(n_ctx, ctx_len, d)
    y_sample = x[m_ctx:].reshape(n_lat, lat_len, d)
    return (y_prompt, y_sample, jnp.stack(nk, axis=1).astype(dt), jnp.stack(nv, axis=1).astype(dt),
            jnp.stack(sr, axis=1).astype(dt), jnp.stack(sw, axis=1).astype(dt))
```

```python
import functools
import math

import jax
import jax.numpy as jnp
import numpy as np
from jax import lax
from jax.experimental import pallas as pl
from jax.experimental.pallas import tpu as pltpu

F32 = jnp.float32
BF16 = jnp.bfloat16

D_MODEL = 1024
DEPTH = 2
GRID_W = 64
NA_HEADS = 8
NA_HEAD_DIM = 64
WIN_R = 8
WIN_C = 16
NEG_INF = -1e30
RET_HEADS = 4
RET_DK = 64
RET_DV = 128
RET_CHUNK = 64
RET_GN_EPS = 1e-5
ROPE_BASE = 10000.0
RWKV_HEADS = 8
RWKV_N = 64
RWKV_WIDTH = 512
RWKV_DECAY_SCALE = 0.606531
RWKV_GN_EPS = 64e-5
RWKV_CHUNK = 64
PEER_HEADS = 8
PEER_NKEYS = 128
PEER_TOPK = 16
LN_EPS = 1e-5
DEEPNORM_ALPHA = (2 * DEPTH) ** 0.25

OFF_GATE = 0
OFF_NA = 3072
OFF_RET_QK = 4608
OFF_RET_V = 5120
OFF_RW_MAIN = 6144
OFF_RW_LORA = 7680
P_IN = 8064
P_IN_PADDED = 8192

VMEM_LIMIT_BYTES = 56 * 1024 * 1024


def _cparams(*sem):
    return pltpu.CompilerParams(dimension_semantics=sem, vmem_limit_bytes=VMEM_LIMIT_BYTES)


def _ln(x, eps=LN_EPS):
    mu = jnp.mean(x, axis=-1, keepdims=True)
    xc = x - mu
    var = jnp.mean(xc * xc, axis=-1, keepdims=True)
    return xc * lax.rsqrt(var + eps)


def _sigmoid(x):
    return 1.0 / (1.0 + jnp.exp(-x))


def _dot(a, b):
    return jnp.dot(a.astype(BF16), b.astype(BF16), preferred_element_type=F32)


def _dot_nt(a, b):
    return lax.dot_general(a.astype(BF16), b.astype(BF16), (((1,), (1,)), ((), ())),
                           preferred_element_type=F32)


def _dot_tn(a, b):
    return lax.dot_general(a.astype(BF16), b.astype(BF16), (((0,), (0,)), ((), ())),
                           preferred_element_type=F32)


def _lockstep(gens):
    out = [None] * len(gens)
    live = list(range(len(gens)))
    while live:
        for i in list(live):
            try:
                next(gens[i])
            except StopIteration as done:
                out[i] = done.value
                live.remove(i)
    return out


def _split3(x):
    hi = x.astype(BF16)
    r1 = x - hi.astype(F32)
    mid = r1.astype(BF16)
    lo = (r1 - mid.astype(F32)).astype(BF16)
    return hi, mid, lo


def _dot_exact_lhs(sel, x):
    hi, mid, lo = _split3(x)
    s = sel.astype(BF16)
    out = jnp.dot(s, lo, preferred_element_type=F32)
    out = out + jnp.dot(s, mid, preferred_element_type=F32)
    return out + jnp.dot(s, hi, preferred_element_type=F32)


def _dot_exact_rhs(x, sel):
    hi, mid, lo = _split3(x)
    s = sel.astype(BF16)
    out = jnp.dot(lo, s, preferred_element_type=F32)
    out = out + jnp.dot(mid, s, preferred_element_type=F32)
    return out + jnp.dot(hi, s, preferred_element_type=F32)


def _mod_kernel(c_ref, w_ref, b_ref, o_ref):
    c = c_ref[...]
    s = c * _sigmoid(c)
    o_ref[0] = _dot(s, w_ref[0]) + b_ref[0]


def _modulation(c8, w_mod, b_mod):
    tn = 1536
    n = w_mod.shape[-1]
    return pl.pallas_call(
        _mod_kernel,
        grid=(DEPTH, n // tn),
        in_specs=[pl.BlockSpec((8, D_MODEL), lambda l, j: (0, 0)),
                  pl.BlockSpec((1, D_MODEL, tn), lambda l, j: (l, 0, j)),
                  pl.BlockSpec((1, 1, tn), lambda l, j: (l, 0, j))],
        out_specs=pl.BlockSpec((1, 8, tn), lambda l, j: (l, 0, j)),
        out_shape=jax.ShapeDtypeStruct((DEPTH, 8, n), F32),
        compiler_params=_cparams("parallel", "parallel"),
        name="modulation",
    )(c8, w_mod, b_mod.reshape(DEPTH, 1, n))


def _lnmod_matmul_kernel(x_ref, sh_ref, sc_ref, w_ref, o_ref, h_ref):
    i = pl.program_id(1)

    @pl.when(pl.program_id(0) == 0)
    def _():
        h = _ln(x_ref[...]) * (1.0 + sc_ref[0]) + sh_ref[0]
        h_ref[i] = h.astype(BF16)

    o_ref[...] = jnp.dot(h_ref[i], w_ref[...], preferred_element_type=F32)


def _lnmod_matmul(x, mod3, w, row_of_tile, tm, tn, sh_blk, sc_blk):
    m = x.shape[0]
    n = w.shape[1]
    last = m // tm - 1
    x_row = lambda j, i: jnp.where(j == 0, i, last)
    return pl.pallas_call(
        _lnmod_matmul_kernel,
        grid=(n // tn, m // tm),
        in_specs=[pl.BlockSpec((tm, D_MODEL), lambda j, i: (x_row(j, i), 0)),
                  pl.BlockSpec((1, 1, D_MODEL), lambda j, i: (row_of_tile(i, tm), 0, sh_blk)),
                  pl.BlockSpec((1, 1, D_MODEL), lambda j, i: (row_of_tile(i, tm), 0, sc_blk)),
                  pl.BlockSpec((D_MODEL, tn), lambda j, i: (0, j))],
        out_specs=pl.BlockSpec((tm, tn), lambda j, i: (i, j)),
        out_shape=jax.ShapeDtypeStruct((m, n), F32),
        scratch_shapes=[pltpu.VMEM((m // tm, tm, D_MODEL), BF16)],
        compiler_params=_cparams("arbitrary", "arbitrary"),
        name="adaln_in_proj",
    )(x, mod3, mod3, w)


def _ctx_attn_kernel(q_ref, k_ref, v_ref, o_ref, kc_ref, vc_ref):
    scale = NA_HEAD_DIM ** -0.5
    half = q_ref.shape[0] // 2

    def head(hh, q0):
        sl = slice(hh * NA_HEAD_DIM, (hh + 1) * NA_HEAD_DIM)
        k = k_ref[:, sl]
        v = v_ref[:, sl]
        if q0 == 0:
            kc_ref[0, hh] = k
            vc_ref[0, hh] = v
        s = _dot_nt(q_ref[q0:q0 + half, sl], k) * scale
        yield
        m = jnp.max(s, axis=-1, keepdims=True)
        p = jnp.exp(s - m)
        l = jnp.sum(p, axis=-1, keepdims=True)
        yield
        o_ref[q0:q0 + half, sl] = _dot(p, v) / l

    _lockstep([head(hh, q0) for hh in range(2) for q0 in (0, half)])


def _ctx_attention(z, n_seq, seq_len):
    cb = OFF_NA // 128
    cache_spec = pl.BlockSpec((1, 2, seq_len, NA_HEAD_DIM), lambda b, h: (b, h, 0, 0))
    cache_shape = jax.ShapeDtypeStruct((n_seq, NA_HEADS, seq_len, NA_HEAD_DIM), F32)
    return pl.pallas_call(
        _ctx_attn_kernel,
        grid=(n_seq, NA_HEADS // 2),
        in_specs=[pl.BlockSpec((seq_len, 128), lambda b, h: (b, cb + h)),
                  pl.BlockSpec((seq_len, 128), lambda b, h: (b, cb + 4 + h)),
                  pl.BlockSpec((seq_len, 128), lambda b, h: (b, cb + 8 + h))],
        out_specs=[pl.BlockSpec((seq_len, 128), lambda b, h: (b, h)), cache_spec, cache_spec],
        out_shape=[jax.ShapeDtypeStruct((n_seq * seq_len, NA_HEADS * NA_HEAD_DIM), F32), cache_shape, cache_shape],
        compiler_params=_cparams("parallel", "parallel"),
        name="ctx_attention",
    )(z, z, z)


def _na_bias_kernel(rpb_ref, rowsel_ref, colsel_ref, neg_ref, o_ref):
    picked = _dot_exact_lhs(rowsel_ref[...], rpb_ref[0])
    o_ref[0] = _dot_exact_rhs(picked, colsel_ref[...]) + neg_ref[...]


def _na_bias_table(rpb, rows):
    kr = min(WIN_R, rows)
    nh, n_dr, n_dc = rpb.shape
    rep = np.array([0, 1, 2, 3, 4, rows - 3, rows - 2, rows - 1])
    start = np.clip(rep - kr // 2, 0, rows - kr)
    dr = start[:, None] - rep[:, None] + WIN_R - 1 + np.arange(kr)[None, :]
    qcol = np.arange(GRID_W)
    kcol = np.arange(GRID_W)
    q_start = np.clip(qcol - WIN_C // 2, 0, GRID_W - WIN_C)
    rel = kcol[None, :] - q_start[:, None]
    valid = (rel >= 0) & (rel < WIN_C)
    dc = np.clip(kcol[None, :] - qcol[:, None], -(WIN_C - 1), WIN_C - 1) + WIN_C - 1
    row_sel = np.zeros((8 * kr, 16), np.float32)
    row_sel[np.arange(8 * kr), dr.reshape(-1)] = 1.0
    col_sel = np.zeros((32, GRID_W * GRID_W), np.float32)
    qq, kk = np.nonzero(valid)
    col_sel[dc[qq, kk], qq * GRID_W + kk] = 1.0
    neg = np.where(valid, 0.0, NEG_INF).astype(np.float32).reshape(1, -1)
    rpb_p = jnp.pad(rpb.astype(F32), ((0, 0), (0, 16 - n_dr), (0, 32 - n_dc)))
    full = lambda a: pl.BlockSpec(a.shape, lambda h: (0,) * a.ndim)
    consts = [jnp.asarray(row_sel, BF16), jnp.asarray(col_sel, BF16), jnp.asarray(neg)]
    b = pl.pallas_call(
        _na_bias_kernel,
        grid=(nh,),
        in_specs=[pl.BlockSpec((1, 16, 32), lambda h: (h, 0, 0))] + [full(a) for a in consts],
        out_specs=pl.BlockSpec((1, 8 * kr, GRID_W * GRID_W), lambda h: (h, 0, 0)),
        out_shape=jax.ShapeDtypeStruct((nh, 8 * kr, GRID_W * GRID_W), F32),
        compiler_params=_cparams("parallel"),
        name="na_bias",
    )(rpb_p, *consts)
    b = b.reshape(nh, 8, kr, GRID_W, GRID_W)
    return b.transpose(0, 1, 3, 2, 4).reshape(nh, 8, GRID_W, kr * GRID_W)


NA_ROWS_PER_STEP = 4


def _na_latent_kernel(q_ref, k_ref, v_ref, ck_ref, cv_ref, b_ref, o_ref, *, rows):
    scale = NA_HEAD_DIM ** -0.5
    kr = min(WIN_R, rows)

    def query_row(r, hh):
        start = jnp.clip(r - kr // 2, 0, rows - kr)
        typ = jnp.where(r < 4, r, jnp.where(r > rows - 4, r - (rows - 8), 4))
        q0 = pl.multiple_of(r * GRID_W, GRID_W)
        k0 = pl.multiple_of(start * GRID_W, GRID_W)
        sl = slice(hh * NA_HEAD_DIM, (hh + 1) * NA_HEAD_DIM)
        q = q_ref[pl.ds(q0, GRID_W), sl]
        kw = k_ref[pl.ds(k0, kr * GRID_W), sl]
        vw = v_ref[pl.ds(k0, kr * GRID_W), sl]
        s_w = _dot_nt(q, kw) * scale + b_ref[hh, typ]
        s_c = _dot_nt(q, ck_ref[0, hh]) * scale
        yield
        m = jnp.maximum(jnp.max(s_w, axis=-1, keepdims=True), jnp.max(s_c, axis=-1, keepdims=True))
        p_w = jnp.exp(s_w - m)
        p_c = jnp.exp(s_c - m)
        l = jnp.sum(p_w, axis=-1, keepdims=True) + jnp.sum(p_c, axis=-1, keepdims=True)
        yield
        o_ref[pl.ds(q0, GRID_W), sl] = (_dot(p_w, vw) + _dot(p_c, cv_ref[0, hh])) / l

    def body(rp, carry):
        _lockstep([query_row(rp * NA_ROWS_PER_STEP + u, hh) for u in range(NA_ROWS_PER_STEP) for hh in range(2)])
        return carry

    lax.fori_loop(0, rows // NA_ROWS_PER_STEP, body, 0)


def _na_latent(z, ctx_k, ctx_v, bias, n_seq, seq_len, row_blk0):
    cb = OFF_NA // 128
    rows = seq_len // GRID_W
    past = ctx_k.shape[2]
    return pl.pallas_call(
        functools.partial(_na_latent_kernel, rows=rows),
        grid=(n_seq, NA_HEADS // 2),
        in_specs=[pl.BlockSpec((seq_len, 128), lambda b, h: (row_blk0 + b, cb + h)),
                  pl.BlockSpec((seq_len, 128), lambda b, h: (row_blk0 + b, cb + 4 + h)),
                  pl.BlockSpec((seq_len, 128), lambda b, h: (row_blk0 + b, cb + 8 + h)),
                  pl.BlockSpec((1, 2, past, NA_HEAD_DIM), lambda b, h: (b, h, 0, 0)),
                  pl.BlockSpec((1, 2, past, NA_HEAD_DIM), lambda b, h: (b, h, 0, 0)),
                  pl.BlockSpec((2, 8, GRID_W, bias.shape[-1]), lambda b, h: (h, 0, 0, 0))],
        out_specs=pl.BlockSpec((seq_len, 128), lambda b, h: (b, h)),
        out_shape=jax.ShapeDtypeStruct((n_seq * seq_len, NA_HEADS * NA_HEAD_DIM), F32),
        compiler_params=_cparams("parallel", "parallel"),
        name="na_latent",
    )(z, z, z, ctx_k, ctx_v, bias)


def _ret_decay_tables(decay_logit):
    c = RET_CHUNK
    lg = jnp.log(jax.nn.sigmoid(decay_logit.astype(F32)))[:, :, None, None]
    i = jnp.arange(c, dtype=F32)
    diff = i[:, None] - i[None, :]
    m_f = jnp.where(diff >= 0, jnp.exp(jnp.maximum(diff, 0.0) * lg[0]), 0.0)
    m_b = jnp.where(diff <= 0, jnp.exp(jnp.maximum(-diff, 0.0) * lg[1]), 0.0)
    mask = jnp.stack([m_f, m_b])
    ones = jnp.ones((1, 1, 1, RET_DK), F32)
    q_f = jnp.exp((i + 1.0)[None, :, None] * lg[0]) * ones[0]
    q_b = jnp.exp((c - i)[None, :, None] * lg[1]) * ones[0]
    k_f = jnp.exp((c - 1.0 - i)[None, :, None] * lg[0]) * ones[0]
    k_b = jnp.exp(i[None, :, None] * lg[1]) * ones[0]
    c_dec = jnp.exp(c * lg) * jnp.ones((1, 1, RET_DK, RET_DV), F32)
    return mask, jnp.stack([q_f, q_b]), jnp.stack([k_f, k_b]), c_dec


def _rope_tables(seq_len):
    quarter = RET_DK // 4
    pos = np.arange(seq_len)
    freqs = 1.0 / (ROPE_BASE ** (jnp.arange(quarter, dtype=F32) / quarter))
    a_row = jnp.asarray(pos // GRID_W, F32)[:, None] * freqs[None, :]
    a_col = jnp.asarray(pos % GRID_W, F32)[:, None] * freqs[None, :]
    cos = jnp.concatenate([jnp.cos(a_row)] * 2 + [jnp.cos(a_col)] * 2, axis=-1)
    sin = jnp.concatenate([-jnp.sin(a_row), jnp.sin(a_row), -jnp.sin(a_col), jnp.sin(a_col)], axis=-1)
    return jnp.tile(cos, (1, RET_HEADS)), jnp.tile(sin, (1, RET_HEADS))


def _retention_kernel(*refs, seq_len, rope):
    if rope:
        (q_ref, k_ref, v_ref, g_ref, s0_ref, mask_ref, qd_ref, kd_ref, cd_ref, gnw_ref, gnb_ref,
         cos_ref, sin_ref, o_ref, sfin_ref, qs_ref, ks_ref, acc_ref) = refs
    else:
        (q_ref, k_ref, v_ref, g_ref, s0_ref, mask_ref, qd_ref, kd_ref, cd_ref, gnw_ref, gnb_ref,
         o_ref, sfin_ref, qs_ref, ks_ref, acc_ref) = refs
    c = RET_CHUNK
    n = seq_len // c
    quarter = RET_DK // 4
    q = q_ref[...]
    k = k_ref[...] * (RET_DK ** -0.5)
    if rope:
        lane = lax.broadcasted_iota(jnp.int32, q.shape, 1)
        first = (lane % (2 * quarter)) < quarter
        width = q.shape[1]

        def rot(x):
            swapped = jnp.where(first, pltpu.roll(x, width - quarter, 1), pltpu.roll(x, quarter, 1))
            return x * cos_ref[...] + swapped * sin_ref[...]

        q = rot(q)
        k = rot(k)
    qs_ref[...] = q
    ks_ref[...] = k

    sfin_ref[...] = s0_ref[...]

    def chain(h, d, ci):
        ksl = slice(h * RET_DK, (h + 1) * RET_DK)
        vsl = slice(h * RET_DV, (h + 1) * RET_DV)
        cc = ci if d == 0 else n - 1 - ci
        rows = pl.ds(pl.multiple_of(cc * c, c), c)
        qc = qs_ref[rows, ksl]
        kc = ks_ref[rows, ksl]
        vc = v_ref[rows, vsl]
        att = _dot_nt(qc, kc) * mask_ref[d, h]
        s = sfin_ref[0, d, h]
        yield
        acc_ref[d, rows, vsl] = _dot(att, vc) + _dot(qc * qd_ref[d, h], s)
        yield
        sfin_ref[0, d, h] = s * cd_ref[d, h] + _dot_tn(kc * kd_ref[d, h], vc)

    def step(ci, carry):
        _lockstep([chain(h, d, ci) for h in range(RET_HEADS) for d in range(2)])
        return carry

    lax.fori_loop(0, n, step, 0)

    for h in range(RET_HEADS):
        vsl = slice(h * RET_DV, (h + 1) * RET_DV)
        y = _ln(acc_ref[0, :, vsl] + acc_ref[1, :, vsl], RET_GN_EPS) * gnw_ref[:, vsl] + gnb_ref[:, vsl]
        g = g_ref[:, vsl]
        o_ref[:, vsl] = y * (g * _sigmoid(g))


def _retention(z, s0, tables, gn_w, gn_b, n_seq, seq_len, row_blk0, rope_tabs):
    mask, qd, kd, cd = tables
    rope = rope_tabs is not None
    qk_w = RET_HEADS * RET_DK
    v_w = RET_HEADS * RET_DV
    full = lambda a: pl.BlockSpec(a.shape, lambda b: (0,) * a.ndim)
    in_specs = [pl.BlockSpec((seq_len, qk_w), lambda b: (row_blk0 + b, OFF_RET_QK // qk_w)),
                pl.BlockSpec((seq_len, qk_w), lambda b: (row_blk0 + b, OFF_RET_QK // qk_w + 1)),
                pl.BlockSpec((seq_len, v_w), lambda b: (row_blk0 + b, OFF_RET_V // v_w)),
                pl.BlockSpec((seq_len, v_w), lambda b: (row_blk0 + b, OFF_RET_V // v_w + 1)),
                pl.BlockSpec((1, 2, RET_HEADS, RET_DK, RET_DV), lambda b: (b, 0, 0, 0, 0)),
                full(mask), full(qd), full(kd), full(cd), full(gn_w), full(gn_b)]
    args = [z, z, z, z, s0, mask, qd, kd, cd, gn_w, gn_b]
    if rope:
        in_specs += [full(rope_tabs[0]), full(rope_tabs[1])]
        args += list(rope_tabs)
    return pl.pallas_call(
        functools.partial(_retention_kernel, seq_len=seq_len, rope=rope),
        grid=(n_seq,),
        in_specs=in_specs,
        out_specs=[pl.BlockSpec((seq_len, v_w), lambda b: (b, 0)),
                   pl.BlockSpec((1, 2, RET_HEADS, RET_DK, RET_DV), lambda b: (b, 0, 0, 0, 0))],
        out_shape=[jax.ShapeDtypeStruct((n_seq * seq_len, v_w), F32),
                   jax.ShapeDtypeStruct((n_seq, 2, RET_HEADS, RET_DK, RET_DV), F32)],
        scratch_shapes=[pltpu.VMEM((seq_len, qk_w), F32), pltpu.VMEM((seq_len, qk_w), F32),
                        pltpu.VMEM((2, seq_len, v_w), F32)],
        compiler_params=_cparams("parallel"),
        name="retention_rope" if rope else "retention",
    )(*args)


def _rwkv_prep_kernel(zm_ref, zl_ref, pm_ref, nm_ref, pl_ref, nl_ref, mum_ref, mul_ref,
                      w0_ref, wup_ref, a0_ref, aup_ref, gup_ref, kk_ref, ka_ref, rk_ref, ones_ref, tri_ref,
                      r_out, kap_out, v_out, lw_out, cs_out, kd_out, beta_out, g_out, bonus_out,
                      *, tiles_per_seq_of_tile):
    i = pl.program_id(0)
    pos, per = tiles_per_seq_of_tile(i)
    has_prev = (pos != 0).astype(F32)
    has_next = (pos != per - 1).astype(F32)
    tt = zm_ref.shape[0]

    def shift(z_ref, p_ref, n_ref, mu_ref):
        z = z_ref[...]
        row = lax.broadcasted_iota(jnp.int32, z.shape, 0)
        prev = jnp.where(row == 0, p_ref[7:8, :] * has_prev, pltpu.roll(z, 1, 0))
        nxt = jnp.where(row == tt - 1, n_ref[0:1, :] * has_next, pltpu.roll(z, tt - 1, 0))
        return z + mu_ref[...] * (0.5 * (prev + nxt) - z)

    zm = shift(zm_ref, pm_ref, nm_ref, mum_ref)
    zl = shift(zl_ref, pl_ref, nl_ref, mul_ref)
    w = RWKV_WIDTH
    r_c, k_c, v_c = zm[:, 0:w], zm[:, w:2 * w], zm[:, 2 * w:3 * w]
    ones = ones_ref[...]
    g_out[...] = _dot(_sigmoid(zl[:, 256:384]), gup_ref[...])
    kk = k_c * kk_ref[...]
    ss = _dot_exact_rhs(kk * kk, ones)
    kap = kk * lax.rsqrt(jnp.maximum(ss, 1e-24))
    bonus_out[...] = _dot_exact_rhs(r_c * k_c * rk_ref[...], ones) * v_c
    r_out[...] = r_c
    kap_out[...] = kap
    v_out[...] = v_c
    for d in range(2):
        wl = zl[:, d * 64:(d + 1) * 64]
        al = zl[:, 128 + d * 64:128 + (d + 1) * 64]
        lw = -RWKV_DECAY_SCALE * _sigmoid(w0_ref[d] + _dot(jnp.tanh(wl), wup_ref[d]))
        a = _sigmoid(a0_ref[d] + _dot(al, aup_ref[d]))
        k_d = k_c * (1.0 + (a - 1.0) * ka_ref[...])
        beta = kap * a
        cs = _dot_exact_lhs(tri_ref[d], lw)
        lw_out[d] = lw
        cs_out[d] = cs
        kd_out[d] = k_d
        beta_out[d] = beta


def _rwkv_prep(z, p, tiles_per_seq_of_tile, tt):
    m = z.shape[0]
    nt = m // tt
    w = RWKV_WIDTH
    mb, lb = OFF_RW_MAIN // (3 * w), OFF_RW_LORA // 384
    hm = tt // 8
    last8 = m // 8 - 1
    prev_idx = lambda i: jnp.maximum(i * hm - 1, 0)
    next_idx = lambda i: jnp.minimum((i + 1) * hm, last8)
    full = lambda a: pl.BlockSpec(a.shape, lambda i: (0,) * a.ndim)
    heads = jnp.arange(w) // RWKV_N
    ones = (heads[:, None] == heads[None, :]).astype(BF16)
    t = np.arange(tt)
    same = (t[:, None] // RWKV_CHUNK) == (t[None, :] // RWKV_CHUNK)
    tri = jnp.asarray(np.stack([same & (t[None, :] <= t[:, None]), same & (t[None, :] >= t[:, None])]), BF16)
    params = [p['mu_main'], p['mu_lora'], p['rwkv_w0'], p['rwkv_w_up'], p['rwkv_a0'], p['rwkv_a_up'],
              p['rwkv_g_up'], p['rwkv_k_k'], p['rwkv_k_a'], p['rwkv_r_k'], ones, tri]
    hd = lambda: jax.ShapeDtypeStruct((m, w), F32)
    dhd = lambda: jax.ShapeDtypeStruct((2, m, w), F32)
    tok = lambda: jax.ShapeDtypeStruct((m, w), F32)
    hd_spec = pl.BlockSpec((tt, w), lambda i: (i, 0))
    dhd_spec = pl.BlockSpec((2, tt, w), lambda i: (0, i, 0))
    tok_spec = pl.BlockSpec((tt, w), lambda i: (i, 0))
    return pl.pallas_call(
        functools.partial(_rwkv_prep_kernel, tiles_per_seq_of_tile=tiles_per_seq_of_tile),
        grid=(nt,),
        in_specs=[pl.BlockSpec((tt, 3 * w), lambda i: (i, mb)),
                  pl.BlockSpec((tt, 384), lambda i: (i, lb)),
                  pl.BlockSpec((8, 3 * w), lambda i: (prev_idx(i), mb)),
                  pl.BlockSpec((8, 3 * w), lambda i: (next_idx(i), mb)),
                  pl.BlockSpec((8, 384), lambda i: (prev_idx(i), lb)),
                  pl.BlockSpec((8, 384), lambda i: (next_idx(i), lb))] + [full(a) for a in params],
        out_specs=[hd_spec, hd_spec, hd_spec, dhd_spec, dhd_spec, dhd_spec, dhd_spec, tok_spec, tok_spec],
        out_shape=[hd(), hd(), hd(), dhd(), dhd(), dhd(), dhd(), tok(), tok()],
        compiler_params=_cparams("parallel"),
        name="rwkv_prep",
    )(z, z, z, z, z, z, *params)


def _dot3(a, b, dims):
    a_hi = a.astype(BF16)
    b_hi = b.astype(BF16)
    a_lo = (a - a_hi.astype(F32)).astype(BF16)
    b_lo = (b - b_hi.astype(F32)).astype(BF16)
    dn = (dims, ((), ()))
    out = lax.dot_general(a_lo, b_hi, dn, preferred_element_type=F32)
    out = out + lax.dot_general(a_hi, b_lo, dn, preferred_element_type=F32)
    return out + lax.dot_general(a_hi, b_hi, dn, preferred_element_type=F32)


_NN = ((1,), (0,))
_NT = ((1,), (1,))
_TN = ((0,), (0,))


def _dot1(a, b, dims):
    return lax.dot_general(a.astype(BF16), b.astype(BF16), (dims, ((), ())), preferred_element_type=F32)


def _dot_rhs2(a, b, dims):
    assert dims == _NN
    b_hi = b.astype(BF16)
    b_lo = (b - b_hi.astype(F32)).astype(BF16)
    out = jnp.dot(a.astype(BF16), jnp.concatenate([b_hi, b_lo], axis=1), preferred_element_type=F32)
    n = b.shape[1]
    return out[:, n:] + out[:, :n]


_RWKV_MM = _dot1
RWKV_CHAINS = 16
RWKV_MAX_GROUP = 8


def _rwkv_scan_kernel(r_ref, kap_ref, v_ref, lw_ref, cs_ref, kd_ref, beta_ref, h0_ref, o_ref, hfin_ref, *,
                      seq_len, group, heads):
    c = RWKV_CHUNK
    n = seq_len // c
    d = pl.program_id(0)
    sgn = 1 - 2 * d
    ti = lax.broadcasted_iota(jnp.int32, (c, c), 0) * sgn
    tj = lax.broadcasted_iota(jnp.int32, (c, c), 1) * sgn
    strict = tj < ti
    incl = tj <= ti
    eye_f = (ti == tj).astype(F32)

    def chunk_terms(hh, cc):
        rows = pl.ds(pl.multiple_of(cc * c, c), c)
        hl = slice(hh * RWKV_N, (hh + 1) * RWKV_N)
        r = r_ref[rows, hl]
        kap = kap_ref[rows, hl]
        v = v_ref[rows, hl]
        lw = lw_ref[0, rows, hl]
        cs = cs_ref[0, rows, hl]
        kd = kd_ref[0, rows, hl]
        beta = beta_ref[0, rows, hl]
        tot = jnp.sum(lw, axis=0, keepdims=True)
        kap_t = kap * jnp.exp(cs - lw)
        r_t = r * jnp.exp(cs)
        e_ncs = jnp.exp(-cs)
        k_t = kd * e_ncs
        b_t = beta * e_ncs
        e_end = jnp.exp(tot - cs)
        k_end = kd * e_end
        b_end = beta * e_end
        yield
        p = _RWKV_MM(jnp.concatenate([kap_t, r_t], axis=0), jnp.concatenate([b_t, k_t], axis=0), _NT)
        l_b = jnp.where(strict, p[:c, :c], 0.0)
        l_k = jnp.where(strict, p[:c, c:], 0.0)
        m_b = jnp.where(incl, p[c:, :c], 0.0)
        m_k = jnp.where(incl, p[c:, c:], 0.0)
        yield
        x = -l_b
        y = jnp.concatenate([kap_t, _RWKV_MM(l_k, v, _NN)], axis=1)
        yield
        y = y + _dot_rhs2(x, y, _NN)
        pw = x
        for _ in range(5):
            pw = _RWKV_MM(pw, pw, _NN)
            yield
            y = y + _dot_rhs2(pw, y, _NN)
        yield
        mby = _RWKV_MM(m_b, y, _NN)
        bty = _RWKV_MM(b_end, y, _TN)
        rr = r_t - mby[:, :RWKV_N]
        o0 = _RWKV_MM(m_k, v, _NN) - mby[:, RWKV_N:]
        pc = eye_f * jnp.exp(tot) - bty[:, :RWKV_N]
        qc = _RWKV_MM(k_end, v, _TN) - bty[:, RWKV_N:]
        return rows, rr, o0, pc, qc

    def step(gi, hstates):
        keys = [(hh, u) for u in range(group) for hh in range(heads)]
        terms = dict(zip(keys, _lockstep(
            [chunk_terms(hh, (gi * group + u) + d * (n - 1 - 2 * (gi * group + u))) for hh, u in keys])))
        hstates = list(hstates)
        for u in range(group):
            for hh in range(heads):
                rows, rr, o0, pc, qc = terms[(hh, u)]
                o_ref[0, hh, rows, :] = _dot3(rr, hstates[hh], _NN) + o0
                hstates[hh] = _dot3(pc, hstates[hh], _NN) + qc
        return tuple(hstates)

    fin = lax.fori_loop(0, n // group, step, tuple(h0_ref[0, 0, hh] for hh in range(heads)))
    for hh in range(heads):
        hfin_ref[0, 0, hh] = fin[hh]


def _rwkv_scan(r, kap, v, lw, cs, kd, beta, h0, n_seq, seq_len, row_blk0):
    group = min(seq_len // RWKV_CHUNK, RWKV_MAX_GROUP)
    hp = RWKV_CHAINS // group
    hd_spec = pl.BlockSpec((seq_len, hp * RWKV_N), lambda d, h, b: (row_blk0 + b, h))
    dhd_spec = pl.BlockSpec((1, seq_len, hp * RWKV_N), lambda d, h, b: (d, row_blk0 + b, h))
    st_spec = pl.BlockSpec((1, 1, hp, RWKV_N, RWKV_N), lambda d, h, b: (b, d, h, 0, 0))
    return pl.pallas_call(
        functools.partial(_rwkv_scan_kernel, seq_len=seq_len, group=group, heads=hp),
        grid=(2, RWKV_HEADS // hp, n_seq),
        in_specs=[hd_spec, hd_spec, hd_spec, dhd_spec, dhd_spec, dhd_spec, dhd_spec, st_spec],
        out_specs=[pl.BlockSpec((1, hp, seq_len, RWKV_N), lambda d, h, b: (d, h, b, 0)), st_spec],
        out_shape=[jax.ShapeDtypeStruct((2, RWKV_HEADS, n_seq * seq_len, RWKV_N), F32),
                   jax.ShapeDtypeStruct((n_seq, 2, RWKV_HEADS, RWKV_N, RWKV_N), F32)],
        compiler_params=_cparams("parallel", "parallel", "parallel"),
        name="rwkv_scan",
    )(r, kap, v, lw, cs, kd, beta, h0)


def _merge_kernel(oscan_c_ref, oscan_l_ref, ona_c_ref, ona_l_ref, ort_c_ref, ort_l_ref, bonus_ref, grw_ref,
                  ga_ref, gb_ref, gc_ref, x_ref, gt_ref, gnw_ref, gnb_ref, wbr_ref, wout_ref, lng_ref, lnb_ref,
                  o_ref, orw_ref, *, ctx_tiles):
    is_ctx = pl.program_id(0) < ctx_tiles
    pick = lambda c_val, l_val: jnp.where(is_ctx, c_val, l_val)
    for h in range(RWKV_HEADS):
        sl = slice(h * RWKV_N, (h + 1) * RWKV_N)
        o = pick(oscan_c_ref[0, h] + oscan_c_ref[1, h], oscan_l_ref[0, h] + oscan_l_ref[1, h])
        y = _ln(o, RWKV_GN_EPS) * gnw_ref[h] + gnb_ref[h]
        orw_ref[:, sl] = (y + bonus_ref[:, sl]) * grw_ref[:, sl]
    merged = (_sigmoid(ga_ref[...]) * _dot(pick(ona_c_ref[...], ona_l_ref[...]), wbr_ref[0])
              + _sigmoid(gb_ref[...]) * _dot(pick(ort_c_ref[...], ort_l_ref[...]), wbr_ref[1])
              + _sigmoid(gc_ref[...]) * _dot(orw_ref[...], wbr_ref[2]))
    t = _dot(merged, wout_ref[...])
    o_ref[...] = _ln(DEEPNORM_ALPHA * x_ref[...] + gt_ref[0] * t) * lng_ref[...] + lnb_ref[...]


def _merge(oscan, o_na, o_rt, bonus, g_rw, z, x, mod3, p, row_of_tile, tm):
    m = x.shape[0]
    bw = RWKV_WIDTH
    ctx_tiles = o_na[0].shape[0] // tm
    full = lambda a: pl.BlockSpec(a.shape, lambda i: (0,) * a.ndim)
    tok = pl.BlockSpec((tm, bw), lambda i: (i, 0))
    c_row = lambda i: jnp.minimum(i, ctx_tiles - 1)
    l_row = lambda i: jnp.maximum(i - ctx_tiles, 0)
    tok_c = pl.BlockSpec((tm, bw), lambda i: (c_row(i), 0))
    tok_l = pl.BlockSpec((tm, bw), lambda i: (l_row(i), 0))
    params = [p['rwkv_gn_w'], p['rwkv_gn_b'], p['w_br'], p['w_out'], p['ln_a_g'], p['ln_a_b']]
    return pl.pallas_call(
        functools.partial(_merge_kernel, ctx_tiles=ctx_tiles),
        grid=(m // tm,),
        in_specs=[pl.BlockSpec((2, RWKV_HEADS, tm, RWKV_N), lambda i: (0, 0, c_row(i), 0)),
                  pl.BlockSpec((2, RWKV_HEADS, tm, RWKV_N), lambda i: (0, 0, l_row(i), 0)),
                  tok_c, tok_l, tok_c, tok_l, tok, tok,
                  pl.BlockSpec((tm, D_MODEL), lambda i: (i, 0)),
                  pl.BlockSpec((tm, D_MODEL), lambda i: (i, 1)),
                  pl.BlockSpec((tm, D_MODEL), lambda i: (i, 2)),
                  pl.BlockSpec((tm, D_MODEL), lambda i: (i, 0)),
                  pl.BlockSpec((1, 1, D_MODEL), lambda i: (row_of_tile(i, tm), 0, 2))]
                 + [full(a) for a in params],
        out_specs=pl.BlockSpec((tm, D_MODEL), lambda i: (i, 0)),
        out_shape=jax.ShapeDtypeStruct((m, D_MODEL), F32),
        scratch_shapes=[pltpu.VMEM((tm, bw), F32)],
        compiler_params=_cparams("parallel"),
        name="merge_out_proj",
    )(*oscan, *o_na, *o_rt, bonus, g_rw, z, z, z, x, mod3, *params)


def _top_values(s, k, with_rank=False):
    out = []
    rank = jnp.full(s.shape, float(k), F32)
    for b in range(k):
        m = jnp.max(s, axis=0, keepdims=True)
        out.append(m)
        hit = s == m
        if with_rank:
            rank = jnp.where(hit, float(b), rank)
        s = jnp.where(hit, -jnp.inf, s)
    return (out, rank) if with_rank else out


def _peer_gate_kernel(x_ref, sh_ref, sc_ref, wq_ref, keys_ref, hb_ref, cnt_ref, rank1_ref, e0_ref, e1_ref):
    h = _ln(x_ref[...]) * (1.0 + sc_ref[0]) + sh_ref[0]
    hb_ref[...] = h.astype(BF16)
    ht = h.T.astype(BF16)
    qt = jnp.dot(wq_ref[...], ht, preferred_element_type=F32)
    half = keys_ref.shape[-1]
    for hd in range(PEER_HEADS):
        scores, tops = [], []
        for part in range(2):
            row0 = (hd * 2 + part) * half
            s = _dot(keys_ref[hd, part], qt[row0:row0 + half, :])
            scores.append(s)
            if part == 0:
                tops.append(_top_values(s, PEER_TOPK))
            else:
                top1, rank1 = _top_values(s, PEER_TOPK, with_rank=True)
                tops.append(top1)
        pairs = [tops[0][a] + tops[1][b] for a in range(PEER_TOPK) for b in range(PEER_TOPK // (a + 1))]
        pairs += [jnp.full_like(pairs[0], -jnp.inf)] * (-len(pairs) % 8)
        best = _top_values(jnp.concatenate(pairs, axis=0), PEER_TOPK)
        zsum = jnp.exp(best[0] - best[0])
        for t in range(1, PEER_TOPK):
            zsum = zsum + jnp.exp(best[t] - best[0])
        tau = best[PEER_TOPK - 1]
        cnt = jnp.zeros(scores[0].shape, F32)
        for b, t1 in enumerate(tops[1]):
            cnt = jnp.where(scores[0] + t1 >= tau, float(b + 1), cnt)
        cnt_ref[hd] = cnt
        rank1_ref[hd] = rank1.astype(BF16)
        e0_ref[hd] = jnp.exp(scores[0] - tops[0][0])
        e1_ref[hd] = (jnp.exp(scores[1] - tops[1][0]) / zsum).astype(BF16)


def _peer_gate(x, mod3, wq_t, keys, row_of_tile, tn):
    m = x.shape[0]
    full = lambda a: pl.BlockSpec(a.shape, lambda i: (0,) * a.ndim)
    sk = pl.BlockSpec((PEER_HEADS, PEER_NKEYS, tn), lambda i: (0, 0, i))
    sk_f32 = jax.ShapeDtypeStruct((PEER_HEADS, PEER_NKEYS, m), F32)
    sk_bf16 = jax.ShapeDtypeStruct((PEER_HEADS, PEER_NKEYS, m), BF16)
    return pl.pallas_call(
        _peer_gate_kernel,
        grid=(m // tn,),
        in_specs=[pl.BlockSpec((tn, D_MODEL), lambda i: (i, 0)),
                  pl.BlockSpec((1, 1, D_MODEL), lambda i: (row_of_tile(i, tn), 0, 3)),
                  pl.BlockSpec((1, 1, D_MODEL), lambda i: (row_of_tile(i, tn), 0, 4)),
                  full(wq_t), full(keys)],
        out_specs=[pl.BlockSpec((tn, D_MODEL), lambda i: (i, 0)), sk, sk, sk, sk],
        out_shape=[jax.ShapeDtypeStruct((m, D_MODEL), BF16), sk_f32, sk_bf16, sk_f32, sk_bf16],
        compiler_params=_cparams("parallel"),
        name="peer_gate",
    )(x, mod3, mod3, wq_t, keys)


def _gelu_exact(x):
    return 0.5 * x * (1.0 + lax.erf(x * (2.0 ** -0.5)))


PEER_SUB_ROWS = 32
PEER_SUB_LANES = 256
PEER_GROUP_KEYS = 8
PEER_GROUP = PEER_GROUP_KEYS * PEER_NKEYS


def _peer_expert_kernel(hb_ref, cnt_ref, rank1_ref, e0_ref, e1_ref, u_ref, vt_ref, x_ref, gt_ref,
                        lng_ref, lnb_ref, o_ref, acc_ref, st_ref, st2_ref, act_ref, *, groups):
    e = pl.program_id(1)
    tn = hb_ref.shape[0]

    @pl.when(e == 0)
    def _():
        acc_ref[...] = jnp.zeros_like(acc_ref)

    def pre_activations(g, dst_ref):
        rows = pl.ds(pl.multiple_of(g * PEER_GROUP, PEER_GROUP), PEER_GROUP)
        dst_ref[...] = lax.dot_general(u_ref[rows, :], hb_ref[...], (_NT, ((), ())), preferred_element_type=F32)

    def gated_activations(g, src_ref):
        row0 = pl.multiple_of(g * PEER_GROUP, PEER_GROUP)
        for r in range(PEER_GROUP_KEYS):
            i = (e * groups + g) * PEER_GROUP_KEYS + r
            cnt_row = [cnt_ref[hd, pl.ds(i, 1), :] for hd in range(PEER_HEADS)]
            e0_row = [e0_ref[hd, pl.ds(i, 1), :] for hd in range(PEER_HEADS)]
            for lb in range(tn // PEER_SUB_LANES):
                ls = slice(lb * PEER_SUB_LANES, (lb + 1) * PEER_SUB_LANES)
                sub = (PEER_SUB_ROWS, PEER_SUB_LANES)
                cnt = [jnp.broadcast_to(c[:, ls].astype(BF16), sub) for c in cnt_row]
                e0 = [jnp.broadcast_to(c[:, ls].astype(BF16), sub) for c in e0_row]
                zero = jnp.zeros(sub, BF16)
                for rb in range(PEER_NKEYS // PEER_SUB_ROWS):
                    js = slice(rb * PEER_SUB_ROWS, (rb + 1) * PEER_SUB_ROWS)
                    gate = None
                    for hd in range(PEER_HEADS):
                        term = jnp.where(rank1_ref[hd, js, ls] < cnt[hd], e1_ref[hd, js, ls], zero) * e0[hd]
                        gate = term if gate is None else gate + term
                    off = r * PEER_NKEYS + rb * PEER_SUB_ROWS
                    act = _gelu_exact(src_ref[off:off + PEER_SUB_ROWS, ls]).astype(BF16) * gate
                    act_ref[pl.ds(row0 + off, PEER_SUB_ROWS), ls] = act

    scratch = (st_ref, st2_ref)
    pre_activations(0, scratch[0])
    for g in range(groups):
        if g + 1 < groups:
            pre_activations(g + 1, scratch[(g + 1) % 2])
        gated_activations(g, scratch[g % 2])
        cols = slice(g * PEER_GROUP, (g + 1) * PEER_GROUP)
        acc_ref[...] += jnp.dot(vt_ref[:, cols], act_ref[cols, :], preferred_element_type=F32)

    @pl.when(e == pl.num_programs(1) - 1)
    def _():
        f = acc_ref[...].T
        o_ref[...] = _ln(DEEPNORM_ALPHA * x_ref[...] + gt_ref[0] * f) * lng_ref[...] + lnb_ref[...]


def _peer_expert(hb, cnt, rank1, e0, e1, u, v_t, x, mod3, ln_g, ln_b, row_of_tile, tn, eb):
    m = x.shape[0]
    n_exp = u.shape[0]
    groups = eb // PEER_GROUP
    sk = pl.BlockSpec((PEER_HEADS, PEER_NKEYS, tn), lambda i, e: (0, 0, i))
    full = lambda a: pl.BlockSpec(a.shape, lambda i, e: (0,) * a.ndim)
    return pl.pallas_call(
        functools.partial(_peer_expert_kernel, groups=groups),
        grid=(m // tn, n_exp // eb),
        in_specs=[pl.BlockSpec((tn, D_MODEL), lambda i, e: (i, 0)), sk, sk, sk, sk,
                  pl.BlockSpec((eb, D_MODEL), lambda i, e: (e, 0)),
                  pl.BlockSpec((D_MODEL, eb), lambda i, e: (0, e)),
                  pl.BlockSpec((tn, D_MODEL), lambda i, e: (i, 0)),
                  pl.BlockSpec((1, 1, D_MODEL), lambda i, e: (row_of_tile(i, tn), 0, 5)),
                  full(ln_g), full(ln_b)],
        out_specs=pl.BlockSpec((tn, D_MODEL), lambda i, e: (i, 0)),
        out_shape=jax.ShapeDtypeStruct((m, D_MODEL), F32),
        scratch_shapes=[pltpu.VMEM((D_MODEL, tn), F32), pltpu.VMEM((PEER_GROUP, tn), F32),
                        pltpu.VMEM((PEER_GROUP, tn), F32), pltpu.VMEM((eb, tn), BF16)],
        compiler_params=_cparams("parallel", "arbitrary"),
        name="peer_experts",
    )(hb, cnt, rank1, e0, e1, u, v_t, x, mod3, ln_g, ln_b)


def _permute_in_proj(w_in):
    na, qk, rv = 3 * 512, 2 * 256, 2 * 512
    o_na, o_qk, o_rv, o_rw, o_g = 0, na, na + qk, na + qk + rv, na + qk + rv + 1920
    pad = jnp.zeros((w_in.shape[0], P_IN_PADDED - P_IN), w_in.dtype)
    return jnp.concatenate([w_in[:, o_g:], w_in[:, o_na:o_qk], w_in[:, o_qk:o_rv], w_in[:, o_rv:o_rw],
                            w_in[:, o_rw:o_g], pad], axis=1)


def kernel(x_prompt, x_sample, cache_na_k, cache_na_v, state_ret, state_rwkv, c, c_ctx, w_mod, b_mod, w_in, na_rpb, ret_decay_logit, ret_gn_w, ret_gn_b, rwkv_mu, rwkv_w0, rwkv_w_up, rwkv_a0, rwkv_a_up, rwkv_g_up, rwkv_k_k, rwkv_k_a, rwkv_r_k, rwkv_gn_w, rwkv_gn_b, w_br, w_out, ln_a_g, ln_a_b, ln_f_g, ln_f_b, peer_wq, peer_keys, peer_u, peer_v):
    n_ctx, ctx_len, d = x_prompt.shape
    n_lat, lat_len, _ = x_sample.shape
    m_ctx = n_ctx * ctx_len
    assert m_ctx % lat_len == 0 and n_lat == 2
    lat_blk0 = m_ctx // lat_len

    def row_of_tile(i, tm):
        return jnp.where(i < m_ctx // tm, 0, 1 + (i - m_ctx // tm) // (lat_len // tm))

    prep_tt = 256
    assert ctx_len == prep_tt

    def tiles_per_seq_of_tile(i):
        lat = i >= m_ctx // prep_tt
        per = jnp.where(lat, lat_len // prep_tt, 1)
        pos = jnp.where(lat, (i - m_ctx // prep_tt) % (lat_len // prep_tt), 0)
        return pos, per

    x = jnp.concatenate([x_prompt.reshape(m_ctx, d), x_sample.reshape(n_lat * lat_len, d)], axis=0)
    c8 = jnp.concatenate([c_ctx[None], c, jnp.zeros((8 - 1 - n_lat, d), F32)], axis=0)
    mod = _modulation(c8, w_mod, b_mod)
    rope_tabs = _rope_tables(lat_len)
    zeros_ret = jnp.zeros((n_ctx, 2, RET_HEADS, RET_DK, RET_DV), F32)
    zeros_rw = jnp.zeros((n_ctx, 2, RWKV_HEADS, RWKV_N, RWKV_N), F32)

    nk, nv, sr, sw = [], [], [], []
    for l in range(DEPTH):
        mod3 = mod[l].reshape(8, 1, 6 * d)
        w_in_p = _permute_in_proj(w_in[l]).astype(BF16)
        z = _lnmod_matmul(x, mod3, w_in_p, row_of_tile, 1024, 2048, 0, 1)

        o_na_c, k_cache, v_cache = _ctx_attention(z, n_ctx, ctx_len)
        o_na_l = _na_latent(z, cache_na_k[:, l], cache_na_v[:, l], _na_bias_table(na_rpb[l], lat_len // GRID_W),
                            n_lat, lat_len, lat_blk0)
        nk.append(k_cache)
        nv.append(v_cache)

        tables = _ret_decay_tables(ret_decay_logit[l])
        gn_w, gn_b = ret_gn_w[l][None], ret_gn_b[l][None]
        o_rt_c, s_ret = _retention(z, zeros_ret, tables, gn_w, gn_b, n_ctx, ctx_len, 0, None)
        o_rt_l, _ = _retention(z, state_ret[:, l], tables, gn_w, gn_b, n_lat, lat_len, lat_blk0, rope_tabs)
        sr.append(s_ret)

        mu = rwkv_mu[l]
        prm = {
            'mu_main': mu[None, :1536], 'mu_lora': mu[None, 1536:],
            'rwkv_w0': rwkv_w0[l][:, None], 'rwkv_w_up': rwkv_w_up[l].astype(BF16),
            'rwkv_a0': rwkv_a0[l][:, None], 'rwkv_a_up': rwkv_a_up[l].astype(BF16),
            'rwkv_g_up': rwkv_g_up[l].astype(BF16), 'rwkv_k_k': rwkv_k_k[l][None],
            'rwkv_k_a': rwkv_k_a[l][None], 'rwkv_r_k': rwkv_r_k[l].reshape(1, RWKV_WIDTH),
        }
        r, kap, v, lw, cs, kd, beta, g_rw, bonus = _rwkv_prep(z, prm, tiles_per_seq_of_tile, prep_tt)
        o_c, h_fin = _rwkv_scan(r, kap, v, lw, cs, kd, beta, zeros_rw, n_ctx, ctx_len, 0)
        h0_lat = jnp.swapaxes(state_rwkv[:, l], -1, -2)
        o_l, _ = _rwkv_scan(r, kap, v, lw, cs, kd, beta, h0_lat, n_lat, lat_len, lat_blk0)
        sw.append(jnp.swapaxes(h_fin, -1, -2))

        mp = {
            'rwkv_gn_w': rwkv_gn_w[l].reshape(RWKV_HEADS, 1, RWKV_N),
            'rwkv_gn_b': rwkv_gn_b[l].reshape(RWKV_HEADS, 1, RWKV_N),
            'w_br': w_br[l].astype(BF16), 'w_out': w_out[l].astype(BF16),
            'ln_a_g': ln_a_g[l][None], 'ln_a_b': ln_a_b[l][None],
        }
        x = _merge((o_c, o_l), (o_na_c, o_na_l), (o_rt_c, o_rt_l), bonus, g_rw, z, x, mod3, mp, row_of_tile, 256)

        wq_t = peer_wq[l].T.astype(BF16)
        hb, cnt, rank1, e0, e1 = _peer_gate(x, mod3, wq_t, peer_keys[l].astype(BF16), row_of_tile, 256)
        x = _peer_expert(hb, cnt, rank1, e0, e1, peer_u[l].astype(BF16), peer_v[l].T.astype(BF16), x, mod3,
                         ln_f_g[l][None], ln_f_b[l][None], row_of_tile, 512, 2048)

    dt = x_prompt.dtype
    y_prompt = x[:m_ctx].reshape(n_ctx, ctx_len, d)
    y_sample = x[m_ctx:].reshape(n_lat, lat_len, d)
    return (y_prompt, y_sample, jnp.stack(nk, axis=1).astype(dt), jnp.stack(nv, axis=1).astype(dt),
            jnp.stack(sr, axis=1).astype(dt), jnp.stack(sw, axis=1).astype(dt))
```
